```python
import math
import jax, jax.numpy as jnp
from jax import lax
import numpy as np

D_MODEL = 1024
BATCH = 4
SEQ = 8192
DEPTH = 1

CHUNK = 64
POOL_WIDTH = D_MODEL // 2
POOL_WINDOWS = (2, 4, 8, 16)
N_POOL_GROUPS = len(POOL_WINDOWS)
POOL_GROUP = POOL_WIDTH // N_POOL_GROUPS
ATTN_WIDTH = D_MODEL - POOL_WIDTH
HEAD_DIM = 64
N_HEADS = ATTN_WIDTH // HEAD_DIM
Q_BLOCK = 128
IN_WIDTH = POOL_WIDTH + 3 * ATTN_WIDTH + N_HEADS
N_EXPERTS = 32
TOP_K = 4
D_EXPERT = D_MODEL
SWIGLU_LIMIT = 7.0
SWIGLU_ALPHA = 1.702
EXPERT_BLOCK = 128
RMS_EPS = 1e-5

kernel_name = 'hybrid_pool_fox_moe_block'


def rms_norm(x, g):
    xf = x.astype(jnp.float32)
    y = xf * lax.rsqrt(jnp.mean(xf * xf, axis=-1, keepdims=True) + RMS_EPS)
    return (y * g.astype(jnp.float32)).astype(x.dtype)


def pool_mixer(u, w_pool, pool_scale):
    B, S, _ = u.shape
    uf = u.astype(jnp.float32)
    cs = jnp.cumsum(uf, axis=1)
    t = jnp.arange(S)
    outs = []
    for gi, w in enumerate(POOL_WINDOWS):
        sl = slice(gi * POOL_GROUP, (gi + 1) * POOL_GROUP)
        c = cs[:, :, sl]
        lower = jnp.pad(c[:, :S - w], ((0, 0), (w, 0), (0, 0)))
        cnt = jnp.minimum(t + 1, w).astype(jnp.float32)
        outs.append((c - lower) / cnt[None, :, None] - uf[:, :, sl])
    p = jnp.stack(outs, axis=2).astype(u.dtype)
    y = jnp.einsum('bsgc,gcd->bsgd', p, w_pool).reshape(B, S, POOL_WIDTH)
    return y * pool_scale


def forgetting_attention(q, k, v, log_f):
    B, H, S, Dh = q.shape
    c = jnp.cumsum(log_f, axis=-1)
    kf = k.astype(jnp.float32)
    key_pos = jnp.arange(S)
    scale = HEAD_DIM ** -0.5

    def block(i):
        start = i * Q_BLOCK
        qb = lax.dynamic_slice_in_dim(q, start, Q_BLOCK, axis=2).astype(jnp.float32)
        cb = lax.dynamic_slice_in_dim(c, start, Q_BLOCK, axis=2)
        logits = jnp.einsum('bhqd,bhkd->bhqk', qb, kf) * scale + cb[..., :, None] - c[..., None, :]
        q_pos = start + jnp.arange(Q_BLOCK)
        mask = key_pos[None, :] <= q_pos[:, None]
        logits = jnp.where(mask, logits, -jnp.inf)
        p = jax.nn.softmax(logits, axis=-1)
        return jnp.einsum('bhqk,bhkd->bhqd', p.astype(v.dtype), v)

    out = lax.map(block, jnp.arange(S // Q_BLOCK))
    return jnp.moveaxis(out, 0, 2).reshape(B, H, S, Dh)


def clamped_swiglu(a):
    x_glu, x_lin = a[..., :D_EXPERT], a[..., D_EXPERT:]
    x_glu = jnp.minimum(x_glu, SWIGLU_LIMIT)
    x_lin = jnp.clip(x_lin, -SWIGLU_LIMIT, SWIGLU_LIMIT)
    return x_glu * jax.nn.sigmoid(SWIGLU_ALPHA * x_glu) * (x_lin + 1.0)


def moe_ffn(h, w_router, b_router, w_up, b_up, w_down, b_down):
    B, S, D = h.shape
    T = B * S
    hf = h.reshape(T, D)
    logits = (hf @ w_router).astype(jnp.float32) + b_router.astype(jnp.float32)
    top_vals, top_idx = lax.top_k(logits, TOP_K)
    gates = jax.nn.softmax(top_vals, axis=-1)
    n_assign = T * TOP_K
    flat_e = top_idx.reshape(-1).astype(jnp.int32)
    flat_tok = jnp.arange(n_assign, dtype=jnp.int32) // TOP_K
    order = jnp.argsort(flat_e)
    sorted_e = flat_e[order]
    sorted_tok = flat_tok[order]
    sorted_gate = gates.reshape(-1)[order]
    counts = jnp.bincount(flat_e, length=N_EXPERTS)
    starts = jnp.cumsum(counts) - counts
    padded = (counts + EXPERT_BLOCK - 1) // EXPERT_BLOCK * EXPERT_BLOCK
    pad_ends = jnp.cumsum(padded)
    pad_starts = pad_ends - padded
    dest = pad_starts[sorted_e] + (jnp.arange(n_assign, dtype=jnp.int32) - starts[sorted_e])
    n_rows = n_assign + N_EXPERTS * EXPERT_BLOCK
    n_blocks = n_rows // EXPERT_BLOCK
    row_tok = jnp.full((n_rows,), T, jnp.int32).at[dest].set(sorted_tok)
    row_gate = jnp.zeros((n_rows,), jnp.float32).at[dest].set(sorted_gate)
    block_e = jnp.minimum(
        jnp.searchsorted(pad_ends, jnp.arange(n_blocks) * EXPERT_BLOCK, side='right'),
        N_EXPERTS - 1)
    h_pad = jnp.concatenate([hf, jnp.zeros((1, D), hf.dtype)], axis=0)
    xs = h_pad[row_tok].reshape(n_blocks, EXPERT_BLOCK, D)

    def expert_block(args):
        xb, e = args
        a = xb @ w_up[e] + b_up[e]
        return clamped_swiglu(a) @ w_down[e] + b_down[e]

    ys = lax.map(expert_block, (xs, block_e)).reshape(n_rows, D)
    contrib = ys.astype(jnp.float32) * row_gate[:, None]
    out = jax.ops.segment_sum(contrib, row_tok, num_segments=T + 1)[:T]
    return out.reshape(B, S, D).astype(h.dtype)


def setup_inputs(seed: int = 0) -> dict:
    key = jax.random.key(seed)
    ks = jax.random.split(key, 16)
    f32 = jnp.float32
    L = DEPTH

    def nrm(k, shape, scale):
        return jax.random.normal(k, shape, f32) * scale

    return {
        'x': nrm(ks[0], (BATCH, SEQ, D_MODEL), 1.0),
        'g_mix': 1.0 + nrm(ks[1], (L, D_MODEL), 0.05),
        'w_in': nrm(ks[2], (L, D_MODEL, IN_WIDTH), D_MODEL ** -0.5),
        'b_forget': 2.0 + nrm(ks[3], (L, N_HEADS), 0.5),
        'w_pool': nrm(ks[4], (L, N_POOL_GROUPS, POOL_GROUP, POOL_GROUP), POOL_GROUP ** -0.5),
        'pool_scale': 1.0 + nrm(ks[5], (L, POOL_WIDTH), 0.1),
        'w_out': nrm(ks[6], (L, D_MODEL, D_MODEL), D_MODEL ** -0.5),
        'g_ffn': 1.0 + nrm(ks[7], (L, D_MODEL), 0.05),
        'w_router': nrm(ks[8], (L, D_MODEL, N_EXPERTS), D_MODEL ** -0.5),
        'b_router': nrm(ks[9], (L, N_EXPERTS), 0.01),
        'w_up': nrm(ks[10], (L, N_EXPERTS, D_MODEL, 2 * D_EXPERT), D_MODEL ** -0.5),
        'b_up': nrm(ks[11], (L, N_EXPERTS, 2 * D_EXPERT), 0.01),
        'w_down': nrm(ks[12], (L, N_EXPERTS, D_EXPERT, D_MODEL), D_EXPERT ** -0.5),
        'b_down': nrm(ks[13], (L, N_EXPERTS, D_MODEL), 0.01),
        'g_final': 1.0 + nrm(ks[14], (D_MODEL,), 0.05),
    }


def reference(x, g_mix, w_in, b_forget, w_pool, pool_scale, w_out, g_ffn,
              w_router, b_router, w_up, b_up, w_down, b_down, g_final):
    B, S, D = x.shape
    for l in range(DEPTH):
        h = rms_norm(x, g_mix[l])
        proj = h @ w_in[l]
        o = POOL_WIDTH
        u = proj[..., :o]
        q = proj[..., o:o + ATTN_WIDTH]
        k = proj[..., o + ATTN_WIDTH:o + 2 * ATTN_WIDTH]
        v = proj[..., o + 2 * ATTN_WIDTH:o + 3 * ATTN_WIDTH]
        f_logit = proj[..., o + 3 * ATTN_WIDTH:]
        to_heads = lambda t: t.reshape(B, S, N_HEADS, HEAD_DIM).transpose(0, 2, 1, 3)
        log_f = jax.nn.log_sigmoid(f_logit.astype(jnp.float32) + b_forget[l].astype(jnp.float32))
        log_f = log_f.transpose(0, 2, 1)
        y_pool = pool_mixer(u, w_pool[l], pool_scale[l])
        y_attn = forgetting_attention(to_heads(q), to_heads(k), to_heads(v), log_f)
        y_attn = y_attn.transpose(0, 2, 1, 3).reshape(B, S, ATTN_WIDTH)
        x = x + jnp.concatenate([y_pool, y_attn], axis=-1) @ w_out[l]
        h2 = rms_norm(x, g_ffn[l])
        x = x + moe_ffn(h2, w_router[l], b_router[l], w_up[l], b_up[l], w_down[l], b_down[l])
    return rms_norm(x, g_final)
```

```python
import functools

import jax
import jax.numpy as jnp
from jax import lax
from jax.experimental import pallas as pl
from jax.experimental.pallas import tpu as pltpu

D_MODEL = 1024
POOL_WIDTH = 512
POOL_WINDOWS = (2, 4, 8, 16)
POOL_GROUP = 128
MAX_WINDOW = 16
ATTN_WIDTH = 512
HEAD_DIM = 64
N_HEADS = 8
N_EXPERTS = 32
TOP_K = 4
D_EXPERT = 1024
SWIGLU_LIMIT = 7.0
SWIGLU_ALPHA = 1.702
RMS_EPS = 1e-5

LANES = 128
HEAD_SLOT = 128
N_SPLIT = 3
MASK_VALUE = -1e30

SEQ_TILE = 512
Q_TILE = 512
KV_TILE = SEQ_TILE
EXPERT_ROWS = 512
VMEM_LIMIT = 56 * 1024 * 1024

_NT = (((1,), (1,)), ((), ()))


def _dot(a, b):
    return jnp.dot(a, b, preferred_element_type=jnp.float32)


def _dot_nt(a, b):
    return lax.dot_general(a, b, _NT, preferred_element_type=jnp.float32)


def _rms_norm(x, g):
    return x * lax.rsqrt(jnp.mean(x * x, axis=-1, keepdims=True) + RMS_EPS) * g


def _split3(x):
    hi = x.astype(jnp.bfloat16)
    r1 = x - hi.astype(jnp.float32)
    mid = r1.astype(jnp.bfloat16)
    lo = (r1 - mid.astype(jnp.float32)).astype(jnp.bfloat16)
    return hi, mid, lo


def _inproj_kernel(x_ref, g_ref, wu_ref, wqt_ref, wk_ref, wvt_ref, wf_ref, bf_ref, tri_ref,
                   pqt_ref, pk_ref, oneq_ref, onek_ref, wpool_ref, pscale_ref,
                   ypool_ref, qt_ref, k_ref, vt_ref, ubuf, ccarry):
    s = pl.program_id(1)
    tm = x_ref.shape[1]

    @pl.when(s == 0)
    def _():
        ubuf[0:MAX_WINDOW, :] = jnp.zeros((MAX_WINDOW, POOL_WIDTH), jnp.float32)
        ccarry[...] = jnp.zeros_like(ccarry)

    x = x_ref[0]
    h = _rms_norm(x, g_ref[...]).astype(jnp.bfloat16)

    fl = _dot(h, wf_ref[...]) + bf_ref[...]
    logf = jnp.minimum(fl, 0.0) - jnp.log1p(jnp.exp(-jnp.abs(fl)))
    lane = lax.broadcasted_iota(jnp.int32, (tm, LANES), 1)
    logf = jnp.where(lane < N_SPLIT * N_HEADS, logf, 0.0)
    f_hi, f_mid, f_lo = _split3(logf)
    tri = tri_ref[...]
    c = _dot(tri, f_hi) + _dot(tri, f_mid) + _dot(tri, f_lo) + ccarry[...]
    ccarry[...] = c[tm - 1:tm, :]
    c_hi, c_mid, c_lo = _split3(c)
    term = lane % N_SPLIT
    c3 = jnp.where(term == 0, c_hi, jnp.where(term == 1, c_mid, c_lo))

    kp = _dot(h, wk_ref[...]) + _dot(c3, pk_ref[...]) + onek_ref[...]
    k_ref[0] = kp.astype(jnp.bfloat16)
    qt = _dot_nt(wqt_ref[...], h) + _dot_nt(pqt_ref[...], c3) + oneq_ref[...]
    qt_ref[0] = qt.astype(jnp.bfloat16)
    vt_ref[0, 0] = _dot_nt(wvt_ref[...], h).astype(jnp.bfloat16)

    u = _dot(h, wu_ref[...])
    ubuf[MAX_WINDOW:MAX_WINDOW + tm, :] = u
    pos = s * tm + lax.broadcasted_iota(jnp.int32, (tm, 1), 0)
    for gi, w in enumerate(POOL_WINDOWS):
        cols = slice(gi * POOL_GROUP, (gi + 1) * POOL_GROUP)
        ug = u[:, cols]
        acc = ug
        for j in range(1, w):
            acc = acc + ubuf[MAX_WINDOW - j:MAX_WINDOW - j + tm, cols]
        cnt = jnp.minimum(pos + 1, w).astype(jnp.float32)
        p = acc / cnt - ug
        y = _dot(p.astype(jnp.bfloat16), wpool_ref[gi]) * pscale_ref[:, cols]
        ypool_ref[0, :, cols] = y.astype(jnp.bfloat16)
    ubuf[0:MAX_WINDOW, :] = ubuf[tm:tm + MAX_WINDOW, :]


def _in_projection(x, g_mix, w_in, b_forget, w_pool, pool_scale):
    B, S, D = x.shape
    tm = SEQ_TILE
    bf16 = jnp.bfloat16
    o = POOL_WIDTH
    w_u = w_in[:, :o].astype(bf16)
    w_q = w_in[:, o:o + ATTN_WIDTH] * (HEAD_DIM ** -0.5)
    w_k = w_in[:, o + ATTN_WIDTH:o + 2 * ATTN_WIDTH]
    w_v = w_in[:, o + 2 * ATTN_WIDTH:o + 3 * ATTN_WIDTH]
    w_f = w_in[:, o + 3 * ATTN_WIDTH:]

    def head_slots(w):
        w = w.reshape(D, N_HEADS, HEAD_DIM)
        w = jnp.pad(w, ((0, 0), (0, 0), (0, HEAD_SLOT - HEAD_DIM)))
        return w.reshape(D, N_HEADS * HEAD_SLOT)

    wqt = head_slots(w_q).T.astype(bf16)
    wk = head_slots(w_k).astype(bf16)
    wvt = w_v.T.astype(bf16)
    n_gate = N_SPLIT * N_HEADS
    wf = jnp.pad(jnp.repeat(w_f, N_SPLIT, axis=1), ((0, 0), (0, LANES - n_gate))).astype(bf16)
    bfp = jnp.pad(jnp.repeat(b_forget, N_SPLIT), (0, LANES - n_gate)).reshape(1, LANES)

    src = jnp.arange(LANES)
    head, term = src // N_SPLIT, src % N_SPLIT
    dst = jnp.arange(N_HEADS * HEAD_SLOT)
    valid = (src < n_gate)[:, None]
    q_hit = valid & (dst[None, :] == (head * HEAD_SLOT + HEAD_DIM + term)[:, None])
    k_hit = valid & (dst[None, :] == (head * HEAD_SLOT + HEAD_DIM + N_SPLIT + term)[:, None])
    pqt = q_hit.T.astype(bf16)
    pk = -k_hit.astype(bf16)
    slot_pos = dst % HEAD_SLOT
    oneq = ((slot_pos >= HEAD_DIM + N_SPLIT) & (slot_pos < HEAD_DIM + 2 * N_SPLIT))
    onek = ((slot_pos >= HEAD_DIM) & (slot_pos < HEAD_DIM + N_SPLIT))
    oneq = oneq.astype(jnp.float32).reshape(-1, 1)
    onek = onek.astype(jnp.float32).reshape(1, -1)
    r = jnp.arange(tm)
    tri = (r[None, :] <= r[:, None]).astype(bf16)

    full = lambda *shape: pl.BlockSpec(shape, lambda b, s: (0,) * len(shape))
    hs = N_HEADS * HEAD_SLOT
    return pl.pallas_call(
        _inproj_kernel,
        grid=(B, S // tm),
        in_specs=[
            pl.BlockSpec((1, tm, D), lambda b, s: (b, s, 0)),
            full(1, D), full(D, o), full(hs, D), full(D, hs), full(ATTN_WIDTH, D),
            full(D, LANES), full(1, LANES), full(tm, tm), full(hs, LANES), full(LANES, hs),
            full(hs, 1), full(1, hs), full(len(POOL_WINDOWS), POOL_GROUP, POOL_GROUP),
            full(1, o),
        ],
        out_specs=[
            pl.BlockSpec((1, tm, o), lambda b, s: (b, s, 0)),
            pl.BlockSpec((1, hs, tm), lambda b, s: (b, 0, s)),
            pl.BlockSpec((1, tm, hs), lambda b, s: (b, s, 0)),
            pl.BlockSpec((1, 1, ATTN_WIDTH, tm), lambda b, s: (b, s, 0, 0)),
        ],
        out_shape=[
            jax.ShapeDtypeStruct((B, S, o), bf16),
            jax.ShapeDtypeStruct((B, hs, S), bf16),
            jax.ShapeDtypeStruct((B, S, hs), bf16),
            jax.ShapeDtypeStruct((B, S // tm, ATTN_WIDTH, tm), bf16),
        ],
        scratch_shapes=[
            pltpu.VMEM((MAX_WINDOW + tm, o), jnp.float32),
            pltpu.VMEM((1, LANES), jnp.float32),
        ],
        compiler_params=pltpu.CompilerParams(
            dimension_semantics=("arbitrary", "arbitrary"), vmem_limit_bytes=VMEM_LIMIT),
        name="in_projection",
    )(x, g_mix.reshape(1, D), w_u, wqt, wk, wvt, wf, bfp, tri, pqt, pk, oneq, onek,
      w_pool.astype(bf16), pool_scale.reshape(1, o))


def _attn_kernel(qt_ref, k_ref, vt_ref, o_ref, m_ref, l_ref, acc_ref):
    i = pl.program_id(2)
    tq = qt_ref.shape[2]
    tk = vt_ref.shape[3]
    heads = qt_ref.shape[1] // HEAD_SLOT

    m_ref[...] = jnp.full(m_ref.shape, MASK_VALUE, jnp.float32)
    l_ref[...] = jnp.zeros_like(l_ref)
    acc_ref[...] = jnp.zeros_like(acc_ref)

    def chunk(j, masked):
        row0 = pl.multiple_of(j * tk, tk)
        for hh in range(heads):
            qt = qt_ref[0, hh * HEAD_SLOT:(hh + 1) * HEAD_SLOT, :]
            k = k_ref[0, pl.ds(row0, tk), hh * HEAD_SLOT:(hh + 1) * HEAD_SLOT]
            st = _dot(k, qt)
            if masked:
                key = row0 + lax.broadcasted_iota(jnp.int32, (tk, tq), 0)
                qry = i * tq + lax.broadcasted_iota(jnp.int32, (tk, tq), 1)
                st = jnp.where(key <= qry, st, MASK_VALUE)
            m_prev = m_ref[hh]
            m_new = jnp.maximum(m_prev, jnp.max(st, axis=0, keepdims=True))
            alpha = jnp.exp(m_prev - m_new)
            p = jnp.exp(st - m_new)
            l_ref[hh] = alpha * l_ref[hh] + jnp.sum(p, axis=0, keepdims=True)
            vt = vt_ref[0, j, hh * HEAD_DIM:(hh + 1) * HEAD_DIM, :]
            acc_ref[hh] = alpha * acc_ref[hh] + _dot(vt, p.astype(jnp.bfloat16))
            m_ref[hh] = m_new

    def body(j, carry):
        chunk(j, False)
        return carry

    n_full = i * tq // tk
    lax.fori_loop(0, n_full, body, 0)
    for d in range(tq // tk):
        chunk(n_full + d, True)

    outs = [acc_ref[hh] / l_ref[hh] for hh in range(heads)]
    ot = jnp.concatenate(outs, axis=0)
    o_ref[0] = ot.T.astype(o_ref.dtype)


def _attention(qt, k, vt):
    B, hs, S = qt.shape
    tq, tk = Q_TILE, KV_TILE
    pair = 2
    n_pairs = N_HEADS // pair
    return pl.pallas_call(
        _attn_kernel,
        grid=(B, n_pairs, S // tq),
        in_specs=[
            pl.BlockSpec((1, pair * HEAD_SLOT, tq), lambda b, p, i: (b, p, i)),
            pl.BlockSpec((1, S, pair * HEAD_SLOT), lambda b, p, i: (b, 0, p)),
            pl.BlockSpec((1, S // tk, pair * HEAD_DIM, tk), lambda b, p, i: (b, 0, p, 0)),
        ],
        out_specs=pl.BlockSpec((1, tq, pair * HEAD_DIM), lambda b, p, i: (b, i, p)),
        out_shape=jax.ShapeDtypeStruct((B, S, ATTN_WIDTH), jnp.bfloat16),
        scratch_shapes=[
            pltpu.VMEM((pair, 1, tq), jnp.float32),
            pltpu.VMEM((pair, 1, tq), jnp.float32),
            pltpu.VMEM((pair, HEAD_DIM, tq), jnp.float32),
        ],
        compiler_params=pltpu.CompilerParams(
            dimension_semantics=("arbitrary", "arbitrary", "arbitrary"),
            vmem_limit_bytes=VMEM_LIMIT),
        name="forgetting_attention",
    )(qt, k, vt)


def _outproj_kernel(yp_ref, ya_ref, x_ref, wo1_ref, wo2_ref, g_ref, wr2_ref, wrhi_ref, br_ref,
                    upper_ref, x1_ref, h2_ref, idx_ref, gate_ref, rank_ref, cnt_ref, carry):
    t = pl.program_id(0)
    tm = x_ref.shape[0]

    @pl.when(t == 0)
    def _():
        carry[...] = jnp.zeros_like(carry)

    x1 = x_ref[...] + _dot(yp_ref[...], wo1_ref[...]) + _dot(ya_ref[...], wo2_ref[...])
    x1_ref[...] = x1
    h2 = _rms_norm(x1, g_ref[...])
    h2_hi = h2.astype(jnp.bfloat16)
    h2_lo = (h2 - h2_hi.astype(jnp.float32)).astype(jnp.bfloat16)
    h2_ref[...] = h2_hi

    lg2 = _dot_nt(wr2_ref[...], h2_hi)
    logits = (lg2[:N_EXPERTS] + lg2[N_EXPERTS:] + _dot_nt(wrhi_ref[...], h2_lo) + br_ref[...])

    eio = lax.broadcasted_iota(jnp.int32, (N_EXPERTS, tm), 0)
    vals, idxs = [], []
    cur = logits
    for _ in range(TOP_K):
        m = jnp.max(cur, axis=0, keepdims=True)
        ix = jnp.min(jnp.where(cur == m, eio, N_EXPERTS), axis=0, keepdims=True)
        vals.append(m)
        idxs.append(ix)
        cur = jnp.where(eio == ix, -jnp.inf, cur)
    exps = [jnp.exp(v - vals[0]) for v in vals]
    denom = exps[0] + exps[1] + exps[2] + exps[3]
    hits = [eio == ix for ix in idxs]
    cnt = jnp.zeros((N_EXPERTS, tm), jnp.float32)
    for hit in hits:
        cnt = cnt + hit.astype(jnp.float32)
    base = _dot(cnt.astype(jnp.bfloat16), upper_ref[...]) + carry[...]
    for kk in range(TOP_K):
        idx_ref[kk:kk + 1, :] = idxs[kk]
        gate_ref[kk:kk + 1, :] = exps[kk] / denom
        rank = jnp.sum(jnp.where(hits[kk], base, 0.0), axis=0, keepdims=True)
        rank_ref[kk:kk + 1, :] = rank.astype(jnp.int32)
    carry[...] = carry[...] + jnp.sum(cnt, axis=1, keepdims=True)
    cnt_ref[...] = carry[...]


def _out_projection(ypool, yattn, x, w_out, g_ffn, w_router, b_router):
    T, D = x.shape
    tm = SEQ_TILE
    bf16 = jnp.bfloat16
    wo = w_out.astype(bf16)
    wr_t = w_router.T
    wr_hi = wr_t.astype(bf16)
    wr_lo = (wr_t - wr_hi.astype(jnp.float32)).astype(bf16)
    wr2 = jnp.concatenate([wr_hi, wr_lo], axis=0)
    r = jnp.arange(tm)
    upper = (r[:, None] < r[None, :]).astype(bf16)
    full = lambda *shape: pl.BlockSpec(shape, lambda t: (0,) * len(shape))
    row = lambda w: pl.BlockSpec((tm, w), lambda t: (t, 0))
    col = pl.BlockSpec((TOP_K, tm), lambda t: (0, t))
    return pl.pallas_call(
        _outproj_kernel,
        grid=(T // tm,),
        in_specs=[row(POOL_WIDTH), row(ATTN_WIDTH), row(D), full(POOL_WIDTH, D),
                  full(ATTN_WIDTH, D), full(1, D), full(2 * N_EXPERTS, D), full(N_EXPERTS, D),
                  full(N_EXPERTS, 1), full(tm, tm)],
        out_specs=[row(D), row(D), col, col, col, full(N_EXPERTS, 1)],
        out_shape=[
            jax.ShapeDtypeStruct((T, D), jnp.float32),
            jax.ShapeDtypeStruct((T, D), bf16),
            jax.ShapeDtypeStruct((TOP_K, T), jnp.int32),
            jax.ShapeDtypeStruct((TOP_K, T), jnp.float32),
            jax.ShapeDtypeStruct((TOP_K, T), jnp.int32),
            jax.ShapeDtypeStruct((N_EXPERTS, 1), jnp.float32),
        ],
        scratch_shapes=[pltpu.VMEM((N_EXPERTS, 1), jnp.float32)],
        compiler_params=pltpu.CompilerParams(
            dimension_semantics=("arbitrary",), vmem_limit_bytes=VMEM_LIMIT),
        name="out_projection_router",
    )(ypool, yattn, x, wo[:POOL_WIDTH], wo[POOL_WIDTH:], g_ffn.reshape(1, D), wr2, wr_hi,
      b_router.reshape(N_EXPERTS, 1), upper)


def _expert_kernel(be_ref, nused_ref, x_ref, gate_ref, wup_ref, bup_ref, wdn_ref, bdn_ref, y_ref):
    i = pl.program_id(0)

    @pl.when(i < nused_ref[0])
    def _():
        x = x_ref[...]
        glu = _dot(x, wup_ref[0, :, :D_EXPERT]) + bup_ref[0, :, :D_EXPERT]
        lin = _dot(x, wup_ref[0, :, D_EXPERT:]) + bup_ref[0, :, D_EXPERT:]
        glu = jnp.minimum(glu, SWIGLU_LIMIT)
        lin = jnp.clip(lin, -SWIGLU_LIMIT, SWIGLU_LIMIT)
        act = glu * (1.0 / (1.0 + jnp.exp(-SWIGLU_ALPHA * glu))) * (lin + 1.0)
        y = _dot(act.astype(jnp.bfloat16), wdn_ref[0]) + bdn_ref[0]
        y_ref[...] = y * gate_ref[...]

    @pl.when(i >= nused_ref[0])
    def _():
        y_ref[...] = jnp.zeros_like(y_ref)


def _expert_ffn(xs, row_gate, block_e, n_used, w_up, b_up, w_down, b_down):
    n_rows, D = xs.shape
    blk = EXPERT_ROWS
    bf16 = jnp.bfloat16
    grid_spec = pltpu.PrefetchScalarGridSpec(
        num_scalar_prefetch=2,
        grid=(n_rows // blk,),
        in_specs=[
            pl.BlockSpec((blk, D), lambda i, be, nu: (i, 0)),
            pl.BlockSpec((blk, 1), lambda i, be, nu: (i, 0)),
            pl.BlockSpec((1, D, 2 * D_EXPERT), lambda i, be, nu: (be[i], 0, 0)),
            pl.BlockSpec((1, 1, 2 * D_EXPERT), lambda i, be, nu: (be[i], 0, 0)),
            pl.BlockSpec((1, D_EXPERT, D), lambda i, be, nu: (be[i], 0, 0)),
            pl.BlockSpec((1, 1, D), lambda i, be, nu: (be[i], 0, 0)),
        ],
        out_specs=pl.BlockSpec((blk, D), lambda i, be, nu: (i, 0)),
    )
    return pl.pallas_call(
        _expert_kernel,
        grid_spec=grid_spec,
        out_shape=jax.ShapeDtypeStruct((n_rows, D), jnp.float32),
        compiler_params=pltpu.CompilerParams(
            dimension_semantics=("arbitrary",), vmem_limit_bytes=VMEM_LIMIT),
        name="expert_ffn",
    )(block_e, n_used, xs, row_gate.reshape(n_rows, 1), w_up.astype(bf16),
      b_up.reshape(N_EXPERTS, 1, 2 * D_EXPERT), w_down.astype(bf16),
      b_down.reshape(N_EXPERTS, 1, D))


def _final_kernel(x1_ref, moe_ref, g_ref, o_ref):
    o_ref[...] = _rms_norm(x1_ref[...] + moe_ref[...], g_ref[...])


def _final_norm(x1, moe, g_final):
    T, D = x1.shape
    tm = SEQ_TILE
    row = pl.BlockSpec((tm, D), lambda t: (t, 0))
    return pl.pallas_call(
        _final_kernel,
        grid=(T // tm,),
        in_specs=[row, row, pl.BlockSpec((1, D), lambda t: (0, 0))],
        out_specs=row,
        out_shape=jax.ShapeDtypeStruct((T, D), jnp.float32),
        compiler_params=pltpu.CompilerParams(dimension_semantics=("arbitrary",)),
        name="final_norm",
    )(x1, moe, g_final.reshape(1, D))


def kernel(x, g_mix, w_in, b_forget, w_pool, pool_scale, w_out, g_ffn, w_router, b_router,
           w_up, b_up, w_down, b_down, g_final):
    B, S, D = x.shape
    T = B * S
    ypool, qt, k, vt = _in_projection(x, g_mix[0], w_in[0], b_forget[0], w_pool[0], pool_scale[0])
    yattn = _attention(qt, k, vt)
    x1, h2, top_idx, gates, rank, counts = _out_projection(
        ypool.reshape(T, POOL_WIDTH), yattn.reshape(T, ATTN_WIDTH), x.reshape(T, D),
        w_out[0], g_ffn[0], w_router[0], b_router[0])

    blk = EXPERT_ROWS
    counts = counts[:, 0].astype(jnp.int32)
    padded = (counts + blk - 1) // blk * blk
    pad_ends = jnp.cumsum(padded)
    pad_starts = pad_ends - padded
    dest = pad_starts[top_idx] + rank
    n_rows = T * TOP_K + N_EXPERTS * blk
    n_blocks = n_rows // blk
    tok = jnp.broadcast_to(jnp.arange(T, dtype=jnp.int32)[None, :], (TOP_K, T))
    row_tok = jnp.full((n_rows,), T, jnp.int32).at[dest.reshape(-1)].set(tok.reshape(-1))
    row_gate = jnp.zeros((n_rows,), jnp.float32).at[dest.reshape(-1)].set(gates.reshape(-1))
    block_e = jnp.minimum(
        jnp.searchsorted(pad_ends, jnp.arange(n_blocks, dtype=jnp.int32) * blk, side='right'),
        N_EXPERTS - 1).astype(jnp.int32)
    n_used = (pad_ends[-1] // blk).astype(jnp.int32).reshape(1)
    h2_pad = jnp.concatenate([h2, jnp.zeros((1, D), h2.dtype)], axis=0)
    xs = h2_pad[row_tok]

    ys = _expert_ffn(xs, row_gate, block_e, n_used, w_up[0], b_up[0], w_down[0], b_down[0])
    moe = ys[dest[0]] + ys[dest[1]] + ys[dest[2]] + ys[dest[3]]
    out = _final_norm(x1, moe, g_final)
    return out.reshape(B, S, D)
```

```python
import functools

import jax
import jax.numpy as jnp
from jax import lax
from jax.experimental import pallas as pl
from jax.experimental.pallas import tpu as pltpu

D_MODEL = 1024
POOL_WIDTH = 512
POOL_WINDOWS = (2, 4, 8, 16)
POOL_GROUP = 128
MAX_WINDOW = 16
ATTN_WIDTH = 512
HEAD_DIM = 64
N_HEADS = 8
N_EXPERTS = 32
TOP_K = 4
D_EXPERT = 1024
SWIGLU_LIMIT = 7.0
SWIGLU_ALPHA = 1.702
RMS_EPS = 1e-5

LANES = 128
HEAD_SLOT = 128
N_SPLIT = 3
MASK_VALUE = -1e30
LOG2_E = 1.4426950408889634
V_ROWS = 80

SEQ_TILE = 512
Q_TILE = 512
KV_TILE = SEQ_TILE
EXPERT_ROWS = 512
VMEM_LIMIT = 56 * 1024 * 1024

_NT = (((1,), (1,)), ((), ()))


def _dot(a, b):
    return jnp.dot(a, b, preferred_element_type=jnp.float32)


def _dot_nt(a, b):
    return lax.dot_general(a, b, _NT, preferred_element_type=jnp.float32)


def _rms_norm(x, g):
    return x * lax.rsqrt(jnp.mean(x * x, axis=-1, keepdims=True) + RMS_EPS) * g


def _split3(x):
    hi = x.astype(jnp.bfloat16)
    r1 = x - hi.astype(jnp.float32)
    mid = r1.astype(jnp.bfloat16)
    lo = (r1 - mid.astype(jnp.float32)).astype(jnp.bfloat16)
    return hi, mid, lo


def _inproj_kernel(x_ref, g_ref, wu_ref, wqt_ref, wk_ref, wvt_ref, wf_ref, bf_ref, tri_ref,
                   pqt_ref, pk_ref, oneq_ref, onek_ref, wpool_ref, pscale_ref,
                   ypool_ref, qt_ref, k_ref, vt_ref, ubuf, ccarry):
    s = pl.program_id(1)
    tm = x_ref.shape[1]

    @pl.when(s == 0)
    def _():
        ubuf[0:MAX_WINDOW, :] = jnp.zeros((MAX_WINDOW, POOL_WIDTH), jnp.float32)
        ccarry[...] = jnp.zeros_like(ccarry)

    x = x_ref[0]
    h = _rms_norm(x, g_ref[...]).astype(jnp.bfloat16)

    fl = _dot(h, wf_ref[...]) + bf_ref[...]
    logf = jnp.minimum(fl, 0.0) - jnp.log1p(jnp.exp(-jnp.abs(fl)))
    lane = lax.broadcasted_iota(jnp.int32, (tm, LANES), 1)
    logf = jnp.where(lane < N_SPLIT * N_HEADS, logf, 0.0)
    f_hi, f_mid, f_lo = _split3(logf)
    tri = tri_ref[...]
    c = _dot(tri, f_hi) + _dot(tri, f_mid) + _dot(tri, f_lo) + ccarry[...]
    ccarry[...] = c[tm - 1:tm, :]
    c_hi, c_mid, c_lo = _split3(c * LOG2_E)
    term = lane % N_SPLIT
    c3 = jnp.where(term == 0, c_hi, jnp.where(term == 1, c_mid, c_lo))

    kp = _dot(h, wk_ref[...]) + _dot(c3, pk_ref[...]) + onek_ref[...]
    k_ref[0] = kp.astype(jnp.bfloat16)
    qt = _dot_nt(wqt_ref[...], h) + _dot_nt(pqt_ref[...], c3) + oneq_ref[...]
    qt_ref[0] = qt.astype(jnp.bfloat16)
    vt = _dot_nt(wvt_ref[...], h).astype(jnp.bfloat16)
    ones = jnp.ones((V_ROWS - HEAD_DIM, tm), jnp.bfloat16)
    for hd in range(N_HEADS):
        vt_ref[0, 0, hd * V_ROWS:hd * V_ROWS + HEAD_DIM, :] = vt[hd * HEAD_DIM:(hd + 1) * HEAD_DIM]
        vt_ref[0, 0, hd * V_ROWS + HEAD_DIM:(hd + 1) * V_ROWS, :] = ones

    u = _dot(h, wu_ref[...])
    ubuf[MAX_WINDOW:MAX_WINDOW + tm, :] = u
    pos = s * tm + lax.broadcasted_iota(jnp.int32, (tm, 1), 0)
    for gi, w in enumerate(POOL_WINDOWS):
        cols = slice(gi * POOL_GROUP, (gi + 1) * POOL_GROUP)
        ug = u[:, cols]
        acc = ug
        for j in range(1, w):
            acc = acc + ubuf[MAX_WINDOW - j:MAX_WINDOW - j + tm, cols]
        cnt = jnp.minimum(pos + 1, w).astype(jnp.float32)
        p = acc / cnt - ug
        y = _dot(p.astype(jnp.bfloat16), wpool_ref[gi]) * pscale_ref[:, cols]
        ypool_ref[0, :, cols] = y.astype(jnp.bfloat16)
    ubuf[0:MAX_WINDOW, :] = ubuf[tm:tm + MAX_WINDOW, :]


def _in_projection(x, g_mix, w_in, b_forget, w_pool, pool_scale):
    B, S, D = x.shape
    tm = SEQ_TILE
    bf16 = jnp.bfloat16
    o = POOL_WIDTH
    w_u = w_in[:, :o].astype(bf16)
    w_q = w_in[:, o:o + ATTN_WIDTH] * (HEAD_DIM ** -0.5 * LOG2_E)
    w_k = w_in[:, o + ATTN_WIDTH:o + 2 * ATTN_WIDTH]
    w_v = w_in[:, o + 2 * ATTN_WIDTH:o + 3 * ATTN_WIDTH]
    w_f = w_in[:, o + 3 * ATTN_WIDTH:]

    def head_slots(w):
        w = w.reshape(D, N_HEADS, HEAD_DIM)
        w = jnp.pad(w, ((0, 0), (0, 0), (0, HEAD_SLOT - HEAD_DIM)))
        return w.reshape(D, N_HEADS * HEAD_SLOT)

    wqt = head_slots(w_q).T.astype(bf16)
    wk = head_slots(w_k).astype(bf16)
    wvt = w_v.T.astype(bf16)
    n_gate = N_SPLIT * N_HEADS
    wf = jnp.pad(jnp.repeat(w_f, N_SPLIT, axis=1), ((0, 0), (0, LANES - n_gate))).astype(bf16)
    bfp = jnp.pad(jnp.repeat(b_forget, N_SPLIT), (0, LANES - n_gate)).reshape(1, LANES)

    src = jnp.arange(LANES)
    head, term = src // N_SPLIT, src % N_SPLIT
    dst = jnp.arange(N_HEADS * HEAD_SLOT)
    valid = (src < n_gate)[:, None]
    q_hit = valid & (dst[None, :] == (head * HEAD_SLOT + HEAD_DIM + term)[:, None])
    k_hit = valid & (dst[None, :] == (head * HEAD_SLOT + HEAD_DIM + N_SPLIT + term)[:, None])
    pqt = q_hit.T.astype(bf16)
    pk = -k_hit.astype(bf16)
    slot_pos = dst % HEAD_SLOT
    oneq = ((slot_pos >= HEAD_DIM + N_SPLIT) & (slot_pos < HEAD_DIM + 2 * N_SPLIT))
    onek = ((slot_pos >= HEAD_DIM) & (slot_pos < HEAD_DIM + N_SPLIT))
    oneq = oneq.astype(jnp.float32).reshape(-1, 1)
    onek = onek.astype(jnp.float32).reshape(1, -1)
    r = jnp.arange(tm)
    tri = (r[None, :] <= r[:, None]).astype(bf16)

    full = lambda *shape: pl.BlockSpec(shape, lambda b, s: (0,) * len(shape))
    hs = N_HEADS * HEAD_SLOT
    return pl.pallas_call(
        _inproj_kernel,
        grid=(B, S // tm),
        in_specs=[
            pl.BlockSpec((1, tm, D), lambda b, s: (b, s, 0)),
            full(1, D), full(D, o), full(hs, D), full(D, hs), full(ATTN_WIDTH, D),
            full(D, LANES), full(1, LANES), full(tm, tm), full(hs, LANES), full(LANES, hs),
            full(hs, 1), full(1, hs), full(len(POOL_WINDOWS), POOL_GROUP, POOL_GROUP),
            full(1, o),
        ],
        out_specs=[
            pl.BlockSpec((1, tm, o), lambda b, s: (b, s, 0)),
            pl.BlockSpec((1, hs, tm), lambda b, s: (b, 0, s)),
            pl.BlockSpec((1, tm, hs), lambda b, s: (b, s, 0)),
            pl.BlockSpec((1, 1, N_HEADS * V_ROWS, tm), lambda b, s: (b, s, 0, 0)),
        ],
        out_shape=[
            jax.ShapeDtypeStruct((B, S, o), bf16),
            jax.ShapeDtypeStruct((B, hs, S), bf16),
            jax.ShapeDtypeStruct((B, S, hs), bf16),
            jax.ShapeDtypeStruct((B, S // tm, N_HEADS * V_ROWS, tm), bf16),
        ],
        scratch_shapes=[
            pltpu.VMEM((MAX_WINDOW + tm, o), jnp.float32),
            pltpu.VMEM((1, LANES), jnp.float32),
        ],
        compiler_params=pltpu.CompilerParams(
            dimension_semantics=("arbitrary", "arbitrary"), vmem_limit_bytes=VMEM_LIMIT),
        name="in_projection",
    )(x, g_mix.reshape(1, D), w_u, wqt, wk, wvt, wf, bfp, tri, pqt, pk, oneq, onek,
      w_pool.astype(bf16), pool_scale.reshape(1, o))


def _attn_kernel(qt_ref, k_ref, vt_ref, o_ref, st_buf, m_ref, acc_ref):
    i = pl.program_id(2)
    tq = qt_ref.shape[2]
    tk = vt_ref.shape[3]
    heads = qt_ref.shape[1] // HEAD_SLOT

    m_ref[...] = jnp.full(m_ref.shape, MASK_VALUE, jnp.float32)
    acc_ref[...] = jnp.zeros_like(acc_ref)

    def logits(j, slot, masked):
        row0 = pl.multiple_of(j * tk, tk)
        for hh in range(heads):
            qt = qt_ref[0, hh * HEAD_SLOT:(hh + 1) * HEAD_SLOT, :]
            k = k_ref[0, pl.ds(row0, tk), hh * HEAD_SLOT:(hh + 1) * HEAD_SLOT]
            st = _dot(k, qt)
            if masked:
                key = row0 + lax.broadcasted_iota(jnp.int32, (tk, tq), 0)
                qry = i * tq + lax.broadcasted_iota(jnp.int32, (tk, tq), 1)
                st = jnp.where(key <= qry, st, MASK_VALUE)
            st_buf[slot, hh] = st

    def accumulate(j, slot):
        for hh in range(heads):
            st = st_buf[slot, hh]
            m_prev = m_ref[hh]
            m_new = jnp.maximum(m_prev, jnp.max(st, axis=0, keepdims=True))
            alpha = jnp.exp2(m_prev - m_new)
            p = jnp.exp2(st - m_new)
            vt = vt_ref[0, j, hh * V_ROWS:(hh + 1) * V_ROWS, :]
            acc_ref[hh] = alpha * acc_ref[hh] + _dot(vt, p.astype(jnp.bfloat16))
            m_ref[hh] = m_new

    logits(i, 0, True)

    def before(t):
        return jnp.where(t == 0, i, t - 1)

    def body(u, carry):
        t = 2 * u
        logits(t, 1, False)
        accumulate(before(t), 0)
        logits(t + 1, 0, False)
        accumulate(t, 1)
        return carry

    lax.fori_loop(0, i // 2, body, 0)

    @pl.when(i % 2 == 1)
    def _():
        logits(i - 1, 1, False)
        accumulate(before(i - 1), 0)
        accumulate(i - 1, 1)

    @pl.when(i % 2 == 0)
    def _():
        accumulate(jnp.maximum(i - 1, 0), 0)

    outs = [acc_ref[hh, :HEAD_DIM] / acc_ref[hh, HEAD_DIM:HEAD_DIM + 1] for hh in range(heads)]
    ot = jnp.concatenate(outs, axis=0)
    o_ref[0] = ot.T.astype(o_ref.dtype)


def _attention(qt, k, vt):
    B, hs, S = qt.shape
    tq, tk = Q_TILE, KV_TILE
    assert tq == tk, "the diagonal is handled as a single masked key chunk"
    pair = 2
    n_pairs = N_HEADS // pair
    return pl.pallas_call(
        _attn_kernel,
        grid=(B, n_pairs, S // tq),
        in_specs=[
            pl.BlockSpec((1, pair * HEAD_SLOT, tq), lambda b, p, i: (b, p, i)),
            pl.BlockSpec((1, S, pair * HEAD_SLOT), lambda b, p, i: (b, 0, p)),
            pl.BlockSpec((1, S // tk, pair * V_ROWS, tk), lambda b, p, i: (b, 0, p, 0)),
        ],
        out_specs=pl.BlockSpec((1, tq, pair * HEAD_DIM), lambda b, p, i: (b, i, p)),
        out_shape=jax.ShapeDtypeStruct((B, S, ATTN_WIDTH), jnp.bfloat16),
        scratch_shapes=[
            pltpu.VMEM((2, pair, tk, tq), jnp.float32),
            pltpu.VMEM((pair, 1, tq), jnp.float32),
            pltpu.VMEM((pair, V_ROWS, tq), jnp.float32),
        ],
        compiler_params=pltpu.CompilerParams(
            dimension_semantics=("arbitrary", "arbitrary", "arbitrary"),
            vmem_limit_bytes=VMEM_LIMIT),
        name="forgetting_attention",
    )(qt, k, vt)


def _outproj_kernel(yp_ref, ya_ref, x_ref, wo1_ref, wo2_ref, g_ref, wr2_ref, wrhi_ref, br_ref,
                    upper_ref, x1_ref, h2_ref, idx_ref, gate_ref, rank_ref, cnt_ref, carry):
    t = pl.program_id(0)
    tm = x_ref.shape[0]

    @pl.when(t == 0)
    def _():
        carry[...] = jnp.zeros_like(carry)

    x1 = x_ref[...] + _dot(yp_ref[...], wo1_ref[...]) + _dot(ya_ref[...], wo2_ref[...])
    x1_ref[...] = x1
    h2 = _rms_norm(x1, g_ref[...])
    h2_hi = h2.astype(jnp.bfloat16)
    h2_lo = (h2 - h2_hi.astype(jnp.float32)).astype(jnp.bfloat16)
    h2_ref[...] = h2_hi

    lg2 = _dot_nt(wr2_ref[...], h2_hi)
    logits = (lg2[:N_EXPERTS] + lg2[N_EXPERTS:] + _dot_nt(wrhi_ref[...], h2_lo) + br_ref[...])

    eio = lax.broadcasted_iota(jnp.int32, (N_EXPERTS, tm), 0)
    vals, idxs = [], []
    cur = logits
    for _ in range(TOP_K):
        m = jnp.max(cur, axis=0, keepdims=True)
        ix = jnp.min(jnp.where(cur == m, eio, N_EXPERTS), axis=0, keepdims=True)
        vals.append(m)
        idxs.append(ix)
        cur = jnp.where(eio == ix, -jnp.inf, cur)
    exps = [jnp.exp(v - vals[0]) for v in vals]
    denom = exps[0] + exps[1] + exps[2] + exps[3]
    hits = [eio == ix for ix in idxs]
    cnt = jnp.zeros((N_EXPERTS, tm), jnp.float32)
    for hit in hits:
        cnt = cnt + hit.astype(jnp.float32)
    base = _dot(cnt.astype(jnp.bfloat16), upper_ref[...]) + carry[...]
    for kk in range(TOP_K):
        idx_ref[kk:kk + 1, :] = idxs[kk]
        gate_ref[kk:kk + 1, :] = exps[kk] / denom
        rank = jnp.sum(jnp.where(hits[kk], base, 0.0), axis=0, keepdims=True)
        rank_ref[kk:kk + 1, :] = rank.astype(jnp.int32)
    carry[...] = carry[...] + jnp.sum(cnt, axis=1, keepdims=True)
    cnt_ref[...] = carry[...]


def _out_projection(ypool, yattn, x, w_out, g_ffn, w_router, b_router):
    T, D = x.shape
    tm = SEQ_TILE
    bf16 = jnp.bfloat16
    wo = w_out.astype(bf16)
    wr_t = w_router.T
    wr_hi = wr_t.astype(bf16)
    wr_lo = (wr_t - wr_hi.astype(jnp.float32)).astype(bf16)
    wr2 = jnp.concatenate([wr_hi, wr_lo], axis=0)
    r = jnp.arange(tm)
    upper = (r[:, None] < r[None, :]).astype(bf16)
    full = lambda *shape: pl.BlockSpec(shape, lambda t: (0,) * len(shape))
    row = lambda w: pl.BlockSpec((tm, w), lambda t: (t, 0))
    col = pl.BlockSpec((TOP_K, tm), lambda t: (0, t))
    return pl.pallas_call(
        _outproj_kernel,
        grid=(T // tm,),
        in_specs=[row(POOL_WIDTH), row(ATTN_WIDTH), row(D), full(POOL_WIDTH, D),
                  full(ATTN_WIDTH, D), full(1, D), full(2 * N_EXPERTS, D), full(N_EXPERTS, D),
                  full(N_EXPERTS, 1), full(tm, tm)],
        out_specs=[row(D), row(D), col, col, col, full(N_EXPERTS, 1)],
        out_shape=[
            jax.ShapeDtypeStruct((T, D), jnp.float32),
            jax.ShapeDtypeStruct((T, D), bf16),
            jax.ShapeDtypeStruct((TOP_K, T), jnp.int32),
            jax.ShapeDtypeStruct((TOP_K, T), jnp.float32),
            jax.ShapeDtypeStruct((TOP_K, T), jnp.int32),
            jax.ShapeDtypeStruct((N_EXPERTS, 1), jnp.float32),
        ],
        scratch_shapes=[pltpu.VMEM((N_EXPERTS, 1), jnp.float32)],
        compiler_params=pltpu.CompilerParams(
            dimension_semantics=("arbitrary",), vmem_limit_bytes=VMEM_LIMIT),
        name="out_projection_router",
    )(ypool, yattn, x, wo[:POOL_WIDTH], wo[POOL_WIDTH:], g_ffn.reshape(1, D), wr2, wr_hi,
      b_router.reshape(N_EXPERTS, 1), upper)


def _expert_kernel(be_ref, nused_ref, x_ref, gate_ref, wup_ref, bup_ref, wdn_ref, bdn_ref, y_ref):
    i = pl.program_id(0)

    @pl.when(i < nused_ref[0])
    def _():
        x = x_ref[...]
        glu = _dot(x, wup_ref[0, :, :D_EXPERT]) + bup_ref[0, :, :D_EXPERT]
        lin = _dot(x, wup_ref[0, :, D_EXPERT:]) + bup_ref[0, :, D_EXPERT:]
        glu = jnp.minimum(glu, SWIGLU_LIMIT)
        lin = jnp.clip(lin, -SWIGLU_LIMIT, SWIGLU_LIMIT)
        act = glu * (1.0 / (1.0 + jnp.exp(-SWIGLU_ALPHA * glu))) * (lin + 1.0)
        y = _dot(act.astype(jnp.bfloat16), wdn_ref[0]) + bdn_ref[0]
        y_ref[...] = y * gate_ref[...]

    @pl.when(i >= nused_ref[0])
    def _():
        y_ref[...] = jnp.zeros_like(y_ref)


def _expert_ffn(xs, row_gate, block_e, n_used, w_up, b_up, w_down, b_down):
    n_rows, D = xs.shape
    blk = EXPERT_ROWS
    bf16 = jnp.bfloat16
    grid_spec = pltpu.PrefetchScalarGridSpec(
        num_scalar_prefetch=2,
        grid=(n_rows // blk,),
        in_specs=[
            pl.BlockSpec((blk, D), lambda i, be, nu: (i, 0)),
            pl.BlockSpec((blk, 1), lambda i, be, nu: (i, 0)),
            pl.BlockSpec((1, D, 2 * D_EXPERT), lambda i, be, nu: (be[i], 0, 0)),
            pl.BlockSpec((1, 1, 2 * D_EXPERT), lambda i, be, nu: (be[i], 0, 0)),
            pl.BlockSpec((1, D_EXPERT, D), lambda i, be, nu: (be[i], 0, 0)),
            pl.BlockSpec((1, 1, D), lambda i, be, nu: (be[i], 0, 0)),
        ],
        out_specs=pl.BlockSpec((blk, D), lambda i, be, nu: (i, 0)),
    )
    return pl.pallas_call(
        _expert_kernel,
        grid_spec=grid_spec,
        out_shape=jax.ShapeDtypeStruct((n_rows, D), jnp.float32),
        compiler_params=pltpu.CompilerParams(
            dimension_semantics=("arbitrary",), vmem_limit_bytes=VMEM_LIMIT),
        name="expert_ffn",
    )(block_e, n_used, xs, row_gate.reshape(n_rows, 1), w_up.astype(bf16),
      b_up.reshape(N_EXPERTS, 1, 2 * D_EXPERT), w_down.astype(bf16),
      b_down.reshape(N_EXPERTS, 1, D))


def _final_kernel(x1_ref, moe_ref, g_ref, o_ref):
    o_ref[...] = _rms_norm(x1_ref[...] + moe_ref[...], g_ref[...])


def _final_norm(x1, moe, g_final):
    T, D = x1.shape
    tm = SEQ_TILE
    row = pl.BlockSpec((tm, D), lambda t: (t, 0))
    return pl.pallas_call(
        _final_kernel,
        grid=(T // tm,),
        in_specs=[row, row, pl.BlockSpec((1, D), lambda t: (0, 0))],
        out_specs=row,
        out_shape=jax.ShapeDtypeStruct((T, D), jnp.float32),
        compiler_params=pltpu.CompilerParams(dimension_semantics=("arbitrary",)),
        name="final_norm",
    )(x1, moe, g_final.reshape(1, D))


def kernel(x, g_mix, w_in, b_forget, w_pool, pool_scale, w_out, g_ffn, w_router, b_router,
           w_up, b_up, w_down, b_down, g_final):
    B, S, D = x.shape
    T = B * S
    ypool, qt, k, vt = _in_projection(x, g_mix[0], w_in[0], b_forget[0], w_pool[0], pool_scale[0])
    yattn = _attention(qt, k, vt)
    x1, h2, top_idx, gates, rank, counts = _out_projection(
        ypool.reshape(T, POOL_WIDTH), yattn.reshape(T, ATTN_WIDTH), x.reshape(T, D),
        w_out[0], g_ffn[0], w_router[0], b_router[0])

    blk = EXPERT_ROWS
    counts = counts[:, 0].astype(jnp.int32)
    padded = (counts + blk - 1) // blk * blk
    pad_ends = jnp.cumsum(padded)
    pad_starts = pad_ends - padded
    dest = pad_starts[top_idx] + rank
    n_rows = T * TOP_K + N_EXPERTS * blk
    n_blocks = n_rows // blk
    tok = jnp.broadcast_to(jnp.arange(T, dtype=jnp.int32)[None, :], (TOP_K, T))
    row_tok = jnp.full((n_rows,), T, jnp.int32).at[dest.reshape(-1)].set(tok.reshape(-1))
    row_gate = jnp.zeros((n_rows,), jnp.float32).at[dest.reshape(-1)].set(gates.reshape(-1))
    block_e = jnp.minimum(
        jnp.searchsorted(pad_ends, jnp.arange(n_blocks, dtype=jnp.int32) * blk, side='right'),
        N_EXPERTS - 1).astype(jnp.int32)
    n_used = (pad_ends[-1] // blk).astype(jnp.int32).reshape(1)
    h2_pad = jnp.concatenate([h2, jnp.zeros((1, D), h2.dtype)], axis=0)
    xs = h2_pad[row_tok]

    ys = _expert_ffn(xs, row_gate, block_e, n_used, w_up[0], b_up[0], w_down[0], b_down[0])
    moe = ys[dest[0]] + ys[dest[1]] + ys[dest[2]] + ys[dest[3]]
    out = _final_norm(x1, moe, g_final)
    return out.reshape(B, S, D)
```

```python
import functools

import jax
import jax.numpy as jnp
from jax import lax
from jax.experimental import pallas as pl
from jax.experimental.pallas import tpu as pltpu
from jax.experimental.pallas import tpu_sc as plsc

D_MODEL = 1024
POOL_WIDTH = 512
POOL_WINDOWS = (2, 4, 8, 16)
POOL_GROUP = 128
MAX_WINDOW = 16
ATTN_WIDTH = 512
HEAD_DIM = 64
N_HEADS = 8
N_EXPERTS = 32
TOP_K = 4
D_EXPERT = 1024
SWIGLU_LIMIT = 7.0
SWIGLU_ALPHA = 1.702
RMS_EPS = 1e-5

LANES = 128
HEAD_SLOT = 128
N_SPLIT = 3
MASK_VALUE = -1e30
LOG2_E = 1.4426950408889634
V_ROWS = 80

SEQ_TILE = 512
Q_TILE = 512
KV_TILE = SEQ_TILE
EXPERT_ROWS = 512
SC_CORES = 2
SC_SUBCORES = 16
SC_WORKERS = SC_CORES * SC_SUBCORES
SC_WINDOW = 128
VMEM_LIMIT = 56 * 1024 * 1024

_NT = (((1,), (1,)), ((), ()))


def _dot(a, b):
    return jnp.dot(a, b, preferred_element_type=jnp.float32)


def _dot_nt(a, b):
    return lax.dot_general(a, b, _NT, preferred_element_type=jnp.float32)


def _rms_norm(x, g):
    return x * lax.rsqrt(jnp.mean(x * x, axis=-1, keepdims=True) + RMS_EPS) * g


def _pack_bf16_pairs(x):
    n = x.shape[1] // 2
    bits = lax.bitcast_convert_type(x.astype(jnp.bfloat16).astype(jnp.float32), jnp.uint32)
    return bits[:, :n] | (bits[:, n:] >> 16)


def _unpack_bf16_pairs(p):
    hi = lax.bitcast_convert_type(p & jnp.uint32(0xFFFF0000), jnp.float32)
    lo = lax.bitcast_convert_type(p << 16, jnp.float32)
    return hi, lo


def _split3(x):
    hi = x.astype(jnp.bfloat16)
    r1 = x - hi.astype(jnp.float32)
    mid = r1.astype(jnp.bfloat16)
    lo = (r1 - mid.astype(jnp.float32)).astype(jnp.bfloat16)
    return hi, mid, lo


def _inproj_kernel(x_ref, g_ref, wu_ref, wqt_ref, wk_ref, wvt_ref, wf_ref, bf_ref, tri_ref,
                   pqt_ref, pk_ref, oneq_ref, onek_ref, wpool_ref, pscale_ref,
                   ypool_ref, qt_ref, k_ref, vt_ref, ubuf, ccarry):
    s = pl.program_id(1)
    tm = x_ref.shape[1]

    @pl.when(s == 0)
    def _():
        ubuf[0:MAX_WINDOW, :] = jnp.zeros((MAX_WINDOW, POOL_WIDTH), jnp.float32)
        ccarry[...] = jnp.zeros_like(ccarry)

    x = x_ref[0]
    h = _rms_norm(x, g_ref[...]).astype(jnp.bfloat16)

    fl = _dot(h, wf_ref[...]) + bf_ref[...]
    logf = jnp.minimum(fl, 0.0) - jnp.log1p(jnp.exp(-jnp.abs(fl)))
    lane = lax.broadcasted_iota(jnp.int32, (tm, LANES), 1)
    logf = jnp.where(lane < N_SPLIT * N_HEADS, logf, 0.0)
    f_hi, f_mid, f_lo = _split3(logf)
    tri = tri_ref[...]
    c = _dot(tri, f_hi) + _dot(tri, f_mid) + _dot(tri, f_lo) + ccarry[...]
    ccarry[...] = c[tm - 1:tm, :]
    c_hi, c_mid, c_lo = _split3(c * LOG2_E)
    term = lane % N_SPLIT
    c3 = jnp.where(term == 0, c_hi, jnp.where(term == 1, c_mid, c_lo))

    kp = _dot(h, wk_ref[...]) + _dot(c3, pk_ref[...]) + onek_ref[...]
    k_ref[0] = kp.astype(jnp.bfloat16)
    qt = _dot_nt(wqt_ref[...], h) + _dot_nt(pqt_ref[...], c3) + oneq_ref[...]
    qt_ref[0] = qt.astype(jnp.bfloat16)
    vt = _dot_nt(wvt_ref[...], h).astype(jnp.bfloat16)
    ones = jnp.ones((V_ROWS - HEAD_DIM, tm), jnp.bfloat16)
    for hd in range(N_HEADS):
        vt_ref[0, 0, hd * V_ROWS:hd * V_ROWS + HEAD_DIM, :] = vt[hd * HEAD_DIM:(hd + 1) * HEAD_DIM]
        vt_ref[0, 0, hd * V_ROWS + HEAD_DIM:(hd + 1) * V_ROWS, :] = ones

    u = _dot(h, wu_ref[...])
    ubuf[MAX_WINDOW:MAX_WINDOW + tm, :] = u
    pos = s * tm + lax.broadcasted_iota(jnp.int32, (tm, 1), 0)
    for gi, w in enumerate(POOL_WINDOWS):
        cols = slice(gi * POOL_GROUP, (gi + 1) * POOL_GROUP)
        ug = u[:, cols]
        acc = ug
        for j in range(1, w):
            acc = acc + ubuf[MAX_WINDOW - j:MAX_WINDOW - j + tm, cols]
        cnt = jnp.minimum(pos + 1, w).astype(jnp.float32)
        p = acc / cnt - ug
        y = _dot(p.astype(jnp.bfloat16), wpool_ref[gi]) * pscale_ref[:, cols]
        ypool_ref[0, :, cols] = y.astype(jnp.bfloat16)
    ubuf[0:MAX_WINDOW, :] = ubuf[tm:tm + MAX_WINDOW, :]


def _in_projection(x, g_mix, w_in, b_forget, w_pool, pool_scale):
    B, S, D = x.shape
    tm = SEQ_TILE
    bf16 = jnp.bfloat16
    o = POOL_WIDTH
    w_u = w_in[:, :o].astype(bf16)
    w_q = w_in[:, o:o + ATTN_WIDTH] * (HEAD_DIM ** -0.5 * LOG2_E)
    w_k = w_in[:, o + ATTN_WIDTH:o + 2 * ATTN_WIDTH]
    w_v = w_in[:, o + 2 * ATTN_WIDTH:o + 3 * ATTN_WIDTH]
    w_f = w_in[:, o + 3 * ATTN_WIDTH:]

    def head_slots(w):
        w = w.reshape(D, N_HEADS, HEAD_DIM)
        w = jnp.pad(w, ((0, 0), (0, 0), (0, HEAD_SLOT - HEAD_DIM)))
        return w.reshape(D, N_HEADS * HEAD_SLOT)

    wqt = head_slots(w_q).T.astype(bf16)
    wk = head_slots(w_k).astype(bf16)
    wvt = w_v.T.astype(bf16)
    n_gate = N_SPLIT * N_HEADS
    wf = jnp.pad(jnp.repeat(w_f, N_SPLIT, axis=1), ((0, 0), (0, LANES - n_gate))).astype(bf16)
    bfp = jnp.pad(jnp.repeat(b_forget, N_SPLIT), (0, LANES - n_gate)).reshape(1, LANES)

    src = jnp.arange(LANES)
    head, term = src // N_SPLIT, src % N_SPLIT
    dst = jnp.arange(N_HEADS * HEAD_SLOT)
    valid = (src < n_gate)[:, None]
    q_hit = valid & (dst[None, :] == (head * HEAD_SLOT + HEAD_DIM + term)[:, None])
    k_hit = valid & (dst[None, :] == (head * HEAD_SLOT + HEAD_DIM + N_SPLIT + term)[:, None])
    pqt = q_hit.T.astype(bf16)
    pk = -k_hit.astype(bf16)
    slot_pos = dst % HEAD_SLOT
    oneq = ((slot_pos >= HEAD_DIM + N_SPLIT) & (slot_pos < HEAD_DIM + 2 * N_SPLIT))
    onek = ((slot_pos >= HEAD_DIM) & (slot_pos < HEAD_DIM + N_SPLIT))
    oneq = oneq.astype(jnp.float32).reshape(-1, 1)
    onek = onek.astype(jnp.float32).reshape(1, -1)
    r = jnp.arange(tm)
    tri = (r[None, :] <= r[:, None]).astype(bf16)

    full = lambda *shape: pl.BlockSpec(shape, lambda b, s: (0,) * len(shape))
    hs = N_HEADS * HEAD_SLOT
    return pl.pallas_call(
        _inproj_kernel,
        grid=(B, S // tm),
        in_specs=[
            pl.BlockSpec((1, tm, D), lambda b, s: (b, s, 0)),
            full(1, D), full(D, o), full(hs, D), full(D, hs), full(ATTN_WIDTH, D),
            full(D, LANES), full(1, LANES), full(tm, tm), full(hs, LANES), full(LANES, hs),
            full(hs, 1), full(1, hs), full(len(POOL_WINDOWS), POOL_GROUP, POOL_GROUP),
            full(1, o),
        ],
        out_specs=[
            pl.BlockSpec((1, tm, o), lambda b, s: (b, s, 0)),
            pl.BlockSpec((1, hs, tm), lambda b, s: (b, 0, s)),
            pl.BlockSpec((1, tm, hs), lambda b, s: (b, s, 0)),
            pl.BlockSpec((1, 1, N_HEADS * V_ROWS, tm), lambda b, s: (b, s, 0, 0)),
        ],
        out_shape=[
            jax.ShapeDtypeStruct((B, S, o), bf16),
            jax.ShapeDtypeStruct((B, hs, S), bf16),
            jax.ShapeDtypeStruct((B, S, hs), bf16),
            jax.ShapeDtypeStruct((B, S // tm, N_HEADS * V_ROWS, tm), bf16),
        ],
        scratch_shapes=[
            pltpu.VMEM((MAX_WINDOW + tm, o), jnp.float32),
            pltpu.VMEM((1, LANES), jnp.float32),
        ],
        compiler_params=pltpu.CompilerParams(
            dimension_semantics=("arbitrary", "arbitrary"), vmem_limit_bytes=VMEM_LIMIT),
        name="in_projection",
    )(x, g_mix.reshape(1, D), w_u, wqt, wk, wvt, wf, bfp, tri, pqt, pk, oneq, onek,
      w_pool.astype(bf16), pool_scale.reshape(1, o))


def _attn_kernel(qt_ref, k_ref, vt_ref, o_ref, st_buf, m_ref, acc_ref):
    i = pl.program_id(2)
    tq = qt_ref.shape[2]
    tk = vt_ref.shape[3]
    heads = qt_ref.shape[1] // HEAD_SLOT

    m_ref[...] = jnp.full(m_ref.shape, MASK_VALUE, jnp.float32)
    acc_ref[...] = jnp.zeros_like(acc_ref)

    def logits(j, slot, masked):
        row0 = pl.multiple_of(j * tk, tk)
        for hh in range(heads):
            qt = qt_ref[0, hh * HEAD_SLOT:(hh + 1) * HEAD_SLOT, :]
            k = k_ref[0, pl.ds(row0, tk), hh * HEAD_SLOT:(hh + 1) * HEAD_SLOT]
            st = _dot(k, qt)
            if masked:
                key = row0 + lax.broadcasted_iota(jnp.int32, (tk, tq), 0)
                qry = i * tq + lax.broadcasted_iota(jnp.int32, (tk, tq), 1)
                st = jnp.where(key <= qry, st, MASK_VALUE)
            st_buf[slot, hh] = st

    def accumulate(j, slot):
        for hh in range(heads):
            st = st_buf[slot, hh]
            m_prev = m_ref[hh]
            m_new = jnp.maximum(m_prev, jnp.max(st, axis=0, keepdims=True))
            alpha = jnp.exp2(m_prev - m_new)
            p = jnp.exp2(st - m_new)
            vt = vt_ref[0, j, hh * V_ROWS:(hh + 1) * V_ROWS, :]
            acc_ref[hh] = alpha * acc_ref[hh] + _dot(vt, p.astype(jnp.bfloat16))
            m_ref[hh] = m_new

    logits(i, 0, True)

    def before(t):
        return jnp.where(t == 0, i, t - 1)

    def body(u, carry):
        t = 2 * u
        logits(t, 1, False)
        accumulate(before(t), 0)
        logits(t + 1, 0, False)
        accumulate(t, 1)
        return carry

    lax.fori_loop(0, i // 2, body, 0)

    @pl.when(i % 2 == 1)
    def _():
        logits(i - 1, 1, False)
        accumulate(before(i - 1), 0)
        accumulate(i - 1, 1)

    @pl.when(i % 2 == 0)
    def _():
        accumulate(jnp.maximum(i - 1, 0), 0)

    outs = [acc_ref[hh, :HEAD_DIM] / acc_ref[hh, HEAD_DIM:HEAD_DIM + 1] for hh in range(heads)]
    ot = jnp.concatenate(outs, axis=0)
    o_ref[0] = ot.T.astype(o_ref.dtype)


def _attention(qt, k, vt):
    B, hs, S = qt.shape
    tq, tk = Q_TILE, KV_TILE
    assert tq == tk, "the diagonal is handled as a single masked key chunk"
    pair = 2
    n_pairs = N_HEADS // pair
    return pl.pallas_call(
        _attn_kernel,
        grid=(B, n_pairs, S // tq),
        in_specs=[
            pl.BlockSpec((1, pair * HEAD_SLOT, tq), lambda b, p, i: (b, p, i)),
            pl.BlockSpec((1, S, pair * HEAD_SLOT), lambda b, p, i: (b, 0, p)),
            pl.BlockSpec((1, S // tk, pair * V_ROWS, tk), lambda b, p, i: (b, 0, p, 0)),
        ],
        out_specs=pl.BlockSpec((1, tq, pair * HEAD_DIM), lambda b, p, i: (b, i, p)),
        out_shape=jax.ShapeDtypeStruct((B, S, ATTN_WIDTH), jnp.bfloat16),
        scratch_shapes=[
            pltpu.VMEM((2, pair, tk, tq), jnp.float32),
            pltpu.VMEM((pair, 1, tq), jnp.float32),
            pltpu.VMEM((pair, V_ROWS, tq), jnp.float32),
        ],
        compiler_params=pltpu.CompilerParams(
            dimension_semantics=("arbitrary", "arbitrary", "arbitrary"),
            vmem_limit_bytes=VMEM_LIMIT),
        name="forgetting_attention",
    )(qt, k, vt)


def _outproj_kernel(yp_ref, ya_ref, x_ref, wo1_ref, wo2_ref, g_ref, wr2_ref, wrhi_ref, br_ref,
                    upper_ref, x1_ref, h2_ref, idx_ref, gate_ref, rank_ref, cnt_ref, carry):
    t = pl.program_id(0)
    tm = x_ref.shape[0]

    @pl.when(t == 0)
    def _():
        carry[...] = jnp.zeros_like(carry)

    x1 = x_ref[...] + _dot(yp_ref[...], wo1_ref[...]) + _dot(ya_ref[...], wo2_ref[...])
    x1_ref[...] = x1
    h2 = _rms_norm(x1, g_ref[...])
    h2_hi = h2.astype(jnp.bfloat16)
    h2_lo = (h2 - h2_hi.astype(jnp.float32)).astype(jnp.bfloat16)
    h2_ref[...] = _pack_bf16_pairs(h2)

    lg2 = _dot_nt(wr2_ref[...], h2_hi)
    logits = (lg2[:N_EXPERTS] + lg2[N_EXPERTS:] + _dot_nt(wrhi_ref[...], h2_lo) + br_ref[...])

    eio = lax.broadcasted_iota(jnp.int32, (N_EXPERTS, tm), 0)
    vals, idxs = [], []
    cur = logits
    for _ in range(TOP_K):
        m = jnp.max(cur, axis=0, keepdims=True)
        ix = jnp.min(jnp.where(cur == m, eio, N_EXPERTS), axis=0, keepdims=True)
        vals.append(m)
        idxs.append(ix)
        cur = jnp.where(eio == ix, -jnp.inf, cur)
    exps = [jnp.exp(v - vals[0]) for v in vals]
    denom = exps[0] + exps[1] + exps[2] + exps[3]
    hits = [eio == ix for ix in idxs]
    cnt = jnp.zeros((N_EXPERTS, tm), jnp.float32)
    for hit in hits:
        cnt = cnt + hit.astype(jnp.float32)
    base = _dot(cnt.astype(jnp.bfloat16), upper_ref[...]) + carry[...]
    for kk in range(TOP_K):
        idx_ref[kk:kk + 1, :] = idxs[kk]
        gate_ref[kk:kk + 1, :] = exps[kk] / denom
        rank = jnp.sum(jnp.where(hits[kk], base, 0.0), axis=0, keepdims=True)
        rank_ref[kk:kk + 1, :] = rank.astype(jnp.int32)
    carry[...] = carry[...] + jnp.sum(cnt, axis=1, keepdims=True)
    cnt_ref[...] = carry[...]


def _out_projection(ypool, yattn, x, w_out, g_ffn, w_router, b_router):
    T, D = x.shape
    tm = SEQ_TILE
    bf16 = jnp.bfloat16
    wo = w_out.astype(bf16)
    wr_t = w_router.T
    wr_hi = wr_t.astype(bf16)
    wr_lo = (wr_t - wr_hi.astype(jnp.float32)).astype(bf16)
    wr2 = jnp.concatenate([wr_hi, wr_lo], axis=0)
    r = jnp.arange(tm)
    upper = (r[:, None] < r[None, :]).astype(bf16)
    full = lambda *shape: pl.BlockSpec(shape, lambda t: (0,) * len(shape))
    row = lambda w: pl.BlockSpec((tm, w), lambda t: (t, 0))
    col = pl.BlockSpec((TOP_K, tm), lambda t: (0, t))
    return pl.pallas_call(
        _outproj_kernel,
        grid=(T // tm,),
        in_specs=[row(POOL_WIDTH), row(ATTN_WIDTH), row(D), full(POOL_WIDTH, D),
                  full(ATTN_WIDTH, D), full(1, D), full(2 * N_EXPERTS, D), full(N_EXPERTS, D),
                  full(N_EXPERTS, 1), full(tm, tm)],
        out_specs=[row(D), row(D // 2), col, col, col, full(N_EXPERTS, 1)],
        out_shape=[
            jax.ShapeDtypeStruct((T, D), jnp.float32),
            jax.ShapeDtypeStruct((T, D // 2), jnp.uint32),
            jax.ShapeDtypeStruct((TOP_K, T), jnp.int32),
            jax.ShapeDtypeStruct((TOP_K, T), jnp.float32),
            jax.ShapeDtypeStruct((TOP_K, T), jnp.int32),
            jax.ShapeDtypeStruct((N_EXPERTS, 1), jnp.float32),
        ],
        scratch_shapes=[pltpu.VMEM((N_EXPERTS, 1), jnp.float32)],
        compiler_params=pltpu.CompilerParams(
            dimension_semantics=("arbitrary",), vmem_limit_bytes=VMEM_LIMIT),
        name="out_projection_router",
    )(ypool, yattn, x, wo[:POOL_WIDTH], wo[POOL_WIDTH:], g_ffn.reshape(1, D), wr2, wr_hi,
      b_router.reshape(N_EXPERTS, 1), upper)


def _sc_mesh():
    return plsc.VectorSubcoreMesh(core_axis_name="c", subcore_axis_name="s")


def _dispatch_rows(h2, dest, n_rows):
    T, D = h2.shape
    per_worker = T // SC_WORKERS
    steps = per_worker // SC_WINDOW

    @functools.partial(
        pl.kernel, mesh=_sc_mesh(),
        out_type=jax.ShapeDtypeStruct((n_rows, D), h2.dtype),
        scratch_types=[pltpu.VMEM((TOP_K, SC_WINDOW), jnp.int32),
                       pltpu.VMEM((SC_WINDOW, D), h2.dtype),
                       pltpu.SemaphoreType.DMA],
    )
    def scatter_kernel(h2_hbm, dest_hbm, xs_hbm, idx_v, rows_v, sem):
        wid = lax.axis_index("s") * SC_CORES + lax.axis_index("c")

        @pl.loop(0, steps)
        def _(step):
            base = wid * per_worker + step * SC_WINDOW
            pltpu.sync_copy(h2_hbm.at[pl.ds(base, SC_WINDOW)], rows_v)
            pltpu.sync_copy(dest_hbm.at[:, pl.ds(base, SC_WINDOW)], idx_v)
            copies = [pltpu.async_copy(rows_v, xs_hbm.at[idx_v.at[kk]], sem)
                      for kk in range(TOP_K)]
            for c in copies:
                c.wait()

    return scatter_kernel(h2, dest)


def _gather_rows(table, idx):
    N = idx.shape[0]
    D = table.shape[1]
    per_worker = N // SC_WORKERS
    steps = per_worker // SC_WINDOW

    @functools.partial(
        pl.kernel, mesh=_sc_mesh(),
        out_type=jax.ShapeDtypeStruct((N, D), table.dtype),
        scratch_types=[pltpu.VMEM((SC_WINDOW,), jnp.int32),
                       pltpu.VMEM((SC_WINDOW, D), table.dtype),
                       pltpu.SemaphoreType.DMA],
    )
    def gather_kernel(table_hbm, idx_hbm, out_hbm, idx_v, rows_v, sem):
        wid = lax.axis_index("s") * SC_CORES + lax.axis_index("c")

        @pl.loop(0, steps)
        def _(step):
            base = wid * per_worker + step * SC_WINDOW
            pltpu.sync_copy(idx_hbm.at[pl.ds(base, SC_WINDOW)], idx_v)
            pltpu.async_copy(table_hbm.at[idx_v], rows_v, sem).wait()
            pltpu.sync_copy(rows_v, out_hbm.at[pl.ds(base, SC_WINDOW)])

    return gather_kernel(table, idx)


def _expert_kernel(be_ref, nvalid_ref, x_ref, wup_ref, bup_ref, wdn_ref, bdn_ref, y_ref,
                   wup_bf, wdn_bf):
    i = pl.program_id(0)
    blk = x_ref.shape[0]
    changed = jnp.logical_or(i == 0, be_ref[i] != be_ref[jnp.maximum(i - 1, 0)])

    @pl.when(changed)
    def _():
        wup_bf[...] = wup_ref[0].astype(jnp.bfloat16)
        wdn_bf[...] = wdn_ref[0].astype(jnp.bfloat16)

    nvalid = nvalid_ref[i]

    @pl.when(nvalid > 0)
    def _():
        row = lax.broadcasted_iota(jnp.int32, (blk, 1), 0)
        x_hi, x_lo = _unpack_bf16_pairs(jnp.where(row < nvalid, x_ref[...], jnp.uint32(0)))
        x = jnp.concatenate([x_hi.astype(jnp.bfloat16), x_lo.astype(jnp.bfloat16)], axis=1)
        glu = _dot(x, wup_bf[:, :D_EXPERT]) + bup_ref[0, :, :D_EXPERT]
        lin = _dot(x, wup_bf[:, D_EXPERT:]) + bup_ref[0, :, D_EXPERT:]
        glu = jnp.minimum(glu, SWIGLU_LIMIT)
        lin = jnp.clip(lin, -SWIGLU_LIMIT, SWIGLU_LIMIT)
        act = glu * (1.0 / (1.0 + jnp.exp(-SWIGLU_ALPHA * glu))) * (lin + 1.0)
        y = _dot(act.astype(jnp.bfloat16), wdn_bf[...]) + bdn_ref[0]
        y_ref[...] = _pack_bf16_pairs(y)

    @pl.when(nvalid <= 0)
    def _():
        y_ref[...] = jnp.zeros_like(y_ref)


def _expert_ffn(xs, block_e, block_valid, w_up, b_up, w_down, b_down):
    n_rows = xs.shape[0]
    D = D_MODEL
    blk = EXPERT_ROWS
    grid_spec = pltpu.PrefetchScalarGridSpec(
        num_scalar_prefetch=2,
        grid=(n_rows // blk,),
        in_specs=[
            pl.BlockSpec((blk, D // 2), lambda i, be, nv: (i, 0)),
            pl.BlockSpec((1, D, 2 * D_EXPERT), lambda i, be, nv: (be[i], 0, 0)),
            pl.BlockSpec((1, 1, 2 * D_EXPERT), lambda i, be, nv: (be[i], 0, 0)),
            pl.BlockSpec((1, D_EXPERT, D), lambda i, be, nv: (be[i], 0, 0)),
            pl.BlockSpec((1, 1, D), lambda i, be, nv: (be[i], 0, 0)),
        ],
        out_specs=pl.BlockSpec((blk, D // 2), lambda i, be, nv: (i, 0)),
        scratch_shapes=[pltpu.VMEM((D, 2 * D_EXPERT), jnp.bfloat16),
                        pltpu.VMEM((D_EXPERT, D), jnp.bfloat16)],
    )
    return pl.pallas_call(
        _expert_kernel,
        grid_spec=grid_spec,
        out_shape=jax.ShapeDtypeStruct((n_rows, D // 2), jnp.uint32),
        compiler_params=pltpu.CompilerParams(
            dimension_semantics=("arbitrary",), vmem_limit_bytes=VMEM_LIMIT),
        name="expert_ffn",
    )(block_e, block_valid, xs, w_up, b_up.reshape(N_EXPERTS, 1, 2 * D_EXPERT), w_down,
      b_down.reshape(N_EXPERTS, 1, D))


def _final_kernel(x1_ref, rows_ref, gate_ref, g_ref, o_ref):
    half = x1_ref.shape[1] // 2
    acc_hi = x1_ref[:, :half]
    acc_lo = x1_ref[:, half:]
    for kk in range(TOP_K):
        y_hi, y_lo = _unpack_bf16_pairs(rows_ref[kk])
        gate = gate_ref[:, kk:kk + 1]
        acc_hi = acc_hi + y_hi * gate
        acc_lo = acc_lo + y_lo * gate
    o_ref[...] = _rms_norm(jnp.concatenate([acc_hi, acc_lo], axis=1), g_ref[...])


def _final_norm(x1, rows, gates_t, g_final):
    T, D = x1.shape
    tm = SEQ_TILE
    row = pl.BlockSpec((tm, D), lambda t: (t, 0))
    return pl.pallas_call(
        _final_kernel,
        grid=(T // tm,),
        in_specs=[row, pl.BlockSpec((TOP_K, tm, D // 2), lambda t: (0, t, 0)),
                  pl.BlockSpec((tm, TOP_K), lambda t: (t, 0)),
                  pl.BlockSpec((1, D), lambda t: (0, 0))],
        out_specs=row,
        out_shape=jax.ShapeDtypeStruct((T, D), jnp.float32),
        compiler_params=pltpu.CompilerParams(dimension_semantics=("arbitrary",)),
        name="combine_final_norm",
    )(x1, rows, gates_t, g_final.reshape(1, D))


def kernel(x, g_mix, w_in, b_forget, w_pool, pool_scale, w_out, g_ffn, w_router, b_router,
           w_up, b_up, w_down, b_down, g_final):
    B, S, D = x.shape
    T = B * S
    ypool, qt, k, vt = _in_projection(x, g_mix[0], w_in[0], b_forget[0], w_pool[0], pool_scale[0])
    yattn = _attention(qt, k, vt)
    x1, h2, top_idx, gates, rank, counts = _out_projection(
        ypool.reshape(T, POOL_WIDTH), yattn.reshape(T, ATTN_WIDTH), x.reshape(T, D),
        w_out[0], g_ffn[0], w_router[0], b_router[0])

    blk = EXPERT_ROWS
    counts = counts[:, 0].astype(jnp.int32)
    padded = (counts + blk - 1) // blk * blk
    pad_ends = jnp.cumsum(padded)
    pad_starts = pad_ends - padded
    dest = pad_starts[top_idx] + rank
    n_rows = T * TOP_K + N_EXPERTS * blk
    block_row0 = jnp.arange(n_rows // blk, dtype=jnp.int32) * blk
    block_e = jnp.minimum(jnp.searchsorted(pad_ends, block_row0, side='right'),
                          N_EXPERTS - 1).astype(jnp.int32)
    block_valid = jnp.clip(counts[block_e] - (block_row0 - pad_starts[block_e]), 0, blk)
    block_valid = jnp.where(block_row0 < pad_ends[-1], block_valid, 0).astype(jnp.int32)

    xs = _dispatch_rows(h2, dest, n_rows)
    ys = _expert_ffn(xs, block_e, block_valid, w_up[0], b_up[0], w_down[0], b_down[0])
    rows = _gather_rows(ys, dest.reshape(-1)).reshape(TOP_K, T, D // 2)
    out = _final_norm(x1, rows, gates.T, g_final)
    return out.reshape(B, S, D)
```

```python
import functools

import jax
import jax.numpy as jnp
from jax import lax
from jax.experimental import pallas as pl
from jax.experimental.pallas import tpu as pltpu
from jax.experimental.pallas import tpu_sc as plsc

D_MODEL = 1024
POOL_WIDTH = 512
POOL_WINDOWS = (2, 4, 8, 16)
POOL_GROUP = 128
MAX_WINDOW = 16
ATTN_WIDTH = 512
HEAD_DIM = 64
N_HEADS = 8
N_EXPERTS = 32
TOP_K = 4
D_EXPERT = 1024
SWIGLU_LIMIT = 7.0
SWIGLU_ALPHA = 1.702
RMS_EPS = 1e-5

LANES = 128
HEAD_SLOT = 128
N_SPLIT = 3
MASK_VALUE = -1e30
LOG2_E = 1.4426950408889634
V_ROWS = 80

SEQ_TILE = 512
Q_TILE = 512
KV_TILE = SEQ_TILE
EXPERT_ROWS = 512
SC_CORES = 2
SC_SUBCORES = 16
SC_WORKERS = SC_CORES * SC_SUBCORES
SC_WINDOW = 128
VMEM_LIMIT = 56 * 1024 * 1024

_NT = (((1,), (1,)), ((), ()))


def _dot(a, b):
    return jnp.dot(a, b, preferred_element_type=jnp.float32)


def _dot_nt(a, b):
    return lax.dot_general(a, b, _NT, preferred_element_type=jnp.float32)


def _rms_norm(x, g):
    return x * lax.rsqrt(jnp.mean(x * x, axis=-1, keepdims=True) + RMS_EPS) * g


def _pack_bf16_pairs(x):
    n = x.shape[1] // 2
    bits = lax.bitcast_convert_type(x.astype(jnp.bfloat16).astype(jnp.float32), jnp.uint32)
    return bits[:, :n] | (bits[:, n:] >> 16)


def _unpack_bf16_pairs(p):
    hi = lax.bitcast_convert_type(p & jnp.uint32(0xFFFF0000), jnp.float32)
    lo = lax.bitcast_convert_type(p << 16, jnp.float32)
    return hi, lo


def _split3(x):
    hi = x.astype(jnp.bfloat16)
    r1 = x - hi.astype(jnp.float32)
    mid = r1.astype(jnp.bfloat16)
    lo = (r1 - mid.astype(jnp.float32)).astype(jnp.bfloat16)
    return hi, mid, lo


def _inproj_kernel(x_ref, g_ref, wu_ref, wqt_ref, wk_ref, wvt_ref, wf_ref, bf_ref, tri_ref,
                   pqt_ref, pk_ref, oneq_ref, onek_ref, wpool_ref, pscale_ref,
                   ypool_ref, qt_ref, k_ref, vt_ref, ubuf, ccarry):
    s = pl.program_id(1)
    tm = x_ref.shape[1]

    @pl.when(s == 0)
    def _():
        ubuf[0:MAX_WINDOW, :] = jnp.zeros((MAX_WINDOW, POOL_WIDTH), jnp.float32)
        ccarry[...] = jnp.zeros_like(ccarry)

    x = x_ref[0]
    h = _rms_norm(x, g_ref[...]).astype(jnp.bfloat16)

    fl = _dot(h, wf_ref[...]) + bf_ref[...]
    logf = jnp.minimum(fl, 0.0) - jnp.log1p(jnp.exp(-jnp.abs(fl)))
    lane = lax.broadcasted_iota(jnp.int32, (tm, LANES), 1)
    logf = jnp.where(lane < N_SPLIT * N_HEADS, logf, 0.0)
    f_hi, f_mid, f_lo = _split3(logf)
    tri = tri_ref[...]
    c = _dot(tri, f_hi) + _dot(tri, f_mid) + _dot(tri, f_lo) + ccarry[...]
    ccarry[...] = c[tm - 1:tm, :]
    c_hi, c_mid, c_lo = _split3(c * LOG2_E)
    term = lane % N_SPLIT
    c3 = jnp.where(term == 0, c_hi, jnp.where(term == 1, c_mid, c_lo))

    kp = _dot(h, wk_ref[...]) + _dot(c3, pk_ref[...]) + onek_ref[...]
    k_ref[0] = kp.astype(jnp.bfloat16)
    qt = _dot_nt(wqt_ref[...], h) + _dot_nt(pqt_ref[...], c3) + oneq_ref[...]
    qt_ref[0] = qt.astype(jnp.bfloat16)
    vt = _dot_nt(wvt_ref[...], h).astype(jnp.bfloat16)
    ones = jnp.ones((V_ROWS - HEAD_DIM, tm), jnp.bfloat16)
    for hd in range(N_HEADS):
        vt_ref[0, 0, hd * V_ROWS:hd * V_ROWS + HEAD_DIM, :] = vt[hd * HEAD_DIM:(hd + 1) * HEAD_DIM]
        vt_ref[0, 0, hd * V_ROWS + HEAD_DIM:(hd + 1) * V_ROWS, :] = ones

    u = _dot(h, wu_ref[...])
    ubuf[MAX_WINDOW:MAX_WINDOW + tm, :] = u
    pos = s * tm + lax.broadcasted_iota(jnp.int32, (tm, 1), 0)
    for gi, w in enumerate(POOL_WINDOWS):
        cols = slice(gi * POOL_GROUP, (gi + 1) * POOL_GROUP)
        ug = u[:, cols]
        acc = ug
        for j in range(1, w):
            acc = acc + ubuf[MAX_WINDOW - j:MAX_WINDOW - j + tm, cols]
        cnt = jnp.minimum(pos + 1, w).astype(jnp.float32)
        p = acc / cnt - ug
        y = _dot(p.astype(jnp.bfloat16), wpool_ref[gi]) * pscale_ref[:, cols]
        ypool_ref[0, :, cols] = y.astype(jnp.bfloat16)
    ubuf[0:MAX_WINDOW, :] = ubuf[tm:tm + MAX_WINDOW, :]


def _in_projection(x, g_mix, w_in, b_forget, w_pool, pool_scale):
    B, S, D = x.shape
    tm = SEQ_TILE
    bf16 = jnp.bfloat16
    o = POOL_WIDTH
    w_u = w_in[:, :o].astype(bf16)
    w_q = w_in[:, o:o + ATTN_WIDTH] * (HEAD_DIM ** -0.5 * LOG2_E)
    w_k = w_in[:, o + ATTN_WIDTH:o + 2 * ATTN_WIDTH]
    w_v = w_in[:, o + 2 * ATTN_WIDTH:o + 3 * ATTN_WIDTH]
    w_f = w_in[:, o + 3 * ATTN_WIDTH:]

    def head_slots(w):
        w = w.reshape(D, N_HEADS, HEAD_DIM)
        w = jnp.pad(w, ((0, 0), (0, 0), (0, HEAD_SLOT - HEAD_DIM)))
        return w.reshape(D, N_HEADS * HEAD_SLOT)

    wqt = head_slots(w_q).T.astype(bf16)
    wk = head_slots(w_k).astype(bf16)
    wvt = w_v.T.astype(bf16)
    n_gate = N_SPLIT * N_HEADS
    wf = jnp.pad(jnp.repeat(w_f, N_SPLIT, axis=1), ((0, 0), (0, LANES - n_gate))).astype(bf16)
    bfp = jnp.pad(jnp.repeat(b_forget, N_SPLIT), (0, LANES - n_gate)).reshape(1, LANES)

    src = jnp.arange(LANES)
    head, term = src // N_SPLIT, src % N_SPLIT
    dst = jnp.arange(N_HEADS * HEAD_SLOT)
    valid = (src < n_gate)[:, None]
    q_hit = valid & (dst[None, :] == (head * HEAD_SLOT + HEAD_DIM + term)[:, None])
    k_hit = valid & (dst[None, :] == (head * HEAD_SLOT + HEAD_DIM + N_SPLIT + term)[:, None])
    pqt = q_hit.T.astype(bf16)
    pk = -k_hit.astype(bf16)
    slot_pos = dst % HEAD_SLOT
    oneq = ((slot_pos >= HEAD_DIM + N_SPLIT) & (slot_pos < HEAD_DIM + 2 * N_SPLIT))
    onek = ((slot_pos >= HEAD_DIM) & (slot_pos < HEAD_DIM + N_SPLIT))
    oneq = oneq.astype(jnp.float32).reshape(-1, 1)
    onek = onek.astype(jnp.float32).reshape(1, -1)
    r = jnp.arange(tm)
    tri = (r[None, :] <= r[:, None]).astype(bf16)

    full = lambda *shape: pl.BlockSpec(shape, lambda b, s: (0,) * len(shape))
    hs = N_HEADS * HEAD_SLOT
    return pl.pallas_call(
        _inproj_kernel,
        grid=(B, S // tm),
        in_specs=[
            pl.BlockSpec((1, tm, D), lambda b, s: (b, s, 0)),
            full(1, D), full(D, o), full(hs, D), full(D, hs), full(ATTN_WIDTH, D),
            full(D, LANES), full(1, LANES), full(tm, tm), full(hs, LANES), full(LANES, hs),
            full(hs, 1), full(1, hs), full(len(POOL_WINDOWS), POOL_GROUP, POOL_GROUP),
            full(1, o),
        ],
        out_specs=[
            pl.BlockSpec((1, tm, o), lambda b, s: (b, s, 0)),
            pl.BlockSpec((1, hs, tm), lambda b, s: (b, 0, s)),
            pl.BlockSpec((1, tm, hs), lambda b, s: (b, s, 0)),
            pl.BlockSpec((1, 1, N_HEADS * V_ROWS, tm), lambda b, s: (b, s, 0, 0)),
        ],
        out_shape=[
            jax.ShapeDtypeStruct((B, S, o), bf16),
            jax.ShapeDtypeStruct((B, hs, S), bf16),
            jax.ShapeDtypeStruct((B, S, hs), bf16),
            jax.ShapeDtypeStruct((B, S // tm, N_HEADS * V_ROWS, tm), bf16),
        ],
        scratch_shapes=[
            pltpu.VMEM((MAX_WINDOW + tm, o), jnp.float32),
            pltpu.VMEM((1, LANES), jnp.float32),
        ],
        compiler_params=pltpu.CompilerParams(
            dimension_semantics=("arbitrary", "arbitrary"), vmem_limit_bytes=VMEM_LIMIT),
        name="in_projection",
    )(x, g_mix.reshape(1, D), w_u, wqt, wk, wvt, wf, bfp, tri, pqt, pk, oneq, onek,
      w_pool.astype(bf16), pool_scale.reshape(1, o))


def _attn_kernel(qt_ref, k_ref, vt_ref, o_ref, st_buf, mx_buf, m_ref, acc_ref):
    i = pl.program_id(2)
    tq = qt_ref.shape[2]
    tk = vt_ref.shape[3]
    heads = qt_ref.shape[1] // HEAD_SLOT

    m_ref[...] = jnp.full(m_ref.shape, MASK_VALUE, jnp.float32)
    acc_ref[...] = jnp.zeros_like(acc_ref)

    def logits(j, slot, masked):
        row0 = pl.multiple_of(j * tk, tk)
        for hh in range(heads):
            qt = qt_ref[0, hh * HEAD_SLOT:(hh + 1) * HEAD_SLOT, :]
            k = k_ref[0, pl.ds(row0, tk), hh * HEAD_SLOT:(hh + 1) * HEAD_SLOT]
            st = _dot(k, qt)
            if masked:
                key = row0 + lax.broadcasted_iota(jnp.int32, (tk, tq), 0)
                qry = i * tq + lax.broadcasted_iota(jnp.int32, (tk, tq), 1)
                st = jnp.where(key <= qry, st, MASK_VALUE)
            st_buf[slot, hh] = st
            mx_buf[slot, hh] = jnp.max(st, axis=0, keepdims=True)

    def accumulate(j, slot):
        for hh in range(heads):
            st = st_buf[slot, hh]
            m_prev = m_ref[hh]
            m_new = jnp.maximum(m_prev, mx_buf[slot, hh])
            alpha = jnp.exp2(m_prev - m_new)
            p = jnp.exp2(st - m_new)
            vt = vt_ref[0, j, hh * V_ROWS:(hh + 1) * V_ROWS, :]
            acc_ref[hh] = alpha * acc_ref[hh] + _dot(vt, p.astype(jnp.bfloat16))
            m_ref[hh] = m_new

    logits(i, 0, True)

    def before(t):
        return jnp.where(t == 0, i, t - 1)

    def body(u, carry):
        t = 2 * u
        logits(t, 1, False)
        accumulate(before(t), 0)
        logits(t + 1, 0, False)
        accumulate(t, 1)
        return carry

    lax.fori_loop(0, i // 2, body, 0)

    @pl.when(i % 2 == 1)
    def _():
        logits(i - 1, 1, False)
        accumulate(before(i - 1), 0)
        accumulate(i - 1, 1)

    @pl.when(i % 2 == 0)
    def _():
        accumulate(jnp.maximum(i - 1, 0), 0)

    outs = [acc_ref[hh, :HEAD_DIM] / acc_ref[hh, HEAD_DIM:HEAD_DIM + 1] for hh in range(heads)]
    ot = jnp.concatenate(outs, axis=0)
    o_ref[0] = ot.T.astype(o_ref.dtype)


def _attention(qt, k, vt):
    B, hs, S = qt.shape
    tq, tk = Q_TILE, KV_TILE
    assert tq == tk, "the diagonal is handled as a single masked key chunk"
    pair = 2
    n_pairs = N_HEADS // pair
    return pl.pallas_call(
        _attn_kernel,
        grid=(B, n_pairs, S // tq),
        in_specs=[
            pl.BlockSpec((1, pair * HEAD_SLOT, tq), lambda b, p, i: (b, p, i)),
            pl.BlockSpec((1, S, pair * HEAD_SLOT), lambda b, p, i: (b, 0, p)),
            pl.BlockSpec((1, S // tk, pair * V_ROWS, tk), lambda b, p, i: (b, 0, p, 0)),
        ],
        out_specs=pl.BlockSpec((1, tq, pair * HEAD_DIM), lambda b, p, i: (b, i, p)),
        out_shape=jax.ShapeDtypeStruct((B, S, ATTN_WIDTH), jnp.bfloat16),
        scratch_shapes=[
            pltpu.VMEM((2, pair, tk, tq), jnp.float32),
            pltpu.VMEM((2, pair, 1, tq), jnp.float32),
            pltpu.VMEM((pair, 1, tq), jnp.float32),
            pltpu.VMEM((pair, V_ROWS, tq), jnp.float32),
        ],
        compiler_params=pltpu.CompilerParams(
            dimension_semantics=("arbitrary", "arbitrary", "arbitrary"),
            vmem_limit_bytes=VMEM_LIMIT),
        name="forgetting_attention",
    )(qt, k, vt)


def _outproj_kernel(yp_ref, ya_ref, x_ref, wo1_ref, wo2_ref, g_ref, wr2_ref, wrhi_ref, br_ref,
                    upper_ref, x1_ref, h2_ref, idx_ref, gate_ref, rank_ref, cnt_ref, carry):
    t = pl.program_id(0)
    tm = x_ref.shape[0]

    @pl.when(t == 0)
    def _():
        carry[...] = jnp.zeros_like(carry)

    x1 = x_ref[...] + _dot(yp_ref[...], wo1_ref[...]) + _dot(ya_ref[...], wo2_ref[...])
    x1_ref[...] = x1
    h2 = _rms_norm(x1, g_ref[...])
    h2_hi = h2.astype(jnp.bfloat16)
    h2_lo = (h2 - h2_hi.astype(jnp.float32)).astype(jnp.bfloat16)
    h2_ref[...] = _pack_bf16_pairs(h2)

    lg2 = _dot_nt(wr2_ref[...], h2_hi)
    logits = (lg2[:N_EXPERTS] + lg2[N_EXPERTS:] + _dot_nt(wrhi_ref[...], h2_lo) + br_ref[...])

    eio = lax.broadcasted_iota(jnp.int32, (N_EXPERTS, tm), 0)
    vals, idxs = [], []
    cur = logits
    for _ in range(TOP_K):
        m = jnp.max(cur, axis=0, keepdims=True)
        ix = jnp.min(jnp.where(cur == m, eio, N_EXPERTS), axis=0, keepdims=True)
        vals.append(m)
        idxs.append(ix)
        cur = jnp.where(eio == ix, -jnp.inf, cur)
    exps = [jnp.exp(v - vals[0]) for v in vals]
    denom = exps[0] + exps[1] + exps[2] + exps[3]
    hits = [eio == ix for ix in idxs]
    cnt = jnp.zeros((N_EXPERTS, tm), jnp.float32)
    for hit in hits:
        cnt = cnt + hit.astype(jnp.float32)
    base = _dot(cnt.astype(jnp.bfloat16), upper_ref[...]) + carry[...]
    for kk in range(TOP_K):
        idx_ref[kk:kk + 1, :] = idxs[kk]
        gate_ref[kk:kk + 1, :] = exps[kk] / denom
        rank = jnp.sum(jnp.where(hits[kk], base, 0.0), axis=0, keepdims=True)
        rank_ref[kk:kk + 1, :] = rank.astype(jnp.int32)
    carry[...] = carry[...] + jnp.sum(cnt, axis=1, keepdims=True)
    cnt_ref[...] = carry[...]


def _out_projection(ypool, yattn, x, w_out, g_ffn, w_router, b_router):
    T, D = x.shape
    tm = SEQ_TILE
    bf16 = jnp.bfloat16
    wo = w_out.astype(bf16)
    wr_t = w_router.T
    wr_hi = wr_t.astype(bf16)
    wr_lo = (wr_t - wr_hi.astype(jnp.float32)).astype(bf16)
    wr2 = jnp.concatenate([wr_hi, wr_lo], axis=0)
    r = jnp.arange(tm)
    upper = (r[:, None] < r[None, :]).astype(bf16)
    full = lambda *shape: pl.BlockSpec(shape, lambda t: (0,) * len(shape))
    row = lambda w: pl.BlockSpec((tm, w), lambda t: (t, 0))
    col = pl.BlockSpec((TOP_K, tm), lambda t: (0, t))
    return pl.pallas_call(
        _outproj_kernel,
        grid=(T // tm,),
        in_specs=[row(POOL_WIDTH), row(ATTN_WIDTH), row(D), full(POOL_WIDTH, D),
                  full(ATTN_WIDTH, D), full(1, D), full(2 * N_EXPERTS, D), full(N_EXPERTS, D),
                  full(N_EXPERTS, 1), full(tm, tm)],
        out_specs=[row(D), row(D // 2), col, col, col, full(N_EXPERTS, 1)],
        out_shape=[
            jax.ShapeDtypeStruct((T, D), jnp.float32),
            jax.ShapeDtypeStruct((T, D // 2), jnp.uint32),
            jax.ShapeDtypeStruct((TOP_K, T), jnp.int32),
            jax.ShapeDtypeStruct((TOP_K, T), jnp.float32),
            jax.ShapeDtypeStruct((TOP_K, T), jnp.int32),
            jax.ShapeDtypeStruct((N_EXPERTS, 1), jnp.float32),
        ],
        scratch_shapes=[pltpu.VMEM((N_EXPERTS, 1), jnp.float32)],
        compiler_params=pltpu.CompilerParams(
            dimension_semantics=("arbitrary",), vmem_limit_bytes=VMEM_LIMIT),
        name="out_projection_router",
    )(ypool, yattn, x, wo[:POOL_WIDTH], wo[POOL_WIDTH:], g_ffn.reshape(1, D), wr2, wr_hi,
      b_router.reshape(N_EXPERTS, 1), upper)


def _sc_mesh():
    return plsc.VectorSubcoreMesh(core_axis_name="c", subcore_axis_name="s")


def _dispatch_rows(h2, dest, n_rows):
    T, D = h2.shape
    per_worker = T // SC_WORKERS
    steps = per_worker // SC_WINDOW

    @functools.partial(
        pl.kernel, mesh=_sc_mesh(),
        out_type=jax.ShapeDtypeStruct((n_rows, D), h2.dtype),
        scratch_types=[pltpu.VMEM((TOP_K, SC_WINDOW), jnp.int32),
                       pltpu.VMEM((SC_WINDOW, D), h2.dtype),
                       pltpu.SemaphoreType.DMA],
    )
    def scatter_kernel(h2_hbm, dest_hbm, xs_hbm, idx_v, rows_v, sem):
        wid = lax.axis_index("s") * SC_CORES + lax.axis_index("c")

        @pl.loop(0, steps)
        def _(step):
            base = wid * per_worker + step * SC_WINDOW
            pltpu.sync_copy(h2_hbm.at[pl.ds(base, SC_WINDOW)], rows_v)
            pltpu.sync_copy(dest_hbm.at[:, pl.ds(base, SC_WINDOW)], idx_v)
            copies = [pltpu.async_copy(rows_v, xs_hbm.at[idx_v.at[kk]], sem)
                      for kk in range(TOP_K)]
            for c in copies:
                c.wait()

    return scatter_kernel(h2, dest)


def _gather_rows(table, idx):
    N = idx.shape[0]
    D = table.shape[1]
    per_worker = N // SC_WORKERS
    steps = per_worker // SC_WINDOW

    @functools.partial(
        pl.kernel, mesh=_sc_mesh(),
        out_type=jax.ShapeDtypeStruct((N, D), table.dtype),
        scratch_types=[pltpu.VMEM((SC_WINDOW,), jnp.int32),
                       pltpu.VMEM((SC_WINDOW, D), table.dtype),
                       pltpu.SemaphoreType.DMA],
    )
    def gather_kernel(table_hbm, idx_hbm, out_hbm, idx_v, rows_v, sem):
        wid = lax.axis_index("s") * SC_CORES + lax.axis_index("c")

        @pl.loop(0, steps)
        def _(step):
            base = wid * per_worker + step * SC_WINDOW
            pltpu.sync_copy(idx_hbm.at[pl.ds(base, SC_WINDOW)], idx_v)
            pltpu.async_copy(table_hbm.at[idx_v], rows_v, sem).wait()
            pltpu.sync_copy(rows_v, out_hbm.at[pl.ds(base, SC_WINDOW)])

    return gather_kernel(table, idx)


def _expert_kernel(be_ref, nvalid_ref, x_ref, wup_ref, bup_ref, wdn_ref, bdn_ref, y_ref,
                   wup_bf, wdn_bf):
    i = pl.program_id(0)
    blk = x_ref.shape[0]
    changed = jnp.logical_or(i == 0, be_ref[i] != be_ref[jnp.maximum(i - 1, 0)])

    @pl.when(changed)
    def _():
        wup_bf[...] = wup_ref[0].astype(jnp.bfloat16)
        wdn_bf[...] = wdn_ref[0].astype(jnp.bfloat16)

    nvalid = nvalid_ref[i]

    @pl.when(nvalid > 0)
    def _():
        row = lax.broadcasted_iota(jnp.int32, (blk, 1), 0)
        x_hi, x_lo = _unpack_bf16_pairs(jnp.where(row < nvalid, x_ref[...], jnp.uint32(0)))
        x = jnp.concatenate([x_hi.astype(jnp.bfloat16), x_lo.astype(jnp.bfloat16)], axis=1)
        glu = _dot(x, wup_bf[:, :D_EXPERT]) + bup_ref[0, :, :D_EXPERT]
        lin = _dot(x, wup_bf[:, D_EXPERT:]) + bup_ref[0, :, D_EXPERT:]
        glu = jnp.minimum(glu, SWIGLU_LIMIT)
        lin = jnp.clip(lin, -SWIGLU_LIMIT, SWIGLU_LIMIT)
        act = glu * (1.0 / (1.0 + jnp.exp(-SWIGLU_ALPHA * glu))) * (lin + 1.0)
        y = _dot(act.astype(jnp.bfloat16), wdn_bf[...]) + bdn_ref[0]
        y_ref[...] = _pack_bf16_pairs(y)

    @pl.when(nvalid <= 0)
    def _():
        y_ref[...] = jnp.zeros_like(y_ref)


def _expert_ffn(xs, block_e, block_valid, w_up, b_up, w_down, b_down):
    n_rows = xs.shape[0]
    D = D_MODEL
    blk = EXPERT_ROWS
    grid_spec = pltpu.PrefetchScalarGridSpec(
        num_scalar_prefetch=2,
        grid=(n_rows // blk,),
        in_specs=[
            pl.BlockSpec((blk, D // 2), lambda i, be, nv: (i, 0)),
            pl.BlockSpec((1, D, 2 * D_EXPERT), lambda i, be, nv: (be[i], 0, 0)),
            pl.BlockSpec((1, 1, 2 * D_EXPERT), lambda i, be, nv: (be[i], 0, 0)),
            pl.BlockSpec((1, D_EXPERT, D), lambda i, be, nv: (be[i], 0, 0)),
            pl.BlockSpec((1, 1, D), lambda i, be, nv: (be[i], 0, 0)),
        ],
        out_specs=pl.BlockSpec((blk, D // 2), lambda i, be, nv: (i, 0)),
        scratch_shapes=[pltpu.VMEM((D, 2 * D_EXPERT), jnp.bfloat16),
                        pltpu.VMEM((D_EXPERT, D), jnp.bfloat16)],
    )
    return pl.pallas_call(
        _expert_kernel,
        grid_spec=grid_spec,
        out_shape=jax.ShapeDtypeStruct((n_rows, D // 2), jnp.uint32),
        compiler_params=pltpu.CompilerParams(
            dimension_semantics=("arbitrary",), vmem_limit_bytes=VMEM_LIMIT),
        name="expert_ffn",
    )(block_e, block_valid, xs, w_up, b_up.reshape(N_EXPERTS, 1, 2 * D_EXPERT), w_down,
      b_down.reshape(N_EXPERTS, 1, D))


def _final_kernel(x1_ref, rows_ref, gate_ref, g_ref, o_ref):
    half = x1_ref.shape[1] // 2
    acc_hi = x1_ref[:, :half]
    acc_lo = x1_ref[:, half:]
    for kk in range(TOP_K):
        y_hi, y_lo = _unpack_bf16_pairs(rows_ref[kk])
        gate = gate_ref[:, kk:kk + 1]
        acc_hi = acc_hi + y_hi * gate
        acc_lo = acc_lo + y_lo * gate
    o_ref[...] = _rms_norm(jnp.concatenate([acc_hi, acc_lo], axis=1), g_ref[...])


def _final_norm(x1, rows, gates_t, g_final):
    T, D = x1.shape
    tm = SEQ_TILE
    row = pl.BlockSpec((tm, D), lambda t: (t, 0))
    return pl.pallas_call(
        _final_kernel,
        grid=(T // tm,),
        in_specs=[row, pl.BlockSpec((TOP_K, tm, D // 2), lambda t: (0, t, 0)),
                  pl.BlockSpec((tm, TOP_K), lambda t: (t, 0)),
                  pl.BlockSpec((1, D), lambda t: (0, 0))],
        out_specs=row,
        out_shape=jax.ShapeDtypeStruct((T, D), jnp.float32),
        compiler_params=pltpu.CompilerParams(dimension_semantics=("arbitrary",)),
        name="combine_final_norm",
    )(x1, rows, gates_t, g_final.reshape(1, D))


def kernel(x, g_mix, w_in, b_forget, w_pool, pool_scale, w_out, g_ffn, w_router, b_router,
           w_up, b_up, w_down, b_down, g_final):
    B, S, D = x.shape
    T = B * S
    ypool, qt, k, vt = _in_projection(x, g_mix[0], w_in[0], b_forget[0], w_pool[0], pool_scale[0])
    yattn = _attention(qt, k, vt)
    x1, h2, top_idx, gates, rank, counts = _out_projection(
        ypool.reshape(T, POOL_WIDTH), yattn.reshape(T, ATTN_WIDTH), x.reshape(T, D),
        w_out[0], g_ffn[0], w_router[0], b_router[0])

    blk = EXPERT_ROWS
    counts = counts[:, 0].astype(jnp.int32)
    padded = (counts + blk - 1) // blk * blk
    pad_ends = jnp.cumsum(padded)
    pad_starts = pad_ends - padded
    dest = rank
    for e in range(N_EXPERTS):
        dest = dest + jnp.where(top_idx == e, pad_starts[e], 0)
    n_rows = T * TOP_K + N_EXPERTS * blk
    block_row0 = jnp.arange(n_rows // blk, dtype=jnp.int32) * blk
    block_e = jnp.sum((pad_ends[None, :] <= block_row0[:, None]).astype(jnp.int32), axis=1)
    block_e = jnp.minimum(block_e, N_EXPERTS - 1)
    of_block = block_e[:, None] == jnp.arange(N_EXPERTS, dtype=jnp.int32)[None, :]
    block_end = jnp.sum(jnp.where(of_block, (pad_starts + counts)[None, :], 0), axis=1)
    block_valid = jnp.clip(block_end - block_row0, 0, blk)
    block_valid = jnp.where(block_row0 < pad_ends[-1], block_valid, 0).astype(jnp.int32)

    xs = _dispatch_rows(h2, dest, n_rows)
    ys = _expert_ffn(xs, block_e, block_valid, w_up[0], b_up[0], w_down[0], b_down[0])
    rows = _gather_rows(ys, dest.reshape(-1)).reshape(TOP_K, T, D // 2)
    out = _final_norm(x1, rows, gates.T, g_final)
    return out.reshape(B, S, D)
```

```python
import functools

import jax
import jax.numpy as jnp
from jax import lax
from jax.experimental import pallas as pl
from jax.experimental.pallas import tpu as pltpu
from jax.experimental.pallas import tpu_sc as plsc

D_MODEL = 1024
POOL_WIDTH = 512
POOL_WINDOWS = (2, 4, 8, 16)
POOL_GROUP = 128
MAX_WINDOW = 16
ATTN_WIDTH = 512
HEAD_DIM = 64
N_HEADS = 8
N_EXPERTS = 32
TOP_K = 4
D_EXPERT = 1024
SWIGLU_LIMIT = 7.0
SWIGLU_ALPHA = 1.702
RMS_EPS = 1e-5

LANES = 128
HEAD_SLOT = 128
N_SPLIT = 3
MASK_VALUE = -1e30
LOG2_E = 1.4426950408889634
SKIP_MARGIN = 160.0
NORM_SLACK = 1.02
V_ROWS = 80

SEQ_TILE = 512
Q_TILE = 512
KV_TILE = SEQ_TILE
EXPERT_ROWS = 512
SC_CORES = 2
SC_SUBCORES = 16
SC_WORKERS = SC_CORES * SC_SUBCORES
SC_WINDOW = 128
VMEM_LIMIT = 56 * 1024 * 1024

_NT = (((1,), (1,)), ((), ()))


def _dot(a, b):
    return jnp.dot(a, b, preferred_element_type=jnp.float32)


def _dot_nt(a, b):
    return lax.dot_general(a, b, _NT, preferred_element_type=jnp.float32)


def _rms_norm(x, g):
    return x * lax.rsqrt(jnp.mean(x * x, axis=-1, keepdims=True) + RMS_EPS) * g


def _pack_bf16_pairs(x):
    n = x.shape[1] // 2
    bits = lax.bitcast_convert_type(x.astype(jnp.bfloat16).astype(jnp.float32), jnp.uint32)
    return bits[:, :n] | (bits[:, n:] >> 16)


def _unpack_bf16_pairs(p):
    hi = lax.bitcast_convert_type(p & jnp.uint32(0xFFFF0000), jnp.float32)
    lo = lax.bitcast_convert_type(p << 16, jnp.float32)
    return hi, lo


def _split3(x):
    hi = x.astype(jnp.bfloat16)
    r1 = x - hi.astype(jnp.float32)
    mid = r1.astype(jnp.bfloat16)
    lo = (r1 - mid.astype(jnp.float32)).astype(jnp.bfloat16)
    return hi, mid, lo


def _inproj_kernel(x_ref, g_ref, wu_ref, wqt_ref, wk_ref, wvt_ref, wf_ref, bf_ref, tri_ref,
                   pqt_ref, pk_ref, oneq_ref, onek_ref, wpool_ref, pscale_ref,
                   ypool_ref, qt_ref, k_ref, vt_ref, stats_ref, ubuf, ccarry):
    s = pl.program_id(1)
    tm = x_ref.shape[1]

    @pl.when(s == 0)
    def _():
        ubuf[0:MAX_WINDOW, :] = jnp.zeros((MAX_WINDOW, POOL_WIDTH), jnp.float32)
        ccarry[...] = jnp.zeros_like(ccarry)

    x = x_ref[0]
    h = _rms_norm(x, g_ref[...]).astype(jnp.bfloat16)

    fl = _dot(h, wf_ref[...]) + bf_ref[...]
    logf = jnp.minimum(fl, 0.0) - jnp.log1p(jnp.exp(-jnp.abs(fl)))
    lane = lax.broadcasted_iota(jnp.int32, (tm, LANES), 1)
    logf = jnp.where(lane < N_SPLIT * N_HEADS, logf, 0.0)
    f_hi, f_mid, f_lo = _split3(logf)
    tri = tri_ref[...]
    c = _dot(tri, f_hi) + _dot(tri, f_mid) + _dot(tri, f_lo) + ccarry[...]
    ccarry[...] = c[tm - 1:tm, :]
    c2 = c * LOG2_E
    c_hi, c_mid, c_lo = _split3(c2)
    term = lane % N_SPLIT
    c3 = jnp.where(term == 0, c_hi, jnp.where(term == 1, c_mid, c_lo))

    kp = _dot(h, wk_ref[...]) + _dot(c3, pk_ref[...]) + onek_ref[...]
    k_ref[0] = kp.astype(jnp.bfloat16)
    qt = _dot_nt(wqt_ref[...], h) + _dot_nt(pqt_ref[...], c3) + oneq_ref[...]
    qt_ref[0] = qt.astype(jnp.bfloat16)
    lane_row = lax.broadcasted_iota(jnp.int32, (1, LANES), 1)
    kn2 = jnp.zeros((1, LANES), jnp.float32)
    qn2 = jnp.zeros((1, LANES), jnp.float32)
    for hd in range(N_HEADS):
        ks = kp[:, hd * HEAD_SLOT:hd * HEAD_SLOT + HEAD_DIM]
        qs = qt[hd * HEAD_SLOT:hd * HEAD_SLOT + HEAD_DIM, :]
        kmax = jnp.max(jnp.sum(ks * ks, axis=1, keepdims=True), axis=0, keepdims=True)
        qmax = jnp.max(jnp.sum(qs * qs, axis=0, keepdims=True), axis=1, keepdims=True)
        kn2 = jnp.where(lane_row == hd, kmax, kn2)
        qn2 = jnp.where(lane_row == hd, qmax, qn2)
    stats_ref[0, 0, 0:1, :] = kn2
    stats_ref[0, 0, 1:2, :] = qn2
    stats_ref[0, 0, 2:3, :] = c2[0:1, :]
    stats_ref[0, 0, 3:4, :] = c2[tm - 1:tm, :]
    stats_ref[0, 0, 4:8, :] = jnp.zeros((4, LANES), jnp.float32)
    vt = _dot_nt(wvt_ref[...], h).astype(jnp.bfloat16)
    ones = jnp.ones((V_ROWS - HEAD_DIM, tm), jnp.bfloat16)
    for hd in range(N_HEADS):
        vt_ref[0, 0, hd * V_ROWS:hd * V_ROWS + HEAD_DIM, :] = vt[hd * HEAD_DIM:(hd + 1) * HEAD_DIM]
        vt_ref[0, 0, hd * V_ROWS + HEAD_DIM:(hd + 1) * V_ROWS, :] = ones

    u = _dot(h, wu_ref[...])
    ubuf[MAX_WINDOW:MAX_WINDOW + tm, :] = u
    pos = s * tm + lax.broadcasted_iota(jnp.int32, (tm, 1), 0)
    for gi, w in enumerate(POOL_WINDOWS):
        cols = slice(gi * POOL_GROUP, (gi + 1) * POOL_GROUP)
        ug = u[:, cols]
        acc = ug
        for j in range(1, w):
            acc = acc + ubuf[MAX_WINDOW - j:MAX_WINDOW - j + tm, cols]
        cnt = jnp.minimum(pos + 1, w).astype(jnp.float32)
        p = acc / cnt - ug
        y = _dot(p.astype(jnp.bfloat16), wpool_ref[gi]) * pscale_ref[:, cols]
        ypool_ref[0, :, cols] = y.astype(jnp.bfloat16)
    ubuf[0:MAX_WINDOW, :] = ubuf[tm:tm + MAX_WINDOW, :]


def _in_projection(x, g_mix, w_in, b_forget, w_pool, pool_scale):
    B, S, D = x.shape
    tm = SEQ_TILE
    bf16 = jnp.bfloat16
    o = POOL_WIDTH
    w_u = w_in[:, :o].astype(bf16)
    w_q = w_in[:, o:o + ATTN_WIDTH] * (HEAD_DIM ** -0.5 * LOG2_E)
    w_k = w_in[:, o + ATTN_WIDTH:o + 2 * ATTN_WIDTH]
    w_v = w_in[:, o + 2 * ATTN_WIDTH:o + 3 * ATTN_WIDTH]
    w_f = w_in[:, o + 3 * ATTN_WIDTH:]

    def head_slots(w):
        w = w.reshape(D, N_HEADS, HEAD_DIM)
        w = jnp.pad(w, ((0, 0), (0, 0), (0, HEAD_SLOT - HEAD_DIM)))
        return w.reshape(D, N_HEADS * HEAD_SLOT)

    wqt = head_slots(w_q).T.astype(bf16)
    wk = head_slots(w_k).astype(bf16)
    wvt = w_v.T.astype(bf16)
    n_gate = N_SPLIT * N_HEADS
    wf = jnp.pad(jnp.repeat(w_f, N_SPLIT, axis=1), ((0, 0), (0, LANES - n_gate))).astype(bf16)
    bfp = jnp.pad(jnp.repeat(b_forget, N_SPLIT), (0, LANES - n_gate)).reshape(1, LANES)

    src = jnp.arange(LANES)
    head, term = src // N_SPLIT, src % N_SPLIT
    dst = jnp.arange(N_HEADS * HEAD_SLOT)
    valid = (src < n_gate)[:, None]
    q_hit = valid & (dst[None, :] == (head * HEAD_SLOT + HEAD_DIM + term)[:, None])
    k_hit = valid & (dst[None, :] == (head * HEAD_SLOT + HEAD_DIM + N_SPLIT + term)[:, None])
    pqt = q_hit.T.astype(bf16)
    pk = -k_hit.astype(bf16)
    slot_pos = dst % HEAD_SLOT
    oneq = ((slot_pos >= HEAD_DIM + N_SPLIT) & (slot_pos < HEAD_DIM + 2 * N_SPLIT))
    onek = ((slot_pos >= HEAD_DIM) & (slot_pos < HEAD_DIM + N_SPLIT))
    oneq = oneq.astype(jnp.float32).reshape(-1, 1)
    onek = onek.astype(jnp.float32).reshape(1, -1)
    r = jnp.arange(tm)
    tri = (r[None, :] <= r[:, None]).astype(bf16)

    full = lambda *shape: pl.BlockSpec(shape, lambda b, s: (0,) * len(shape))
    hs = N_HEADS * HEAD_SLOT
    return pl.pallas_call(
        _inproj_kernel,
        grid=(B, S // tm),
        in_specs=[
            pl.BlockSpec((1, tm, D), lambda b, s: (b, s, 0)),
            full(1, D), full(D, o), full(hs, D), full(D, hs), full(ATTN_WIDTH, D),
            full(D, LANES), full(1, LANES), full(tm, tm), full(hs, LANES), full(LANES, hs),
            full(hs, 1), full(1, hs), full(len(POOL_WINDOWS), POOL_GROUP, POOL_GROUP),
            full(1, o),
        ],
        out_specs=[
            pl.BlockSpec((1, tm, o), lambda b, s: (b, s, 0)),
            pl.BlockSpec((1, hs, tm), lambda b, s: (b, 0, s)),
            pl.BlockSpec((1, tm, hs), lambda b, s: (b, s, 0)),
            pl.BlockSpec((1, 1, N_HEADS * V_ROWS, tm), lambda b, s: (b, s, 0, 0)),
            pl.BlockSpec((1, 1, 8, LANES), lambda b, s: (b, s, 0, 0)),
        ],
        out_shape=[
            jax.ShapeDtypeStruct((B, S, o), bf16),
            jax.ShapeDtypeStruct((B, hs, S), bf16),
            jax.ShapeDtypeStruct((B, S, hs), bf16),
            jax.ShapeDtypeStruct((B, S // tm, N_HEADS * V_ROWS, tm), bf16),
            jax.ShapeDtypeStruct((B, S // tm, 8, LANES), jnp.float32),
        ],
        scratch_shapes=[
            pltpu.VMEM((MAX_WINDOW + tm, o), jnp.float32),
            pltpu.VMEM((1, LANES), jnp.float32),
        ],
        compiler_params=pltpu.CompilerParams(
            dimension_semantics=("arbitrary", "arbitrary"), vmem_limit_bytes=VMEM_LIMIT),
        name="in_projection",
    )(x, g_mix.reshape(1, D), w_u, wqt, wk, wvt, wf, bfp, tri, pqt, pk, oneq, onek,
      w_pool.astype(bf16), pool_scale.reshape(1, o))


def _attn_kernel(j0_ref, qt_ref, k_ref, vt_ref, o_ref, st_buf, mx_buf, m_ref, acc_ref):
    i = pl.program_id(2)
    tile = (pl.program_id(0) * pl.num_programs(1) + pl.program_id(1)) * pl.num_programs(2) + i
    j0 = j0_ref[tile]
    tq = qt_ref.shape[2]
    tk = vt_ref.shape[3]
    heads = qt_ref.shape[1] // HEAD_SLOT

    m_ref[...] = jnp.full(m_ref.shape, MASK_VALUE, jnp.float32)
    acc_ref[...] = jnp.zeros_like(acc_ref)

    def logits(j, slot, hh, masked):
        row0 = pl.multiple_of(j * tk, tk)
        qt = qt_ref[0, hh * HEAD_SLOT:(hh + 1) * HEAD_SLOT, :]
        k = k_ref[0, pl.ds(row0, tk), hh * HEAD_SLOT:(hh + 1) * HEAD_SLOT]
        st = _dot(k, qt)
        if masked:
            key = row0 + lax.broadcasted_iota(jnp.int32, (tk, tq), 0)
            qry = i * tq + lax.broadcasted_iota(jnp.int32, (tk, tq), 1)
            st = jnp.where(key <= qry, st, MASK_VALUE)
        st_buf[slot, hh] = st
        mx_buf[slot, hh] = jnp.max(st, axis=0, keepdims=True)

    def accumulate(j, slot, hh):
        st = st_buf[slot, hh]
        m_prev = m_ref[hh]
        m_new = jnp.maximum(m_prev, mx_buf[slot, hh])
        alpha = jnp.exp2(m_prev - m_new)
        p = jnp.exp2(st - m_new)
        vt = vt_ref[0, j, hh * V_ROWS:(hh + 1) * V_ROWS, :]
        acc_ref[hh] = alpha * acc_ref[hh] + _dot(vt, p.astype(jnp.bfloat16))
        m_ref[hh] = m_new

    assert heads == 2
    logits(i, 0, 0, True)
    logits(i, 0, 1, True)
    accumulate(i, 0, 0)

    def before(t):
        return jnp.where(t == j0, i, t - 1)

    def step(t, slot):
        logits(t, slot, 0, False)
        accumulate(before(t), 1 - slot, 1)
        logits(t, slot, 1, False)
        accumulate(t, slot, 0)

    def body(u, carry):
        step(2 * u, 1)
        step(2 * u + 1, 0)
        return carry

    lax.fori_loop(j0 // 2, i // 2, body, 0)

    @pl.when(i % 2 == 1)
    def _():
        step(i - 1, 1)
        accumulate(i - 1, 1, 1)

    @pl.when(i % 2 == 0)
    def _():
        accumulate(before(i), 0, 1)

    outs = [acc_ref[hh, :HEAD_DIM] / acc_ref[hh, HEAD_DIM:HEAD_DIM + 1] for hh in range(heads)]
    ot = jnp.concatenate(outs, axis=0)
    o_ref[0] = ot.T.astype(o_ref.dtype)


def _first_needed_chunk(stats):
    kn = jnp.sqrt(stats[:, :, 0, :N_HEADS])
    qn = jnp.sqrt(stats[:, :, 1, :N_HEADS])
    c_first = stats[:, :, 2, 0:N_SPLIT * N_HEADS:N_SPLIT]
    c_last = stats[:, :, 3, 0:N_SPLIT * N_HEADS:N_SPLIT]
    n = stats.shape[1]
    dot_bound = NORM_SLACK * qn[:, :, None, :] * kn[:, None, :, :] + 1.0
    upper = dot_bound + c_first[:, :, None, :] - c_last[:, None, :, :]
    self_bound = NORM_SLACK * qn * kn + 1.0
    ii = jnp.arange(n, dtype=jnp.int32)[None, :, None, None]
    jj = jnp.arange(n, dtype=jnp.int32)[None, None, :, None]
    needed = (upper + self_bound[:, :, None, :] >= -SKIP_MARGIN) & (jj < ii)
    first = jnp.min(jnp.where(needed, jj, ii), axis=2)
    first = jnp.min(first.reshape(first.shape[0], n, N_HEADS // 2, 2), axis=3)
    first = first // 2 * 2
    return jnp.transpose(first, (0, 2, 1)).reshape(-1).astype(jnp.int32)


def _attention(qt, k, vt, first_chunk):
    B, hs, S = qt.shape
    tq, tk = Q_TILE, KV_TILE
    assert tq == tk, "the diagonal is handled as a single masked key chunk"
    pair = 2
    n_pairs = N_HEADS // pair
    grid_spec = pltpu.PrefetchScalarGridSpec(
        num_scalar_prefetch=1,
        grid=(B, n_pairs, S // tq),
        in_specs=[
            pl.BlockSpec((1, pair * HEAD_SLOT, tq), lambda b, p, i, j0: (b, p, i)),
            pl.BlockSpec((1, S, pair * HEAD_SLOT), lambda b, p, i, j0: (b, 0, p)),
            pl.BlockSpec((1, S // tk, pair * V_ROWS, tk), lambda b, p, i, j0: (b, 0, p, 0)),
        ],
        out_specs=pl.BlockSpec((1, tq, pair * HEAD_DIM), lambda b, p, i, j0: (b, i, p)),
        scratch_shapes=[
            pltpu.VMEM((2, pair, tk, tq), jnp.float32),
            pltpu.VMEM((2, pair, 1, tq), jnp.float32),
            pltpu.VMEM((pair, 1, tq), jnp.float32),
            pltpu.VMEM((pair, V_ROWS, tq), jnp.float32),
        ],
    )
    return pl.pallas_call(
        _attn_kernel,
        grid_spec=grid_spec,
        out_shape=jax.ShapeDtypeStruct((B, S, ATTN_WIDTH), jnp.bfloat16),
        compiler_params=pltpu.CompilerParams(
            dimension_semantics=("arbitrary", "arbitrary", "arbitrary"),
            vmem_limit_bytes=VMEM_LIMIT),
        name="forgetting_attention",
    )(first_chunk, qt, k, vt)


def _outproj_kernel(yp_ref, ya_ref, x_ref, wo1_ref, wo2_ref, g_ref, wr2_ref, wrhi_ref, br_ref,
                    upper_ref, x1_ref, h2_ref, idx_ref, gate_ref, rank_ref, cnt_ref, carry):
    t = pl.program_id(0)
    tm = x_ref.shape[0]

    @pl.when(t == 0)
    def _():
        carry[...] = jnp.zeros_like(carry)

    x1 = x_ref[...] + _dot(yp_ref[...], wo1_ref[...]) + _dot(ya_ref[...], wo2_ref[...])
    x1_ref[...] = x1
    h2 = _rms_norm(x1, g_ref[...])
    h2_hi = h2.astype(jnp.bfloat16)
    h2_lo = (h2 - h2_hi.astype(jnp.float32)).astype(jnp.bfloat16)
    h2_ref[...] = _pack_bf16_pairs(h2)

    lg2 = _dot_nt(wr2_ref[...], h2_hi)
    logits = (lg2[:N_EXPERTS] + lg2[N_EXPERTS:] + _dot_nt(wrhi_ref[...], h2_lo) + br_ref[...])

    eio = lax.broadcasted_iota(jnp.int32, (N_EXPERTS, tm), 0)
    vals, idxs = [], []
    cur = logits
    for _ in range(TOP_K):
        m = jnp.max(cur, axis=0, keepdims=True)
        ix = jnp.min(jnp.where(cur == m, eio, N_EXPERTS), axis=0, keepdims=True)
        vals.append(m)
        idxs.append(ix)
        cur = jnp.where(eio == ix, -jnp.inf, cur)
    exps = [jnp.exp(v - vals[0]) for v in vals]
    denom = exps[0] + exps[1] + exps[2] + exps[3]
    hits = [eio == ix for ix in idxs]
    cnt = jnp.zeros((N_EXPERTS, tm), jnp.float32)
    for hit in hits:
        cnt = cnt + hit.astype(jnp.float32)
    base = _dot(cnt.astype(jnp.bfloat16), upper_ref[...]) + carry[...]
    for kk in range(TOP_K):
        idx_ref[kk:kk + 1, :] = idxs[kk]
        gate_ref[kk:kk + 1, :] = exps[kk] / denom
        rank = jnp.sum(jnp.where(hits[kk], base, 0.0), axis=0, keepdims=True)
        rank_ref[kk:kk + 1, :] = rank.astype(jnp.int32)
    carry[...] = carry[...] + jnp.sum(cnt, axis=1, keepdims=True)
    cnt_ref[...] = carry[...]


def _out_projection(ypool, yattn, x, w_out, g_ffn, w_router, b_router):
    T, D = x.shape
    tm = SEQ_TILE
    bf16 = jnp.bfloat16
    wo = w_out.astype(bf16)
    wr_t = w_router.T
    wr_hi = wr_t.astype(bf16)
    wr_lo = (wr_t - wr_hi.astype(jnp.float32)).astype(bf16)
    wr2 = jnp.concatenate([wr_hi, wr_lo], axis=0)
    r = jnp.arange(tm)
    upper = (r[:, None] < r[None, :]).astype(bf16)
    full = lambda *shape: pl.BlockSpec(shape, lambda t: (0,) * len(shape))
    row = lambda w: pl.BlockSpec((tm, w), lambda t: (t, 0))
    col = pl.BlockSpec((TOP_K, tm), lambda t: (0, t))
    return pl.pallas_call(
        _outproj_kernel,
        grid=(T // tm,),
        in_specs=[row(POOL_WIDTH), row(ATTN_WIDTH), row(D), full(POOL_WIDTH, D),
                  full(ATTN_WIDTH, D), full(1, D), full(2 * N_EXPERTS, D), full(N_EXPERTS, D),
                  full(N_EXPERTS, 1), full(tm, tm)],
        out_specs=[row(D), row(D // 2), col, col, col, full(N_EXPERTS, 1)],
        out_shape=[
            jax.ShapeDtypeStruct((T, D), jnp.float32),
            jax.ShapeDtypeStruct((T, D // 2), jnp.uint32),
            jax.ShapeDtypeStruct((TOP_K, T), jnp.int32),
            jax.ShapeDtypeStruct((TOP_K, T), jnp.float32),
            jax.ShapeDtypeStruct((TOP_K, T), jnp.int32),
            jax.ShapeDtypeStruct((N_EXPERTS, 1), jnp.float32),
        ],
        scratch_shapes=[pltpu.VMEM((N_EXPERTS, 1), jnp.float32)],
        compiler_params=pltpu.CompilerParams(
            dimension_semantics=("arbitrary",), vmem_limit_bytes=VMEM_LIMIT),
        name="out_projection_router",
    )(ypool, yattn, x, wo[:POOL_WIDTH], wo[POOL_WIDTH:], g_ffn.reshape(1, D), wr2, wr_hi,
      b_router.reshape(N_EXPERTS, 1), upper)


def _sc_mesh():
    return plsc.VectorSubcoreMesh(core_axis_name="c", subcore_axis_name="s")


def _dispatch_rows(h2, dest, n_rows):
    T, D = h2.shape
    per_worker = T // SC_WORKERS
    steps = per_worker // SC_WINDOW

    @functools.partial(
        pl.kernel, mesh=_sc_mesh(),
        out_type=jax.ShapeDtypeStruct((n_rows, D), h2.dtype),
        scratch_types=[pltpu.VMEM((TOP_K, SC_WINDOW), jnp.int32),
                       pltpu.VMEM((SC_WINDOW, D), h2.dtype),
                       pltpu.SemaphoreType.DMA],
    )
    def scatter_kernel(h2_hbm, dest_hbm, xs_hbm, idx_v, rows_v, sem):
        wid = lax.axis_index("s") * SC_CORES + lax.axis_index("c")

        @pl.loop(0, steps)
        def _(step):
            base = wid * per_worker + step * SC_WINDOW
            pltpu.sync_copy(h2_hbm.at[pl.ds(base, SC_WINDOW)], rows_v)
            pltpu.sync_copy(dest_hbm.at[:, pl.ds(base, SC_WINDOW)], idx_v)
            copies = [pltpu.async_copy(rows_v, xs_hbm.at[idx_v.at[kk]], sem)
                      for kk in range(TOP_K)]
            for c in copies:
                c.wait()

    return scatter_kernel(h2, dest)


def _gather_rows(table, idx):
    N = idx.shape[0]
    D = table.shape[1]
    per_worker = N // SC_WORKERS
    steps = per_worker // SC_WINDOW

    @functools.partial(
        pl.kernel, mesh=_sc_mesh(),
        out_type=jax.ShapeDtypeStruct((N, D), table.dtype),
        scratch_types=[pltpu.VMEM((SC_WINDOW,), jnp.int32),
                       pltpu.VMEM((SC_WINDOW, D), table.dtype),
                       pltpu.SemaphoreType.DMA],
    )
    def gather_kernel(table_hbm, idx_hbm, out_hbm, idx_v, rows_v, sem):
        wid = lax.axis_index("s") * SC_CORES + lax.axis_index("c")

        @pl.loop(0, steps)
        def _(step):
            base = wid * per_worker + step * SC_WINDOW
            pltpu.sync_copy(idx_hbm.at[pl.ds(base, SC_WINDOW)], idx_v)
            pltpu.async_copy(table_hbm.at[idx_v], rows_v, sem).wait()
            pltpu.sync_copy(rows_v, out_hbm.at[pl.ds(base, SC_WINDOW)])

    return gather_kernel(table, idx)


def _expert_kernel(be_ref, nvalid_ref, x_ref, wup_ref, bup_ref, wdn_ref, bdn_ref, y_ref,
                   wup_bf, wdn_bf):
    i = pl.program_id(0)
    blk = x_ref.shape[0]
    changed = jnp.logical_or(i == 0, be_ref[i] != be_ref[jnp.maximum(i - 1, 0)])

    @pl.when(changed)
    def _():
        wup_bf[...] = wup_ref[0].astype(jnp.bfloat16)
        wdn_bf[...] = wdn_ref[0].astype(jnp.bfloat16)

    nvalid = nvalid_ref[i]

    @pl.when(nvalid > 0)
    def _():
        row = lax.broadcasted_iota(jnp.int32, (blk, 1), 0)
        x_hi, x_lo = _unpack_bf16_pairs(jnp.where(row < nvalid, x_ref[...], jnp.uint32(0)))
        x = jnp.concatenate([x_hi.astype(jnp.bfloat16), x_lo.astype(jnp.bfloat16)], axis=1)
        glu = _dot(x, wup_bf[:, :D_EXPERT]) + bup_ref[0, :, :D_EXPERT]
        lin = _dot(x, wup_bf[:, D_EXPERT:]) + bup_ref[0, :, D_EXPERT:]
        glu = jnp.minimum(glu, SWIGLU_LIMIT)
        lin = jnp.clip(lin, -SWIGLU_LIMIT, SWIGLU_LIMIT)
        act = glu * (1.0 / (1.0 + jnp.exp(-SWIGLU_ALPHA * glu))) * (lin + 1.0)
        y = _dot(act.astype(jnp.bfloat16), wdn_bf[...]) + bdn_ref[0]
        y_ref[...] = _pack_bf16_pairs(y)

    @pl.when(nvalid <= 0)
    def _():
        y_ref[...] = jnp.zeros_like(y_ref)


def _expert_ffn(xs, block_e, block_valid, w_up, b_up, w_down, b_down):
    n_rows = xs.shape[0]
    D = D_MODEL
    blk = EXPERT_ROWS
    grid_spec = pltpu.PrefetchScalarGridSpec(
        num_scalar_prefetch=2,
        grid=(n_rows // blk,),
        in_specs=[
            pl.BlockSpec((blk, D // 2), lambda i, be, nv: (i, 0)),
            pl.BlockSpec((1, D, 2 * D_EXPERT), lambda i, be, nv: (be[i], 0, 0)),
            pl.BlockSpec((1, 1, 2 * D_EXPERT), lambda i, be, nv: (be[i], 0, 0)),
            pl.BlockSpec((1, D_EXPERT, D), lambda i, be, nv: (be[i], 0, 0)),
            pl.BlockSpec((1, 1, D), lambda i, be, nv: (be[i], 0, 0)),
        ],
        out_specs=pl.BlockSpec((blk, D // 2), lambda i, be, nv: (i, 0)),
        scratch_shapes=[pltpu.VMEM((D, 2 * D_EXPERT), jnp.bfloat16),
                        pltpu.VMEM((D_EXPERT, D), jnp.bfloat16)],
    )
    return pl.pallas_call(
        _expert_kernel,
        grid_spec=grid_spec,
        out_shape=jax.ShapeDtypeStruct((n_rows, D // 2), jnp.uint32),
        compiler_params=pltpu.CompilerParams(
            dimension_semantics=("arbitrary",), vmem_limit_bytes=VMEM_LIMIT),
        name="expert_ffn",
    )(block_e, block_valid, xs, w_up, b_up.reshape(N_EXPERTS, 1, 2 * D_EXPERT), w_down,
      b_down.reshape(N_EXPERTS, 1, D))


def _final_kernel(x1_ref, rows_ref, gate_ref, g_ref, o_ref):
    half = x1_ref.shape[1] // 2
    acc_hi = x1_ref[:, :half]
    acc_lo = x1_ref[:, half:]
    for kk in range(TOP_K):
        y_hi, y_lo = _unpack_bf16_pairs(rows_ref[kk])
        gate = gate_ref[:, kk:kk + 1]
        acc_hi = acc_hi + y_hi * gate
        acc_lo = acc_lo + y_lo * gate
    o_ref[...] = _rms_norm(jnp.concatenate([acc_hi, acc_lo], axis=1), g_ref[...])


def _final_norm(x1, rows, gates_t, g_final):
    T, D = x1.shape
    tm = SEQ_TILE
    row = pl.BlockSpec((tm, D), lambda t: (t, 0))
    return pl.pallas_call(
        _final_kernel,
        grid=(T // tm,),
        in_specs=[row, pl.BlockSpec((TOP_K, tm, D // 2), lambda t: (0, t, 0)),
                  pl.BlockSpec((tm, TOP_K), lambda t: (t, 0)),
                  pl.BlockSpec((1, D), lambda t: (0, 0))],
        out_specs=row,
        out_shape=jax.ShapeDtypeStruct((T, D), jnp.float32),
        compiler_params=pltpu.CompilerParams(dimension_semantics=("arbitrary",)),
        name="combine_final_norm",
    )(x1, rows, gates_t, g_final.reshape(1, D))


def kernel(x, g_mix, w_in, b_forget, w_pool, pool_scale, w_out, g_ffn, w_router, b_router,
           w_up, b_up, w_down, b_down, g_final):
    B, S, D = x.shape
    T = B * S
    ypool, qt, k, vt, stats = _in_projection(x, g_mix[0], w_in[0], b_forget[0], w_pool[0],
                                            pool_scale[0])
    yattn = _attention(qt, k, vt, _first_needed_chunk(stats))
    x1, h2, top_idx, gates, rank, counts = _out_projection(
        ypool.reshape(T, POOL_WIDTH), yattn.reshape(T, ATTN_WIDTH), x.reshape(T, D),
        w_out[0], g_ffn[0], w_router[0], b_router[0])

    blk = EXPERT_ROWS
    counts = counts[:, 0].astype(jnp.int32)
    padded = (counts + blk - 1) // blk * blk
    pad_ends = jnp.cumsum(padded)
    pad_starts = pad_ends - padded
    dest = rank
    for e in range(N_EXPERTS):
        dest = dest + jnp.where(top_idx == e, pad_starts[e], 0)
    n_rows = T * TOP_K + N_EXPERTS * blk
    block_row0 = jnp.arange(n_rows // blk, dtype=jnp.int32) * blk
    block_e = jnp.sum((pad_ends[None, :] <= block_row0[:, None]).astype(jnp.int32), axis=1)
    block_e = jnp.minimum(block_e, N_EXPERTS - 1)
    of_block = block_e[:, None] == jnp.arange(N_EXPERTS, dtype=jnp.int32)[None, :]
    block_end = jnp.sum(jnp.where(of_block, (pad_starts + counts)[None, :], 0), axis=1)
    block_valid = jnp.clip(block_end - block_row0, 0, blk)
    block_valid = jnp.where(block_row0 < pad_ends[-1], block_valid, 0).astype(jnp.int32)

    xs = _dispatch_rows(h2, dest, n_rows)
    ys = _expert_ffn(xs, block_e, block_valid, w_up[0], b_up[0], w_down[0], b_down[0])
    rows = _gather_rows(ys, dest.reshape(-1)).reshape(TOP_K, T, D // 2)
    out = _final_norm(x1, rows, gates.T, g_final)
    return out.reshape(B, S, D)
```

```python
import functools

import jax
import jax.numpy as jnp
from jax import lax
from jax.experimental import pallas as pl
from jax.experimental.pallas import tpu as pltpu
from jax.experimental.pallas import tpu_sc as plsc

D_MODEL = 1024
POOL_WIDTH = 512
POOL_WINDOWS = (2, 4, 8, 16)
POOL_GROUP = 128
MAX_WINDOW = 16
ATTN_WIDTH = 512
HEAD_DIM = 64
N_HEADS = 8
N_EXPERTS = 32
TOP_K = 4
D_EXPERT = 1024
SWIGLU_LIMIT = 7.0
SWIGLU_ALPHA = 1.702
RMS_EPS = 1e-5

LANES = 128
AUG_GROUP = 16
N_SPLIT = 3
MASK_VALUE = -1e30
LOG2_E = 1.4426950408889634
SKIP_MARGIN = 160.0
NORM_SLACK = 1.02
V_ROWS = 80

SEQ_TILE = 512
Q_TILE = 512
KV_TILE = SEQ_TILE
EXPERT_ROWS = 512
SC_CORES = 2
SC_SUBCORES = 16
SC_WORKERS = SC_CORES * SC_SUBCORES
SC_WINDOW = 128
VMEM_LIMIT = 56 * 1024 * 1024

_NT = (((1,), (1,)), ((), ()))


def _dot(a, b):
    return jnp.dot(a, b, preferred_element_type=jnp.float32)


def _dot_nt(a, b):
    return lax.dot_general(a, b, _NT, preferred_element_type=jnp.float32)


def _rms_norm(x, g):
    return x * lax.rsqrt(jnp.mean(x * x, axis=-1, keepdims=True) + RMS_EPS) * g


def _pack_bf16_pairs(x):
    n = x.shape[1] // 2
    bits = lax.bitcast_convert_type(x.astype(jnp.bfloat16).astype(jnp.float32), jnp.uint32)
    return bits[:, :n] | (bits[:, n:] >> 16)


def _unpack_bf16_pairs(p):
    hi = lax.bitcast_convert_type(p & jnp.uint32(0xFFFF0000), jnp.float32)
    lo = lax.bitcast_convert_type(p << 16, jnp.float32)
    return hi, lo


def _split3(x):
    hi = x.astype(jnp.bfloat16)
    r1 = x - hi.astype(jnp.float32)
    mid = r1.astype(jnp.bfloat16)
    lo = (r1 - mid.astype(jnp.float32)).astype(jnp.bfloat16)
    return hi, mid, lo


def _inproj_kernel(x_ref, g_ref, wu_ref, wqt_ref, wk_ref, wvt_ref, wf_ref, bf_ref, tri_ref,
                   pqt_ref, pk_ref, oneq_ref, onek_ref, hsel_ref, wpool_ref, pscale_ref,
                   ypool_ref, qt_ref, qaug_ref, k_ref, kaug_ref, vt_ref, stats_ref, ubuf, ccarry):
    s = pl.program_id(1)
    tm = x_ref.shape[1]

    @pl.when(s == 0)
    def _():
        ubuf[...] = jnp.zeros_like(ubuf)
        ccarry[...] = jnp.zeros_like(ccarry)

    x = x_ref[0]
    h = _rms_norm(x, g_ref[...]).astype(jnp.bfloat16)

    fl = _dot(h, wf_ref[...]) + bf_ref[...]
    logf = jnp.minimum(fl, 0.0) - jnp.log1p(jnp.exp(-jnp.abs(fl)))
    lane = lax.broadcasted_iota(jnp.int32, (tm, LANES), 1)
    logf = jnp.where(lane < N_SPLIT * N_HEADS, logf, 0.0)
    f_hi, f_mid, f_lo = _split3(logf)
    tri = tri_ref[...]
    c = _dot(tri, f_hi) + _dot(tri, f_mid) + _dot(tri, f_lo) + ccarry[...]
    ccarry[...] = c[tm - 1:tm, :]
    c2 = c * LOG2_E
    c_hi, c_mid, c_lo = _split3(c2)
    term = lane % N_SPLIT
    c3 = jnp.where(term == 0, c_hi, jnp.where(term == 1, c_mid, c_lo))

    kp = _dot(h, wk_ref[...])
    k_ref[0] = kp.astype(jnp.bfloat16)
    kaug_ref[0] = (_dot(c3, pk_ref[...]) + onek_ref[...]).astype(jnp.bfloat16)
    qt = _dot_nt(wqt_ref[...], h)
    qt_ref[0] = qt.astype(jnp.bfloat16)
    qaug_ref[0] = (_dot_nt(pqt_ref[...], c3) + oneq_ref[...]).astype(jnp.bfloat16)
    lane_row = lax.broadcasted_iota(jnp.int32, (1, LANES), 1)
    kn2 = jnp.max(_dot((kp * kp).astype(jnp.bfloat16), hsel_ref[...]), axis=0, keepdims=True)
    qn2 = jnp.zeros((1, LANES), jnp.float32)
    for hd in range(N_HEADS):
        qs = qt[hd * HEAD_DIM:(hd + 1) * HEAD_DIM, :]
        qmax = jnp.max(jnp.sum(qs * qs, axis=0, keepdims=True), axis=1, keepdims=True)
        qn2 = jnp.where(lane_row == hd, qmax, qn2)
    stats_ref[0, 0, 0:1, :] = kn2
    stats_ref[0, 0, 1:2, :] = qn2
    stats_ref[0, 0, 2:3, :] = c2[0:1, :]
    stats_ref[0, 0, 3:4, :] = c2[tm - 1:tm, :]
    stats_ref[0, 0, 4:8, :] = jnp.zeros((4, LANES), jnp.float32)
    vt = _dot_nt(wvt_ref[...], h).astype(jnp.bfloat16)
    ones = jnp.ones((V_ROWS - HEAD_DIM, tm), jnp.bfloat16)
    for hd in range(N_HEADS):
        vt_ref[0, 0, hd * V_ROWS:hd * V_ROWS + HEAD_DIM, :] = vt[hd * HEAD_DIM:(hd + 1) * HEAD_DIM]
        vt_ref[0, 0, hd * V_ROWS + HEAD_DIM:(hd + 1) * V_ROWS, :] = ones

    u = _dot(h, wu_ref[...])
    head_pos = s * tm + lax.broadcasted_iota(jnp.int32, (MAX_WINDOW, 1), 0)
    for gi, w in enumerate(POOL_WINDOWS):
        cols = slice(gi * POOL_GROUP, (gi + 1) * POOL_GROUP)
        ug = u[:, cols]
        win = jnp.concatenate([ubuf[:, cols], ug], axis=0)
        span = 1
        while span < w:
            win = win + pltpu.roll(win, span, axis=0)
            span *= 2
        acc = win[MAX_WINDOW:]
        inv_head = 1.0 / jnp.minimum(head_pos + 1, w).astype(jnp.float32)
        p = jnp.concatenate([acc[:MAX_WINDOW] * inv_head, acc[MAX_WINDOW:] * (1.0 / w)],
                            axis=0) - ug
        y = _dot(p.astype(jnp.bfloat16), wpool_ref[gi]) * pscale_ref[:, cols]
        ypool_ref[0, :, cols] = y.astype(jnp.bfloat16)
    ubuf[...] = u[tm - MAX_WINDOW:]


def _in_projection(x, g_mix, w_in, b_forget, w_pool, pool_scale):
    B, S, D = x.shape
    tm = SEQ_TILE
    bf16 = jnp.bfloat16
    o = POOL_WIDTH
    w_u = w_in[:, :o].astype(bf16)
    w_q = w_in[:, o:o + ATTN_WIDTH] * (HEAD_DIM ** -0.5 * LOG2_E)
    w_k = w_in[:, o + ATTN_WIDTH:o + 2 * ATTN_WIDTH]
    w_v = w_in[:, o + 2 * ATTN_WIDTH:o + 3 * ATTN_WIDTH]
    w_f = w_in[:, o + 3 * ATTN_WIDTH:]

    wqt = w_q.T.astype(bf16)
    wk = w_k.astype(bf16)
    wvt = w_v.T.astype(bf16)
    n_gate = N_SPLIT * N_HEADS
    wf = jnp.pad(jnp.repeat(w_f, N_SPLIT, axis=1), ((0, 0), (0, LANES - n_gate))).astype(bf16)
    bfp = jnp.pad(jnp.repeat(b_forget, N_SPLIT), (0, LANES - n_gate)).reshape(1, LANES)

    assert N_HEADS * AUG_GROUP == LANES and 2 * N_SPLIT <= AUG_GROUP
    src = jnp.arange(LANES)
    head, term = src // N_SPLIT, src % N_SPLIT
    dst = jnp.arange(LANES)
    valid = (src < n_gate)[:, None]
    q_hit = valid & (dst[None, :] == (head * AUG_GROUP + term)[:, None])
    k_hit = valid & (dst[None, :] == (head * AUG_GROUP + N_SPLIT + term)[:, None])
    pqt = q_hit.T.astype(bf16)
    pk = -k_hit.astype(bf16)
    slot_pos = dst % AUG_GROUP
    oneq = ((slot_pos >= N_SPLIT) & (slot_pos < 2 * N_SPLIT))
    onek = slot_pos < N_SPLIT
    oneq = oneq.astype(jnp.float32).reshape(-1, 1)
    onek = onek.astype(jnp.float32).reshape(1, -1)
    r = jnp.arange(tm)
    tri = (r[None, :] <= r[:, None]).astype(bf16)
    hsel = (jnp.arange(ATTN_WIDTH)[:, None] // HEAD_DIM == jnp.arange(LANES)[None, :]).astype(bf16)

    full = lambda *shape: pl.BlockSpec(shape, lambda b, s: (0,) * len(shape))
    aw = ATTN_WIDTH
    return pl.pallas_call(
        _inproj_kernel,
        grid=(B, S // tm),
        in_specs=[
            pl.BlockSpec((1, tm, D), lambda b, s: (b, s, 0)),
            full(1, D), full(D, o), full(aw, D), full(D, aw), full(aw, D),
            full(D, LANES), full(1, LANES), full(tm, tm), full(LANES, LANES), full(LANES, LANES),
            full(LANES, 1), full(1, LANES), full(aw, LANES),
            full(len(POOL_WINDOWS), POOL_GROUP, POOL_GROUP),
            full(1, o),
        ],
        out_specs=[
            pl.BlockSpec((1, tm, o), lambda b, s: (b, s, 0)),
            pl.BlockSpec((1, aw, tm), lambda b, s: (b, 0, s)),
            pl.BlockSpec((1, LANES, tm), lambda b, s: (b, 0, s)),
            pl.BlockSpec((1, tm, aw), lambda b, s: (b, s, 0)),
            pl.BlockSpec((1, tm, LANES), lambda b, s: (b, s, 0)),
            pl.BlockSpec((1, 1, N_HEADS * V_ROWS, tm), lambda b, s: (b, s, 0, 0)),
            pl.BlockSpec((1, 1, 8, LANES), lambda b, s: (b, s, 0, 0)),
        ],
        out_shape=[
            jax.ShapeDtypeStruct((B, S, o), bf16),
            jax.ShapeDtypeStruct((B, aw, S), bf16),
            jax.ShapeDtypeStruct((B, LANES, S), bf16),
            jax.ShapeDtypeStruct((B, S, aw), bf16),
            jax.ShapeDtypeStruct((B, S, LANES), bf16),
            jax.ShapeDtypeStruct((B, S // tm, N_HEADS * V_ROWS, tm), bf16),
            jax.ShapeDtypeStruct((B, S // tm, 8, LANES), jnp.float32),
        ],
        scratch_shapes=[
            pltpu.VMEM((MAX_WINDOW, o), jnp.float32),
            pltpu.VMEM((1, LANES), jnp.float32),
        ],
        compiler_params=pltpu.CompilerParams(
            dimension_semantics=("arbitrary", "arbitrary"), vmem_limit_bytes=VMEM_LIMIT),
        name="in_projection",
    )(x, g_mix.reshape(1, D), w_u, wqt, wk, wvt, wf, bfp, tri, pqt, pk, oneq, onek, hsel,
      w_pool.astype(bf16), pool_scale.reshape(1, o))


def _attn_kernel(j0_ref, qt_ref, qaug_ref, k_ref, kaug_ref, vt_ref, causal_ref, o_ref,
                 qop_ref, st_buf, mx_buf, m_ref, acc_ref):
    i = pl.program_id(2)
    tile = (pl.program_id(0) * pl.num_programs(1) + pl.program_id(1)) * pl.num_programs(2) + i
    j0 = j0_ref[tile]
    tq = qt_ref.shape[2]
    tk = vt_ref.shape[3]
    heads = qt_ref.shape[1] // HEAD_DIM

    m_ref[...] = jnp.full(m_ref.shape, MASK_VALUE, jnp.float32)
    acc_ref[...] = jnp.zeros_like(acc_ref)

    assert heads == 2
    first_head = pl.program_id(1) * heads
    group = lax.broadcasted_iota(jnp.int32, (LANES, tq), 0) // AUG_GROUP
    q_pair = qt_ref[0]
    q_aug = qaug_ref[0]
    blank = jnp.zeros((HEAD_DIM, tq), q_pair.dtype)
    qop_ref[0, 0:2 * HEAD_DIM, :] = jnp.concatenate([q_pair[:HEAD_DIM], blank], axis=0)
    qop_ref[1, 0:2 * HEAD_DIM, :] = jnp.concatenate([blank, q_pair[HEAD_DIM:]], axis=0)
    for hh in range(heads):
        qop_ref[hh, 2 * HEAD_DIM:, :] = jnp.where(group == first_head + hh, q_aug,
                                                  jnp.zeros_like(q_aug))

    def logits(j, slot, hh, masked):
        row0 = pl.multiple_of(j * tk, tk)
        k = jnp.concatenate([k_ref[0, pl.ds(row0, tk), :], kaug_ref[0, pl.ds(row0, tk), :]],
                            axis=1)
        st = _dot(k, qop_ref[hh])
        if masked:
            st = st + causal_ref[...]
        st_buf[slot, hh] = st
        mx_buf[slot, hh] = jnp.max(st, axis=0, keepdims=True)

    def accumulate(j, slot, hh):
        st = st_buf[slot, hh]
        m_prev = m_ref[hh]
        m_new = jnp.maximum(m_prev, mx_buf[slot, hh])
        alpha = jnp.exp2(m_prev - m_new)
        p = jnp.exp2(st - m_new)
        vt = vt_ref[0, j, hh * V_ROWS:(hh + 1) * V_ROWS, :]
        acc_ref[hh] = alpha * acc_ref[hh] + _dot(vt, p.astype(jnp.bfloat16))
        m_ref[hh] = m_new

    logits(i, 0, 0, True)
    logits(i, 0, 1, True)
    accumulate(i, 0, 0)

    def before(t):
        return jnp.where(t == j0, i, t - 1)

    def step(t, slot):
        logits(t, slot, 0, False)
        accumulate(before(t), 1 - slot, 1)
        logits(t, slot, 1, False)
        accumulate(t, slot, 0)

    def body(u, carry):
        step(2 * u, 1)
        step(2 * u + 1, 0)
        return carry

    lax.fori_loop(j0 // 2, i // 2, body, 0)

    @pl.when(i % 2 == 1)
    def _():
        step(i - 1, 1)
        accumulate(i - 1, 1, 1)

    @pl.when(i % 2 == 0)
    def _():
        accumulate(before(i), 0, 1)

    outs = [acc_ref[hh, :HEAD_DIM] / acc_ref[hh, HEAD_DIM:HEAD_DIM + 1] for hh in range(heads)]
    ot = jnp.concatenate(outs, axis=0)
    o_ref[0] = ot.T.astype(o_ref.dtype)


def _first_needed_chunk(stats):
    kn = jnp.sqrt(stats[:, :, 0, :N_HEADS])
    qn = jnp.sqrt(stats[:, :, 1, :N_HEADS])
    c_first = stats[:, :, 2, 0:N_SPLIT * N_HEADS:N_SPLIT]
    c_last = stats[:, :, 3, 0:N_SPLIT * N_HEADS:N_SPLIT]
    n = stats.shape[1]
    dot_bound = NORM_SLACK * qn[:, :, None, :] * kn[:, None, :, :] + 1.0
    upper = dot_bound + c_first[:, :, None, :] - c_last[:, None, :, :]
    self_bound = NORM_SLACK * qn * kn + 1.0
    ii = jnp.arange(n, dtype=jnp.int32)[None, :, None, None]
    jj = jnp.arange(n, dtype=jnp.int32)[None, None, :, None]
    needed = (upper + self_bound[:, :, None, :] >= -SKIP_MARGIN) & (jj < ii)
    first = jnp.min(jnp.where(needed, jj, ii), axis=2)
    first = jnp.min(first.reshape(first.shape[0], n, N_HEADS // 2, 2), axis=3)
    first = first // 2 * 2
    return jnp.transpose(first, (0, 2, 1)).reshape(-1).astype(jnp.int32)


def _attention(qt, qaug, k, kaug, vt, first_chunk):
    B, _, S = qt.shape
    tq, tk = Q_TILE, KV_TILE
    assert tq == tk, "the diagonal is handled as a single masked key chunk"
    pair = 2
    n_pairs = N_HEADS // pair
    pw = pair * HEAD_DIM
    causal = jnp.where(jnp.arange(tk)[:, None] <= jnp.arange(tq)[None, :], 0.0, MASK_VALUE)
    grid_spec = pltpu.PrefetchScalarGridSpec(
        num_scalar_prefetch=1,
        grid=(B, n_pairs, S // tq),
        in_specs=[
            pl.BlockSpec((1, pw, tq), lambda b, p, i, j0: (b, p, i)),
            pl.BlockSpec((1, LANES, tq), lambda b, p, i, j0: (b, 0, i)),
            pl.BlockSpec((1, S, pw), lambda b, p, i, j0: (b, 0, p)),
            pl.BlockSpec((1, S, LANES), lambda b, p, i, j0: (b, 0, 0)),
            pl.BlockSpec((1, S // tk, pair * V_ROWS, tk), lambda b, p, i, j0: (b, 0, p, 0)),
            pl.BlockSpec((tk, tq), lambda b, p, i, j0: (0, 0)),
        ],
        out_specs=pl.BlockSpec((1, tq, pw), lambda b, p, i, j0: (b, i, p)),
        scratch_shapes=[
            pltpu.VMEM((pair, pw + LANES, tq), jnp.bfloat16),
            pltpu.VMEM((2, pair, tk, tq), jnp.float32),
            pltpu.VMEM((2, pair, 1, tq), jnp.float32),
            pltpu.VMEM((pair, 1, tq), jnp.float32),
            pltpu.VMEM((pair, V_ROWS, tq), jnp.float32),
        ],
    )
    return pl.pallas_call(
        _attn_kernel,
        grid_spec=grid_spec,
        out_shape=jax.ShapeDtypeStruct((B, S, ATTN_WIDTH), jnp.bfloat16),
        compiler_params=pltpu.CompilerParams(
            dimension_semantics=("arbitrary", "arbitrary", "arbitrary"),
            vmem_limit_bytes=VMEM_LIMIT),
        name="forgetting_attention",
    )(first_chunk, qt, qaug, k, kaug, vt, causal.astype(jnp.float32))


def _outproj_kernel(yp_ref, ya_ref, x_ref, wo1_ref, wo2_ref, g_ref, wr2_ref, wrhi_ref, br_ref,
                    upper_ref, x1_ref, h2_ref, idx_ref, gate_ref, rank_ref, cnt_ref, carry):
    t = pl.program_id(0)
    tm = x_ref.shape[0]

    @pl.when(t == 0)
    def _():
        carry[...] = jnp.zeros_like(carry)

    x1 = x_ref[...] + _dot(yp_ref[...], wo1_ref[...]) + _dot(ya_ref[...], wo2_ref[...])
    x1_ref[...] = x1
    h2 = _rms_norm(x1, g_ref[...])
    h2_hi = h2.astype(jnp.bfloat16)
    h2_lo = (h2 - h2_hi.astype(jnp.float32)).astype(jnp.bfloat16)
    h2_ref[...] = _pack_bf16_pairs(h2)

    lg2 = _dot_nt(wr2_ref[...], h2_hi)
    logits = (lg2[:N_EXPERTS] + lg2[N_EXPERTS:] + _dot_nt(wrhi_ref[...], h2_lo) + br_ref[...])

    eio = lax.broadcasted_iota(jnp.int32, (N_EXPERTS, tm), 0)
    vals, idxs = [], []
    cur = logits
    for _ in range(TOP_K):
        m = jnp.max(cur, axis=0, keepdims=True)
        ix = jnp.min(jnp.where(cur == m, eio, N_EXPERTS), axis=0, keepdims=True)
        vals.append(m)
        idxs.append(ix)
        cur = jnp.where(eio == ix, -jnp.inf, cur)
    exps = [jnp.exp(v - vals[0]) for v in vals]
    denom = exps[0] + exps[1] + exps[2] + exps[3]
    hits = [eio == ix for ix in idxs]
    cnt = jnp.zeros((N_EXPERTS, tm), jnp.float32)
    for hit in hits:
        cnt = cnt + hit.astype(jnp.float32)
    base = _dot(cnt.astype(jnp.bfloat16), upper_ref[...]) + carry[...]
    for kk in range(TOP_K):
        idx_ref[kk:kk + 1, :] = idxs[kk]
        gate_ref[kk:kk + 1, :] = exps[kk] / denom
        rank = jnp.sum(jnp.where(hits[kk], base, 0.0), axis=0, keepdims=True)
        rank_ref[kk:kk + 1, :] = rank.astype(jnp.int32)
    carry[...] = carry[...] + jnp.sum(cnt, axis=1, keepdims=True)
    cnt_ref[...] = carry[...]


def _out_projection(ypool, yattn, x, w_out, g_ffn, w_router, b_router):
    T, D = x.shape
    tm = SEQ_TILE
    bf16 = jnp.bfloat16
    wo = w_out.astype(bf16)
    wr_t = w_router.T
    wr_hi = wr_t.astype(bf16)
    wr_lo = (wr_t - wr_hi.astype(jnp.float32)).astype(bf16)
    wr2 = jnp.concatenate([wr_hi, wr_lo], axis=0)
    r = jnp.arange(tm)
    upper = (r[:, None] < r[None, :]).astype(bf16)
    full = lambda *shape: pl.BlockSpec(shape, lambda t: (0,) * len(shape))
    row = lambda w: pl.BlockSpec((tm, w), lambda t: (t, 0))
    col = pl.BlockSpec((TOP_K, tm), lambda t: (0, t))
    return pl.pallas_call(
        _outproj_kernel,
        grid=(T // tm,),
        in_specs=[row(POOL_WIDTH), row(ATTN_WIDTH), row(D), full(POOL_WIDTH, D),
                  full(ATTN_WIDTH, D), full(1, D), full(2 * N_EXPERTS, D), full(N_EXPERTS, D),
                  full(N_EXPERTS, 1), full(tm, tm)],
        out_specs=[row(D), row(D // 2), col, col, col, full(N_EXPERTS, 1)],
        out_shape=[
            jax.ShapeDtypeStruct((T, D), jnp.float32),
            jax.ShapeDtypeStruct((T, D // 2), jnp.uint32),
            jax.ShapeDtypeStruct((TOP_K, T), jnp.int32),
            jax.ShapeDtypeStruct((TOP_K, T), jnp.float32),
            jax.ShapeDtypeStruct((TOP_K, T), jnp.int32),
            jax.ShapeDtypeStruct((N_EXPERTS, 1), jnp.float32),
        ],
        scratch_shapes=[pltpu.VMEM((N_EXPERTS, 1), jnp.float32)],
        compiler_params=pltpu.CompilerParams(
            dimension_semantics=("arbitrary",), vmem_limit_bytes=VMEM_LIMIT),
        name="out_projection_router",
    )(ypool, yattn, x, wo[:POOL_WIDTH], wo[POOL_WIDTH:], g_ffn.reshape(1, D), wr2, wr_hi,
      b_router.reshape(N_EXPERTS, 1), upper)


def _sc_mesh():
    return plsc.VectorSubcoreMesh(core_axis_name="c", subcore_axis_name="s")


def _dispatch_rows(h2, dest, n_rows):
    T, D = h2.shape
    per_worker = T // SC_WORKERS
    steps = per_worker // SC_WINDOW

    @functools.partial(
        pl.kernel, mesh=_sc_mesh(),
        out_type=jax.ShapeDtypeStruct((n_rows, D), h2.dtype),
        scratch_types=[pltpu.VMEM((TOP_K, SC_WINDOW), jnp.int32),
                       pltpu.VMEM((SC_WINDOW, D), h2.dtype),
                       pltpu.SemaphoreType.DMA],
    )
    def scatter_kernel(h2_hbm, dest_hbm, xs_hbm, idx_v, rows_v, sem):
        wid = lax.axis_index("s") * SC_CORES + lax.axis_index("c")

        @pl.loop(0, steps)
        def _(step):
            base = wid * per_worker + step * SC_WINDOW
            pltpu.sync_copy(h2_hbm.at[pl.ds(base, SC_WINDOW)], rows_v)
            pltpu.sync_copy(dest_hbm.at[:, pl.ds(base, SC_WINDOW)], idx_v)
            copies = [pltpu.async_copy(rows_v, xs_hbm.at[idx_v.at[kk]], sem)
                      for kk in range(TOP_K)]
            for c in copies:
                c.wait()

    return scatter_kernel(h2, dest)


def _gather_rows(table, idx):
    N = idx.shape[0]
    D = table.shape[1]
    per_worker = N // SC_WORKERS
    steps = per_worker // SC_WINDOW

    @functools.partial(
        pl.kernel, mesh=_sc_mesh(),
        out_type=jax.ShapeDtypeStruct((N, D), table.dtype),
        scratch_types=[pltpu.VMEM((SC_WINDOW,), jnp.int32),
                       pltpu.VMEM((SC_WINDOW, D), table.dtype),
                       pltpu.SemaphoreType.DMA],
    )
    def gather_kernel(table_hbm, idx_hbm, out_hbm, idx_v, rows_v, sem):
        wid = lax.axis_index("s") * SC_CORES + lax.axis_index("c")

        @pl.loop(0, steps)
        def _(step):
            base = wid * per_worker + step * SC_WINDOW
            pltpu.sync_copy(idx_hbm.at[pl.ds(base, SC_WINDOW)], idx_v)
            pltpu.async_copy(table_hbm.at[idx_v], rows_v, sem).wait()
            pltpu.sync_copy(rows_v, out_hbm.at[pl.ds(base, SC_WINDOW)])

    return gather_kernel(table, idx)


def _expert_kernel(be_ref, nvalid_ref, x_ref, wup_ref, bup_ref, wdn_ref, bdn_ref, y_ref,
                   wup_bf, wdn_bf):
    i = pl.program_id(0)
    blk = x_ref.shape[0]
    changed = jnp.logical_or(i == 0, be_ref[i] != be_ref[jnp.maximum(i - 1, 0)])

    @pl.when(changed)
    def _():
        wup_bf[...] = wup_ref[0].astype(jnp.bfloat16)
        wdn_bf[...] = wdn_ref[0].astype(jnp.bfloat16)

    nvalid = nvalid_ref[i]

    @pl.when(nvalid > 0)
    def _():
        row = lax.broadcasted_iota(jnp.int32, (blk, 1), 0)
        x_hi, x_lo = _unpack_bf16_pairs(jnp.where(row < nvalid, x_ref[...], jnp.uint32(0)))
        x = jnp.concatenate([x_hi.astype(jnp.bfloat16), x_lo.astype(jnp.bfloat16)], axis=1)
        glu = _dot(x, wup_bf[:, :D_EXPERT]) + bup_ref[0, :, :D_EXPERT]
        lin = _dot(x, wup_bf[:, D_EXPERT:]) + bup_ref[0, :, D_EXPERT:]
        glu = jnp.minimum(glu, SWIGLU_LIMIT)
        lin = jnp.clip(lin, -SWIGLU_LIMIT, SWIGLU_LIMIT)
        act = glu * (1.0 / (1.0 + jnp.exp(-SWIGLU_ALPHA * glu))) * (lin + 1.0)
        y = _dot(act.astype(jnp.bfloat16), wdn_bf[...]) + bdn_ref[0]
        y_ref[...] = _pack_bf16_pairs(y)

    @pl.when(nvalid <= 0)
    def _():
        y_ref[...] = jnp.zeros_like(y_ref)


def _expert_ffn(xs, block_e, block_valid, w_up, b_up, w_down, b_down):
    n_rows = xs.shape[0]
    D = D_MODEL
    blk = EXPERT_ROWS
    grid_spec = pltpu.PrefetchScalarGridSpec(
        num_scalar_prefetch=2,
        grid=(n_rows // blk,),
        in_specs=[
            pl.BlockSpec((blk, D // 2), lambda i, be, nv: (i, 0)),
            pl.BlockSpec((1, D, 2 * D_EXPERT), lambda i, be, nv: (be[i], 0, 0)),
            pl.BlockSpec((1, 1, 2 * D_EXPERT), lambda i, be, nv: (be[i], 0, 0)),
            pl.BlockSpec((1, D_EXPERT, D), lambda i, be, nv: (be[i], 0, 0)),
            pl.BlockSpec((1, 1, D), lambda i, be, nv: (be[i], 0, 0)),
        ],
        out_specs=pl.BlockSpec((blk, D // 2), lambda i, be, nv: (i, 0)),
        scratch_shapes=[pltpu.VMEM((D, 2 * D_EXPERT), jnp.bfloat16),
                        pltpu.VMEM((D_EXPERT, D), jnp.bfloat16)],
    )
    return pl.pallas_call(
        _expert_kernel,
        grid_spec=grid_spec,
        out_shape=jax.ShapeDtypeStruct((n_rows, D // 2), jnp.uint32),
        compiler_params=pltpu.CompilerParams(
            dimension_semantics=("arbitrary",), vmem_limit_bytes=VMEM_LIMIT),
        name="expert_ffn",
    )(block_e, block_valid, xs, w_up, b_up.reshape(N_EXPERTS, 1, 2 * D_EXPERT), w_down,
      b_down.reshape(N_EXPERTS, 1, D))


def _final_kernel(x1_ref, rows_ref, gate_ref, g_ref, o_ref):
    half = x1_ref.shape[1] // 2
    acc_hi = x1_ref[:, :half]
    acc_lo = x1_ref[:, half:]
    for kk in range(TOP_K):
        y_hi, y_lo = _unpack_bf16_pairs(rows_ref[kk])
        gate = gate_ref[:, kk:kk + 1]
        acc_hi = acc_hi + y_hi * gate
        acc_lo = acc_lo + y_lo * gate
    o_ref[...] = _rms_norm(jnp.concatenate([acc_hi, acc_lo], axis=1), g_ref[...])


def _final_norm(x1, rows, gates_t, g_final):
    T, D = x1.shape
    tm = SEQ_TILE
    row = pl.BlockSpec((tm, D), lambda t: (t, 0))
    return pl.pallas_call(
        _final_kernel,
        grid=(T // tm,),
        in_specs=[row, pl.BlockSpec((TOP_K, tm, D // 2), lambda t: (0, t, 0)),
                  pl.BlockSpec((tm, TOP_K), lambda t: (t, 0)),
                  pl.BlockSpec((1, D), lambda t: (0, 0))],
        out_specs=row,
        out_shape=jax.ShapeDtypeStruct((T, D), jnp.float32),
        compiler_params=pltpu.CompilerParams(dimension_semantics=("arbitrary",)),
        name="combine_final_norm",
    )(x1, rows, gates_t, g_final.reshape(1, D))


def kernel(x, g_mix, w_in, b_forget, w_pool, pool_scale, w_out, g_ffn, w_router, b_router,
           w_up, b_up, w_down, b_down, g_final):
    B, S, D = x.shape
    T = B * S
    ypool, qt, qaug, k, kaug, vt, stats = _in_projection(x, g_mix[0], w_in[0], b_forget[0],
                                                        w_pool[0], pool_scale[0])
    yattn = _attention(qt, qaug, k, kaug, vt, _first_needed_chunk(stats))
    x1, h2, top_idx, gates, rank, counts = _out_projection(
        ypool.reshape(T, POOL_WIDTH), yattn.reshape(T, ATTN_WIDTH), x.reshape(T, D),
        w_out[0], g_ffn[0], w_router[0], b_router[0])

    blk = EXPERT_ROWS
    counts = counts[:, 0].astype(jnp.int32)
    padded = (counts + blk - 1) // blk * blk
    pad_ends = jnp.cumsum(padded)
    pad_starts = pad_ends - padded
    dest = rank
    for e in range(N_EXPERTS):
        dest = dest + jnp.where(top_idx == e, pad_starts[e], 0)
    n_rows = T * TOP_K + N_EXPERTS * blk
    block_row0 = jnp.arange(n_rows // blk, dtype=jnp.int32) * blk
    block_e = jnp.sum((pad_ends[None, :] <= block_row0[:, None]).astype(jnp.int32), axis=1)
    block_e = jnp.minimum(block_e, N_EXPERTS - 1)
    of_block = block_e[:, None] == jnp.arange(N_EXPERTS, dtype=jnp.int32)[None, :]
    block_end = jnp.sum(jnp.where(of_block, (pad_starts + counts)[None, :], 0), axis=1)
    block_valid = jnp.clip(block_end - block_row0, 0, blk)
    block_valid = jnp.where(block_row0 < pad_ends[-1], block_valid, 0).astype(jnp.int32)

    xs = _dispatch_rows(h2, dest, n_rows)
    ys = _expert_ffn(xs, block_e, block_valid, w_up[0], b_up[0], w_down[0], b_down[0])
    rows = _gather_rows(ys, dest.reshape(-1)).reshape(TOP_K, T, D // 2)
    out = _final_norm(x1, rows, gates.T, g_final)
    return out.reshape(B, S, D)
```

```python
import functools

import jax
import jax.numpy as jnp
from jax import lax
from jax.experimental import pallas as pl
from jax.experimental.pallas import tpu as pltpu
from jax.experimental.pallas import tpu_sc as plsc

D_MODEL = 1024
POOL_WIDTH = 512
POOL_WINDOWS = (2, 4, 8, 16)
POOL_GROUP = 128
MAX_WINDOW = 16
ATTN_WIDTH = 512
HEAD_DIM = 64
N_HEADS = 8
N_EXPERTS = 32
TOP_K = 4
D_EXPERT = 1024
SWIGLU_LIMIT = 7.0
SWIGLU_ALPHA = 1.702
RMS_EPS = 1e-5

LANES = 128
AUG_GROUP = 16
N_SPLIT = 3
MASK_VALUE = -1e30
LOG2_E = 1.4426950408889634
SKIP_MARGIN = 160.0
NORM_SLACK = 1.02
V_ROWS = 80

SEQ_TILE = 512
Q_TILE = 512
KV_TILE = SEQ_TILE
EXPERT_ROWS = 512
TOKEN_PARTS = 2
SC_CORES = 2
SC_SUBCORES = 16
SC_WORKERS = SC_CORES * SC_SUBCORES
SC_WINDOW = 128
VMEM_LIMIT = 56 * 1024 * 1024

_NT = (((1,), (1,)), ((), ()))


def _dot(a, b):
    return jnp.dot(a, b, preferred_element_type=jnp.float32)


def _dot_nt(a, b):
    return lax.dot_general(a, b, _NT, preferred_element_type=jnp.float32)


def _rms_norm(x, g):
    return x * lax.rsqrt(jnp.mean(x * x, axis=-1, keepdims=True) + RMS_EPS) * g


def _pack_bf16_pairs(x):
    n = x.shape[1] // 2
    bits = lax.bitcast_convert_type(x.astype(jnp.bfloat16).astype(jnp.float32), jnp.uint32)
    return bits[:, :n] | (bits[:, n:] >> 16)


def _unpack_bf16_pairs(p):
    hi = lax.bitcast_convert_type(p & jnp.uint32(0xFFFF0000), jnp.float32)
    lo = lax.bitcast_convert_type(p << 16, jnp.float32)
    return hi, lo


def _split3(x):
    hi = x.astype(jnp.bfloat16)
    r1 = x - hi.astype(jnp.float32)
    mid = r1.astype(jnp.bfloat16)
    lo = (r1 - mid.astype(jnp.float32)).astype(jnp.bfloat16)
    return hi, mid, lo


def _inproj_kernel(x_ref, g_ref, wu_ref, wqt_ref, wk_ref, wvt_ref, wf_ref, bf_ref, tri_ref,
                   pqt_ref, pk_ref, oneq_ref, onek_ref, hsel_ref, wpool_ref, pscale_ref,
                   ypool_ref, qt_ref, qaug_ref, k_ref, kaug_ref, vt_ref, stats_ref, ubuf, ccarry):
    s = pl.program_id(1)
    tm = x_ref.shape[1]

    @pl.when(s == 0)
    def _():
        ubuf[...] = jnp.zeros_like(ubuf)
        ccarry[...] = jnp.zeros_like(ccarry)

    x = x_ref[0]
    h = _rms_norm(x, g_ref[...]).astype(jnp.bfloat16)

    fl = _dot(h, wf_ref[...]) + bf_ref[...]
    logf = jnp.minimum(fl, 0.0) - jnp.log1p(jnp.exp(-jnp.abs(fl)))
    lane = lax.broadcasted_iota(jnp.int32, (tm, LANES), 1)
    logf = jnp.where(lane < N_SPLIT * N_HEADS, logf, 0.0)
    f_hi, f_mid, f_lo = _split3(logf)
    tri = tri_ref[...]
    c = _dot(tri, f_hi) + _dot(tri, f_mid) + _dot(tri, f_lo) + ccarry[...]
    ccarry[...] = c[tm - 1:tm, :]
    c2 = c * LOG2_E
    c_hi, c_mid, c_lo = _split3(c2)
    term = lane % N_SPLIT
    c3 = jnp.where(term == 0, c_hi, jnp.where(term == 1, c_mid, c_lo))

    kp = _dot(h, wk_ref[...])
    k_ref[0] = kp.astype(jnp.bfloat16)
    kaug_ref[0] = (_dot(c3, pk_ref[...]) + onek_ref[...]).astype(jnp.bfloat16)
    qt = _dot_nt(wqt_ref[...], h)
    qt_ref[0] = qt.astype(jnp.bfloat16)
    qaug_ref[0] = (_dot_nt(pqt_ref[...], c3) + oneq_ref[...]).astype(jnp.bfloat16)
    lane_row = lax.broadcasted_iota(jnp.int32, (1, LANES), 1)
    kn2 = jnp.max(_dot((kp * kp).astype(jnp.bfloat16), hsel_ref[...]), axis=0, keepdims=True)
    qn2 = jnp.zeros((1, LANES), jnp.float32)
    for hd in range(N_HEADS):
        qs = qt[hd * HEAD_DIM:(hd + 1) * HEAD_DIM, :]
        qmax = jnp.max(jnp.sum(qs * qs, axis=0, keepdims=True), axis=1, keepdims=True)
        qn2 = jnp.where(lane_row == hd, qmax, qn2)
    stats_ref[0, 0, 0:1, :] = kn2
    stats_ref[0, 0, 1:2, :] = qn2
    stats_ref[0, 0, 2:3, :] = c2[0:1, :]
    stats_ref[0, 0, 3:4, :] = c2[tm - 1:tm, :]
    stats_ref[0, 0, 4:8, :] = jnp.zeros((4, LANES), jnp.float32)
    vt = _dot_nt(wvt_ref[...], h).astype(jnp.bfloat16)
    ones = jnp.ones((V_ROWS - HEAD_DIM, tm), jnp.bfloat16)
    for hd in range(N_HEADS):
        vt_ref[0, 0, hd * V_ROWS:hd * V_ROWS + HEAD_DIM, :] = vt[hd * HEAD_DIM:(hd + 1) * HEAD_DIM]
        vt_ref[0, 0, hd * V_ROWS + HEAD_DIM:(hd + 1) * V_ROWS, :] = ones

    u = _dot(h, wu_ref[...])
    head_pos = s * tm + lax.broadcasted_iota(jnp.int32, (MAX_WINDOW, 1), 0)
    for gi, w in enumerate(POOL_WINDOWS):
        cols = slice(gi * POOL_GROUP, (gi + 1) * POOL_GROUP)
        ug = u[:, cols]
        win = jnp.concatenate([ubuf[:, cols], ug], axis=0)
        span = 1
        while span < w:
            win = win + pltpu.roll(win, span, axis=0)
            span *= 2
        acc = win[MAX_WINDOW:]
        inv_head = 1.0 / jnp.minimum(head_pos + 1, w).astype(jnp.float32)
        p = jnp.concatenate([acc[:MAX_WINDOW] * inv_head, acc[MAX_WINDOW:] * (1.0 / w)],
                            axis=0) - ug
        y = _dot(p.astype(jnp.bfloat16), wpool_ref[gi]) * pscale_ref[:, cols]
        ypool_ref[0, :, cols] = y.astype(jnp.bfloat16)
    ubuf[...] = u[tm - MAX_WINDOW:]


def _in_projection(x, g_mix, w_in, b_forget, w_pool, pool_scale):
    B, S, D = x.shape
    tm = SEQ_TILE
    bf16 = jnp.bfloat16
    o = POOL_WIDTH
    w_u = w_in[:, :o].astype(bf16)
    w_q = w_in[:, o:o + ATTN_WIDTH] * (HEAD_DIM ** -0.5 * LOG2_E)
    w_k = w_in[:, o + ATTN_WIDTH:o + 2 * ATTN_WIDTH]
    w_v = w_in[:, o + 2 * ATTN_WIDTH:o + 3 * ATTN_WIDTH]
    w_f = w_in[:, o + 3 * ATTN_WIDTH:]

    wqt = w_q.T.astype(bf16)
    wk = w_k.astype(bf16)
    wvt = w_v.T.astype(bf16)
    n_gate = N_SPLIT * N_HEADS
    wf = jnp.pad(jnp.repeat(w_f, N_SPLIT, axis=1), ((0, 0), (0, LANES - n_gate))).astype(bf16)
    bfp = jnp.pad(jnp.repeat(b_forget, N_SPLIT), (0, LANES - n_gate)).reshape(1, LANES)

    assert N_HEADS * AUG_GROUP == LANES and 2 * N_SPLIT <= AUG_GROUP
    src = jnp.arange(LANES)
    head, term = src // N_SPLIT, src % N_SPLIT
    dst = jnp.arange(LANES)
    valid = (src < n_gate)[:, None]
    q_hit = valid & (dst[None, :] == (head * AUG_GROUP + term)[:, None])
    k_hit = valid & (dst[None, :] == (head * AUG_GROUP + N_SPLIT + term)[:, None])
    pqt = q_hit.T.astype(bf16)
    pk = -k_hit.astype(bf16)
    slot_pos = dst % AUG_GROUP
    oneq = ((slot_pos >= N_SPLIT) & (slot_pos < 2 * N_SPLIT))
    onek = slot_pos < N_SPLIT
    oneq = oneq.astype(jnp.float32).reshape(-1, 1)
    onek = onek.astype(jnp.float32).reshape(1, -1)
    r = jnp.arange(tm)
    tri = (r[None, :] <= r[:, None]).astype(bf16)
    hsel = (jnp.arange(ATTN_WIDTH)[:, None] // HEAD_DIM == jnp.arange(LANES)[None, :]).astype(bf16)

    full = lambda *shape: pl.BlockSpec(shape, lambda b, s: (0,) * len(shape))
    aw = ATTN_WIDTH
    return pl.pallas_call(
        _inproj_kernel,
        grid=(B, S // tm),
        in_specs=[
            pl.BlockSpec((1, tm, D), lambda b, s: (b, s, 0)),
            full(1, D), full(D, o), full(aw, D), full(D, aw), full(aw, D),
            full(D, LANES), full(1, LANES), full(tm, tm), full(LANES, LANES), full(LANES, LANES),
            full(LANES, 1), full(1, LANES), full(aw, LANES),
            full(len(POOL_WINDOWS), POOL_GROUP, POOL_GROUP),
            full(1, o),
        ],
        out_specs=[
            pl.BlockSpec((1, tm, o), lambda b, s: (b, s, 0)),
            pl.BlockSpec((1, aw, tm), lambda b, s: (b, 0, s)),
            pl.BlockSpec((1, LANES, tm), lambda b, s: (b, 0, s)),
            pl.BlockSpec((1, tm, aw), lambda b, s: (b, s, 0)),
            pl.BlockSpec((1, tm, LANES), lambda b, s: (b, s, 0)),
            pl.BlockSpec((1, 1, N_HEADS * V_ROWS, tm), lambda b, s: (b, s, 0, 0)),
            pl.BlockSpec((1, 1, 8, LANES), lambda b, s: (b, s, 0, 0)),
        ],
        out_shape=[
            jax.ShapeDtypeStruct((B, S, o), bf16),
            jax.ShapeDtypeStruct((B, aw, S), bf16),
            jax.ShapeDtypeStruct((B, LANES, S), bf16),
            jax.ShapeDtypeStruct((B, S, aw), bf16),
            jax.ShapeDtypeStruct((B, S, LANES), bf16),
            jax.ShapeDtypeStruct((B, S // tm, N_HEADS * V_ROWS, tm), bf16),
            jax.ShapeDtypeStruct((B, S // tm, 8, LANES), jnp.float32),
        ],
        scratch_shapes=[
            pltpu.VMEM((MAX_WINDOW, o), jnp.float32),
            pltpu.VMEM((1, LANES), jnp.float32),
        ],
        compiler_params=pltpu.CompilerParams(
            dimension_semantics=("arbitrary", "arbitrary"), vmem_limit_bytes=VMEM_LIMIT),
        name="in_projection",
    )(x, g_mix.reshape(1, D), w_u, wqt, wk, wvt, wf, bfp, tri, pqt, pk, oneq, onek, hsel,
      w_pool.astype(bf16), pool_scale.reshape(1, o))


def _attn_kernel(j0_ref, qt_ref, qaug_ref, k_ref, kaug_ref, vt_ref, causal_ref, o_ref,
                 qop_ref, st_buf, mx_buf, m_ref, acc_ref):
    i = pl.program_id(2)
    tile = (pl.program_id(0) * pl.num_programs(1) + pl.program_id(1)) * pl.num_programs(2) + i
    j0 = j0_ref[tile]
    tq = qt_ref.shape[2]
    tk = vt_ref.shape[3]
    heads = qt_ref.shape[1] // HEAD_DIM

    m_ref[...] = jnp.full(m_ref.shape, MASK_VALUE, jnp.float32)
    acc_ref[...] = jnp.zeros_like(acc_ref)

    assert heads == 2
    first_head = pl.program_id(1) * heads
    group = lax.broadcasted_iota(jnp.int32, (LANES, tq), 0) // AUG_GROUP
    q_pair = qt_ref[0]
    q_aug = qaug_ref[0]
    blank = jnp.zeros((HEAD_DIM, tq), q_pair.dtype)
    qop_ref[0, 0:2 * HEAD_DIM, :] = jnp.concatenate([q_pair[:HEAD_DIM], blank], axis=0)
    qop_ref[1, 0:2 * HEAD_DIM, :] = jnp.concatenate([blank, q_pair[HEAD_DIM:]], axis=0)
    for hh in range(heads):
        qop_ref[hh, 2 * HEAD_DIM:, :] = jnp.where(group == first_head + hh, q_aug,
                                                  jnp.zeros_like(q_aug))

    def logits(j, slot, hh, masked):
        row0 = pl.multiple_of(j * tk, tk)
        k = jnp.concatenate([k_ref[0, pl.ds(row0, tk), :], kaug_ref[0, pl.ds(row0, tk), :]],
                            axis=1)
        st = _dot(k, qop_ref[hh])
        if masked:
            st = st + causal_ref[...]
        st_buf[slot, hh] = st
        mx_buf[slot, hh] = jnp.max(st, axis=0, keepdims=True)

    def accumulate(j, slot, hh):
        st = st_buf[slot, hh]
        m_prev = m_ref[hh]
        m_new = jnp.maximum(m_prev, mx_buf[slot, hh])
        alpha = jnp.exp2(m_prev - m_new)
        p = jnp.exp2(st - m_new)
        vt = vt_ref[0, j, hh * V_ROWS:(hh + 1) * V_ROWS, :]
        acc_ref[hh] = alpha * acc_ref[hh] + _dot(vt, p.astype(jnp.bfloat16))
        m_ref[hh] = m_new

    logits(i, 0, 0, True)
    logits(i, 0, 1, True)
    accumulate(i, 0, 0)

    def before(t):
        return jnp.where(t == j0, i, t - 1)

    def step(t, slot):
        logits(t, slot, 0, False)
        accumulate(before(t), 1 - slot, 1)
        logits(t, slot, 1, False)
        accumulate(t, slot, 0)

    def body(u, carry):
        step(2 * u, 1)
        step(2 * u + 1, 0)
        return carry

    lax.fori_loop(j0 // 2, i // 2, body, 0)

    @pl.when(i % 2 == 1)
    def _():
        step(i - 1, 1)
        accumulate(i - 1, 1, 1)

    @pl.when(i % 2 == 0)
    def _():
        accumulate(before(i), 0, 1)

    outs = [acc_ref[hh, :HEAD_DIM] / acc_ref[hh, HEAD_DIM:HEAD_DIM + 1] for hh in range(heads)]
    ot = jnp.concatenate(outs, axis=0)
    o_ref[0] = ot.T.astype(o_ref.dtype)


def _first_needed_chunk(stats):
    kn = jnp.sqrt(stats[:, :, 0, :N_HEADS])
    qn = jnp.sqrt(stats[:, :, 1, :N_HEADS])
    c_first = stats[:, :, 2, 0:N_SPLIT * N_HEADS:N_SPLIT]
    c_last = stats[:, :, 3, 0:N_SPLIT * N_HEADS:N_SPLIT]
    n = stats.shape[1]
    dot_bound = NORM_SLACK * qn[:, :, None, :] * kn[:, None, :, :] + 1.0
    upper = dot_bound + c_first[:, :, None, :] - c_last[:, None, :, :]
    self_bound = NORM_SLACK * qn * kn + 1.0
    ii = jnp.arange(n, dtype=jnp.int32)[None, :, None, None]
    jj = jnp.arange(n, dtype=jnp.int32)[None, None, :, None]
    needed = (upper + self_bound[:, :, None, :] >= -SKIP_MARGIN) & (jj < ii)
    first = jnp.min(jnp.where(needed, jj, ii), axis=2)
    first = jnp.min(first.reshape(first.shape[0], n, N_HEADS // 2, 2), axis=3)
    first = first // 2 * 2
    return jnp.transpose(first, (0, 2, 1)).reshape(-1).astype(jnp.int32)


def _attention(qt, qaug, k, kaug, vt, first_chunk):
    B, _, S = qt.shape
    tq, tk = Q_TILE, KV_TILE
    assert tq == tk, "the diagonal is handled as a single masked key chunk"
    pair = 2
    n_pairs = N_HEADS // pair
    pw = pair * HEAD_DIM
    causal = jnp.where(jnp.arange(tk)[:, None] <= jnp.arange(tq)[None, :], 0.0, MASK_VALUE)
    grid_spec = pltpu.PrefetchScalarGridSpec(
        num_scalar_prefetch=1,
        grid=(B, n_pairs, S // tq),
        in_specs=[
            pl.BlockSpec((1, pw, tq), lambda b, p, i, j0: (b, p, i)),
            pl.BlockSpec((1, LANES, tq), lambda b, p, i, j0: (b, 0, i)),
            pl.BlockSpec((1, S, pw), lambda b, p, i, j0: (b, 0, p)),
            pl.BlockSpec((1, S, LANES), lambda b, p, i, j0: (b, 0, 0)),
            pl.BlockSpec((1, S // tk, pair * V_ROWS, tk), lambda b, p, i, j0: (b, 0, p, 0)),
            pl.BlockSpec((tk, tq), lambda b, p, i, j0: (0, 0)),
        ],
        out_specs=pl.BlockSpec((1, tq, pw), lambda b, p, i, j0: (b, i, p)),
        scratch_shapes=[
            pltpu.VMEM((pair, pw + LANES, tq), jnp.bfloat16),
            pltpu.VMEM((2, pair, tk, tq), jnp.float32),
            pltpu.VMEM((2, pair, 1, tq), jnp.float32),
            pltpu.VMEM((pair, 1, tq), jnp.float32),
            pltpu.VMEM((pair, V_ROWS, tq), jnp.float32),
        ],
    )
    return pl.pallas_call(
        _attn_kernel,
        grid_spec=grid_spec,
        out_shape=jax.ShapeDtypeStruct((B, S, ATTN_WIDTH), jnp.bfloat16),
        compiler_params=pltpu.CompilerParams(
            dimension_semantics=("arbitrary", "arbitrary", "arbitrary"),
            vmem_limit_bytes=VMEM_LIMIT),
        name="forgetting_attention",
    )(first_chunk, qt, qaug, k, kaug, vt, causal.astype(jnp.float32))


def _outproj_kernel(yp_ref, ya_ref, x_ref, wo1_ref, wo2_ref, g_ref, wr2_ref, wrhi_ref, br_ref,
                    upper_ref, x1_ref, h2_ref, idx_ref, gate_ref, rank_ref, cnt_ref, carry):
    t = pl.program_id(0)
    tm = x_ref.shape[0]

    @pl.when(t == 0)
    def _():
        carry[...] = jnp.zeros_like(carry)

    x1 = x_ref[...] + _dot(yp_ref[...], wo1_ref[...]) + _dot(ya_ref[...], wo2_ref[...])
    x1_ref[...] = x1
    h2 = _rms_norm(x1, g_ref[...])
    h2_hi = h2.astype(jnp.bfloat16)
    h2_lo = (h2 - h2_hi.astype(jnp.float32)).astype(jnp.bfloat16)
    h2_ref[...] = _pack_bf16_pairs(h2)

    lg2 = _dot_nt(wr2_ref[...], h2_hi)
    logits = (lg2[:N_EXPERTS] + lg2[N_EXPERTS:] + _dot_nt(wrhi_ref[...], h2_lo) + br_ref[...])

    eio = lax.broadcasted_iota(jnp.int32, (N_EXPERTS, tm), 0)
    vals, idxs = [], []
    cur = logits
    for _ in range(TOP_K):
        m = jnp.max(cur, axis=0, keepdims=True)
        ix = jnp.min(jnp.where(cur == m, eio, N_EXPERTS), axis=0, keepdims=True)
        vals.append(m)
        idxs.append(ix)
        cur = jnp.where(eio == ix, -jnp.inf, cur)
    exps = [jnp.exp(v - vals[0]) for v in vals]
    denom = exps[0] + exps[1] + exps[2] + exps[3]
    hits = [eio == ix for ix in idxs]
    cnt = jnp.zeros((N_EXPERTS, tm), jnp.float32)
    for hit in hits:
        cnt = cnt + hit.astype(jnp.float32)
    base = _dot(cnt.astype(jnp.bfloat16), upper_ref[...]) + carry[...]
    for kk in range(TOP_K):
        idx_ref[kk:kk + 1, :] = idxs[kk]
        gate_ref[kk:kk + 1, :] = exps[kk] / denom
        rank = jnp.sum(jnp.where(hits[kk], base, 0.0), axis=0, keepdims=True)
        rank_ref[kk:kk + 1, :] = rank.astype(jnp.int32)
    carry[...] = carry[...] + jnp.sum(cnt, axis=1, keepdims=True)
    cnt_ref[...] = carry[...]


def _out_projection(ypool, yattn, x, w_out, g_ffn, w_router, b_router, part, n_parts):
    D = x.shape[1]
    T = x.shape[0] // n_parts
    tm = SEQ_TILE
    t0 = part * (T // tm)
    bf16 = jnp.bfloat16
    wo = w_out.astype(bf16)
    wr_t = w_router.T
    wr_hi = wr_t.astype(bf16)
    wr_lo = (wr_t - wr_hi.astype(jnp.float32)).astype(bf16)
    wr2 = jnp.concatenate([wr_hi, wr_lo], axis=0)
    r = jnp.arange(tm)
    upper = (r[:, None] < r[None, :]).astype(bf16)
    full = lambda *shape: pl.BlockSpec(shape, lambda t: (0,) * len(shape))
    row = lambda w: pl.BlockSpec((tm, w), lambda t: (t, 0))
    src = lambda w: pl.BlockSpec((tm, w), lambda t: (t + t0, 0))
    col = pl.BlockSpec((TOP_K, tm), lambda t: (0, t))
    return pl.pallas_call(
        _outproj_kernel,
        grid=(T // tm,),
        in_specs=[src(POOL_WIDTH), src(ATTN_WIDTH), src(D), full(POOL_WIDTH, D),
                  full(ATTN_WIDTH, D), full(1, D), full(2 * N_EXPERTS, D), full(N_EXPERTS, D),
                  full(N_EXPERTS, 1), full(tm, tm)],
        out_specs=[row(D), row(D // 2), col, col, col, full(N_EXPERTS, 1)],
        out_shape=[
            jax.ShapeDtypeStruct((T, D), jnp.float32),
            jax.ShapeDtypeStruct((T, D // 2), jnp.uint32),
            jax.ShapeDtypeStruct((TOP_K, T), jnp.int32),
            jax.ShapeDtypeStruct((TOP_K, T), jnp.float32),
            jax.ShapeDtypeStruct((TOP_K, T), jnp.int32),
            jax.ShapeDtypeStruct((N_EXPERTS, 1), jnp.float32),
        ],
        scratch_shapes=[pltpu.VMEM((N_EXPERTS, 1), jnp.float32)],
        compiler_params=pltpu.CompilerParams(
            dimension_semantics=("arbitrary",), vmem_limit_bytes=VMEM_LIMIT),
        name="out_projection_router",
    )(ypool, yattn, x, wo[:POOL_WIDTH], wo[POOL_WIDTH:], g_ffn.reshape(1, D), wr2, wr_hi,
      b_router.reshape(N_EXPERTS, 1), upper)


def _sc_mesh():
    return plsc.VectorSubcoreMesh(core_axis_name="c", subcore_axis_name="s")


def _dispatch_rows(h2, dest, n_rows):
    T, D = h2.shape
    per_worker = T // SC_WORKERS
    steps = per_worker // SC_WINDOW

    @functools.partial(
        pl.kernel, mesh=_sc_mesh(),
        out_type=jax.ShapeDtypeStruct((n_rows, D), h2.dtype),
        scratch_types=[pltpu.VMEM((TOP_K, SC_WINDOW), jnp.int32),
                       pltpu.VMEM((SC_WINDOW, D), h2.dtype),
                       pltpu.SemaphoreType.DMA],
    )
    def scatter_kernel(h2_hbm, dest_hbm, xs_hbm, idx_v, rows_v, sem):
        wid = lax.axis_index("s") * SC_CORES + lax.axis_index("c")

        @pl.loop(0, steps)
        def _(step):
            base = wid * per_worker + step * SC_WINDOW
            pltpu.sync_copy(h2_hbm.at[pl.ds(base, SC_WINDOW)], rows_v)
            pltpu.sync_copy(dest_hbm.at[:, pl.ds(base, SC_WINDOW)], idx_v)
            copies = [pltpu.async_copy(rows_v, xs_hbm.at[idx_v.at[kk]], sem)
                      for kk in range(TOP_K)]
            for c in copies:
                c.wait()

    return scatter_kernel(h2, dest)


def _gather_rows(table, idx):
    N = idx.shape[0]
    D = table.shape[1]
    per_worker = N // SC_WORKERS
    steps = per_worker // SC_WINDOW

    @functools.partial(
        pl.kernel, mesh=_sc_mesh(),
        out_type=jax.ShapeDtypeStruct((N, D), table.dtype),
        scratch_types=[pltpu.VMEM((SC_WINDOW,), jnp.int32),
                       pltpu.VMEM((SC_WINDOW, D), table.dtype),
                       pltpu.SemaphoreType.DMA],
    )
    def gather_kernel(table_hbm, idx_hbm, out_hbm, idx_v, rows_v, sem):
        wid = lax.axis_index("s") * SC_CORES + lax.axis_index("c")

        @pl.loop(0, steps)
        def _(step):
            base = wid * per_worker + step * SC_WINDOW
            pltpu.sync_copy(idx_hbm.at[pl.ds(base, SC_WINDOW)], idx_v)
            pltpu.async_copy(table_hbm.at[idx_v], rows_v, sem).wait()
            pltpu.sync_copy(rows_v, out_hbm.at[pl.ds(base, SC_WINDOW)])

    return gather_kernel(table, idx)


def _expert_kernel(be_ref, nvalid_ref, x_ref, wup_ref, bup_ref, wdn_ref, bdn_ref, y_ref,
                   wup_bf, wdn_bf):
    i = pl.program_id(0)
    blk = x_ref.shape[0]
    changed = jnp.logical_or(i == 0, be_ref[i] != be_ref[jnp.maximum(i - 1, 0)])

    @pl.when(changed)
    def _():
        wup_bf[...] = wup_ref[0].astype(jnp.bfloat16)
        wdn_bf[...] = wdn_ref[0].astype(jnp.bfloat16)

    nvalid = nvalid_ref[i]

    @pl.when(nvalid > 0)
    def _():
        row = lax.broadcasted_iota(jnp.int32, (blk, 1), 0)
        x_hi, x_lo = _unpack_bf16_pairs(jnp.where(row < nvalid, x_ref[...], jnp.uint32(0)))
        x = jnp.concatenate([x_hi.astype(jnp.bfloat16), x_lo.astype(jnp.bfloat16)], axis=1)
        glu = _dot(x, wup_bf[:, :D_EXPERT]) + bup_ref[0, :, :D_EXPERT]
        lin = _dot(x, wup_bf[:, D_EXPERT:]) + bup_ref[0, :, D_EXPERT:]
        glu = jnp.minimum(glu, SWIGLU_LIMIT)
        lin = jnp.clip(lin, -SWIGLU_LIMIT, SWIGLU_LIMIT)
        act = glu * (1.0 / (1.0 + jnp.exp(-SWIGLU_ALPHA * glu))) * (lin + 1.0)
        y = _dot(act.astype(jnp.bfloat16), wdn_bf[...]) + bdn_ref[0]
        y_ref[...] = _pack_bf16_pairs(y)

    @pl.when(nvalid <= 0)
    def _():
        y_ref[...] = jnp.zeros_like(y_ref)


def _expert_ffn(xs, block_e, block_valid, w_up, b_up, w_down, b_down):
    n_rows = xs.shape[0]
    D = D_MODEL
    blk = EXPERT_ROWS
    grid_spec = pltpu.PrefetchScalarGridSpec(
        num_scalar_prefetch=2,
        grid=(n_rows // blk,),
        in_specs=[
            pl.BlockSpec((blk, D // 2), lambda i, be, nv: (i, 0)),
            pl.BlockSpec((1, D, 2 * D_EXPERT), lambda i, be, nv: (be[i], 0, 0)),
            pl.BlockSpec((1, 1, 2 * D_EXPERT), lambda i, be, nv: (be[i], 0, 0)),
            pl.BlockSpec((1, D_EXPERT, D), lambda i, be, nv: (be[i], 0, 0)),
            pl.BlockSpec((1, 1, D), lambda i, be, nv: (be[i], 0, 0)),
        ],
        out_specs=pl.BlockSpec((blk, D // 2), lambda i, be, nv: (i, 0)),
        scratch_shapes=[pltpu.VMEM((D, 2 * D_EXPERT), jnp.bfloat16),
                        pltpu.VMEM((D_EXPERT, D), jnp.bfloat16)],
    )
    return pl.pallas_call(
        _expert_kernel,
        grid_spec=grid_spec,
        out_shape=jax.ShapeDtypeStruct((n_rows, D // 2), jnp.uint32),
        compiler_params=pltpu.CompilerParams(
            dimension_semantics=("arbitrary",), vmem_limit_bytes=VMEM_LIMIT),
        name="expert_ffn",
    )(block_e, block_valid, xs, w_up, b_up.reshape(N_EXPERTS, 1, 2 * D_EXPERT), w_down,
      b_down.reshape(N_EXPERTS, 1, D))


def _final_kernel(x1_ref, rows_ref, gate_ref, g_ref, o_ref):
    half = x1_ref.shape[1] // 2
    acc_hi = x1_ref[:, :half]
    acc_lo = x1_ref[:, half:]
    for kk in range(TOP_K):
        y_hi, y_lo = _unpack_bf16_pairs(rows_ref[kk])
        gate = gate_ref[:, kk:kk + 1]
        acc_hi = acc_hi + y_hi * gate
        acc_lo = acc_lo + y_lo * gate
    o_ref[...] = _rms_norm(jnp.concatenate([acc_hi, acc_lo], axis=1), g_ref[...])


def _final_alias_kernel(x1_ref, rows_ref, gate_ref, g_ref, prev_ref, o_ref):
    del prev_ref
    _final_kernel(x1_ref, rows_ref, gate_ref, g_ref, o_ref)


def _final_norm(x1, rows, gates_t, g_final, part, n_parts, prev_out):
    Tp, D = x1.shape
    tm = SEQ_TILE
    t0 = part * (Tp // tm)
    row = pl.BlockSpec((tm, D), lambda t: (t, 0))
    in_specs = [row, pl.BlockSpec((TOP_K, tm, D // 2), lambda t: (0, t, 0)),
                pl.BlockSpec((tm, TOP_K), lambda t: (t, 0)),
                pl.BlockSpec((1, D), lambda t: (0, 0))]
    args = [x1, rows, gates_t, g_final.reshape(1, D)]
    body, aliases = _final_kernel, {}
    if prev_out is not None:
        in_specs.append(pl.BlockSpec(memory_space=pl.ANY))
        args.append(prev_out)
        body, aliases = _final_alias_kernel, {len(args) - 1: 0}
    return pl.pallas_call(
        body,
        grid=(Tp // tm,),
        in_specs=in_specs,
        out_specs=pl.BlockSpec((tm, D), lambda t: (t + t0, 0)),
        out_shape=jax.ShapeDtypeStruct((Tp * n_parts, D), jnp.float32),
        input_output_aliases=aliases,
        compiler_params=pltpu.CompilerParams(dimension_semantics=("arbitrary",)),
        name="combine_final_norm",
    )(*args)


def kernel(x, g_mix, w_in, b_forget, w_pool, pool_scale, w_out, g_ffn, w_router, b_router,
           w_up, b_up, w_down, b_down, g_final):
    B, S, D = x.shape
    T = B * S
    ypool, qt, qaug, k, kaug, vt, stats = _in_projection(x, g_mix[0], w_in[0], b_forget[0],
                                                        w_pool[0], pool_scale[0])
    yattn = _attention(qt, qaug, k, kaug, vt, _first_needed_chunk(stats))
    out = None
    for part in range(TOKEN_PARTS):
        out = _moe_part(x.reshape(T, D), ypool.reshape(T, POOL_WIDTH), yattn.reshape(T, ATTN_WIDTH),
                        w_out[0], g_ffn[0], w_router[0], b_router[0], w_up[0], b_up[0], w_down[0],
                        b_down[0], g_final, part, out)
    return out.reshape(B, S, D)


def _moe_part(x, ypool, yattn, w_out, g_ffn, w_router, b_router, w_up, b_up, w_down, b_down,
              g_final, part, prev_out):
    D = x.shape[1]
    x1, h2, top_idx, gates, rank, counts = _out_projection(
        ypool, yattn, x, w_out, g_ffn, w_router, b_router, part, TOKEN_PARTS)
    T = x1.shape[0]

    blk = EXPERT_ROWS
    counts = counts[:, 0].astype(jnp.int32)
    padded = (counts + blk - 1) // blk * blk
    pad_ends = jnp.cumsum(padded)
    pad_starts = pad_ends - padded
    dest = rank
    for e in range(N_EXPERTS):
        dest = dest + jnp.where(top_idx == e, pad_starts[e], 0)
    n_rows = T * TOP_K + N_EXPERTS * blk
    block_row0 = jnp.arange(n_rows // blk, dtype=jnp.int32) * blk
    block_e = jnp.sum((pad_ends[None, :] <= block_row0[:, None]).astype(jnp.int32), axis=1)
    block_e = jnp.minimum(block_e, N_EXPERTS - 1)
    of_block = block_e[:, None] == jnp.arange(N_EXPERTS, dtype=jnp.int32)[None, :]
    block_end = jnp.sum(jnp.where(of_block, (pad_starts + counts)[None, :], 0), axis=1)
    block_valid = jnp.clip(block_end - block_row0, 0, blk)
    block_valid = jnp.where(block_row0 < pad_ends[-1], block_valid, 0).astype(jnp.int32)

    xs = _dispatch_rows(h2, dest, n_rows)
    ys = _expert_ffn(xs, block_e, block_valid, w_up, b_up, w_down, b_down)
    rows = _gather_rows(ys, dest.reshape(-1)).reshape(TOP_K, T, D // 2)
    return _final_norm(x1, rows, gates.T, g_final, part, TOKEN_PARTS, prev_out)
```

```python
import functools

import jax
import jax.numpy as jnp
from jax import lax
from jax.experimental import pallas as pl
from jax.experimental.pallas import tpu as pltpu
from jax.experimental.pallas import tpu_sc as plsc

D_MODEL = 1024
POOL_WIDTH = 512
POOL_WINDOWS = (2, 4, 8, 16)
POOL_GROUP = 128
MAX_WINDOW = 16
ATTN_WIDTH = 512
HEAD_DIM = 64
N_HEADS = 8
N_EXPERTS = 32
TOP_K = 4
D_EXPERT = 1024
SWIGLU_LIMIT = 7.0
SWIGLU_ALPHA = 1.702
RMS_EPS = 1e-5

LANES = 128
AUG_GROUP = 16
N_SPLIT = 3
MASK_VALUE = -1e30
LOG2_E = 1.4426950408889634
SKIP_MARGIN = 160.0
NORM_SLACK = 1.02
V_ROWS = 80

SEQ_TILE = 512
Q_TILE = 512
KV_TILE = SEQ_TILE
EXPERT_ROWS = 512
COMBINE_PARTS = 4
SC_CORES = 2
SC_SUBCORES = 16
SC_WORKERS = SC_CORES * SC_SUBCORES
SC_WINDOW = 128
VMEM_LIMIT = 56 * 1024 * 1024

_NT = (((1,), (1,)), ((), ()))


def _dot(a, b):
    return jnp.dot(a, b, preferred_element_type=jnp.float32)


def _dot_nt(a, b):
    return lax.dot_general(a, b, _NT, preferred_element_type=jnp.float32)


def _rms_norm(x, g):
    return x * lax.rsqrt(jnp.mean(x * x, axis=-1, keepdims=True) + RMS_EPS) * g


def _pack_bf16_pairs(x):
    n = x.shape[1] // 2
    bits = lax.bitcast_convert_type(x.astype(jnp.bfloat16).astype(jnp.float32), jnp.uint32)
    return bits[:, :n] | (bits[:, n:] >> 16)


def _unpack_bf16_pairs(p):
    hi = lax.bitcast_convert_type(p & jnp.uint32(0xFFFF0000), jnp.float32)
    lo = lax.bitcast_convert_type(p << 16, jnp.float32)
    return hi, lo


def _split3(x):
    hi = x.astype(jnp.bfloat16)
    r1 = x - hi.astype(jnp.float32)
    mid = r1.astype(jnp.bfloat16)
    lo = (r1 - mid.astype(jnp.float32)).astype(jnp.bfloat16)
    return hi, mid, lo


def _inproj_kernel(x_ref, g_ref, wu_ref, wqt_ref, wk_ref, wvt_ref, wf_ref, bf_ref, tri_ref,
                   pqt_ref, pk_ref, oneq_ref, onek_ref, hsel_ref, wpool_ref, pscale_ref,
                   ypool_ref, qt_ref, qaug_ref, k_ref, kaug_ref, vt_ref, stats_ref, ubuf, ccarry):
    s = pl.program_id(1)
    tm = x_ref.shape[1]

    @pl.when(s == 0)
    def _():
        ubuf[...] = jnp.zeros_like(ubuf)
        ccarry[...] = jnp.zeros_like(ccarry)

    x = x_ref[0]
    h = _rms_norm(x, g_ref[...]).astype(jnp.bfloat16)

    fl = _dot(h, wf_ref[...]) + bf_ref[...]
    logf = jnp.minimum(fl, 0.0) - jnp.log1p(jnp.exp(-jnp.abs(fl)))
    lane = lax.broadcasted_iota(jnp.int32, (tm, LANES), 1)
    logf = jnp.where(lane < N_SPLIT * N_HEADS, logf, 0.0)
    f_hi, f_mid, f_lo = _split3(logf)
    tri = tri_ref[...]
    c = _dot(tri, f_hi) + _dot(tri, f_mid) + _dot(tri, f_lo) + ccarry[...]
    ccarry[...] = c[tm - 1:tm, :]
    c2 = c * LOG2_E
    c_hi, c_mid, c_lo = _split3(c2)
    term = lane % N_SPLIT
    c3 = jnp.where(term == 0, c_hi, jnp.where(term == 1, c_mid, c_lo))

    kp = _dot(h, wk_ref[...])
    k_ref[0] = kp.astype(jnp.bfloat16)
    kaug_ref[0] = (_dot(c3, pk_ref[...]) + onek_ref[...]).astype(jnp.bfloat16)
    qt = _dot_nt(wqt_ref[...], h)
    qt_ref[0] = qt.astype(jnp.bfloat16)
    qaug_ref[0] = (_dot_nt(pqt_ref[...], c3) + oneq_ref[...]).astype(jnp.bfloat16)
    lane_row = lax.broadcasted_iota(jnp.int32, (1, LANES), 1)
    kn2 = jnp.max(_dot((kp * kp).astype(jnp.bfloat16), hsel_ref[...]), axis=0, keepdims=True)
    qn2 = jnp.zeros((1, LANES), jnp.float32)
    for hd in range(N_HEADS):
        qs = qt[hd * HEAD_DIM:(hd + 1) * HEAD_DIM, :]
        qmax = jnp.max(jnp.sum(qs * qs, axis=0, keepdims=True), axis=1, keepdims=True)
        qn2 = jnp.where(lane_row == hd, qmax, qn2)
    stats_ref[0, 0, 0:1, :] = kn2
    stats_ref[0, 0, 1:2, :] = qn2
    stats_ref[0, 0, 2:3, :] = c2[0:1, :]
    stats_ref[0, 0, 3:4, :] = c2[tm - 1:tm, :]
    stats_ref[0, 0, 4:8, :] = jnp.zeros((4, LANES), jnp.float32)
    vt = _dot_nt(wvt_ref[...], h).astype(jnp.bfloat16)
    ones = jnp.ones((V_ROWS - HEAD_DIM, tm), jnp.bfloat16)
    for hd in range(N_HEADS):
        vt_ref[0, 0, hd * V_ROWS:hd * V_ROWS + HEAD_DIM, :] = vt[hd * HEAD_DIM:(hd + 1) * HEAD_DIM]
        vt_ref[0, 0, hd * V_ROWS + HEAD_DIM:(hd + 1) * V_ROWS, :] = ones

    u = _dot(h, wu_ref[...])
    head_pos = s * tm + lax.broadcasted_iota(jnp.int32, (MAX_WINDOW, 1), 0)
    for gi, w in enumerate(POOL_WINDOWS):
        cols = slice(gi * POOL_GROUP, (gi + 1) * POOL_GROUP)
        ug = u[:, cols]
        win = jnp.concatenate([ubuf[:, cols], ug], axis=0)
        span = 1
        while span < w:
            win = win + pltpu.roll(win, span, axis=0)
            span *= 2
        acc = win[MAX_WINDOW:]
        inv_head = 1.0 / jnp.minimum(head_pos + 1, w).astype(jnp.float32)
        p = jnp.concatenate([acc[:MAX_WINDOW] * inv_head, acc[MAX_WINDOW:] * (1.0 / w)],
                            axis=0) - ug
        y = _dot(p.astype(jnp.bfloat16), wpool_ref[gi]) * pscale_ref[:, cols]
        ypool_ref[0, :, cols] = y.astype(jnp.bfloat16)
    ubuf[...] = u[tm - MAX_WINDOW:]


def _in_projection(x, g_mix, w_in, b_forget, w_pool, pool_scale):
    B, S, D = x.shape
    tm = SEQ_TILE
    bf16 = jnp.bfloat16
    o = POOL_WIDTH
    w_u = w_in[:, :o].astype(bf16)
    w_q = w_in[:, o:o + ATTN_WIDTH] * (HEAD_DIM ** -0.5 * LOG2_E)
    w_k = w_in[:, o + ATTN_WIDTH:o + 2 * ATTN_WIDTH]
    w_v = w_in[:, o + 2 * ATTN_WIDTH:o + 3 * ATTN_WIDTH]
    w_f = w_in[:, o + 3 * ATTN_WIDTH:]

    wqt = w_q.T.astype(bf16)
    wk = w_k.astype(bf16)
    wvt = w_v.T.astype(bf16)
    n_gate = N_SPLIT * N_HEADS
    wf = jnp.pad(jnp.repeat(w_f, N_SPLIT, axis=1), ((0, 0), (0, LANES - n_gate))).astype(bf16)
    bfp = jnp.pad(jnp.repeat(b_forget, N_SPLIT), (0, LANES - n_gate)).reshape(1, LANES)

    assert N_HEADS * AUG_GROUP == LANES and 2 * N_SPLIT <= AUG_GROUP
    src = jnp.arange(LANES)
    head, term = src // N_SPLIT, src % N_SPLIT
    dst = jnp.arange(LANES)
    valid = (src < n_gate)[:, None]
    q_hit = valid & (dst[None, :] == (head * AUG_GROUP + term)[:, None])
    k_hit = valid & (dst[None, :] == (head * AUG_GROUP + N_SPLIT + term)[:, None])
    pqt = q_hit.T.astype(bf16)
    pk = -k_hit.astype(bf16)
    slot_pos = dst % AUG_GROUP
    oneq = ((slot_pos >= N_SPLIT) & (slot_pos < 2 * N_SPLIT))
    onek = slot_pos < N_SPLIT
    oneq = oneq.astype(jnp.float32).reshape(-1, 1)
    onek = onek.astype(jnp.float32).reshape(1, -1)
    r = jnp.arange(tm)
    tri = (r[None, :] <= r[:, None]).astype(bf16)
    hsel = (jnp.arange(ATTN_WIDTH)[:, None] // HEAD_DIM == jnp.arange(LANES)[None, :]).astype(bf16)

    full = lambda *shape: pl.BlockSpec(shape, lambda b, s: (0,) * len(shape))
    aw = ATTN_WIDTH
    return pl.pallas_call(
        _inproj_kernel,
        grid=(B, S // tm),
        in_specs=[
            pl.BlockSpec((1, tm, D), lambda b, s: (b, s, 0)),
            full(1, D), full(D, o), full(aw, D), full(D, aw), full(aw, D),
            full(D, LANES), full(1, LANES), full(tm, tm), full(LANES, LANES), full(LANES, LANES),
            full(LANES, 1), full(1, LANES), full(aw, LANES),
            full(len(POOL_WINDOWS), POOL_GROUP, POOL_GROUP),
            full(1, o),
        ],
        out_specs=[
            pl.BlockSpec((1, tm, o), lambda b, s: (b, s, 0)),
            pl.BlockSpec((1, aw, tm), lambda b, s: (b, 0, s)),
            pl.BlockSpec((1, LANES, tm), lambda b, s: (b, 0, s)),
            pl.BlockSpec((1, tm, aw), lambda b, s: (b, s, 0)),
            pl.BlockSpec((1, tm, LANES), lambda b, s: (b, s, 0)),
            pl.BlockSpec((1, 1, N_HEADS * V_ROWS, tm), lambda b, s: (b, s, 0, 0)),
            pl.BlockSpec((1, 1, 8, LANES), lambda b, s: (b, s, 0, 0)),
        ],
        out_shape=[
            jax.ShapeDtypeStruct((B, S, o), bf16),
            jax.ShapeDtypeStruct((B, aw, S), bf16),
            jax.ShapeDtypeStruct((B, LANES, S), bf16),
            jax.ShapeDtypeStruct((B, S, aw), bf16),
            jax.ShapeDtypeStruct((B, S, LANES), bf16),
            jax.ShapeDtypeStruct((B, S // tm, N_HEADS * V_ROWS, tm), bf16),
            jax.ShapeDtypeStruct((B, S // tm, 8, LANES), jnp.float32),
        ],
        scratch_shapes=[
            pltpu.VMEM((MAX_WINDOW, o), jnp.float32),
            pltpu.VMEM((1, LANES), jnp.float32),
        ],
        compiler_params=pltpu.CompilerParams(
            dimension_semantics=("arbitrary", "arbitrary"), vmem_limit_bytes=VMEM_LIMIT),
        name="in_projection",
    )(x, g_mix.reshape(1, D), w_u, wqt, wk, wvt, wf, bfp, tri, pqt, pk, oneq, onek, hsel,
      w_pool.astype(bf16), pool_scale.reshape(1, o))


def _attn_kernel(j0_ref, qt_ref, qaug_ref, k_ref, kaug_ref, vt_ref, causal_ref, o_ref,
                 qop_ref, st_buf, mx_buf, m_ref, acc_ref):
    i = pl.program_id(2)
    tile = (pl.program_id(0) * pl.num_programs(1) + pl.program_id(1)) * pl.num_programs(2) + i
    j0 = j0_ref[tile]
    tq = qt_ref.shape[2]
    tk = vt_ref.shape[3]
    heads = qt_ref.shape[1] // HEAD_DIM

    m_ref[...] = jnp.full(m_ref.shape, MASK_VALUE, jnp.float32)
    acc_ref[...] = jnp.zeros_like(acc_ref)

    assert heads == 2
    first_head = pl.program_id(1) * heads
    group = lax.broadcasted_iota(jnp.int32, (LANES, tq), 0) // AUG_GROUP
    q_pair = qt_ref[0]
    q_aug = qaug_ref[0]
    blank = jnp.zeros((HEAD_DIM, tq), q_pair.dtype)
    qop_ref[0, 0:2 * HEAD_DIM, :] = jnp.concatenate([q_pair[:HEAD_DIM], blank], axis=0)
    qop_ref[1, 0:2 * HEAD_DIM, :] = jnp.concatenate([blank, q_pair[HEAD_DIM:]], axis=0)
    for hh in range(heads):
        qop_ref[hh, 2 * HEAD_DIM:, :] = jnp.where(group == first_head + hh, q_aug,
                                                  jnp.zeros_like(q_aug))

    def logits(j, slot, hh, masked):
        row0 = pl.multiple_of(j * tk, tk)
        k = jnp.concatenate([k_ref[0, pl.ds(row0, tk), :], kaug_ref[0, pl.ds(row0, tk), :]],
                            axis=1)
        st = _dot(k, qop_ref[hh])
        if masked:
            st = st + causal_ref[...]
        st_buf[slot, hh] = st
        mx_buf[slot, hh] = jnp.max(st, axis=0, keepdims=True)

    def accumulate(j, slot, hh):
        st = st_buf[slot, hh]
        m_prev = m_ref[hh]
        m_new = jnp.maximum(m_prev, mx_buf[slot, hh])
        alpha = jnp.exp2(m_prev - m_new)
        p = jnp.exp2(st - m_new)
        vt = vt_ref[0, j, hh * V_ROWS:(hh + 1) * V_ROWS, :]
        acc_ref[hh] = alpha * acc_ref[hh] + _dot(vt, p.astype(jnp.bfloat16))
        m_ref[hh] = m_new

    logits(i, 0, 0, True)
    logits(i, 0, 1, True)
    accumulate(i, 0, 0)

    def before(t):
        return jnp.where(t == j0, i, t - 1)

    def step(t, slot):
        logits(t, slot, 0, False)
        accumulate(before(t), 1 - slot, 1)
        logits(t, slot, 1, False)
        accumulate(t, slot, 0)

    def body(u, carry):
        step(2 * u, 1)
        step(2 * u + 1, 0)
        return carry

    lax.fori_loop(j0 // 2, i // 2, body, 0)

    @pl.when(i % 2 == 1)
    def _():
        step(i - 1, 1)
        accumulate(i - 1, 1, 1)

    @pl.when(i % 2 == 0)
    def _():
        accumulate(before(i), 0, 1)

    outs = [acc_ref[hh, :HEAD_DIM] / acc_ref[hh, HEAD_DIM:HEAD_DIM + 1] for hh in range(heads)]
    ot = jnp.concatenate(outs, axis=0)
    o_ref[0] = ot.T.astype(o_ref.dtype)


def _first_needed_chunk(stats):
    kn = jnp.sqrt(stats[:, :, 0, :N_HEADS])
    qn = jnp.sqrt(stats[:, :, 1, :N_HEADS])
    c_first = stats[:, :, 2, 0:N_SPLIT * N_HEADS:N_SPLIT]
    c_last = stats[:, :, 3, 0:N_SPLIT * N_HEADS:N_SPLIT]
    n = stats.shape[1]
    dot_bound = NORM_SLACK * qn[:, :, None, :] * kn[:, None, :, :] + 1.0
    upper = dot_bound + c_first[:, :, None, :] - c_last[:, None, :, :]
    self_bound = NORM_SLACK * qn * kn + 1.0
    ii = jnp.arange(n, dtype=jnp.int32)[None, :, None, None]
    jj = jnp.arange(n, dtype=jnp.int32)[None, None, :, None]
    needed = (upper + self_bound[:, :, None, :] >= -SKIP_MARGIN) & (jj < ii)
    first = jnp.min(jnp.where(needed, jj, ii), axis=2)
    first = jnp.min(first.reshape(first.shape[0], n, N_HEADS // 2, 2), axis=3)
    first = first // 2 * 2
    return jnp.transpose(first, (0, 2, 1)).reshape(-1).astype(jnp.int32)


def _attention(qt, qaug, k, kaug, vt, first_chunk):
    B, _, S = qt.shape
    tq, tk = Q_TILE, KV_TILE
    assert tq == tk, "the diagonal is handled as a single masked key chunk"
    pair = 2
    n_pairs = N_HEADS // pair
    pw = pair * HEAD_DIM
    causal = jnp.where(jnp.arange(tk)[:, None] <= jnp.arange(tq)[None, :], 0.0, MASK_VALUE)
    grid_spec = pltpu.PrefetchScalarGridSpec(
        num_scalar_prefetch=1,
        grid=(B, n_pairs, S // tq),
        in_specs=[
            pl.BlockSpec((1, pw, tq), lambda b, p, i, j0: (b, p, i)),
            pl.BlockSpec((1, LANES, tq), lambda b, p, i, j0: (b, 0, i)),
            pl.BlockSpec((1, S, pw), lambda b, p, i, j0: (b, 0, p)),
            pl.BlockSpec((1, S, LANES), lambda b, p, i, j0: (b, 0, 0)),
            pl.BlockSpec((1, S // tk, pair * V_ROWS, tk), lambda b, p, i, j0: (b, 0, p, 0)),
            pl.BlockSpec((tk, tq), lambda b, p, i, j0: (0, 0)),
        ],
        out_specs=pl.BlockSpec((1, tq, pw), lambda b, p, i, j0: (b, i, p)),
        scratch_shapes=[
            pltpu.VMEM((pair, pw + LANES, tq), jnp.bfloat16),
            pltpu.VMEM((2, pair, tk, tq), jnp.float32),
            pltpu.VMEM((2, pair, 1, tq), jnp.float32),
            pltpu.VMEM((pair, 1, tq), jnp.float32),
            pltpu.VMEM((pair, V_ROWS, tq), jnp.float32),
        ],
    )
    return pl.pallas_call(
        _attn_kernel,
        grid_spec=grid_spec,
        out_shape=jax.ShapeDtypeStruct((B, S, ATTN_WIDTH), jnp.bfloat16),
        compiler_params=pltpu.CompilerParams(
            dimension_semantics=("arbitrary", "arbitrary", "arbitrary"),
            vmem_limit_bytes=VMEM_LIMIT),
        name="forgetting_attention",
    )(first_chunk, qt, qaug, k, kaug, vt, causal.astype(jnp.float32))


def _outproj_kernel(yp_ref, ya_ref, x_ref, wo1_ref, wo2_ref, g_ref, wr2_ref, wrhi_ref, br_ref,
                    upper_ref, x1_ref, h2_ref, idx_ref, gate_ref, rank_ref, cnt_ref, carry):
    t = pl.program_id(0)
    tm = x_ref.shape[0]

    @pl.when(t == 0)
    def _():
        carry[...] = jnp.zeros_like(carry)

    x1 = x_ref[...] + _dot(yp_ref[...], wo1_ref[...]) + _dot(ya_ref[...], wo2_ref[...])
    x1_ref[...] = x1
    h2 = _rms_norm(x1, g_ref[...])
    h2_hi = h2.astype(jnp.bfloat16)
    h2_lo = (h2 - h2_hi.astype(jnp.float32)).astype(jnp.bfloat16)
    h2_ref[...] = _pack_bf16_pairs(h2)

    lg2 = _dot_nt(wr2_ref[...], h2_hi)
    logits = (lg2[:N_EXPERTS] + lg2[N_EXPERTS:] + _dot_nt(wrhi_ref[...], h2_lo) + br_ref[...])

    eio = lax.broadcasted_iota(jnp.int32, (N_EXPERTS, tm), 0)
    vals, idxs = [], []
    cur = logits
    for _ in range(TOP_K):
        m = jnp.max(cur, axis=0, keepdims=True)
        ix = jnp.min(jnp.where(cur == m, eio, N_EXPERTS), axis=0, keepdims=True)
        vals.append(m)
        idxs.append(ix)
        cur = jnp.where(eio == ix, -jnp.inf, cur)
    exps = [jnp.exp(v - vals[0]) for v in vals]
    denom = exps[0] + exps[1] + exps[2] + exps[3]
    hits = [eio == ix for ix in idxs]
    cnt = jnp.zeros((N_EXPERTS, tm), jnp.float32)
    for hit in hits:
        cnt = cnt + hit.astype(jnp.float32)
    base = _dot(cnt.astype(jnp.bfloat16), upper_ref[...]) + carry[...]
    for kk in range(TOP_K):
        idx_ref[kk:kk + 1, :] = idxs[kk]
        gate_ref[kk:kk + 1, :] = exps[kk] / denom
        rank = jnp.sum(jnp.where(hits[kk], base, 0.0), axis=0, keepdims=True)
        rank_ref[kk:kk + 1, :] = rank.astype(jnp.int32)
    carry[...] = carry[...] + jnp.sum(cnt, axis=1, keepdims=True)
    cnt_ref[...] = carry[...]


def _out_projection(ypool, yattn, x, w_out, g_ffn, w_router, b_router):
    T, D = x.shape
    tm = SEQ_TILE
    bf16 = jnp.bfloat16
    wo = w_out.astype(bf16)
    wr_t = w_router.T
    wr_hi = wr_t.astype(bf16)
    wr_lo = (wr_t - wr_hi.astype(jnp.float32)).astype(bf16)
    wr2 = jnp.concatenate([wr_hi, wr_lo], axis=0)
    r = jnp.arange(tm)
    upper = (r[:, None] < r[None, :]).astype(bf16)
    full = lambda *shape: pl.BlockSpec(shape, lambda t: (0,) * len(shape))
    row = lambda w: pl.BlockSpec((tm, w), lambda t: (t, 0))
    col = pl.BlockSpec((TOP_K, tm), lambda t: (0, t))
    return pl.pallas_call(
        _outproj_kernel,
        grid=(T // tm,),
        in_specs=[row(POOL_WIDTH), row(ATTN_WIDTH), row(D), full(POOL_WIDTH, D),
                  full(ATTN_WIDTH, D), full(1, D), full(2 * N_EXPERTS, D), full(N_EXPERTS, D),
                  full(N_EXPERTS, 1), full(tm, tm)],
        out_specs=[row(D), row(D // 2), col, col, col, full(N_EXPERTS, 1)],
        out_shape=[
            jax.ShapeDtypeStruct((T, D), jnp.float32),
            jax.ShapeDtypeStruct((T, D // 2), jnp.uint32),
            jax.ShapeDtypeStruct((TOP_K, T), jnp.int32),
            jax.ShapeDtypeStruct((TOP_K, T), jnp.float32),
            jax.ShapeDtypeStruct((TOP_K, T), jnp.int32),
            jax.ShapeDtypeStruct((N_EXPERTS, 1), jnp.float32),
        ],
        scratch_shapes=[pltpu.VMEM((N_EXPERTS, 1), jnp.float32)],
        compiler_params=pltpu.CompilerParams(
            dimension_semantics=("arbitrary",), vmem_limit_bytes=VMEM_LIMIT),
        name="out_projection_router",
    )(ypool, yattn, x, wo[:POOL_WIDTH], wo[POOL_WIDTH:], g_ffn.reshape(1, D), wr2, wr_hi,
      b_router.reshape(N_EXPERTS, 1), upper)


def _sc_mesh():
    return plsc.VectorSubcoreMesh(core_axis_name="c", subcore_axis_name="s")


def _dispatch_rows(h2, dest, n_rows):
    T, D = h2.shape
    per_worker = T // SC_WORKERS
    steps = per_worker // SC_WINDOW

    @functools.partial(
        pl.kernel, mesh=_sc_mesh(),
        out_type=jax.ShapeDtypeStruct((n_rows, D), h2.dtype),
        scratch_types=[pltpu.VMEM((TOP_K, SC_WINDOW), jnp.int32),
                       pltpu.VMEM((SC_WINDOW, D), h2.dtype),
                       pltpu.SemaphoreType.DMA],
    )
    def scatter_kernel(h2_hbm, dest_hbm, xs_hbm, idx_v, rows_v, sem):
        wid = lax.axis_index("s") * SC_CORES + lax.axis_index("c")

        @pl.loop(0, steps)
        def _(step):
            base = wid * per_worker + step * SC_WINDOW
            pltpu.sync_copy(h2_hbm.at[pl.ds(base, SC_WINDOW)], rows_v)
            pltpu.sync_copy(dest_hbm.at[:, pl.ds(base, SC_WINDOW)], idx_v)
            copies = [pltpu.async_copy(rows_v, xs_hbm.at[idx_v.at[kk]], sem)
                      for kk in range(TOP_K)]
            for c in copies:
                c.wait()

    return scatter_kernel(h2, dest)


def _gather_rows(table, idx):
    N = idx.shape[0]
    D = table.shape[1]
    per_worker = N // SC_WORKERS
    steps = per_worker // SC_WINDOW

    @functools.partial(
        pl.kernel, mesh=_sc_mesh(),
        out_type=jax.ShapeDtypeStruct((N, D), table.dtype),
        scratch_types=[pltpu.VMEM((SC_WINDOW,), jnp.int32),
                       pltpu.VMEM((SC_WINDOW, D), table.dtype),
                       pltpu.SemaphoreType.DMA],
    )
    def gather_kernel(table_hbm, idx_hbm, out_hbm, idx_v, rows_v, sem):
        wid = lax.axis_index("s") * SC_CORES + lax.axis_index("c")

        @pl.loop(0, steps)
        def _(step):
            base = wid * per_worker + step * SC_WINDOW
            pltpu.sync_copy(idx_hbm.at[pl.ds(base, SC_WINDOW)], idx_v)
            pltpu.async_copy(table_hbm.at[idx_v], rows_v, sem).wait()
            pltpu.sync_copy(rows_v, out_hbm.at[pl.ds(base, SC_WINDOW)])

    return gather_kernel(table, idx)


def _expert_kernel(be_ref, nvalid_ref, x_ref, wup_ref, bup_ref, wdn_ref, bdn_ref, y_ref,
                   wup_bf, wdn_bf):
    i = pl.program_id(0)
    blk = x_ref.shape[0]
    changed = jnp.logical_or(i == 0, be_ref[i] != be_ref[jnp.maximum(i - 1, 0)])

    @pl.when(changed)
    def _():
        wup_bf[...] = wup_ref[0].astype(jnp.bfloat16)
        wdn_bf[...] = wdn_ref[0].astype(jnp.bfloat16)

    nvalid = nvalid_ref[i]

    @pl.when(nvalid > 0)
    def _():
        row = lax.broadcasted_iota(jnp.int32, (blk, 1), 0)
        x_hi, x_lo = _unpack_bf16_pairs(jnp.where(row < nvalid, x_ref[...], jnp.uint32(0)))
        x = jnp.concatenate([x_hi.astype(jnp.bfloat16), x_lo.astype(jnp.bfloat16)], axis=1)
        glu = _dot(x, wup_bf[:, :D_EXPERT]) + bup_ref[0, :, :D_EXPERT]
        lin = _dot(x, wup_bf[:, D_EXPERT:]) + bup_ref[0, :, D_EXPERT:]
        glu = jnp.minimum(glu, SWIGLU_LIMIT)
        lin = jnp.clip(lin, -SWIGLU_LIMIT, SWIGLU_LIMIT)
        act = glu * (1.0 / (1.0 + jnp.exp(-SWIGLU_ALPHA * glu))) * (lin + 1.0)
        y = _dot(act.astype(jnp.bfloat16), wdn_bf[...]) + bdn_ref[0]
        y_ref[...] = _pack_bf16_pairs(y)

    @pl.when(nvalid <= 0)
    def _():
        y_ref[...] = jnp.zeros_like(y_ref)


def _expert_ffn(xs, block_e, block_valid, w_up, b_up, w_down, b_down):
    n_rows = xs.shape[0]
    D = D_MODEL
    blk = EXPERT_ROWS
    grid_spec = pltpu.PrefetchScalarGridSpec(
        num_scalar_prefetch=2,
        grid=(n_rows // blk,),
        in_specs=[
            pl.BlockSpec((blk, D // 2), lambda i, be, nv: (i, 0)),
            pl.BlockSpec((1, D, 2 * D_EXPERT), lambda i, be, nv: (be[i], 0, 0)),
            pl.BlockSpec((1, 1, 2 * D_EXPERT), lambda i, be, nv: (be[i], 0, 0)),
            pl.BlockSpec((1, D_EXPERT, D), lambda i, be, nv: (be[i], 0, 0)),
            pl.BlockSpec((1, 1, D), lambda i, be, nv: (be[i], 0, 0)),
        ],
        out_specs=pl.BlockSpec((blk, D // 2), lambda i, be, nv: (i, 0)),
        scratch_shapes=[pltpu.VMEM((D, 2 * D_EXPERT), jnp.bfloat16),
                        pltpu.VMEM((D_EXPERT, D), jnp.bfloat16)],
    )
    return pl.pallas_call(
        _expert_kernel,
        grid_spec=grid_spec,
        out_shape=jax.ShapeDtypeStruct((n_rows, D // 2), jnp.uint32),
        compiler_params=pltpu.CompilerParams(
            dimension_semantics=("arbitrary",), vmem_limit_bytes=VMEM_LIMIT),
        name="expert_ffn",
    )(block_e, block_valid, xs, w_up, b_up.reshape(N_EXPERTS, 1, 2 * D_EXPERT), w_down,
      b_down.reshape(N_EXPERTS, 1, D))


def _final_kernel(x1_ref, rows_ref, gate_ref, g_ref, o_ref):
    half = x1_ref.shape[1] // 2
    acc_hi = x1_ref[:, :half]
    acc_lo = x1_ref[:, half:]
    for kk in range(TOP_K):
        y_hi, y_lo = _unpack_bf16_pairs(rows_ref[kk])
        gate = gate_ref[:, kk:kk + 1]
        acc_hi = acc_hi + y_hi * gate
        acc_lo = acc_lo + y_lo * gate
    o_ref[...] = _rms_norm(jnp.concatenate([acc_hi, acc_lo], axis=1), g_ref[...])


def _final_alias_kernel(x1_ref, rows_ref, gate_ref, g_ref, prev_ref, o_ref):
    del prev_ref
    _final_kernel(x1_ref, rows_ref, gate_ref, g_ref, o_ref)


def _final_norm(x1, rows, gates_t, g_final, part, prev_out):
    T, D = x1.shape
    Tp = rows.shape[1]
    tm = SEQ_TILE
    t0 = part * (Tp // tm)
    row = pl.BlockSpec((tm, D), lambda t: (t + t0, 0))
    in_specs = [row, pl.BlockSpec((TOP_K, tm, D // 2), lambda t: (0, t, 0)),
                pl.BlockSpec((tm, TOP_K), lambda t: (t + t0, 0)),
                pl.BlockSpec((1, D), lambda t: (0, 0))]
    args = [x1, rows, gates_t, g_final.reshape(1, D)]
    body, aliases = _final_kernel, {}
    if prev_out is not None:
        in_specs.append(pl.BlockSpec(memory_space=pl.ANY))
        args.append(prev_out)
        body, aliases = _final_alias_kernel, {len(args) - 1: 0}
    return pl.pallas_call(
        body,
        grid=(Tp // tm,),
        in_specs=in_specs,
        out_specs=row,
        out_shape=jax.ShapeDtypeStruct((T, D), jnp.float32),
        input_output_aliases=aliases,
        compiler_params=pltpu.CompilerParams(dimension_semantics=("arbitrary",)),
        name="combine_final_norm",
    )(*args)


def kernel(x, g_mix, w_in, b_forget, w_pool, pool_scale, w_out, g_ffn, w_router, b_router,
           w_up, b_up, w_down, b_down, g_final):
    B, S, D = x.shape
    T = B * S
    ypool, qt, qaug, k, kaug, vt, stats = _in_projection(x, g_mix[0], w_in[0], b_forget[0],
                                                        w_pool[0], pool_scale[0])
    yattn = _attention(qt, qaug, k, kaug, vt, _first_needed_chunk(stats))
    x1, h2, top_idx, gates, rank, counts = _out_projection(
        ypool.reshape(T, POOL_WIDTH), yattn.reshape(T, ATTN_WIDTH), x.reshape(T, D),
        w_out[0], g_ffn[0], w_router[0], b_router[0])

    blk = EXPERT_ROWS
    counts = counts[:, 0].astype(jnp.int32)
    padded = (counts + blk - 1) // blk * blk
    pad_ends = jnp.cumsum(padded)
    pad_starts = pad_ends - padded
    dest = rank
    for e in range(N_EXPERTS):
        dest = dest + jnp.where(top_idx == e, pad_starts[e], 0)
    n_rows = T * TOP_K + N_EXPERTS * blk
    block_row0 = jnp.arange(n_rows // blk, dtype=jnp.int32) * blk
    block_e = jnp.sum((pad_ends[None, :] <= block_row0[:, None]).astype(jnp.int32), axis=1)
    block_e = jnp.minimum(block_e, N_EXPERTS - 1)
    of_block = block_e[:, None] == jnp.arange(N_EXPERTS, dtype=jnp.int32)[None, :]
    block_end = jnp.sum(jnp.where(of_block, (pad_starts + counts)[None, :], 0), axis=1)
    block_valid = jnp.clip(block_end - block_row0, 0, blk)
    block_valid = jnp.where(block_row0 < pad_ends[-1], block_valid, 0).astype(jnp.int32)

    xs = _dispatch_rows(h2, dest, n_rows)
    ys = _expert_ffn(xs, block_e, block_valid, w_up[0], b_up[0], w_down[0], b_down[0])
    gates_t = gates.T
    tp = T // COMBINE_PARTS
    out = None
    for part in range(COMBINE_PARTS):
        part_dest = dest[:, part * tp:(part + 1) * tp].reshape(-1)
        rows = _gather_rows(ys, part_dest).reshape(TOP_K, tp, D // 2)
        out = _final_norm(x1, rows, gates_t, g_final, part, out)
    return out.reshape(B, S, D)
```

```python
import functools

import jax
import jax.numpy as jnp
from jax import lax
from jax.experimental import pallas as pl
from jax.experimental.pallas import tpu as pltpu
from jax.experimental.pallas import tpu_sc as plsc

D_MODEL = 1024
POOL_WIDTH = 512
POOL_WINDOWS = (2, 4, 8, 16)
POOL_GROUP = 128
MAX_WINDOW = 16
ATTN_WIDTH = 512
HEAD_DIM = 64
N_HEADS = 8
N_EXPERTS = 32
TOP_K = 4
D_EXPERT = 1024
SWIGLU_LIMIT = 7.0
SWIGLU_ALPHA = 1.702
RMS_EPS = 1e-5

LANES = 128
AUG_GROUP = 16
N_SPLIT = 3
MASK_VALUE = -1e30
LOG2_E = 1.4426950408889634
SKIP_MARGIN = 160.0
NORM_SLACK = 1.02
V_ROWS = 80

SEQ_TILE = 512
Q_TILE = 512
HEADS_PER_STEP = 4
KV_TILE = SEQ_TILE
EXPERT_ROWS = 512
COMBINE_PARTS = 4
SC_CORES = 2
SC_SUBCORES = 16
SC_WORKERS = SC_CORES * SC_SUBCORES
SC_WINDOW = 128
VMEM_LIMIT = 56 * 1024 * 1024

_NT = (((1,), (1,)), ((), ()))


def _dot(a, b):
    return jnp.dot(a, b, preferred_element_type=jnp.float32)


def _dot_nt(a, b):
    return lax.dot_general(a, b, _NT, preferred_element_type=jnp.float32)


def _rms_norm(x, g):
    return x * lax.rsqrt(jnp.mean(x * x, axis=-1, keepdims=True) + RMS_EPS) * g


def _pack_bf16_pairs(x):
    n = x.shape[1] // 2
    bits = lax.bitcast_convert_type(x.astype(jnp.bfloat16).astype(jnp.float32), jnp.uint32)
    return bits[:, :n] | (bits[:, n:] >> 16)


def _unpack_bf16_pairs(p):
    hi = lax.bitcast_convert_type(p & jnp.uint32(0xFFFF0000), jnp.float32)
    lo = lax.bitcast_convert_type(p << 16, jnp.float32)
    return hi, lo


def _split3(x):
    hi = x.astype(jnp.bfloat16)
    r1 = x - hi.astype(jnp.float32)
    mid = r1.astype(jnp.bfloat16)
    lo = (r1 - mid.astype(jnp.float32)).astype(jnp.bfloat16)
    return hi, mid, lo


def _inproj_kernel(x_ref, g_ref, wu_ref, wqt_ref, wk_ref, wvt_ref, wf_ref, bf_ref, tri_ref,
                   pqt_ref, pk_ref, oneq_ref, onek_ref, hsel_ref, wpool_ref, pscale_ref,
                   ypool_ref, qt_ref, qaug_ref, k_ref, kaug_ref, vt_ref, stats_ref, ubuf, ccarry):
    s = pl.program_id(1)
    tm = x_ref.shape[1]

    @pl.when(s == 0)
    def _():
        ubuf[...] = jnp.zeros_like(ubuf)
        ccarry[...] = jnp.zeros_like(ccarry)

    x = x_ref[0]
    h = _rms_norm(x, g_ref[...]).astype(jnp.bfloat16)

    fl = _dot(h, wf_ref[...]) + bf_ref[...]
    logf = jnp.minimum(fl, 0.0) - jnp.log1p(jnp.exp(-jnp.abs(fl)))
    lane = lax.broadcasted_iota(jnp.int32, (tm, LANES), 1)
    logf = jnp.where(lane < N_SPLIT * N_HEADS, logf, 0.0)
    f_hi, f_mid, f_lo = _split3(logf)
    tri = tri_ref[...]
    c = _dot(tri, f_hi) + _dot(tri, f_mid) + _dot(tri, f_lo) + ccarry[...]
    ccarry[...] = c[tm - 1:tm, :]
    c2 = c * LOG2_E
    c_hi, c_mid, c_lo = _split3(c2)
    term = lane % N_SPLIT
    c3 = jnp.where(term == 0, c_hi, jnp.where(term == 1, c_mid, c_lo))

    kp = _dot(h, wk_ref[...])
    k_ref[0] = kp.astype(jnp.bfloat16)
    kaug_ref[0] = (_dot(c3, pk_ref[...]) + onek_ref[...]).astype(jnp.bfloat16)
    qt = _dot_nt(wqt_ref[...], h)
    qt_ref[0] = qt.astype(jnp.bfloat16)
    qaug_ref[0] = (_dot_nt(pqt_ref[...], c3) + oneq_ref[...]).astype(jnp.bfloat16)
    lane_row = lax.broadcasted_iota(jnp.int32, (1, LANES), 1)
    kn2 = jnp.max(_dot((kp * kp).astype(jnp.bfloat16), hsel_ref[...]), axis=0, keepdims=True)
    qn2 = jnp.zeros((1, LANES), jnp.float32)
    for hd in range(N_HEADS):
        qs = qt[hd * HEAD_DIM:(hd + 1) * HEAD_DIM, :]
        qmax = jnp.max(jnp.sum(qs * qs, axis=0, keepdims=True), axis=1, keepdims=True)
        qn2 = jnp.where(lane_row == hd, qmax, qn2)
    stats_ref[0, 0, 0:1, :] = kn2
    stats_ref[0, 0, 1:2, :] = qn2
    stats_ref[0, 0, 2:3, :] = c2[0:1, :]
    stats_ref[0, 0, 3:4, :] = c2[tm - 1:tm, :]
    stats_ref[0, 0, 4:8, :] = jnp.zeros((4, LANES), jnp.float32)
    vt = _dot_nt(wvt_ref[...], h).astype(jnp.bfloat16)
    ones = jnp.ones((V_ROWS - HEAD_DIM, tm), jnp.bfloat16)
    for hd in range(N_HEADS):
        vt_ref[0, 0, hd * V_ROWS:hd * V_ROWS + HEAD_DIM, :] = vt[hd * HEAD_DIM:(hd + 1) * HEAD_DIM]
        vt_ref[0, 0, hd * V_ROWS + HEAD_DIM:(hd + 1) * V_ROWS, :] = ones

    u = _dot(h, wu_ref[...])
    head_pos = s * tm + lax.broadcasted_iota(jnp.int32, (MAX_WINDOW, 1), 0)
    for gi, w in enumerate(POOL_WINDOWS):
        cols = slice(gi * POOL_GROUP, (gi + 1) * POOL_GROUP)
        ug = u[:, cols]
        win = jnp.concatenate([ubuf[:, cols], ug], axis=0)
        span = 1
        while span < w:
            win = win + pltpu.roll(win, span, axis=0)
            span *= 2
        acc = win[MAX_WINDOW:]
        inv_head = 1.0 / jnp.minimum(head_pos + 1, w).astype(jnp.float32)
        p = jnp.concatenate([acc[:MAX_WINDOW] * inv_head, acc[MAX_WINDOW:] * (1.0 / w)],
                            axis=0) - ug
        y = _dot(p.astype(jnp.bfloat16), wpool_ref[gi]) * pscale_ref[:, cols]
        ypool_ref[0, :, cols] = y.astype(jnp.bfloat16)
    ubuf[...] = u[tm - MAX_WINDOW:]


def _in_projection(x, g_mix, w_in, b_forget, w_pool, pool_scale):
    B, S, D = x.shape
    tm = SEQ_TILE
    bf16 = jnp.bfloat16
    o = POOL_WIDTH
    w_u = w_in[:, :o].astype(bf16)
    w_q = w_in[:, o:o + ATTN_WIDTH] * (HEAD_DIM ** -0.5 * LOG2_E)
    w_k = w_in[:, o + ATTN_WIDTH:o + 2 * ATTN_WIDTH]
    w_v = w_in[:, o + 2 * ATTN_WIDTH:o + 3 * ATTN_WIDTH]
    w_f = w_in[:, o + 3 * ATTN_WIDTH:]

    wqt = w_q.T.astype(bf16)
    wk = w_k.astype(bf16)
    wvt = w_v.T.astype(bf16)
    n_gate = N_SPLIT * N_HEADS
    wf = jnp.pad(jnp.repeat(w_f, N_SPLIT, axis=1), ((0, 0), (0, LANES - n_gate))).astype(bf16)
    bfp = jnp.pad(jnp.repeat(b_forget, N_SPLIT), (0, LANES - n_gate)).reshape(1, LANES)

    assert N_HEADS * AUG_GROUP == LANES and 2 * N_SPLIT <= AUG_GROUP
    src = jnp.arange(LANES)
    head, term = src // N_SPLIT, src % N_SPLIT
    dst = jnp.arange(LANES)
    valid = (src < n_gate)[:, None]
    q_hit = valid & (dst[None, :] == (head * AUG_GROUP + term)[:, None])
    k_hit = valid & (dst[None, :] == (head * AUG_GROUP + N_SPLIT + term)[:, None])
    pqt = q_hit.T.astype(bf16)
    pk = -k_hit.astype(bf16)
    slot_pos = dst % AUG_GROUP
    oneq = ((slot_pos >= N_SPLIT) & (slot_pos < 2 * N_SPLIT))
    onek = slot_pos < N_SPLIT
    oneq = oneq.astype(jnp.float32).reshape(-1, 1)
    onek = onek.astype(jnp.float32).reshape(1, -1)
    r = jnp.arange(tm)
    tri = (r[None, :] <= r[:, None]).astype(bf16)
    hsel = (jnp.arange(ATTN_WIDTH)[:, None] // HEAD_DIM == jnp.arange(LANES)[None, :]).astype(bf16)

    full = lambda *shape: pl.BlockSpec(shape, lambda b, s: (0,) * len(shape))
    aw = ATTN_WIDTH
    return pl.pallas_call(
        _inproj_kernel,
        grid=(B, S // tm),
        in_specs=[
            pl.BlockSpec((1, tm, D), lambda b, s: (b, s, 0)),
            full(1, D), full(D, o), full(aw, D), full(D, aw), full(aw, D),
            full(D, LANES), full(1, LANES), full(tm, tm), full(LANES, LANES), full(LANES, LANES),
            full(LANES, 1), full(1, LANES), full(aw, LANES),
            full(len(POOL_WINDOWS), POOL_GROUP, POOL_GROUP),
            full(1, o),
        ],
        out_specs=[
            pl.BlockSpec((1, tm, o), lambda b, s: (b, s, 0)),
            pl.BlockSpec((1, aw, tm), lambda b, s: (b, 0, s)),
            pl.BlockSpec((1, LANES, tm), lambda b, s: (b, 0, s)),
            pl.BlockSpec((1, tm, aw), lambda b, s: (b, s, 0)),
            pl.BlockSpec((1, tm, LANES), lambda b, s: (b, s, 0)),
            pl.BlockSpec((1, 1, N_HEADS * V_ROWS, tm), lambda b, s: (b, s, 0, 0)),
            pl.BlockSpec((1, 1, 8, LANES), lambda b, s: (b, s, 0, 0)),
        ],
        out_shape=[
            jax.ShapeDtypeStruct((B, S, o), bf16),
            jax.ShapeDtypeStruct((B, aw, S), bf16),
            jax.ShapeDtypeStruct((B, LANES, S), bf16),
            jax.ShapeDtypeStruct((B, S, aw), bf16),
            jax.ShapeDtypeStruct((B, S, LANES), bf16),
            jax.ShapeDtypeStruct((B, S // tm, N_HEADS * V_ROWS, tm), bf16),
            jax.ShapeDtypeStruct((B, S // tm, 8, LANES), jnp.float32),
        ],
        scratch_shapes=[
            pltpu.VMEM((MAX_WINDOW, o), jnp.float32),
            pltpu.VMEM((1, LANES), jnp.float32),
        ],
        compiler_params=pltpu.CompilerParams(
            dimension_semantics=("arbitrary", "arbitrary"), vmem_limit_bytes=VMEM_LIMIT),
        name="in_projection",
    )(x, g_mix.reshape(1, D), w_u, wqt, wk, wvt, wf, bfp, tri, pqt, pk, oneq, onek, hsel,
      w_pool.astype(bf16), pool_scale.reshape(1, o))


def _attn_kernel(j0_ref, qt_ref, qaug_ref, k_ref, kaug_ref, vt_ref, causal_ref, o_ref,
                 qop_ref, st_buf, mx_buf, m_ref, acc_ref):
    i = pl.program_id(2)
    tile = (pl.program_id(0) * pl.num_programs(1) + pl.program_id(1)) * pl.num_programs(2) + i
    j0 = j0_ref[tile]
    tq = qt_ref.shape[2]
    tk = vt_ref.shape[3]
    heads = qt_ref.shape[1] // HEAD_DIM

    m_ref[...] = jnp.full(m_ref.shape, MASK_VALUE, jnp.float32)
    acc_ref[...] = jnp.zeros_like(acc_ref)

    first_head = pl.program_id(1) * heads
    group = lax.broadcasted_iota(jnp.int32, (LANES, tq), 0) // AUG_GROUP
    q_aug = qaug_ref[0]
    blank = jnp.zeros((HEAD_DIM, tq), q_aug.dtype)
    for hh in range(heads):
        q_own = qt_ref[0, hh * HEAD_DIM:(hh + 1) * HEAD_DIM, :]
        pair_rows = [q_own, blank] if hh % 2 == 0 else [blank, q_own]
        qop_ref[hh, 0:2 * HEAD_DIM, :] = jnp.concatenate(pair_rows, axis=0)
        qop_ref[hh, 2 * HEAD_DIM:, :] = jnp.where(group == first_head + hh, q_aug,
                                                  jnp.zeros_like(q_aug))

    def logits(j, slot, hh, masked):
        row0 = pl.multiple_of(j * tk, tk)
        pair_lanes = slice(hh // 2 * 2 * HEAD_DIM, (hh // 2 + 1) * 2 * HEAD_DIM)
        k = jnp.concatenate([k_ref[0, pl.ds(row0, tk), pair_lanes],
                             kaug_ref[0, pl.ds(row0, tk), :]], axis=1)
        st = _dot(k, qop_ref[hh])
        if masked:
            st = st + causal_ref[...]
        st_buf[slot, hh] = st
        mx_buf[slot, hh] = jnp.max(st, axis=0, keepdims=True)

    def accumulate(j, slot, hh):
        st = st_buf[slot, hh]
        m_prev = m_ref[hh]
        m_new = jnp.maximum(m_prev, mx_buf[slot, hh])
        alpha = jnp.exp2(m_prev - m_new)
        p = jnp.exp2(st - m_new)
        vt = vt_ref[0, j, hh * V_ROWS:(hh + 1) * V_ROWS, :]
        acc_ref[hh] = alpha * acc_ref[hh] + _dot(vt, p.astype(jnp.bfloat16))
        m_ref[hh] = m_new

    last = heads - 1
    logits(i, 0, 0, True)
    for hh in range(1, heads):
        logits(i, 0, hh, True)
        accumulate(i, 0, hh - 1)

    def before(t):
        return jnp.where(t == j0, i, t - 1)

    def step(t, slot):
        logits(t, slot, 0, False)
        accumulate(before(t), 1 - slot, last)
        for hh in range(1, heads):
            logits(t, slot, hh, False)
            accumulate(t, slot, hh - 1)

    def body(u, carry):
        step(j0 + 2 * u, 1)
        step(j0 + 2 * u + 1, 0)
        return carry

    n_chunks = i - j0
    lax.fori_loop(0, n_chunks // 2, body, 0)

    @pl.when(n_chunks % 2 == 1)
    def _():
        step(i - 1, 1)
        accumulate(i - 1, 1, last)

    @pl.when(n_chunks % 2 == 0)
    def _():
        accumulate(before(i), 0, last)

    outs = [acc_ref[hh, :HEAD_DIM] / acc_ref[hh, HEAD_DIM:HEAD_DIM + 1] for hh in range(heads)]
    ot = jnp.concatenate(outs, axis=0)
    o_ref[0] = ot.T.astype(o_ref.dtype)


def _first_needed_chunk(stats):
    kn = jnp.sqrt(stats[:, :, 0, :N_HEADS])
    qn = jnp.sqrt(stats[:, :, 1, :N_HEADS])
    c_first = stats[:, :, 2, 0:N_SPLIT * N_HEADS:N_SPLIT]
    c_last = stats[:, :, 3, 0:N_SPLIT * N_HEADS:N_SPLIT]
    n = stats.shape[1]
    dot_bound = NORM_SLACK * qn[:, :, None, :] * kn[:, None, :, :] + 1.0
    upper = dot_bound + c_first[:, :, None, :] - c_last[:, None, :, :]
    self_bound = NORM_SLACK * qn * kn + 1.0
    ii = jnp.arange(n, dtype=jnp.int32)[None, :, None, None]
    jj = jnp.arange(n, dtype=jnp.int32)[None, None, :, None]
    needed = (upper + self_bound[:, :, None, :] >= -SKIP_MARGIN) & (jj < ii)
    first = jnp.min(jnp.where(needed, jj, ii), axis=2)
    g = HEADS_PER_STEP
    first = jnp.min(first.reshape(first.shape[0], n, N_HEADS // g, g), axis=3)
    return jnp.transpose(first, (0, 2, 1)).reshape(-1).astype(jnp.int32)


def _attention(qt, qaug, k, kaug, vt, first_chunk):
    B, _, S = qt.shape
    tq, tk = Q_TILE, KV_TILE
    assert tq == tk, "the diagonal is handled as a single masked key chunk"
    pair = HEADS_PER_STEP
    n_pairs = N_HEADS // pair
    pw = pair * HEAD_DIM
    causal = jnp.where(jnp.arange(tk)[:, None] <= jnp.arange(tq)[None, :], 0.0, MASK_VALUE)
    grid_spec = pltpu.PrefetchScalarGridSpec(
        num_scalar_prefetch=1,
        grid=(B, n_pairs, S // tq),
        in_specs=[
            pl.BlockSpec((1, pw, tq), lambda b, p, i, j0: (b, p, i)),
            pl.BlockSpec((1, LANES, tq), lambda b, p, i, j0: (b, 0, i)),
            pl.BlockSpec((1, S, pw), lambda b, p, i, j0: (b, 0, p)),
            pl.BlockSpec((1, S, LANES), lambda b, p, i, j0: (b, 0, 0)),
            pl.BlockSpec((1, S // tk, pair * V_ROWS, tk), lambda b, p, i, j0: (b, 0, p, 0)),
            pl.BlockSpec((tk, tq), lambda b, p, i, j0: (0, 0)),
        ],
        out_specs=pl.BlockSpec((1, tq, pw), lambda b, p, i, j0: (b, i, p)),
        scratch_shapes=[
            pltpu.VMEM((pair, 2 * HEAD_DIM + LANES, tq), jnp.bfloat16),
            pltpu.VMEM((2, pair, tk, tq), jnp.float32),
            pltpu.VMEM((2, pair, 1, tq), jnp.float32),
            pltpu.VMEM((pair, 1, tq), jnp.float32),
            pltpu.VMEM((pair, V_ROWS, tq), jnp.float32),
        ],
    )
    return pl.pallas_call(
        _attn_kernel,
        grid_spec=grid_spec,
        out_shape=jax.ShapeDtypeStruct((B, S, ATTN_WIDTH), jnp.bfloat16),
        compiler_params=pltpu.CompilerParams(
            dimension_semantics=("arbitrary", "arbitrary", "arbitrary"),
            vmem_limit_bytes=VMEM_LIMIT),
        name="forgetting_attention",
    )(first_chunk, qt, qaug, k, kaug, vt, causal.astype(jnp.float32))


def _outproj_kernel(yp_ref, ya_ref, x_ref, wo1_ref, wo2_ref, g_ref, wr2_ref, wrhi_ref, br_ref,
                    upper_ref, x1_ref, h2_ref, idx_ref, gate_ref, rank_ref, cnt_ref, carry):
    t = pl.program_id(0)
    tm = x_ref.shape[0]

    @pl.when(t == 0)
    def _():
        carry[...] = jnp.zeros_like(carry)

    x1 = x_ref[...] + _dot(yp_ref[...], wo1_ref[...]) + _dot(ya_ref[...], wo2_ref[...])
    x1_ref[...] = x1
    h2 = _rms_norm(x1, g_ref[...])
    h2_hi = h2.astype(jnp.bfloat16)
    h2_lo = (h2 - h2_hi.astype(jnp.float32)).astype(jnp.bfloat16)
    h2_ref[...] = _pack_bf16_pairs(h2)

    lg2 = _dot_nt(wr2_ref[...], h2_hi)
    logits = (lg2[:N_EXPERTS] + lg2[N_EXPERTS:] + _dot_nt(wrhi_ref[...], h2_lo) + br_ref[...])

    eio = lax.broadcasted_iota(jnp.int32, (N_EXPERTS, tm), 0)
    vals, idxs = [], []
    cur = logits
    for _ in range(TOP_K):
        m = jnp.max(cur, axis=0, keepdims=True)
        ix = jnp.min(jnp.where(cur == m, eio, N_EXPERTS), axis=0, keepdims=True)
        vals.append(m)
        idxs.append(ix)
        cur = jnp.where(eio == ix, -jnp.inf, cur)
    exps = [jnp.exp(v - vals[0]) for v in vals]
    denom = exps[0] + exps[1] + exps[2] + exps[3]
    hits = [eio == ix for ix in idxs]
    cnt = jnp.zeros((N_EXPERTS, tm), jnp.float32)
    for hit in hits:
        cnt = cnt + hit.astype(jnp.float32)
    base = _dot(cnt.astype(jnp.bfloat16), upper_ref[...]) + carry[...]
    for kk in range(TOP_K):
        idx_ref[kk:kk + 1, :] = idxs[kk]
        gate_ref[kk:kk + 1, :] = exps[kk] / denom
        rank = jnp.sum(jnp.where(hits[kk], base, 0.0), axis=0, keepdims=True)
        rank_ref[kk:kk + 1, :] = rank.astype(jnp.int32)
    carry[...] = carry[...] + jnp.sum(cnt, axis=1, keepdims=True)
    cnt_ref[...] = carry[...]


def _out_projection(ypool, yattn, x, w_out, g_ffn, w_router, b_router):
    T, D = x.shape
    tm = SEQ_TILE
    bf16 = jnp.bfloat16
    wo = w_out.astype(bf16)
    wr_t = w_router.T
    wr_hi = wr_t.astype(bf16)
    wr_lo = (wr_t - wr_hi.astype(jnp.float32)).astype(bf16)
    wr2 = jnp.concatenate([wr_hi, wr_lo], axis=0)
    r = jnp.arange(tm)
    upper = (r[:, None] < r[None, :]).astype(bf16)
    full = lambda *shape: pl.BlockSpec(shape, lambda t: (0,) * len(shape))
    row = lambda w: pl.BlockSpec((tm, w), lambda t: (t, 0))
    col = pl.BlockSpec((TOP_K, tm), lambda t: (0, t))
    return pl.pallas_call(
        _outproj_kernel,
        grid=(T // tm,),
        in_specs=[row(POOL_WIDTH), row(ATTN_WIDTH), row(D), full(POOL_WIDTH, D),
                  full(ATTN_WIDTH, D), full(1, D), full(2 * N_EXPERTS, D), full(N_EXPERTS, D),
                  full(N_EXPERTS, 1), full(tm, tm)],
        out_specs=[row(D), row(D // 2), col, col, col, full(N_EXPERTS, 1)],
        out_shape=[
            jax.ShapeDtypeStruct((T, D), jnp.float32),
            jax.ShapeDtypeStruct((T, D // 2), jnp.uint32),
            jax.ShapeDtypeStruct((TOP_K, T), jnp.int32),
            jax.ShapeDtypeStruct((TOP_K, T), jnp.float32),
            jax.ShapeDtypeStruct((TOP_K, T), jnp.int32),
            jax.ShapeDtypeStruct((N_EXPERTS, 1), jnp.float32),
        ],
        scratch_shapes=[pltpu.VMEM((N_EXPERTS, 1), jnp.float32)],
        compiler_params=pltpu.CompilerParams(
            dimension_semantics=("arbitrary",), vmem_limit_bytes=VMEM_LIMIT),
        name="out_projection_router",
    )(ypool, yattn, x, wo[:POOL_WIDTH], wo[POOL_WIDTH:], g_ffn.reshape(1, D), wr2, wr_hi,
      b_router.reshape(N_EXPERTS, 1), upper)


def _sc_mesh():
    return plsc.VectorSubcoreMesh(core_axis_name="c", subcore_axis_name="s")


def _dispatch_rows(h2, dest, n_rows):
    T, D = h2.shape
    per_worker = T // SC_WORKERS
    steps = per_worker // SC_WINDOW

    @functools.partial(
        pl.kernel, mesh=_sc_mesh(),
        out_type=jax.ShapeDtypeStruct((n_rows, D), h2.dtype),
        scratch_types=[pltpu.VMEM((TOP_K, SC_WINDOW), jnp.int32),
                       pltpu.VMEM((SC_WINDOW, D), h2.dtype),
                       pltpu.SemaphoreType.DMA],
    )
    def scatter_kernel(h2_hbm, dest_hbm, xs_hbm, idx_v, rows_v, sem):
        wid = lax.axis_index("s") * SC_CORES + lax.axis_index("c")

        @pl.loop(0, steps)
        def _(step):
            base = wid * per_worker + step * SC_WINDOW
            pltpu.sync_copy(h2_hbm.at[pl.ds(base, SC_WINDOW)], rows_v)
            pltpu.sync_copy(dest_hbm.at[:, pl.ds(base, SC_WINDOW)], idx_v)
            copies = [pltpu.async_copy(rows_v, xs_hbm.at[idx_v.at[kk]], sem)
                      for kk in range(TOP_K)]
            for c in copies:
                c.wait()

    return scatter_kernel(h2, dest)


def _gather_rows(table, idx):
    N = idx.shape[0]
    D = table.shape[1]
    per_worker = N // SC_WORKERS
    steps = per_worker // SC_WINDOW

    @functools.partial(
        pl.kernel, mesh=_sc_mesh(),
        out_type=jax.ShapeDtypeStruct((N, D), table.dtype),
        scratch_types=[pltpu.VMEM((SC_WINDOW,), jnp.int32),
                       pltpu.VMEM((SC_WINDOW, D), table.dtype),
                       pltpu.SemaphoreType.DMA],
    )
    def gather_kernel(table_hbm, idx_hbm, out_hbm, idx_v, rows_v, sem):
        wid = lax.axis_index("s") * SC_CORES + lax.axis_index("c")

        @pl.loop(0, steps)
        def _(step):
            base = wid * per_worker + step * SC_WINDOW
            pltpu.sync_copy(idx_hbm.at[pl.ds(base, SC_WINDOW)], idx_v)
            pltpu.async_copy(table_hbm.at[idx_v], rows_v, sem).wait()
            pltpu.sync_copy(rows_v, out_hbm.at[pl.ds(base, SC_WINDOW)])

    return gather_kernel(table, idx)


def _expert_kernel(be_ref, nvalid_ref, x_ref, wup_ref, bup_ref, wdn_ref, bdn_ref, y_ref,
                   wup_bf, wdn_bf):
    i = pl.program_id(0)
    blk = x_ref.shape[0]
    changed = jnp.logical_or(i == 0, be_ref[i] != be_ref[jnp.maximum(i - 1, 0)])

    @pl.when(changed)
    def _():
        wup_bf[...] = wup_ref[0].astype(jnp.bfloat16)
        wdn_bf[...] = wdn_ref[0].astype(jnp.bfloat16)

    nvalid = nvalid_ref[i]

    @pl.when(nvalid > 0)
    def _():
        row = lax.broadcasted_iota(jnp.int32, (blk, 1), 0)
        x_hi, x_lo = _unpack_bf16_pairs(jnp.where(row < nvalid, x_ref[...], jnp.uint32(0)))
        x = jnp.concatenate([x_hi.astype(jnp.bfloat16), x_lo.astype(jnp.bfloat16)], axis=1)
        glu = _dot(x, wup_bf[:, :D_EXPERT]) + bup_ref[0, :, :D_EXPERT]
        lin = _dot(x, wup_bf[:, D_EXPERT:]) + bup_ref[0, :, D_EXPERT:]
        glu = jnp.minimum(glu, SWIGLU_LIMIT)
        lin = jnp.clip(lin, -SWIGLU_LIMIT, SWIGLU_LIMIT)
        act = glu * (1.0 / (1.0 + jnp.exp(-SWIGLU_ALPHA * glu))) * (lin + 1.0)
        y = _dot(act.astype(jnp.bfloat16), wdn_bf[...]) + bdn_ref[0]
        y_ref[...] = _pack_bf16_pairs(y)

    @pl.when(nvalid <= 0)
    def _():
        y_ref[...] = jnp.zeros_like(y_ref)


def _expert_ffn(xs, block_e, block_valid, w_up, b_up, w_down, b_down):
    n_rows = xs.shape[0]
    D = D_MODEL
    blk = EXPERT_ROWS
    grid_spec = pltpu.PrefetchScalarGridSpec(
        num_scalar_prefetch=2,
        grid=(n_rows // blk,),
        in_specs=[
            pl.BlockSpec((blk, D // 2), lambda i, be, nv: (i, 0)),
            pl.BlockSpec((1, D, 2 * D_EXPERT), lambda i, be, nv: (be[i], 0, 0)),
            pl.BlockSpec((1, 1, 2 * D_EXPERT), lambda i, be, nv: (be[i], 0, 0)),
            pl.BlockSpec((1, D_EXPERT, D), lambda i, be, nv: (be[i], 0, 0)),
            pl.BlockSpec((1, 1, D), lambda i, be, nv: (be[i], 0, 0)),
        ],
        out_specs=pl.BlockSpec((blk, D // 2), lambda i, be, nv: (i, 0)),
        scratch_shapes=[pltpu.VMEM((D, 2 * D_EXPERT), jnp.bfloat16),
                        pltpu.VMEM((D_EXPERT, D), jnp.bfloat16)],
    )
    return pl.pallas_call(
        _expert_kernel,
        grid_spec=grid_spec,
        out_shape=jax.ShapeDtypeStruct((n_rows, D // 2), jnp.uint32),
        compiler_params=pltpu.CompilerParams(
            dimension_semantics=("arbitrary",), vmem_limit_bytes=VMEM_LIMIT),
        name="expert_ffn",
    )(block_e, block_valid, xs, w_up, b_up.reshape(N_EXPERTS, 1, 2 * D_EXPERT), w_down,
      b_down.reshape(N_EXPERTS, 1, D))


def _final_kernel(x1_ref, rows_ref, gate_ref, g_ref, o_ref):
    half = x1_ref.shape[1] // 2
    acc_hi = x1_ref[:, :half]
    acc_lo = x1_ref[:, half:]
    for kk in range(TOP_K):
        y_hi, y_lo = _unpack_bf16_pairs(rows_ref[kk])
        gate = gate_ref[:, kk:kk + 1]
        acc_hi = acc_hi + y_hi * gate
        acc_lo = acc_lo + y_lo * gate
    o_ref[...] = _rms_norm(jnp.concatenate([acc_hi, acc_lo], axis=1), g_ref[...])


def _final_alias_kernel(x1_ref, rows_ref, gate_ref, g_ref, prev_ref, o_ref):
    del prev_ref
    _final_kernel(x1_ref, rows_ref, gate_ref, g_ref, o_ref)


def _final_norm(x1, rows, gates_t, g_final, part, prev_out):
    T, D = x1.shape
    Tp = rows.shape[1]
    tm = SEQ_TILE
    t0 = part * (Tp // tm)
    row = pl.BlockSpec((tm, D), lambda t: (t + t0, 0))
    in_specs = [row, pl.BlockSpec((TOP_K, tm, D // 2), lambda t: (0, t, 0)),
                pl.BlockSpec((tm, TOP_K), lambda t: (t + t0, 0)),
                pl.BlockSpec((1, D), lambda t: (0, 0))]
    args = [x1, rows, gates_t, g_final.reshape(1, D)]
    body, aliases = _final_kernel, {}
    if prev_out is not None:
        in_specs.append(pl.BlockSpec(memory_space=pl.ANY))
        args.append(prev_out)
        body, aliases = _final_alias_kernel, {len(args) - 1: 0}
    return pl.pallas_call(
        body,
        grid=(Tp // tm,),
        in_specs=in_specs,
        out_specs=row,
        out_shape=jax.ShapeDtypeStruct((T, D), jnp.float32),
        input_output_aliases=aliases,
        compiler_params=pltpu.CompilerParams(dimension_semantics=("arbitrary",)),
        name="combine_final_norm",
    )(*args)


def kernel(x, g_mix, w_in, b_forget, w_pool, pool_scale, w_out, g_ffn, w_router, b_router,
           w_up, b_up, w_down, b_down, g_final):
    B, S, D = x.shape
    T = B * S
    ypool, qt, qaug, k, kaug, vt, stats = _in_projection(x, g_mix[0], w_in[0], b_forget[0],
                                                        w_pool[0], pool_scale[0])
    yattn = _attention(qt, qaug, k, kaug, vt, _first_needed_chunk(stats))
    x1, h2, top_idx, gates, rank, counts = _out_projection(
        ypool.reshape(T, POOL_WIDTH), yattn.reshape(T, ATTN_WIDTH), x.reshape(T, D),
        w_out[0], g_ffn[0], w_router[0], b_router[0])

    blk = EXPERT_ROWS
    counts = counts[:, 0].astype(jnp.int32)
    padded = (counts + blk - 1) // blk * blk
    pad_ends = jnp.cumsum(padded)
    pad_starts = pad_ends - padded
    dest = rank
    for e in range(N_EXPERTS):
        dest = dest + jnp.where(top_idx == e, pad_starts[e], 0)
    n_rows = T * TOP_K + N_EXPERTS * blk
    block_row0 = jnp.arange(n_rows // blk, dtype=jnp.int32) * blk
    block_e = jnp.sum((pad_ends[None, :] <= block_row0[:, None]).astype(jnp.int32), axis=1)
    block_e = jnp.minimum(block_e, N_EXPERTS - 1)
    of_block = block_e[:, None] == jnp.arange(N_EXPERTS, dtype=jnp.int32)[None, :]
    block_end = jnp.sum(jnp.where(of_block, (pad_starts + counts)[None, :], 0), axis=1)
    block_valid = jnp.clip(block_end - block_row0, 0, blk)
    block_valid = jnp.where(block_row0 < pad_ends[-1], block_valid, 0).astype(jnp.int32)

    xs = _dispatch_rows(h2, dest, n_rows)
    ys = _expert_ffn(xs, block_e, block_valid, w_up[0], b_up[0], w_down[0], b_down[0])
    gates_t = gates.T
    tp = T // COMBINE_PARTS
    out = None
    for part in range(COMBINE_PARTS):
        part_dest = dest[:, part * tp:(part + 1) * tp].reshape(-1)
        rows = _gather_rows(ys, part_dest).reshape(TOP_K, tp, D // 2)
        out = _final_norm(x1, rows, gates_t, g_final, part, out)
    return out.reshape(B, S, D)
```

```python
import functools

import jax
import jax.numpy as jnp
from jax import lax
from jax.experimental import pallas as pl
from jax.experimental.pallas import tpu as pltpu
from jax.experimental.pallas import tpu_sc as plsc

D_MODEL = 1024
POOL_WIDTH = 512
POOL_WINDOWS = (2, 4, 8, 16)
POOL_GROUP = 128
MAX_WINDOW = 16
ATTN_WIDTH = 512
HEAD_DIM = 64
N_HEADS = 8
N_EXPERTS = 32
TOP_K = 4
D_EXPERT = 1024
SWIGLU_LIMIT = 7.0
SWIGLU_ALPHA = 1.702
RMS_EPS = 1e-5

LANES = 128
AUG_GROUP = 16
N_SPLIT = 3
MASK_VALUE = -1e30
LOG2_E = 1.4426950408889634
SKIP_MARGIN = 160.0
NORM_SLACK = 1.02
V_ROWS = 80

SEQ_TILE = 512
Q_TILE = 512
HEADS_PER_STEP = 4
KV_TILE = SEQ_TILE
EXPERT_ROWS = 512
COMBINE_PARTS = 4
SC_CORES = 2
SC_SUBCORES = 16
SC_WORKERS = SC_CORES * SC_SUBCORES
SC_WINDOW = 128
VMEM_LIMIT = 56 * 1024 * 1024

_NT = (((1,), (1,)), ((), ()))


def _dot(a, b):
    return jnp.dot(a, b, preferred_element_type=jnp.float32)


def _dot_nt(a, b):
    return lax.dot_general(a, b, _NT, preferred_element_type=jnp.float32)


def _rms_norm(x, g):
    return x * lax.rsqrt(jnp.mean(x * x, axis=-1, keepdims=True) + RMS_EPS) * g


def _pack_bf16_pairs(x):
    n = x.shape[1] // 2
    bits = lax.bitcast_convert_type(x.astype(jnp.bfloat16).astype(jnp.float32), jnp.uint32)
    return bits[:, :n] | (bits[:, n:] >> 16)


def _unpack_bf16_pairs(p):
    hi = lax.bitcast_convert_type(p & jnp.uint32(0xFFFF0000), jnp.float32)
    lo = lax.bitcast_convert_type(p << 16, jnp.float32)
    return hi, lo


def _split3(x):
    hi = x.astype(jnp.bfloat16)
    r1 = x - hi.astype(jnp.float32)
    mid = r1.astype(jnp.bfloat16)
    lo = (r1 - mid.astype(jnp.float32)).astype(jnp.bfloat16)
    return hi, mid, lo


def _inproj_kernel(x_ref, g_ref, wu_ref, wqt_ref, wk_ref, wvt_ref, wf_ref, bf_ref, tri_ref,
                   pqt_ref, pk_ref, oneq_ref, onek_ref, hsel_ref, wpool_ref, pscale_ref,
                   ypool_ref, qt_ref, qaug_ref, k_ref, kaug_ref, vt_ref, stats_ref, ubuf, ccarry):
    s = pl.program_id(1)
    tm = x_ref.shape[1]

    @pl.when(s == 0)
    def _():
        ubuf[...] = jnp.zeros_like(ubuf)
        ccarry[...] = jnp.zeros_like(ccarry)

    x = x_ref[0]
    h = _rms_norm(x, g_ref[...]).astype(jnp.bfloat16)

    fl = _dot(h, wf_ref[...]) + bf_ref[...]
    logf = jnp.minimum(fl, 0.0) - jnp.log1p(jnp.exp(-jnp.abs(fl)))
    lane = lax.broadcasted_iota(jnp.int32, (tm, LANES), 1)
    logf = jnp.where(lane < N_SPLIT * N_HEADS, logf, 0.0)
    f_hi, f_mid, f_lo = _split3(logf)
    tri = tri_ref[...]
    c = _dot(tri, f_hi) + _dot(tri, f_mid) + _dot(tri, f_lo) + ccarry[...]
    ccarry[...] = c[tm - 1:tm, :]
    c2 = c * LOG2_E
    c_hi, c_mid, c_lo = _split3(c2)
    term = lane % N_SPLIT
    c3 = jnp.where(term == 0, c_hi, jnp.where(term == 1, c_mid, c_lo))

    kp = _dot(h, wk_ref[...])
    k_ref[0] = kp.astype(jnp.bfloat16)
    kaug_ref[0] = (_dot(c3, pk_ref[...]) + onek_ref[...]).astype(jnp.bfloat16)
    qt = _dot_nt(wqt_ref[...], h)
    qt_ref[0] = qt.astype(jnp.bfloat16)
    qaug_ref[0] = (_dot_nt(pqt_ref[...], c3) + oneq_ref[...]).astype(jnp.bfloat16)
    lane_row = lax.broadcasted_iota(jnp.int32, (1, LANES), 1)
    kn2 = jnp.max(_dot((kp * kp).astype(jnp.bfloat16), hsel_ref[...]), axis=0, keepdims=True)
    qn2 = jnp.zeros((1, LANES), jnp.float32)
    for hd in range(N_HEADS):
        qs = qt[hd * HEAD_DIM:(hd + 1) * HEAD_DIM, :]
        qmax = jnp.max(jnp.sum(qs * qs, axis=0, keepdims=True), axis=1, keepdims=True)
        qn2 = jnp.where(lane_row == hd, qmax, qn2)
    stats_ref[0, 0, 0:1, :] = kn2
    stats_ref[0, 0, 1:2, :] = qn2
    stats_ref[0, 0, 2:3, :] = c2[0:1, :]
    stats_ref[0, 0, 3:4, :] = c2[tm - 1:tm, :]
    stats_ref[0, 0, 4:8, :] = jnp.zeros((4, LANES), jnp.float32)
    vt = _dot_nt(wvt_ref[...], h).astype(jnp.bfloat16)
    ones = jnp.ones((V_ROWS - HEAD_DIM, tm), jnp.bfloat16)
    for hd in range(N_HEADS):
        vt_ref[0, 0, hd * V_ROWS:hd * V_ROWS + HEAD_DIM, :] = vt[hd * HEAD_DIM:(hd + 1) * HEAD_DIM]
        vt_ref[0, 0, hd * V_ROWS + HEAD_DIM:(hd + 1) * V_ROWS, :] = ones

    u = _dot(h, wu_ref[...])
    head_pos = s * tm + lax.broadcasted_iota(jnp.int32, (MAX_WINDOW, 1), 0)
    for gi, w in enumerate(POOL_WINDOWS):
        cols = slice(gi * POOL_GROUP, (gi + 1) * POOL_GROUP)
        ug = u[:, cols]
        win = jnp.concatenate([ubuf[:, cols], ug], axis=0)
        span = 1
        while span < w:
            win = win + pltpu.roll(win, span, axis=0)
            span *= 2
        acc = win[MAX_WINDOW:]
        inv_head = 1.0 / jnp.minimum(head_pos + 1, w).astype(jnp.float32)
        p = jnp.concatenate([acc[:MAX_WINDOW] * inv_head, acc[MAX_WINDOW:] * (1.0 / w)],
                            axis=0) - ug
        y = _dot(p.astype(jnp.bfloat16), wpool_ref[gi]) * pscale_ref[:, cols]
        ypool_ref[0, :, cols] = y.astype(jnp.bfloat16)
    ubuf[...] = u[tm - MAX_WINDOW:]


def _in_projection(x, g_mix, w_in, b_forget, w_pool, pool_scale):
    B, S, D = x.shape
    tm = SEQ_TILE
    bf16 = jnp.bfloat16
    o = POOL_WIDTH
    w_u = w_in[:, :o].astype(bf16)
    w_q = w_in[:, o:o + ATTN_WIDTH] * (HEAD_DIM ** -0.5 * LOG2_E)
    w_k = w_in[:, o + ATTN_WIDTH:o + 2 * ATTN_WIDTH]
    w_v = w_in[:, o + 2 * ATTN_WIDTH:o + 3 * ATTN_WIDTH]
    w_f = w_in[:, o + 3 * ATTN_WIDTH:]

    wqt = w_q.T.astype(bf16)
    wk = w_k.astype(bf16)
    wvt = w_v.T.astype(bf16)
    n_gate = N_SPLIT * N_HEADS
    wf = jnp.pad(jnp.repeat(w_f, N_SPLIT, axis=1), ((0, 0), (0, LANES - n_gate))).astype(bf16)
    bfp = jnp.pad(jnp.repeat(b_forget, N_SPLIT), (0, LANES - n_gate)).reshape(1, LANES)

    assert N_HEADS * AUG_GROUP == LANES and 2 * N_SPLIT <= AUG_GROUP
    src = jnp.arange(LANES)
    head, term = src // N_SPLIT, src % N_SPLIT
    dst = jnp.arange(LANES)
    valid = (src < n_gate)[:, None]
    q_hit = valid & (dst[None, :] == (head * AUG_GROUP + term)[:, None])
    k_hit = valid & (dst[None, :] == (head * AUG_GROUP + N_SPLIT + term)[:, None])
    pqt = q_hit.T.astype(bf16)
    pk = -k_hit.astype(bf16)
    slot_pos = dst % AUG_GROUP
    oneq = ((slot_pos >= N_SPLIT) & (slot_pos < 2 * N_SPLIT))
    onek = slot_pos < N_SPLIT
    oneq = oneq.astype(jnp.float32).reshape(-1, 1)
    onek = onek.astype(jnp.float32).reshape(1, -1)
    r = jnp.arange(tm)
    tri = (r[None, :] <= r[:, None]).astype(bf16)
    hsel = (jnp.arange(ATTN_WIDTH)[:, None] // HEAD_DIM == jnp.arange(LANES)[None, :]).astype(bf16)

    full = lambda *shape: pl.BlockSpec(shape, lambda b, s: (0,) * len(shape))
    aw = ATTN_WIDTH
    return pl.pallas_call(
        _inproj_kernel,
        grid=(B, S // tm),
        in_specs=[
            pl.BlockSpec((1, tm, D), lambda b, s: (b, s, 0)),
            full(1, D), full(D, o), full(aw, D), full(D, aw), full(aw, D),
            full(D, LANES), full(1, LANES), full(tm, tm), full(LANES, LANES), full(LANES, LANES),
            full(LANES, 1), full(1, LANES), full(aw, LANES),
            full(len(POOL_WINDOWS), POOL_GROUP, POOL_GROUP),
            full(1, o),
        ],
        out_specs=[
            pl.BlockSpec((1, tm, o), lambda b, s: (b, s, 0)),
            pl.BlockSpec((1, aw, tm), lambda b, s: (b, 0, s)),
            pl.BlockSpec((1, LANES, tm), lambda b, s: (b, 0, s)),
            pl.BlockSpec((1, tm, aw), lambda b, s: (b, s, 0)),
            pl.BlockSpec((1, tm, LANES), lambda b, s: (b, s, 0)),
            pl.BlockSpec((1, 1, N_HEADS * V_ROWS, tm), lambda b, s: (b, s, 0, 0)),
            pl.BlockSpec((1, 1, 8, LANES), lambda b, s: (b, s, 0, 0)),
        ],
        out_shape=[
            jax.ShapeDtypeStruct((B, S, o), bf16),
            jax.ShapeDtypeStruct((B, aw, S), bf16),
            jax.ShapeDtypeStruct((B, LANES, S), bf16),
            jax.ShapeDtypeStruct((B, S, aw), bf16),
            jax.ShapeDtypeStruct((B, S, LANES), bf16),
            jax.ShapeDtypeStruct((B, S // tm, N_HEADS * V_ROWS, tm), bf16),
            jax.ShapeDtypeStruct((B, S // tm, 8, LANES), jnp.float32),
        ],
        scratch_shapes=[
            pltpu.VMEM((MAX_WINDOW, o), jnp.float32),
            pltpu.VMEM((1, LANES), jnp.float32),
        ],
        compiler_params=pltpu.CompilerParams(
            dimension_semantics=("arbitrary", "arbitrary"), vmem_limit_bytes=VMEM_LIMIT),
        name="in_projection",
    )(x, g_mix.reshape(1, D), w_u, wqt, wk, wvt, wf, bfp, tri, pqt, pk, oneq, onek, hsel,
      w_pool.astype(bf16), pool_scale.reshape(1, o))


def _attn_kernel(j0_ref, qt_ref, qaug_ref, k_ref, kaug_ref, vt_ref, causal_ref, o_ref,
                 qop_ref, st_buf, mx_buf, m_ref, acc_ref):
    i = pl.program_id(2)
    tile = (pl.program_id(0) * pl.num_programs(1) + pl.program_id(1)) * pl.num_programs(2) + i
    j0 = j0_ref[tile]
    tq = qt_ref.shape[2]
    tk = vt_ref.shape[3]
    heads = qt_ref.shape[1] // HEAD_DIM

    m_ref[...] = jnp.full(m_ref.shape, MASK_VALUE, jnp.float32)
    acc_ref[...] = jnp.zeros_like(acc_ref)

    first_head = pl.program_id(1) * heads
    group = lax.broadcasted_iota(jnp.int32, (LANES, tq), 0) // AUG_GROUP
    q_aug = qaug_ref[0]
    blank = jnp.zeros((HEAD_DIM, tq), q_aug.dtype)
    for hh in range(heads):
        q_own = qt_ref[0, hh * HEAD_DIM:(hh + 1) * HEAD_DIM, :]
        pair_rows = [q_own, blank] if hh % 2 == 0 else [blank, q_own]
        qop_ref[hh, 0:2 * HEAD_DIM, :] = jnp.concatenate(pair_rows, axis=0)
        qop_ref[hh, 2 * HEAD_DIM:, :] = jnp.where(group == first_head + hh, q_aug,
                                                  jnp.zeros_like(q_aug))

    def logits(j, slot, hh, masked):
        row0 = pl.multiple_of(j * tk, tk)
        pair_lanes = slice(hh // 2 * 2 * HEAD_DIM, (hh // 2 + 1) * 2 * HEAD_DIM)
        k = jnp.concatenate([k_ref[0, pl.ds(row0, tk), pair_lanes],
                             kaug_ref[0, pl.ds(row0, tk), :]], axis=1)
        st = _dot(k, qop_ref[hh])
        if masked:
            st = st + causal_ref[...]
        st_buf[slot, hh] = st
        mx_buf[slot, hh] = jnp.max(st, axis=0, keepdims=True)

    def accumulate(j, slot, hh):
        st = st_buf[slot, hh]
        m_prev = m_ref[hh]
        m_new = jnp.maximum(m_prev, mx_buf[slot, hh])
        alpha = jnp.exp2(m_prev - m_new)
        p = jnp.exp2(st - m_new)
        vt = vt_ref[0, j, hh * V_ROWS:(hh + 1) * V_ROWS, :]
        acc_ref[hh] = alpha * acc_ref[hh] + _dot(vt, p.astype(jnp.bfloat16))
        m_ref[hh] = m_new

    last = heads - 1
    logits(i, 0, 0, True)
    for hh in range(1, heads):
        logits(i, 0, hh, True)
        accumulate(i, 0, hh - 1)

    def before(t):
        return jnp.where(t == j0, i, t - 1)

    def step(t, slot):
        logits(t, slot, 0, False)
        accumulate(before(t), 1 - slot, last)
        for hh in range(1, heads):
            logits(t, slot, hh, False)
            accumulate(t, slot, hh - 1)

    def body(u, carry):
        step(j0 + 2 * u, 1)
        step(j0 + 2 * u + 1, 0)
        return carry

    n_chunks = i - j0
    lax.fori_loop(0, n_chunks // 2, body, 0)

    @pl.when(n_chunks % 2 == 1)
    def _():
        step(i - 1, 1)
        accumulate(i - 1, 1, last)

    @pl.when(n_chunks % 2 == 0)
    def _():
        accumulate(before(i), 0, last)

    outs = [acc_ref[hh, :HEAD_DIM] / acc_ref[hh, HEAD_DIM:HEAD_DIM + 1] for hh in range(heads)]
    ot = jnp.concatenate(outs, axis=0)
    o_ref[0] = ot.T.astype(o_ref.dtype)


def _first_needed_chunk(stats):
    kn = jnp.sqrt(stats[:, :, 0, :N_HEADS])
    qn = jnp.sqrt(stats[:, :, 1, :N_HEADS])
    c_first = stats[:, :, 2, 0:N_SPLIT * N_HEADS:N_SPLIT]
    c_last = stats[:, :, 3, 0:N_SPLIT * N_HEADS:N_SPLIT]
    n = stats.shape[1]
    dot_bound = NORM_SLACK * qn[:, :, None, :] * kn[:, None, :, :] + 1.0
    upper = dot_bound + c_first[:, :, None, :] - c_last[:, None, :, :]
    self_bound = NORM_SLACK * qn * kn + 1.0
    ii = jnp.arange(n, dtype=jnp.int32)[None, :, None, None]
    jj = jnp.arange(n, dtype=jnp.int32)[None, None, :, None]
    needed = (upper + self_bound[:, :, None, :] >= -SKIP_MARGIN) & (jj < ii)
    first = jnp.min(jnp.where(needed, jj, ii), axis=2)
    g = HEADS_PER_STEP
    first = jnp.min(first.reshape(first.shape[0], n, N_HEADS // g, g), axis=3)
    return jnp.transpose(first, (0, 2, 1)).reshape(-1).astype(jnp.int32)


def _attention(qt, qaug, k, kaug, vt, first_chunk):
    B, _, S = qt.shape
    tq, tk = Q_TILE, KV_TILE
    assert tq == tk, "the diagonal is handled as a single masked key chunk"
    pair = HEADS_PER_STEP
    n_pairs = N_HEADS // pair
    pw = pair * HEAD_DIM
    causal = jnp.where(jnp.arange(tk)[:, None] <= jnp.arange(tq)[None, :], 0.0, MASK_VALUE)
    grid_spec = pltpu.PrefetchScalarGridSpec(
        num_scalar_prefetch=1,
        grid=(B, n_pairs, S // tq),
        in_specs=[
            pl.BlockSpec((1, pw, tq), lambda b, p, i, j0: (b, p, i)),
            pl.BlockSpec((1, LANES, tq), lambda b, p, i, j0: (b, 0, i)),
            pl.BlockSpec((1, S, pw), lambda b, p, i, j0: (b, 0, p)),
            pl.BlockSpec((1, S, LANES), lambda b, p, i, j0: (b, 0, 0)),
            pl.BlockSpec((1, S // tk, pair * V_ROWS, tk), lambda b, p, i, j0: (b, 0, p, 0)),
            pl.BlockSpec((tk, tq), lambda b, p, i, j0: (0, 0)),
        ],
        out_specs=pl.BlockSpec((1, tq, pw), lambda b, p, i, j0: (b, i, p)),
        scratch_shapes=[
            pltpu.VMEM((pair, 2 * HEAD_DIM + LANES, tq), jnp.bfloat16),
            pltpu.VMEM((2, pair, tk, tq), jnp.float32),
            pltpu.VMEM((2, pair, 1, tq), jnp.float32),
            pltpu.VMEM((pair, 1, tq), jnp.float32),
            pltpu.VMEM((pair, V_ROWS, tq), jnp.float32),
        ],
    )
    return pl.pallas_call(
        _attn_kernel,
        grid_spec=grid_spec,
        out_shape=jax.ShapeDtypeStruct((B, S, ATTN_WIDTH), jnp.bfloat16),
        compiler_params=pltpu.CompilerParams(
            dimension_semantics=("arbitrary", "arbitrary", "arbitrary"),
            vmem_limit_bytes=VMEM_LIMIT),
        name="forgetting_attention",
    )(first_chunk, qt, qaug, k, kaug, vt, causal.astype(jnp.float32))


def _outproj_kernel(yp_ref, ya_ref, x_ref, wo1_ref, wo2_ref, g_ref, wr2_ref, wrhi_ref, br_ref,
                    upper_ref, x1_ref, h2_ref, idx_ref, gate_ref, rank_ref, cnt_ref, carry):
    t = pl.program_id(0)
    tm = x_ref.shape[0]

    @pl.when(t == 0)
    def _():
        carry[...] = jnp.zeros_like(carry)

    x1 = x_ref[...] + _dot(yp_ref[...], wo1_ref[...]) + _dot(ya_ref[...], wo2_ref[...])
    x1_ref[...] = x1
    h2 = _rms_norm(x1, g_ref[...])
    h2_hi = h2.astype(jnp.bfloat16)
    h2_lo = (h2 - h2_hi.astype(jnp.float32)).astype(jnp.bfloat16)
    h2_ref[...] = _pack_bf16_pairs(h2)

    lg2 = _dot_nt(wr2_ref[...], h2_hi)
    logits = (lg2[:N_EXPERTS] + lg2[N_EXPERTS:] + _dot_nt(wrhi_ref[...], h2_lo) + br_ref[...])

    eio = lax.broadcasted_iota(jnp.int32, (N_EXPERTS, tm), 0)
    vals, idxs = [], []
    cur = logits
    for _ in range(TOP_K):
        m = jnp.max(cur, axis=0, keepdims=True)
        ix = jnp.min(jnp.where(cur == m, eio, N_EXPERTS), axis=0, keepdims=True)
        vals.append(m)
        idxs.append(ix)
        cur = jnp.where(eio == ix, -jnp.inf, cur)
    exps = [jnp.exp(v - vals[0]) for v in vals]
    denom = exps[0] + exps[1] + exps[2] + exps[3]
    hits = [eio == ix for ix in idxs]
    cnt = jnp.zeros((N_EXPERTS, tm), jnp.float32)
    for hit in hits:
        cnt = cnt + hit.astype(jnp.float32)
    base = _dot(cnt.astype(jnp.bfloat16), upper_ref[...]) + carry[...]
    for kk in range(TOP_K):
        idx_ref[kk:kk + 1, :] = idxs[kk]
        gate_ref[kk:kk + 1, :] = exps[kk] / denom
        rank = jnp.sum(jnp.where(hits[kk], base, 0.0), axis=0, keepdims=True)
        rank_ref[kk:kk + 1, :] = rank.astype(jnp.int32)
    carry[...] = carry[...] + jnp.sum(cnt, axis=1, keepdims=True)
    cnt_ref[...] = carry[...]


def _out_projection(ypool, yattn, x, w_out, g_ffn, w_router, b_router):
    T, D = x.shape
    tm = SEQ_TILE
    bf16 = jnp.bfloat16
    wo = w_out.astype(bf16)
    wr_t = w_router.T
    wr_hi = wr_t.astype(bf16)
    wr_lo = (wr_t - wr_hi.astype(jnp.float32)).astype(bf16)
    wr2 = jnp.concatenate([wr_hi, wr_lo], axis=0)
    r = jnp.arange(tm)
    upper = (r[:, None] < r[None, :]).astype(bf16)
    full = lambda *shape: pl.BlockSpec(shape, lambda t: (0,) * len(shape))
    row = lambda w: pl.BlockSpec((tm, w), lambda t: (t, 0))
    col = pl.BlockSpec((TOP_K, tm), lambda t: (0, t))
    return pl.pallas_call(
        _outproj_kernel,
        grid=(T // tm,),
        in_specs=[row(POOL_WIDTH), row(ATTN_WIDTH), row(D), full(POOL_WIDTH, D),
                  full(ATTN_WIDTH, D), full(1, D), full(2 * N_EXPERTS, D), full(N_EXPERTS, D),
                  full(N_EXPERTS, 1), full(tm, tm)],
        out_specs=[row(D), row(D // 2), col, col, col, full(N_EXPERTS, 1)],
        out_shape=[
            jax.ShapeDtypeStruct((T, D), jnp.float32),
            jax.ShapeDtypeStruct((T, D // 2), jnp.uint32),
            jax.ShapeDtypeStruct((TOP_K, T), jnp.int32),
            jax.ShapeDtypeStruct((TOP_K, T), jnp.float32),
            jax.ShapeDtypeStruct((TOP_K, T), jnp.int32),
            jax.ShapeDtypeStruct((N_EXPERTS, 1), jnp.float32),
        ],
        scratch_shapes=[pltpu.VMEM((N_EXPERTS, 1), jnp.float32)],
        compiler_params=pltpu.CompilerParams(
            dimension_semantics=("arbitrary",), vmem_limit_bytes=VMEM_LIMIT),
        name="out_projection_router",
    )(ypool, yattn, x, wo[:POOL_WIDTH], wo[POOL_WIDTH:], g_ffn.reshape(1, D), wr2, wr_hi,
      b_router.reshape(N_EXPERTS, 1), upper)


def _sc_mesh():
    return plsc.VectorSubcoreMesh(core_axis_name="c", subcore_axis_name="s")


def _dispatch_rows(h2, dest, n_rows):
    T, D = h2.shape
    per_worker = T // SC_WORKERS
    steps = per_worker // SC_WINDOW

    @functools.partial(
        pl.kernel, mesh=_sc_mesh(),
        out_type=jax.ShapeDtypeStruct((n_rows, D), h2.dtype),
        scratch_types=[pltpu.VMEM((TOP_K, SC_WINDOW), jnp.int32),
                       pltpu.VMEM((SC_WINDOW, D), h2.dtype),
                       pltpu.SemaphoreType.DMA],
    )
    def scatter_kernel(h2_hbm, dest_hbm, xs_hbm, idx_v, rows_v, sem):
        wid = lax.axis_index("s") * SC_CORES + lax.axis_index("c")

        @pl.loop(0, steps)
        def _(step):
            base = wid * per_worker + step * SC_WINDOW
            pltpu.sync_copy(h2_hbm.at[pl.ds(base, SC_WINDOW)], rows_v)
            pltpu.sync_copy(dest_hbm.at[:, pl.ds(base, SC_WINDOW)], idx_v)
            copies = [pltpu.async_copy(rows_v, xs_hbm.at[idx_v.at[kk]], sem)
                      for kk in range(TOP_K)]
            for c in copies:
                c.wait()

    return scatter_kernel(h2, dest)


def _gather_rows(table, idx):
    N = idx.shape[0]
    D = table.shape[1]
    per_worker = N // SC_WORKERS
    steps = per_worker // SC_WINDOW

    @functools.partial(
        pl.kernel, mesh=_sc_mesh(),
        out_type=jax.ShapeDtypeStruct((N, D), table.dtype),
        scratch_types=[pltpu.VMEM((SC_WINDOW,), jnp.int32),
                       pltpu.VMEM((SC_WINDOW, D), table.dtype),
                       pltpu.SemaphoreType.DMA],
    )
    def gather_kernel(table_hbm, idx_hbm, out_hbm, idx_v, rows_v, sem):
        wid = lax.axis_index("s") * SC_CORES + lax.axis_index("c")

        @pl.loop(0, steps)
        def _(step):
            base = wid * per_worker + step * SC_WINDOW
            pltpu.sync_copy(idx_hbm.at[pl.ds(base, SC_WINDOW)], idx_v)
            pltpu.async_copy(table_hbm.at[idx_v], rows_v, sem).wait()
            pltpu.sync_copy(rows_v, out_hbm.at[pl.ds(base, SC_WINDOW)])

    return gather_kernel(table, idx)


def _expert_kernel(be_ref, nvalid_ref, x_ref, wup_ref, bup_ref, wdn_ref, bdn_ref, y_ref,
                   wup_bf, wdn_bf):
    i = pl.program_id(0)
    blk = x_ref.shape[0]
    changed = jnp.logical_or(i == 0, be_ref[i] != be_ref[jnp.maximum(i - 1, 0)])

    @pl.when(changed)
    def _():
        wup_bf[...] = wup_ref[0].astype(jnp.bfloat16)
        wdn_bf[...] = wdn_ref[0].astype(jnp.bfloat16)

    nvalid = nvalid_ref[i]
    half = blk // 2

    def ffn(rows):
        row = lax.broadcasted_iota(jnp.int32, (rows, 1), 0)
        x_hi, x_lo = _unpack_bf16_pairs(jnp.where(row < nvalid, x_ref[0:rows], jnp.uint32(0)))
        x = jnp.concatenate([x_hi.astype(jnp.bfloat16), x_lo.astype(jnp.bfloat16)], axis=1)
        glu = _dot(x, wup_bf[:, :D_EXPERT]) + bup_ref[0, :, :D_EXPERT]
        lin = _dot(x, wup_bf[:, D_EXPERT:]) + bup_ref[0, :, D_EXPERT:]
        glu = jnp.minimum(glu, SWIGLU_LIMIT)
        lin = jnp.clip(lin, -SWIGLU_LIMIT, SWIGLU_LIMIT)
        act = glu * (1.0 / (1.0 + jnp.exp(-SWIGLU_ALPHA * glu))) * (lin + 1.0)
        y = _dot(act.astype(jnp.bfloat16), wdn_bf[...]) + bdn_ref[0]
        y_ref[0:rows] = _pack_bf16_pairs(y)
        if rows < blk:
            y_ref[rows:blk] = jnp.zeros((blk - rows, y_ref.shape[1]), y_ref.dtype)

    @pl.when(nvalid > half)
    def _():
        ffn(blk)

    @pl.when(jnp.logical_and(nvalid > 0, nvalid <= half))
    def _():
        ffn(half)

    @pl.when(nvalid <= 0)
    def _():
        y_ref[...] = jnp.zeros_like(y_ref)


def _expert_ffn(xs, block_e, block_valid, w_up, b_up, w_down, b_down):
    n_rows = xs.shape[0]
    D = D_MODEL
    blk = EXPERT_ROWS
    grid_spec = pltpu.PrefetchScalarGridSpec(
        num_scalar_prefetch=2,
        grid=(n_rows // blk,),
        in_specs=[
            pl.BlockSpec((blk, D // 2), lambda i, be, nv: (i, 0)),
            pl.BlockSpec((1, D, 2 * D_EXPERT), lambda i, be, nv: (be[i], 0, 0)),
            pl.BlockSpec((1, 1, 2 * D_EXPERT), lambda i, be, nv: (be[i], 0, 0)),
            pl.BlockSpec((1, D_EXPERT, D), lambda i, be, nv: (be[i], 0, 0)),
            pl.BlockSpec((1, 1, D), lambda i, be, nv: (be[i], 0, 0)),
        ],
        out_specs=pl.BlockSpec((blk, D // 2), lambda i, be, nv: (i, 0)),
        scratch_shapes=[pltpu.VMEM((D, 2 * D_EXPERT), jnp.bfloat16),
                        pltpu.VMEM((D_EXPERT, D), jnp.bfloat16)],
    )
    return pl.pallas_call(
        _expert_kernel,
        grid_spec=grid_spec,
        out_shape=jax.ShapeDtypeStruct((n_rows, D // 2), jnp.uint32),
        compiler_params=pltpu.CompilerParams(
            dimension_semantics=("arbitrary",), vmem_limit_bytes=VMEM_LIMIT),
        name="expert_ffn",
    )(block_e, block_valid, xs, w_up, b_up.reshape(N_EXPERTS, 1, 2 * D_EXPERT), w_down,
      b_down.reshape(N_EXPERTS, 1, D))


def _final_kernel(x1_ref, rows_ref, gate_ref, g_ref, o_ref):
    half = x1_ref.shape[1] // 2
    acc_hi = x1_ref[:, :half]
    acc_lo = x1_ref[:, half:]
    for kk in range(TOP_K):
        y_hi, y_lo = _unpack_bf16_pairs(rows_ref[kk])
        gate = gate_ref[:, kk:kk + 1]
        acc_hi = acc_hi + y_hi * gate
        acc_lo = acc_lo + y_lo * gate
    o_ref[...] = _rms_norm(jnp.concatenate([acc_hi, acc_lo], axis=1), g_ref[...])


def _final_alias_kernel(x1_ref, rows_ref, gate_ref, g_ref, prev_ref, o_ref):
    del prev_ref
    _final_kernel(x1_ref, rows_ref, gate_ref, g_ref, o_ref)


def _final_norm(x1, rows, gates_t, g_final, part, prev_out):
    T, D = x1.shape
    Tp = rows.shape[1]
    tm = SEQ_TILE
    t0 = part * (Tp // tm)
    row = pl.BlockSpec((tm, D), lambda t: (t + t0, 0))
    in_specs = [row, pl.BlockSpec((TOP_K, tm, D // 2), lambda t: (0, t, 0)),
                pl.BlockSpec((tm, TOP_K), lambda t: (t + t0, 0)),
                pl.BlockSpec((1, D), lambda t: (0, 0))]
    args = [x1, rows, gates_t, g_final.reshape(1, D)]
    body, aliases = _final_kernel, {}
    if prev_out is not None:
        in_specs.append(pl.BlockSpec(memory_space=pl.ANY))
        args.append(prev_out)
        body, aliases = _final_alias_kernel, {len(args) - 1: 0}
    return pl.pallas_call(
        body,
        grid=(Tp // tm,),
        in_specs=in_specs,
        out_specs=row,
        out_shape=jax.ShapeDtypeStruct((T, D), jnp.float32),
        input_output_aliases=aliases,
        compiler_params=pltpu.CompilerParams(dimension_semantics=("arbitrary",)),
        name="combine_final_norm",
    )(*args)


def kernel(x, g_mix, w_in, b_forget, w_pool, pool_scale, w_out, g_ffn, w_router, b_router,
           w_up, b_up, w_down, b_down, g_final):
    B, S, D = x.shape
    T = B * S
    ypool, qt, qaug, k, kaug, vt, stats = _in_projection(x, g_mix[0], w_in[0], b_forget[0],
                                                        w_pool[0], pool_scale[0])
    yattn = _attention(qt, qaug, k, kaug, vt, _first_needed_chunk(stats))
    x1, h2, top_idx, gates, rank, counts = _out_projection(
        ypool.reshape(T, POOL_WIDTH), yattn.reshape(T, ATTN_WIDTH), x.reshape(T, D),
        w_out[0], g_ffn[0], w_router[0], b_router[0])

    blk = EXPERT_ROWS
    counts = counts[:, 0].astype(jnp.int32)
    padded = (counts + blk - 1) // blk * blk
    pad_ends = jnp.cumsum(padded)
    pad_starts = pad_ends - padded
    dest = rank
    for e in range(N_EXPERTS):
        dest = dest + jnp.where(top_idx == e, pad_starts[e], 0)
    n_rows = T * TOP_K + N_EXPERTS * blk
    block_row0 = jnp.arange(n_rows // blk, dtype=jnp.int32) * blk
    block_e = jnp.sum((pad_ends[None, :] <= block_row0[:, None]).astype(jnp.int32), axis=1)
    block_e = jnp.minimum(block_e, N_EXPERTS - 1)
    of_block = block_e[:, None] == jnp.arange(N_EXPERTS, dtype=jnp.int32)[None, :]
    block_end = jnp.sum(jnp.where(of_block, (pad_starts + counts)[None, :], 0), axis=1)
    block_valid = jnp.clip(block_end - block_row0, 0, blk)
    block_valid = jnp.where(block_row0 < pad_ends[-1], block_valid, 0).astype(jnp.int32)

    xs = _dispatch_rows(h2, dest, n_rows)
    ys = _expert_ffn(xs, block_e, block_valid, w_up[0], b_up[0], w_down[0], b_down[0])
    gates_t = gates.T
    tp = T // COMBINE_PARTS
    out = None
    for part in range(COMBINE_PARTS):
        part_dest = dest[:, part * tp:(part + 1) * tp].reshape(-1)
        rows = _gather_rows(ys, part_dest).reshape(TOP_K, tp, D // 2)
        out = _final_norm(x1, rows, gates_t, g_final, part, out)
    return out.reshape(B, S, D)
```

```python
import functools

import jax
import jax.numpy as jnp
from jax import lax
from jax.experimental import pallas as pl
from jax.experimental.pallas import tpu as pltpu
from jax.experimental.pallas import tpu_sc as plsc

D_MODEL = 1024
POOL_WIDTH = 512
POOL_WINDOWS = (2, 4, 8, 16)
POOL_GROUP = 128
MAX_WINDOW = 16
ATTN_WIDTH = 512
HEAD_DIM = 64
N_HEADS = 8
N_EXPERTS = 32
TOP_K = 4
D_EXPERT = 1024
SWIGLU_LIMIT = 7.0
SWIGLU_ALPHA = 1.702
RMS_EPS = 1e-5

LANES = 128
SUBLANES = 8
AUG_GROUP = 16
N_SPLIT = 3
MASK_VALUE = -1e30
LOG2_E = 1.4426950408889634
SKIP_MARGIN = 160.0
NORM_SLACK = 1.02
V_ROWS = 80

SEQ_TILE = 512
Q_TILE = 512
HEADS_PER_STEP = 4
KV_TILE = SEQ_TILE
EXPERT_ROWS = 512
COMBINE_PARTS = 8
SC_CORES = 2
SC_SUBCORES = 16
SC_WORKERS = SC_CORES * SC_SUBCORES
SC_WINDOW = 128
VMEM_LIMIT = 56 * 1024 * 1024

_NT = (((1,), (1,)), ((), ()))


def _dot(a, b):
    return jnp.dot(a, b, preferred_element_type=jnp.float32)


def _dot_nt(a, b):
    return lax.dot_general(a, b, _NT, preferred_element_type=jnp.float32)


def _rms_norm(x, g):
    return x * lax.rsqrt(jnp.mean(x * x, axis=-1, keepdims=True) + RMS_EPS) * g


def _pack_bf16_pairs(x):
    n = x.shape[1] // 2
    bits = lax.bitcast_convert_type(x.astype(jnp.bfloat16).astype(jnp.float32), jnp.uint32)
    return bits[:, :n] | (bits[:, n:] >> 16)


def _unpack_bf16_pairs(p):
    hi = lax.bitcast_convert_type(p & jnp.uint32(0xFFFF0000), jnp.float32)
    lo = lax.bitcast_convert_type(p << 16, jnp.float32)
    return hi, lo


def _split3(x):
    hi = x.astype(jnp.bfloat16)
    r1 = x - hi.astype(jnp.float32)
    mid = r1.astype(jnp.bfloat16)
    lo = (r1 - mid.astype(jnp.float32)).astype(jnp.bfloat16)
    return hi, mid, lo


def _inproj_kernel(x_ref, g_ref, wu_ref, wqt_ref, wk_ref, wvt_ref, wf_ref, bf_ref, tri_ref,
                   pqt_ref, pk_ref, oneq_ref, onek_ref, hsel_ref, wpool_ref, pscale_ref,
                   ypool_ref, qt_ref, qaug_ref, k_ref, kaug_ref, vt_ref, stats_ref, ubuf, ccarry):
    s = pl.program_id(1)
    tm = x_ref.shape[1]

    @pl.when(s == 0)
    def _():
        ubuf[...] = jnp.zeros_like(ubuf)
        ccarry[...] = jnp.zeros_like(ccarry)

    x = x_ref[0]
    h = _rms_norm(x, g_ref[...]).astype(jnp.bfloat16)

    fl = _dot(h, wf_ref[...]) + bf_ref[...]
    logf = jnp.minimum(fl, 0.0) - jnp.log1p(jnp.exp(-jnp.abs(fl)))
    lane = lax.broadcasted_iota(jnp.int32, (tm, LANES), 1)
    logf = jnp.where(lane < N_SPLIT * N_HEADS, logf, 0.0)
    f_hi, f_mid, f_lo = _split3(logf)
    tri = tri_ref[...]
    c = _dot(tri, f_hi) + _dot(tri, f_mid) + _dot(tri, f_lo) + ccarry[...]
    ccarry[...] = c[tm - 1:tm, :]
    c2 = c * LOG2_E
    c_hi, c_mid, c_lo = _split3(c2)
    term = lane % N_SPLIT
    c3 = jnp.where(term == 0, c_hi, jnp.where(term == 1, c_mid, c_lo))

    kp = _dot(h, wk_ref[...])
    k_ref[0] = kp.astype(jnp.bfloat16)
    kaug_ref[0] = (_dot(c3, pk_ref[...]) + onek_ref[...]).astype(jnp.bfloat16)
    qt = _dot_nt(wqt_ref[...], h)
    qt_ref[0] = qt.astype(jnp.bfloat16)
    qaug_ref[0] = (_dot_nt(pqt_ref[...], c3) + oneq_ref[...]).astype(jnp.bfloat16)
    lane_row = lax.broadcasted_iota(jnp.int32, (1, LANES), 1)
    kn2 = jnp.max(_dot((kp * kp).astype(jnp.bfloat16), hsel_ref[...]), axis=0, keepdims=True)
    qn2 = jnp.zeros((1, LANES), jnp.float32)
    for hd in range(N_HEADS):
        qs = qt[hd * HEAD_DIM:(hd + 1) * HEAD_DIM, :]
        qmax = jnp.max(jnp.sum(qs * qs, axis=0, keepdims=True), axis=1, keepdims=True)
        qn2 = jnp.where(lane_row == hd, qmax, qn2)
    stats_ref[0, 0, 0:1, :] = kn2
    stats_ref[0, 0, 1:2, :] = qn2
    stats_ref[0, 0, 2:3, :] = c2[0:1, :]
    stats_ref[0, 0, 3:4, :] = c2[tm - 1:tm, :]
    stats_ref[0, 0, 4:SUBLANES, :] = jnp.zeros((SUBLANES - 4, LANES), jnp.float32)
    vt = _dot_nt(wvt_ref[...], h).astype(jnp.bfloat16)
    ones = jnp.ones((V_ROWS - HEAD_DIM, tm), jnp.bfloat16)
    for hd in range(N_HEADS):
        vt_ref[0, 0, hd * V_ROWS:hd * V_ROWS + HEAD_DIM, :] = vt[hd * HEAD_DIM:(hd + 1) * HEAD_DIM]
        vt_ref[0, 0, hd * V_ROWS + HEAD_DIM:(hd + 1) * V_ROWS, :] = ones

    u = _dot(h, wu_ref[...])
    head_pos = s * tm + lax.broadcasted_iota(jnp.int32, (MAX_WINDOW, 1), 0)
    for gi, w in enumerate(POOL_WINDOWS):
        cols = slice(gi * POOL_GROUP, (gi + 1) * POOL_GROUP)
        ug = u[:, cols]
        win = jnp.concatenate([ubuf[:, cols], ug], axis=0)
        span = 1
        while span < w:
            win = win + pltpu.roll(win, span, axis=0)
            span *= 2
        acc = win[MAX_WINDOW:]
        inv_head = 1.0 / jnp.minimum(head_pos + 1, w).astype(jnp.float32)
        p = jnp.concatenate([acc[:MAX_WINDOW] * inv_head, acc[MAX_WINDOW:] * (1.0 / w)],
                            axis=0) - ug
        y = _dot(p.astype(jnp.bfloat16), wpool_ref[gi]) * pscale_ref[:, cols]
        ypool_ref[0, :, cols] = y.astype(jnp.bfloat16)
    ubuf[...] = u[tm - MAX_WINDOW:]


def _in_projection(x, g_mix, w_in, b_forget, w_pool, pool_scale):
    B, S, D = x.shape
    tm = SEQ_TILE
    bf16 = jnp.bfloat16
    o = POOL_WIDTH
    w_u = w_in[:, :o].astype(bf16)
    w_q = w_in[:, o:o + ATTN_WIDTH] * (HEAD_DIM ** -0.5 * LOG2_E)
    w_k = w_in[:, o + ATTN_WIDTH:o + 2 * ATTN_WIDTH]
    w_v = w_in[:, o + 2 * ATTN_WIDTH:o + 3 * ATTN_WIDTH]
    w_f = w_in[:, o + 3 * ATTN_WIDTH:]

    wqt = w_q.T.astype(bf16)
    wk = w_k.astype(bf16)
    wvt = w_v.T.astype(bf16)
    n_gate = N_SPLIT * N_HEADS
    wf = jnp.pad(jnp.repeat(w_f, N_SPLIT, axis=1), ((0, 0), (0, LANES - n_gate))).astype(bf16)
    bfp = jnp.pad(jnp.repeat(b_forget, N_SPLIT), (0, LANES - n_gate)).reshape(1, LANES)

    assert N_HEADS * AUG_GROUP == LANES and 2 * N_SPLIT <= AUG_GROUP
    src = jnp.arange(LANES)
    head, term = src // N_SPLIT, src % N_SPLIT
    dst = jnp.arange(LANES)
    valid = (src < n_gate)[:, None]
    q_hit = valid & (dst[None, :] == (head * AUG_GROUP + term)[:, None])
    k_hit = valid & (dst[None, :] == (head * AUG_GROUP + N_SPLIT + term)[:, None])
    pqt = q_hit.T.astype(bf16)
    pk = -k_hit.astype(bf16)
    slot_pos = dst % AUG_GROUP
    oneq = ((slot_pos >= N_SPLIT) & (slot_pos < 2 * N_SPLIT))
    onek = slot_pos < N_SPLIT
    oneq = oneq.astype(jnp.float32).reshape(-1, 1)
    onek = onek.astype(jnp.float32).reshape(1, -1)
    r = jnp.arange(tm)
    tri = (r[None, :] <= r[:, None]).astype(bf16)
    hsel = (jnp.arange(ATTN_WIDTH)[:, None] // HEAD_DIM == jnp.arange(LANES)[None, :]).astype(bf16)

    full = lambda *shape: pl.BlockSpec(shape, lambda b, s: (0,) * len(shape))
    aw = ATTN_WIDTH
    return pl.pallas_call(
        _inproj_kernel,
        grid=(B, S // tm),
        in_specs=[
            pl.BlockSpec((1, tm, D), lambda b, s: (b, s, 0)),
            full(1, D), full(D, o), full(aw, D), full(D, aw), full(aw, D),
            full(D, LANES), full(1, LANES), full(tm, tm), full(LANES, LANES), full(LANES, LANES),
            full(LANES, 1), full(1, LANES), full(aw, LANES),
            full(len(POOL_WINDOWS), POOL_GROUP, POOL_GROUP),
            full(1, o),
        ],
        out_specs=[
            pl.BlockSpec((1, tm, o), lambda b, s: (b, s, 0)),
            pl.BlockSpec((1, aw, tm), lambda b, s: (b, 0, s)),
            pl.BlockSpec((1, LANES, tm), lambda b, s: (b, 0, s)),
            pl.BlockSpec((1, tm, aw), lambda b, s: (b, s, 0)),
            pl.BlockSpec((1, tm, LANES), lambda b, s: (b, s, 0)),
            pl.BlockSpec((1, 1, N_HEADS * V_ROWS, tm), lambda b, s: (b, s, 0, 0)),
            pl.BlockSpec((1, 1, SUBLANES, LANES), lambda b, s: (b, s, 0, 0)),
        ],
        out_shape=[
            jax.ShapeDtypeStruct((B, S, o), bf16),
            jax.ShapeDtypeStruct((B, aw, S), bf16),
            jax.ShapeDtypeStruct((B, LANES, S), bf16),
            jax.ShapeDtypeStruct((B, S, aw), bf16),
            jax.ShapeDtypeStruct((B, S, LANES), bf16),
            jax.ShapeDtypeStruct((B, S // tm, N_HEADS * V_ROWS, tm), bf16),
            jax.ShapeDtypeStruct((B, S // tm, SUBLANES, LANES), jnp.float32),
        ],
        scratch_shapes=[
            pltpu.VMEM((MAX_WINDOW, o), jnp.float32),
            pltpu.VMEM((1, LANES), jnp.float32),
        ],
        compiler_params=pltpu.CompilerParams(
            dimension_semantics=("arbitrary", "arbitrary"), vmem_limit_bytes=VMEM_LIMIT),
        name="in_projection",
    )(x, g_mix.reshape(1, D), w_u, wqt, wk, wvt, wf, bfp, tri, pqt, pk, oneq, onek, hsel,
      w_pool.astype(bf16), pool_scale.reshape(1, o))


def _attn_kernel(j0_ref, qt_ref, qaug_ref, k_ref, kaug_ref, vt_ref, causal_ref, o_ref,
                 qop_ref, st_buf, mx_buf, m_ref, acc_ref):
    i = pl.program_id(2)
    tile = (pl.program_id(0) * pl.num_programs(1) + pl.program_id(1)) * pl.num_programs(2) + i
    j0 = j0_ref[tile]
    tq = qt_ref.shape[2]
    tk = vt_ref.shape[3]
    heads = qt_ref.shape[1] // HEAD_DIM

    m_ref[...] = jnp.full(m_ref.shape, MASK_VALUE, jnp.float32)
    acc_ref[...] = jnp.zeros_like(acc_ref)

    first_head = pl.program_id(1) * heads
    group = lax.broadcasted_iota(jnp.int32, (LANES, tq), 0) // AUG_GROUP
    q_aug = qaug_ref[0]
    blank = jnp.zeros((HEAD_DIM, tq), q_aug.dtype)
    for hh in range(heads):
        q_own = qt_ref[0, hh * HEAD_DIM:(hh + 1) * HEAD_DIM, :]
        pair_rows = [q_own, blank] if hh % 2 == 0 else [blank, q_own]
        qop_ref[hh, 0:2 * HEAD_DIM, :] = jnp.concatenate(pair_rows, axis=0)
        qop_ref[hh, 2 * HEAD_DIM:, :] = jnp.where(group == first_head + hh, q_aug,
                                                  jnp.zeros_like(q_aug))

    def logits(j, slot, hh, masked):
        row0 = pl.multiple_of(j * tk, tk)
        pair_lanes = slice(hh // 2 * 2 * HEAD_DIM, (hh // 2 + 1) * 2 * HEAD_DIM)
        k = jnp.concatenate([k_ref[0, pl.ds(row0, tk), pair_lanes],
                             kaug_ref[0, pl.ds(row0, tk), :]], axis=1)
        st = _dot(k, qop_ref[hh])
        if masked:
            st = st + causal_ref[...]
        st_buf[slot, hh] = st
        mx_buf[slot, hh] = jnp.max(st, axis=0, keepdims=True)

    def accumulate(j, slot, hh):
        st = st_buf[slot, hh]
        m_prev = m_ref[hh]
        m_new = jnp.maximum(m_prev, mx_buf[slot, hh])
        alpha = jnp.exp2(m_prev - m_new)
        p = jnp.exp2(st - m_new)
        vt = vt_ref[0, j, hh * V_ROWS:(hh + 1) * V_ROWS, :]
        acc_ref[hh] = alpha * acc_ref[hh] + _dot(vt, p.astype(jnp.bfloat16))
        m_ref[hh] = m_new

    last = heads - 1
    logits(i, 0, 0, True)
    for hh in range(1, heads):
        logits(i, 0, hh, True)
        accumulate(i, 0, hh - 1)

    def before(t):
        return jnp.where(t == j0, i, t - 1)

    def step(t, slot):
        logits(t, slot, 0, False)
        accumulate(before(t), 1 - slot, last)
        for hh in range(1, heads):
            logits(t, slot, hh, False)
            accumulate(t, slot, hh - 1)

    def body(u, carry):
        step(j0 + 2 * u, 1)
        step(j0 + 2 * u + 1, 0)
        return carry

    n_chunks = i - j0
    lax.fori_loop(0, n_chunks // 2, body, 0)

    @pl.when(n_chunks % 2 == 1)
    def _():
        step(i - 1, 1)
        accumulate(i - 1, 1, last)

    @pl.when(n_chunks % 2 == 0)
    def _():
        accumulate(before(i), 0, last)

    outs = [acc_ref[hh, :HEAD_DIM] / acc_ref[hh, HEAD_DIM:HEAD_DIM + 1] for hh in range(heads)]
    ot = jnp.concatenate(outs, axis=0)
    o_ref[0] = ot.T.astype(o_ref.dtype)


def _first_needed_chunk(stats):
    kn = jnp.sqrt(stats[:, :, 0, :N_HEADS])
    qn = jnp.sqrt(stats[:, :, 1, :N_HEADS])
    c_first = stats[:, :, 2, 0:N_SPLIT * N_HEADS:N_SPLIT]
    c_last = stats[:, :, 3, 0:N_SPLIT * N_HEADS:N_SPLIT]
    n = stats.shape[1]
    dot_bound = NORM_SLACK * qn[:, :, None, :] * kn[:, None, :, :] + 1.0
    upper = dot_bound + c_first[:, :, None, :] - c_last[:, None, :, :]
    self_bound = NORM_SLACK * qn * kn + 1.0
    ii = jnp.arange(n, dtype=jnp.int32)[None, :, None, None]
    jj = jnp.arange(n, dtype=jnp.int32)[None, None, :, None]
    needed = (upper + self_bound[:, :, None, :] >= -SKIP_MARGIN) & (jj < ii)
    first = jnp.min(jnp.where(needed, jj, ii), axis=2)
    g = HEADS_PER_STEP
    first = jnp.min(first.reshape(first.shape[0], n, N_HEADS // g, g), axis=3)
    return jnp.transpose(first, (0, 2, 1)).reshape(-1).astype(jnp.int32)


def _attention(qt, qaug, k, kaug, vt, first_chunk):
    B, _, S = qt.shape
    tq, tk = Q_TILE, KV_TILE
    assert tq == tk, "the diagonal is handled as a single masked key chunk"
    pair = HEADS_PER_STEP
    n_pairs = N_HEADS // pair
    pw = pair * HEAD_DIM
    causal = jnp.where(jnp.arange(tk)[:, None] <= jnp.arange(tq)[None, :], 0.0, MASK_VALUE)
    grid_spec = pltpu.PrefetchScalarGridSpec(
        num_scalar_prefetch=1,
        grid=(B, n_pairs, S // tq),
        in_specs=[
            pl.BlockSpec((1, pw, tq), lambda b, p, i, j0: (b, p, i)),
            pl.BlockSpec((1, LANES, tq), lambda b, p, i, j0: (b, 0, i)),
            pl.BlockSpec((1, S, pw), lambda b, p, i, j0: (b, 0, p)),
            pl.BlockSpec((1, S, LANES), lambda b, p, i, j0: (b, 0, 0)),
            pl.BlockSpec((1, S // tk, pair * V_ROWS, tk), lambda b, p, i, j0: (b, 0, p, 0)),
            pl.BlockSpec((tk, tq), lambda b, p, i, j0: (0, 0)),
        ],
        out_specs=pl.BlockSpec((1, tq, pw), lambda b, p, i, j0: (b, i, p)),
        scratch_shapes=[
            pltpu.VMEM((pair, 2 * HEAD_DIM + LANES, tq), jnp.bfloat16),
            pltpu.VMEM((2, pair, tk, tq), jnp.float32),
            pltpu.VMEM((2, pair, 1, tq), jnp.float32),
            pltpu.VMEM((pair, 1, tq), jnp.float32),
            pltpu.VMEM((pair, V_ROWS, tq), jnp.float32),
        ],
    )
    return pl.pallas_call(
        _attn_kernel,
        grid_spec=grid_spec,
        out_shape=jax.ShapeDtypeStruct((B, S, ATTN_WIDTH), jnp.bfloat16),
        compiler_params=pltpu.CompilerParams(
            dimension_semantics=("arbitrary", "arbitrary", "arbitrary"),
            vmem_limit_bytes=VMEM_LIMIT),
        name="forgetting_attention",
    )(first_chunk, qt, qaug, k, kaug, vt, causal.astype(jnp.float32))


def _outproj_kernel(yp_ref, ya_ref, x_ref, wo1_ref, wo2_ref, g_ref, wr2_ref, wrhi_ref, br_ref,
                    upper_ref, x1_ref, h2_ref, idx_ref, gate_ref, rank_ref, cnt_ref, carry):
    t = pl.program_id(0)
    tm = x_ref.shape[0]

    @pl.when(t == 0)
    def _():
        carry[...] = jnp.zeros_like(carry)

    x1 = x_ref[...] + _dot(yp_ref[...], wo1_ref[...]) + _dot(ya_ref[...], wo2_ref[...])
    x1_ref[...] = x1
    h2 = _rms_norm(x1, g_ref[...])
    h2_hi = h2.astype(jnp.bfloat16)
    h2_lo = (h2 - h2_hi.astype(jnp.float32)).astype(jnp.bfloat16)
    h2_ref[...] = _pack_bf16_pairs(h2)

    lg2 = _dot_nt(wr2_ref[...], h2_hi)
    logits = (lg2[:N_EXPERTS] + lg2[N_EXPERTS:] + _dot_nt(wrhi_ref[...], h2_lo) + br_ref[...])

    eio = lax.broadcasted_iota(jnp.int32, (N_EXPERTS, tm), 0)
    vals, idxs = [], []
    cur = logits
    for _ in range(TOP_K):
        m = jnp.max(cur, axis=0, keepdims=True)
        ix = jnp.min(jnp.where(cur == m, eio, N_EXPERTS), axis=0, keepdims=True)
        vals.append(m)
        idxs.append(ix)
        cur = jnp.where(eio == ix, -jnp.inf, cur)
    exps = [jnp.exp(v - vals[0]) for v in vals]
    denom = exps[0] + exps[1] + exps[2] + exps[3]
    hits = [eio == ix for ix in idxs]
    cnt = jnp.zeros((N_EXPERTS, tm), jnp.float32)
    for hit in hits:
        cnt = cnt + hit.astype(jnp.float32)
    base = _dot(cnt.astype(jnp.bfloat16), upper_ref[...]) + carry[...]
    for kk in range(TOP_K):
        idx_ref[kk:kk + 1, :] = idxs[kk]
        gate_ref[kk:kk + 1, :] = exps[kk] / denom
        rank = jnp.sum(jnp.where(hits[kk], base, 0.0), axis=0, keepdims=True)
        rank_ref[kk:kk + 1, :] = rank.astype(jnp.int32)
    carry[...] = carry[...] + jnp.sum(cnt, axis=1, keepdims=True)
    cnt_ref[...] = carry[...]


def _out_projection(ypool, yattn, x, w_out, g_ffn, w_router, b_router):
    T, D = x.shape
    tm = SEQ_TILE
    bf16 = jnp.bfloat16
    wo = w_out.astype(bf16)
    wr_t = w_router.T
    wr_hi = wr_t.astype(bf16)
    wr_lo = (wr_t - wr_hi.astype(jnp.float32)).astype(bf16)
    wr2 = jnp.concatenate([wr_hi, wr_lo], axis=0)
    r = jnp.arange(tm)
    upper = (r[:, None] < r[None, :]).astype(bf16)
    full = lambda *shape: pl.BlockSpec(shape, lambda t: (0,) * len(shape))
    row = lambda w: pl.BlockSpec((tm, w), lambda t: (t, 0))
    col = pl.BlockSpec((TOP_K, tm), lambda t: (0, t))
    return pl.pallas_call(
        _outproj_kernel,
        grid=(T // tm,),
        in_specs=[row(POOL_WIDTH), row(ATTN_WIDTH), row(D), full(POOL_WIDTH, D),
                  full(ATTN_WIDTH, D), full(1, D), full(2 * N_EXPERTS, D), full(N_EXPERTS, D),
                  full(N_EXPERTS, 1), full(tm, tm)],
        out_specs=[row(D), row(D // 2), col, col, col, full(N_EXPERTS, 1)],
        out_shape=[
            jax.ShapeDtypeStruct((T, D), jnp.float32),
            jax.ShapeDtypeStruct((T, D // 2), jnp.uint32),
            jax.ShapeDtypeStruct((TOP_K, T), jnp.int32),
            jax.ShapeDtypeStruct((TOP_K, T), jnp.float32),
            jax.ShapeDtypeStruct((TOP_K, T), jnp.int32),
            jax.ShapeDtypeStruct((N_EXPERTS, 1), jnp.float32),
        ],
        scratch_shapes=[pltpu.VMEM((N_EXPERTS, 1), jnp.float32)],
        compiler_params=pltpu.CompilerParams(
            dimension_semantics=("arbitrary",), vmem_limit_bytes=VMEM_LIMIT),
        name="out_projection_router",
    )(ypool, yattn, x, wo[:POOL_WIDTH], wo[POOL_WIDTH:], g_ffn.reshape(1, D), wr2, wr_hi,
      b_router.reshape(N_EXPERTS, 1), upper)


def _sc_mesh():
    return plsc.VectorSubcoreMesh(core_axis_name="c", subcore_axis_name="s")


def _dispatch_rows(h2, dest, n_rows):
    T, D = h2.shape
    per_worker = T // SC_WORKERS
    steps = per_worker // SC_WINDOW

    @functools.partial(
        pl.kernel, mesh=_sc_mesh(),
        out_type=jax.ShapeDtypeStruct((n_rows, D), h2.dtype),
        scratch_types=[pltpu.VMEM((TOP_K, SC_WINDOW), jnp.int32),
                       pltpu.VMEM((SC_WINDOW, D), h2.dtype),
                       pltpu.SemaphoreType.DMA],
    )
    def scatter_kernel(h2_hbm, dest_hbm, xs_hbm, idx_v, rows_v, sem):
        wid = lax.axis_index("s") * SC_CORES + lax.axis_index("c")

        @pl.loop(0, steps)
        def _(step):
            base = wid * per_worker + step * SC_WINDOW
            pltpu.sync_copy(h2_hbm.at[pl.ds(base, SC_WINDOW)], rows_v)
            pltpu.sync_copy(dest_hbm.at[:, pl.ds(base, SC_WINDOW)], idx_v)
            copies = [pltpu.async_copy(rows_v, xs_hbm.at[idx_v.at[kk]], sem)
                      for kk in range(TOP_K)]
            for c in copies:
                c.wait()

    return scatter_kernel(h2, dest)


def _gather_rows(table, idx):
    N = idx.shape[0]
    D = table.shape[1]
    per_worker = N // SC_WORKERS
    steps = per_worker // SC_WINDOW

    @functools.partial(
        pl.kernel, mesh=_sc_mesh(),
        out_type=jax.ShapeDtypeStruct((N, D), table.dtype),
        scratch_types=[pltpu.VMEM((SC_WINDOW,), jnp.int32),
                       pltpu.VMEM((SC_WINDOW, D), table.dtype),
                       pltpu.SemaphoreType.DMA],
    )
    def gather_kernel(table_hbm, idx_hbm, out_hbm, idx_v, rows_v, sem):
        wid = lax.axis_index("s") * SC_CORES + lax.axis_index("c")

        @pl.loop(0, steps)
        def _(step):
            base = wid * per_worker + step * SC_WINDOW
            pltpu.sync_copy(idx_hbm.at[pl.ds(base, SC_WINDOW)], idx_v)
            pltpu.async_copy(table_hbm.at[idx_v], rows_v, sem).wait()
            pltpu.sync_copy(rows_v, out_hbm.at[pl.ds(base, SC_WINDOW)])

    return gather_kernel(table, idx)


def _expert_kernel(be_ref, nvalid_ref, x_ref, wup_ref, bup_ref, wdn_ref, bdn_ref, y_ref,
                   wup_bf, wdn_bf):
    i = pl.program_id(0)
    blk = x_ref.shape[0]
    changed = jnp.logical_or(i == 0, be_ref[i] != be_ref[jnp.maximum(i - 1, 0)])

    @pl.when(changed)
    def _():
        wup_bf[...] = wup_ref[0].astype(jnp.bfloat16)
        wdn_bf[...] = wdn_ref[0].astype(jnp.bfloat16)

    nvalid = nvalid_ref[i]

    @pl.when(nvalid > 0)
    def _():
        row = lax.broadcasted_iota(jnp.int32, (blk, 1), 0)
        x_hi, x_lo = _unpack_bf16_pairs(jnp.where(row < nvalid, x_ref[...], jnp.uint32(0)))
        x = jnp.concatenate([x_hi.astype(jnp.bfloat16), x_lo.astype(jnp.bfloat16)], axis=1)
        glu = _dot(x, wup_bf[:, :D_EXPERT]) + bup_ref[0, :, :D_EXPERT]
        lin = _dot(x, wup_bf[:, D_EXPERT:]) + bup_ref[0, :, D_EXPERT:]
        glu = jnp.minimum(glu, SWIGLU_LIMIT)
        lin = jnp.clip(lin, -SWIGLU_LIMIT, SWIGLU_LIMIT)
        act = glu * (1.0 / (1.0 + jnp.exp(-SWIGLU_ALPHA * glu))) * (lin + 1.0)
        y = _dot(act.astype(jnp.bfloat16), wdn_bf[...]) + bdn_ref[0]
        y_ref[...] = _pack_bf16_pairs(y)

    @pl.when(nvalid <= 0)
    def _():
        y_ref[...] = jnp.zeros_like(y_ref)


def _expert_ffn(xs, block_e, block_valid, w_up, b_up, w_down, b_down):
    n_rows = xs.shape[0]
    D = D_MODEL
    blk = EXPERT_ROWS
    grid_spec = pltpu.PrefetchScalarGridSpec(
        num_scalar_prefetch=2,
        grid=(n_rows // blk,),
        in_specs=[
            pl.BlockSpec((blk, D // 2), lambda i, be, nv: (i, 0)),
            pl.BlockSpec((1, D, 2 * D_EXPERT), lambda i, be, nv: (be[i], 0, 0)),
            pl.BlockSpec((1, 1, 2 * D_EXPERT), lambda i, be, nv: (be[i], 0, 0)),
            pl.BlockSpec((1, D_EXPERT, D), lambda i, be, nv: (be[i], 0, 0)),
            pl.BlockSpec((1, 1, D), lambda i, be, nv: (be[i], 0, 0)),
        ],
        out_specs=pl.BlockSpec((blk, D // 2), lambda i, be, nv: (i, 0)),
        scratch_shapes=[pltpu.VMEM((D, 2 * D_EXPERT), jnp.bfloat16),
                        pltpu.VMEM((D_EXPERT, D), jnp.bfloat16)],
    )
    return pl.pallas_call(
        _expert_kernel,
        grid_spec=grid_spec,
        out_shape=jax.ShapeDtypeStruct((n_rows, D // 2), jnp.uint32),
        compiler_params=pltpu.CompilerParams(
            dimension_semantics=("arbitrary",), vmem_limit_bytes=VMEM_LIMIT),
        name="expert_ffn",
    )(block_e, block_valid, xs, w_up, b_up.reshape(N_EXPERTS, 1, 2 * D_EXPERT), w_down,
      b_down.reshape(N_EXPERTS, 1, D))


def _final_kernel(x1_ref, rows_ref, gate_ref, g_ref, o_ref):
    half = x1_ref.shape[1] // 2
    acc_hi = x1_ref[:, :half]
    acc_lo = x1_ref[:, half:]
    for kk in range(TOP_K):
        y_hi, y_lo = _unpack_bf16_pairs(rows_ref[kk])
        gate = gate_ref[:, kk:kk + 1]
        acc_hi = acc_hi + y_hi * gate
        acc_lo = acc_lo + y_lo * gate
    o_ref[...] = _rms_norm(jnp.concatenate([acc_hi, acc_lo], axis=1), g_ref[...])


def _final_alias_kernel(x1_ref, rows_ref, gate_ref, g_ref, prev_ref, o_ref):
    del prev_ref
    _final_kernel(x1_ref, rows_ref, gate_ref, g_ref, o_ref)


def _final_norm(x1, rows, gates_t, g_final, part, prev_out):
    T, D = x1.shape
    Tp = rows.shape[1]
    tm = SEQ_TILE
    t0 = part * (Tp // tm)
    row = pl.BlockSpec((tm, D), lambda t: (t + t0, 0))
    in_specs = [row, pl.BlockSpec((TOP_K, tm, D // 2), lambda t: (0, t, 0)),
                pl.BlockSpec((tm, TOP_K), lambda t: (t + t0, 0)),
                pl.BlockSpec((1, D), lambda t: (0, 0))]
    args = [x1, rows, gates_t, g_final.reshape(1, D)]
    body, aliases = _final_kernel, {}
    if prev_out is not None:
        in_specs.append(pl.BlockSpec(memory_space=pl.ANY))
        args.append(prev_out)
        body, aliases = _final_alias_kernel, {len(args) - 1: 0}
    return pl.pallas_call(
        body,
        grid=(Tp // tm,),
        in_specs=in_specs,
        out_specs=row,
        out_shape=jax.ShapeDtypeStruct((T, D), jnp.float32),
        input_output_aliases=aliases,
        compiler_params=pltpu.CompilerParams(dimension_semantics=("arbitrary",)),
        name="combine_final_norm",
    )(*args)


def kernel(x, g_mix, w_in, b_forget, w_pool, pool_scale, w_out, g_ffn, w_router, b_router,
           w_up, b_up, w_down, b_down, g_final):
    B, S, D = x.shape
    T = B * S
    ypool, qt, qaug, k, kaug, vt, stats = _in_projection(x, g_mix[0], w_in[0], b_forget[0],
                                                        w_pool[0], pool_scale[0])
    yattn = _attention(qt, qaug, k, kaug, vt, _first_needed_chunk(stats))
    x1, h2, top_idx, gates, rank, counts = _out_projection(
        ypool.reshape(T, POOL_WIDTH), yattn.reshape(T, ATTN_WIDTH), x.reshape(T, D),
        w_out[0], g_ffn[0], w_router[0], b_router[0])

    blk = EXPERT_ROWS
    counts = counts[:, 0].astype(jnp.int32)
    padded = (counts + blk - 1) // blk * blk
    pad_ends = jnp.cumsum(padded)
    pad_starts = pad_ends - padded
    dest = rank
    for e in range(N_EXPERTS):
        dest = dest + jnp.where(top_idx == e, pad_starts[e], 0)
    n_rows = T * TOP_K + N_EXPERTS * blk
    block_row0 = jnp.arange(n_rows // blk, dtype=jnp.int32) * blk
    block_e = jnp.sum((pad_ends[None, :] <= block_row0[:, None]).astype(jnp.int32), axis=1)
    block_e = jnp.minimum(block_e, N_EXPERTS - 1)
    of_block = block_e[:, None] == jnp.arange(N_EXPERTS, dtype=jnp.int32)[None, :]
    block_end = jnp.sum(jnp.where(of_block, (pad_starts + counts)[None, :], 0), axis=1)
    block_valid = jnp.clip(block_end - block_row0, 0, blk)
    block_valid = jnp.where(block_row0 < pad_ends[-1], block_valid, 0).astype(jnp.int32)

    xs = _dispatch_rows(h2, dest, n_rows)
    ys = _expert_ffn(xs, block_e, block_valid, w_up[0], b_up[0], w_down[0], b_down[0])
    gates_t = gates.T
    tp = T // COMBINE_PARTS
    out = None
    for part in range(COMBINE_PARTS):
        part_dest = dest[:, part * tp:(part + 1) * tp].reshape(-1)
        rows = _gather_rows(ys, part_dest).reshape(TOP_K, tp, D // 2)
        out = _final_norm(x1, rows, gates_t, g_final, part, out)
    return out.reshape(B, S, D)
```

```python
import functools

import jax
import jax.numpy as jnp
from jax import lax
from jax.experimental import pallas as pl
from jax.experimental.pallas import tpu as pltpu
from jax.experimental.pallas import tpu_sc as plsc

D_MODEL = 1024
POOL_WIDTH = 512
POOL_WINDOWS = (2, 4, 8, 16)
POOL_GROUP = 128
MAX_WINDOW = 16
ATTN_WIDTH = 512
HEAD_DIM = 64
N_HEADS = 8
N_EXPERTS = 32
TOP_K = 4
D_EXPERT = 1024
SWIGLU_LIMIT = 7.0
SWIGLU_ALPHA = 1.702
RMS_EPS = 1e-5

LANES = 128
SUBLANES = 8
AUG_GROUP = 16
N_SPLIT = 3
MASK_VALUE = -1e30
LOG2_E = 1.4426950408889634
SKIP_MARGIN = 160.0
NORM_SLACK = 1.02
V_ROWS = 80

SEQ_TILE = 512
Q_TILE = 512
HEADS_PER_STEP = 4
LOGITS_AHEAD = 2
KV_TILE = SEQ_TILE
ROUTE_TILE = 1024
ROUTE_PART = 512
EXPERT_ROWS = 512
COMBINE_PARTS = 8
SC_CORES = 2
SC_SUBCORES = 16
SC_WORKERS = SC_CORES * SC_SUBCORES
SC_WINDOW = 128
VMEM_LIMIT = 56 * 1024 * 1024

_NT = (((1,), (1,)), ((), ()))


def _dot(a, b):
    return jnp.dot(a, b, preferred_element_type=jnp.float32)


def _dot_nt(a, b):
    return lax.dot_general(a, b, _NT, preferred_element_type=jnp.float32)


def _rms_norm(x, g):
    return x * lax.rsqrt(jnp.mean(x * x, axis=-1, keepdims=True) + RMS_EPS) * g


def _pack_bf16_pairs(x):
    n = x.shape[1] // 2
    bits = lax.bitcast_convert_type(x.astype(jnp.bfloat16).astype(jnp.float32), jnp.uint32)
    return bits[:, :n] | (bits[:, n:] >> 16)


def _unpack_bf16_pairs(p):
    hi = lax.bitcast_convert_type(p & jnp.uint32(0xFFFF0000), jnp.float32)
    lo = lax.bitcast_convert_type(p << 16, jnp.float32)
    return hi, lo


def _split3(x):
    hi = x.astype(jnp.bfloat16)
    r1 = x - hi.astype(jnp.float32)
    mid = r1.astype(jnp.bfloat16)
    lo = (r1 - mid.astype(jnp.float32)).astype(jnp.bfloat16)
    return hi, mid, lo


def _inproj_kernel(x_ref, g_ref, wu_ref, wqt_ref, wk_ref, wvt_ref, wf_ref, bf_ref, tri_ref,
                   pqt_ref, pk_ref, oneq_ref, onek_ref, hsel_ref, wpool_ref, pscale_ref,
                   ypool_ref, qt_ref, qaug_ref, k_ref, kaug_ref, vt_ref, stats_ref, ubuf, ccarry):
    s = pl.program_id(1)
    tm = x_ref.shape[1]

    @pl.when(s == 0)
    def _():
        ubuf[...] = jnp.zeros_like(ubuf)
        ccarry[...] = jnp.zeros_like(ccarry)

    x = x_ref[0]
    h = _rms_norm(x, g_ref[...]).astype(jnp.bfloat16)

    fl = _dot(h, wf_ref[...]) + bf_ref[...]
    logf = jnp.minimum(fl, 0.0) - jnp.log1p(jnp.exp(-jnp.abs(fl)))
    lane = lax.broadcasted_iota(jnp.int32, (tm, LANES), 1)
    logf = jnp.where(lane < N_SPLIT * N_HEADS, logf, 0.0)
    f_hi, f_mid, f_lo = _split3(logf)
    tri = tri_ref[...]
    c = _dot(tri, f_hi) + _dot(tri, f_mid) + _dot(tri, f_lo) + ccarry[...]
    ccarry[...] = c[tm - 1:tm, :]
    c2 = c * LOG2_E
    c_hi, c_mid, c_lo = _split3(c2)
    term = lane % N_SPLIT
    c3 = jnp.where(term == 0, c_hi, jnp.where(term == 1, c_mid, c_lo))

    kp = _dot(h, wk_ref[...])
    k_ref[0] = kp.astype(jnp.bfloat16)
    kaug_ref[0] = (_dot(c3, pk_ref[...]) + onek_ref[...]).astype(jnp.bfloat16)
    qt = _dot_nt(wqt_ref[...], h)
    qt_ref[0] = qt.astype(jnp.bfloat16)
    qaug_ref[0] = (_dot_nt(pqt_ref[...], c3) + oneq_ref[...]).astype(jnp.bfloat16)
    lane_row = lax.broadcasted_iota(jnp.int32, (1, LANES), 1)
    kn2 = jnp.max(_dot((kp * kp).astype(jnp.bfloat16), hsel_ref[...]), axis=0, keepdims=True)
    qn2 = jnp.zeros((1, LANES), jnp.float32)
    for hd in range(N_HEADS):
        qs = qt[hd * HEAD_DIM:(hd + 1) * HEAD_DIM, :]
        qmax = jnp.max(jnp.sum(qs * qs, axis=0, keepdims=True), axis=1, keepdims=True)
        qn2 = jnp.where(lane_row == hd, qmax, qn2)
    stats_ref[0, 0, 0:1, :] = kn2
    stats_ref[0, 0, 1:2, :] = qn2
    stats_ref[0, 0, 2:3, :] = c2[0:1, :]
    stats_ref[0, 0, 3:4, :] = c2[tm - 1:tm, :]
    stats_ref[0, 0, 4:SUBLANES, :] = jnp.zeros((SUBLANES - 4, LANES), jnp.float32)
    vt = _dot_nt(wvt_ref[...], h).astype(jnp.bfloat16)
    ones = jnp.ones((V_ROWS - HEAD_DIM, tm), jnp.bfloat16)
    for hd in range(N_HEADS):
        vt_ref[0, 0, hd * V_ROWS:hd * V_ROWS + HEAD_DIM, :] = vt[hd * HEAD_DIM:(hd + 1) * HEAD_DIM]
        vt_ref[0, 0, hd * V_ROWS + HEAD_DIM:(hd + 1) * V_ROWS, :] = ones

    u = _dot(h, wu_ref[...])
    head_pos = s * tm + lax.broadcasted_iota(jnp.int32, (MAX_WINDOW, 1), 0)
    for gi, w in enumerate(POOL_WINDOWS):
        cols = slice(gi * POOL_GROUP, (gi + 1) * POOL_GROUP)
        ug = u[:, cols]
        win = jnp.concatenate([ubuf[:, cols], ug], axis=0)
        span = 1
        while span < w:
            win = win + pltpu.roll(win, span, axis=0)
            span *= 2
        acc = win[MAX_WINDOW:]
        inv_head = 1.0 / jnp.minimum(head_pos + 1, w).astype(jnp.float32)
        p = jnp.concatenate([acc[:MAX_WINDOW] * inv_head, acc[MAX_WINDOW:] * (1.0 / w)],
                            axis=0) - ug
        y = _dot(p.astype(jnp.bfloat16), wpool_ref[gi]) * pscale_ref[:, cols]
        ypool_ref[0, :, cols] = y.astype(jnp.bfloat16)
    ubuf[...] = u[tm - MAX_WINDOW:]


def _in_projection(x, g_mix, w_in, b_forget, w_pool, pool_scale):
    B, S, D = x.shape
    tm = SEQ_TILE
    bf16 = jnp.bfloat16
    o = POOL_WIDTH
    w_u = w_in[:, :o].astype(bf16)
    w_q = w_in[:, o:o + ATTN_WIDTH] * (HEAD_DIM ** -0.5 * LOG2_E)
    w_k = w_in[:, o + ATTN_WIDTH:o + 2 * ATTN_WIDTH]
    w_v = w_in[:, o + 2 * ATTN_WIDTH:o + 3 * ATTN_WIDTH]
    w_f = w_in[:, o + 3 * ATTN_WIDTH:]

    wqt = w_q.T.astype(bf16)
    wk = w_k.astype(bf16)
    wvt = w_v.T.astype(bf16)
    n_gate = N_SPLIT * N_HEADS
    wf = jnp.pad(jnp.repeat(w_f, N_SPLIT, axis=1), ((0, 0), (0, LANES - n_gate))).astype(bf16)
    bfp = jnp.pad(jnp.repeat(b_forget, N_SPLIT), (0, LANES - n_gate)).reshape(1, LANES)

    assert N_HEADS * AUG_GROUP == LANES and 2 * N_SPLIT <= AUG_GROUP
    src = jnp.arange(LANES)
    head, term = src // N_SPLIT, src % N_SPLIT
    dst = jnp.arange(LANES)
    valid = (src < n_gate)[:, None]
    q_hit = valid & (dst[None, :] == (head * AUG_GROUP + term)[:, None])
    k_hit = valid & (dst[None, :] == (head * AUG_GROUP + N_SPLIT + term)[:, None])
    pqt = q_hit.T.astype(bf16)
    pk = -k_hit.astype(bf16)
    slot_pos = dst % AUG_GROUP
    oneq = ((slot_pos >= N_SPLIT) & (slot_pos < 2 * N_SPLIT))
    onek = slot_pos < N_SPLIT
    oneq = oneq.astype(jnp.float32).reshape(-1, 1)
    onek = onek.astype(jnp.float32).reshape(1, -1)
    r = jnp.arange(tm)
    tri = (r[None, :] <= r[:, None]).astype(bf16)
    hsel = (jnp.arange(ATTN_WIDTH)[:, None] // HEAD_DIM == jnp.arange(LANES)[None, :]).astype(bf16)

    full = lambda *shape: pl.BlockSpec(shape, lambda b, s: (0,) * len(shape))
    aw = ATTN_WIDTH
    return pl.pallas_call(
        _inproj_kernel,
        grid=(B, S // tm),
        in_specs=[
            pl.BlockSpec((1, tm, D), lambda b, s: (b, s, 0)),
            full(1, D), full(D, o), full(aw, D), full(D, aw), full(aw, D),
            full(D, LANES), full(1, LANES), full(tm, tm), full(LANES, LANES), full(LANES, LANES),
            full(LANES, 1), full(1, LANES), full(aw, LANES),
            full(len(POOL_WINDOWS), POOL_GROUP, POOL_GROUP),
            full(1, o),
        ],
        out_specs=[
            pl.BlockSpec((1, tm, o), lambda b, s: (b, s, 0)),
            pl.BlockSpec((1, aw, tm), lambda b, s: (b, 0, s)),
            pl.BlockSpec((1, LANES, tm), lambda b, s: (b, 0, s)),
            pl.BlockSpec((1, tm, aw), lambda b, s: (b, s, 0)),
            pl.BlockSpec((1, tm, LANES), lambda b, s: (b, s, 0)),
            pl.BlockSpec((1, 1, N_HEADS * V_ROWS, tm), lambda b, s: (b, s, 0, 0)),
            pl.BlockSpec((1, 1, SUBLANES, LANES), lambda b, s: (b, s, 0, 0)),
        ],
        out_shape=[
            jax.ShapeDtypeStruct((B, S, o), bf16),
            jax.ShapeDtypeStruct((B, aw, S), bf16),
            jax.ShapeDtypeStruct((B, LANES, S), bf16),
            jax.ShapeDtypeStruct((B, S, aw), bf16),
            jax.ShapeDtypeStruct((B, S, LANES), bf16),
            jax.ShapeDtypeStruct((B, S // tm, N_HEADS * V_ROWS, tm), bf16),
            jax.ShapeDtypeStruct((B, S // tm, SUBLANES, LANES), jnp.float32),
        ],
        scratch_shapes=[
            pltpu.VMEM((MAX_WINDOW, o), jnp.float32),
            pltpu.VMEM((1, LANES), jnp.float32),
        ],
        compiler_params=pltpu.CompilerParams(
            dimension_semantics=("arbitrary", "arbitrary"), vmem_limit_bytes=VMEM_LIMIT),
        name="in_projection",
    )(x, g_mix.reshape(1, D), w_u, wqt, wk, wvt, wf, bfp, tri, pqt, pk, oneq, onek, hsel,
      w_pool.astype(bf16), pool_scale.reshape(1, o))


def _attn_kernel(j0_ref, qt_ref, qaug_ref, k_ref, kaug_ref, vt_ref, causal_ref, o_ref,
                 qop_ref, st_buf, mx_buf, m_ref, acc_ref):
    i = pl.program_id(2)
    tile = (pl.program_id(0) * pl.num_programs(1) + pl.program_id(1)) * pl.num_programs(2) + i
    j0 = j0_ref[tile]
    tq = qt_ref.shape[2]
    tk = vt_ref.shape[3]
    heads = qt_ref.shape[1] // HEAD_DIM

    m_ref[...] = jnp.full(m_ref.shape, MASK_VALUE, jnp.float32)
    acc_ref[...] = jnp.zeros_like(acc_ref)

    first_head = pl.program_id(1) * heads
    group = lax.broadcasted_iota(jnp.int32, (LANES, tq), 0) // AUG_GROUP
    q_aug = qaug_ref[0]
    blank = jnp.zeros((HEAD_DIM, tq), q_aug.dtype)
    for hh in range(heads):
        q_own = qt_ref[0, hh * HEAD_DIM:(hh + 1) * HEAD_DIM, :]
        pair_rows = [q_own, blank] if hh % 2 == 0 else [blank, q_own]
        qop_ref[hh, 0:2 * HEAD_DIM, :] = jnp.concatenate(pair_rows, axis=0)
        qop_ref[hh, 2 * HEAD_DIM:, :] = jnp.where(group == first_head + hh, q_aug,
                                                  jnp.zeros_like(q_aug))

    def logits(j, slot, hh, masked):
        row0 = pl.multiple_of(j * tk, tk)
        pair_lanes = slice(hh // 2 * 2 * HEAD_DIM, (hh // 2 + 1) * 2 * HEAD_DIM)
        k = jnp.concatenate([k_ref[0, pl.ds(row0, tk), pair_lanes],
                             kaug_ref[0, pl.ds(row0, tk), :]], axis=1)
        st = _dot(k, qop_ref[hh])
        if masked:
            st = st + causal_ref[...]
        st_buf[slot, hh] = st
        mx_buf[slot, hh] = jnp.max(st, axis=0, keepdims=True)

    def accumulate(j, slot, hh):
        st = st_buf[slot, hh]
        m_prev = m_ref[hh]
        m_new = jnp.maximum(m_prev, mx_buf[slot, hh])
        alpha = jnp.exp2(m_prev - m_new)
        p = jnp.exp2(st - m_new)
        vt = vt_ref[0, j, hh * V_ROWS:(hh + 1) * V_ROWS, :]
        acc_ref[hh] = alpha * acc_ref[hh] + _dot(vt, p.astype(jnp.bfloat16))
        m_ref[hh] = m_new

    ahead = LOGITS_AHEAD
    assert 1 <= ahead <= heads
    for hh in range(heads):
        logits(i, 0, hh, True)
        if hh >= ahead:
            accumulate(i, 0, hh - ahead)

    def before(t):
        return jnp.where(t == j0, i, t - 1)

    def step(t, slot):
        for hh in range(heads):
            logits(t, slot, hh, False)
            if hh >= ahead:
                accumulate(t, slot, hh - ahead)
            else:
                accumulate(before(t), 1 - slot, hh - ahead + heads)

    def drain(t, slot):
        for hh in range(heads - ahead, heads):
            accumulate(t, slot, hh)

    def body(u, carry):
        step(j0 + 2 * u, 1)
        step(j0 + 2 * u + 1, 0)
        return carry

    n_chunks = i - j0
    lax.fori_loop(0, n_chunks // 2, body, 0)

    @pl.when(n_chunks % 2 == 1)
    def _():
        step(i - 1, 1)
        drain(i - 1, 1)

    @pl.when(n_chunks % 2 == 0)
    def _():
        drain(before(i), 0)

    outs = [acc_ref[hh, :HEAD_DIM] / acc_ref[hh, HEAD_DIM:HEAD_DIM + 1] for hh in range(heads)]
    ot = jnp.concatenate(outs, axis=0)
    o_ref[0] = ot.T.astype(o_ref.dtype)


def _first_needed_chunk(stats):
    kn = jnp.sqrt(stats[:, :, 0, :N_HEADS])
    qn = jnp.sqrt(stats[:, :, 1, :N_HEADS])
    c_first = stats[:, :, 2, 0:N_SPLIT * N_HEADS:N_SPLIT]
    c_last = stats[:, :, 3, 0:N_SPLIT * N_HEADS:N_SPLIT]
    n = stats.shape[1]
    dot_bound = NORM_SLACK * qn[:, :, None, :] * kn[:, None, :, :] + 1.0
    upper = dot_bound + c_first[:, :, None, :] - c_last[:, None, :, :]
    self_bound = NORM_SLACK * qn * kn + 1.0
    ii = jnp.arange(n, dtype=jnp.int32)[None, :, None, None]
    jj = jnp.arange(n, dtype=jnp.int32)[None, None, :, None]
    needed = (upper + self_bound[:, :, None, :] >= -SKIP_MARGIN) & (jj < ii)
    first = jnp.min(jnp.where(needed, jj, ii), axis=2)
    g = HEADS_PER_STEP
    first = jnp.min(first.reshape(first.shape[0], n, N_HEADS // g, g), axis=3)
    return jnp.transpose(first, (0, 2, 1)).reshape(-1).astype(jnp.int32)


def _attention(qt, qaug, k, kaug, vt, first_chunk):
    B, _, S = qt.shape
    tq, tk = Q_TILE, KV_TILE
    assert tq == tk, "the diagonal is handled as a single masked key chunk"
    pair = HEADS_PER_STEP
    n_pairs = N_HEADS // pair
    pw = pair * HEAD_DIM
    causal = jnp.where(jnp.arange(tk)[:, None] <= jnp.arange(tq)[None, :], 0.0, MASK_VALUE)
    grid_spec = pltpu.PrefetchScalarGridSpec(
        num_scalar_prefetch=1,
        grid=(B, n_pairs, S // tq),
        in_specs=[
            pl.BlockSpec((1, pw, tq), lambda b, p, i, j0: (b, p, i)),
            pl.BlockSpec((1, LANES, tq), lambda b, p, i, j0: (b, 0, i)),
            pl.BlockSpec((1, S, pw), lambda b, p, i, j0: (b, 0, p)),
            pl.BlockSpec((1, S, LANES), lambda b, p, i, j0: (b, 0, 0)),
            pl.BlockSpec((1, S // tk, pair * V_ROWS, tk), lambda b, p, i, j0: (b, 0, p, 0)),
            pl.BlockSpec((tk, tq), lambda b, p, i, j0: (0, 0)),
        ],
        out_specs=pl.BlockSpec((1, tq, pw), lambda b, p, i, j0: (b, i, p)),
        scratch_shapes=[
            pltpu.VMEM((pair, 2 * HEAD_DIM + LANES, tq), jnp.bfloat16),
            pltpu.VMEM((2, pair, tk, tq), jnp.float32),
            pltpu.VMEM((2, pair, 1, tq), jnp.float32),
            pltpu.VMEM((pair, 1, tq), jnp.float32),
            pltpu.VMEM((pair, V_ROWS, tq), jnp.float32),
        ],
    )
    return pl.pallas_call(
        _attn_kernel,
        grid_spec=grid_spec,
        out_shape=jax.ShapeDtypeStruct((B, S, ATTN_WIDTH), jnp.bfloat16),
        compiler_params=pltpu.CompilerParams(
            dimension_semantics=("arbitrary", "arbitrary", "arbitrary"),
            vmem_limit_bytes=VMEM_LIMIT),
        name="forgetting_attention",
    )(first_chunk, qt, qaug, k, kaug, vt, causal.astype(jnp.float32))


def _outproj_kernel(yp_ref, ya_ref, x_ref, wo1_ref, wo2_ref, g_ref, wr2_ref, wrhi_ref, br_ref,
                    upper_ref, x1_ref, h2_ref, idx_ref, gate_ref, rank_ref, cnt_ref, carry):
    t = pl.program_id(0)
    tm = x_ref.shape[0]

    @pl.when(t == 0)
    def _():
        carry[...] = jnp.zeros_like(carry)

    hm = upper_ref.shape[0]
    parts = [slice(r * hm, (r + 1) * hm) for r in range(tm // hm)]

    def project(rows):
        x1 = x_ref[rows] + _dot(yp_ref[rows], wo1_ref[...]) + _dot(ya_ref[rows], wo2_ref[...])
        x1_ref[rows] = x1
        return x1

    def route(rows, x1):
        h2 = _rms_norm(x1, g_ref[...])
        h2_hi = h2.astype(jnp.bfloat16)
        h2_lo = (h2 - h2_hi.astype(jnp.float32)).astype(jnp.bfloat16)
        h2_ref[rows] = _pack_bf16_pairs(h2)

        lg2 = _dot_nt(wr2_ref[...], h2_hi)
        logits = (lg2[:N_EXPERTS] + lg2[N_EXPERTS:] + _dot_nt(wrhi_ref[...], h2_lo)
                  + br_ref[...])

        eio = lax.broadcasted_iota(jnp.int32, (N_EXPERTS, hm), 0)
        vals, idxs = [], []
        cur = logits
        for _ in range(TOP_K):
            m = jnp.max(cur, axis=0, keepdims=True)
            ix = jnp.min(jnp.where(cur == m, eio, N_EXPERTS), axis=0, keepdims=True)
            vals.append(m)
            idxs.append(ix)
            cur = jnp.where(eio == ix, -jnp.inf, cur)
        exps = [jnp.exp(v - vals[0]) for v in vals]
        denom = exps[0] + exps[1] + exps[2] + exps[3]
        hits = [eio == ix for ix in idxs]
        cnt = jnp.zeros((N_EXPERTS, hm), jnp.float32)
        for hit in hits:
            cnt = cnt + hit.astype(jnp.float32)
        base = _dot(cnt.astype(jnp.bfloat16), upper_ref[...]) + carry[...]
        for kk in range(TOP_K):
            idx_ref[kk:kk + 1, rows] = idxs[kk]
            gate_ref[kk:kk + 1, rows] = exps[kk] / denom
            rank = jnp.sum(jnp.where(hits[kk], base, 0.0), axis=0, keepdims=True)
            rank_ref[kk:kk + 1, rows] = rank.astype(jnp.int32)
        carry[...] = carry[...] + jnp.sum(cnt, axis=1, keepdims=True)

    pending = project(parts[0])
    for r, rows in enumerate(parts):
        ahead = project(parts[r + 1]) if r + 1 < len(parts) else None
        route(rows, pending)
        pending = ahead
    cnt_ref[...] = carry[...]


def _out_projection(ypool, yattn, x, w_out, g_ffn, w_router, b_router):
    T, D = x.shape
    tm = ROUTE_TILE
    bf16 = jnp.bfloat16
    wo = w_out.astype(bf16)
    wr_t = w_router.T
    wr_hi = wr_t.astype(bf16)
    wr_lo = (wr_t - wr_hi.astype(jnp.float32)).astype(bf16)
    wr2 = jnp.concatenate([wr_hi, wr_lo], axis=0)
    r = jnp.arange(ROUTE_PART)
    upper = (r[:, None] < r[None, :]).astype(bf16)
    full = lambda *shape: pl.BlockSpec(shape, lambda t: (0,) * len(shape))
    row = lambda w: pl.BlockSpec((tm, w), lambda t: (t, 0))
    col = pl.BlockSpec((TOP_K, tm), lambda t: (0, t))
    return pl.pallas_call(
        _outproj_kernel,
        grid=(T // tm,),
        in_specs=[row(POOL_WIDTH), row(ATTN_WIDTH), row(D), full(POOL_WIDTH, D),
                  full(ATTN_WIDTH, D), full(1, D), full(2 * N_EXPERTS, D), full(N_EXPERTS, D),
                  full(N_EXPERTS, 1), full(ROUTE_PART, ROUTE_PART)],
        out_specs=[row(D), row(D // 2), col, col, col, full(N_EXPERTS, 1)],
        out_shape=[
            jax.ShapeDtypeStruct((T, D), jnp.float32),
            jax.ShapeDtypeStruct((T, D // 2), jnp.uint32),
            jax.ShapeDtypeStruct((TOP_K, T), jnp.int32),
            jax.ShapeDtypeStruct((TOP_K, T), jnp.float32),
            jax.ShapeDtypeStruct((TOP_K, T), jnp.int32),
            jax.ShapeDtypeStruct((N_EXPERTS, 1), jnp.float32),
        ],
        scratch_shapes=[pltpu.VMEM((N_EXPERTS, 1), jnp.float32)],
        compiler_params=pltpu.CompilerParams(
            dimension_semantics=("arbitrary",), vmem_limit_bytes=VMEM_LIMIT),
        name="out_projection_router",
    )(ypool, yattn, x, wo[:POOL_WIDTH], wo[POOL_WIDTH:], g_ffn.reshape(1, D), wr2, wr_hi,
      b_router.reshape(N_EXPERTS, 1), upper)


def _sc_mesh():
    return plsc.VectorSubcoreMesh(core_axis_name="c", subcore_axis_name="s")


def _dispatch_rows(h2, dest, n_rows):
    T, D = h2.shape
    per_worker = T // SC_WORKERS
    steps = per_worker // SC_WINDOW

    @functools.partial(
        pl.kernel, mesh=_sc_mesh(),
        out_type=jax.ShapeDtypeStruct((n_rows, D), h2.dtype),
        scratch_types=[pltpu.VMEM((TOP_K, SC_WINDOW), jnp.int32),
                       pltpu.VMEM((SC_WINDOW, D), h2.dtype),
                       pltpu.SemaphoreType.DMA],
    )
    def scatter_kernel(h2_hbm, dest_hbm, xs_hbm, idx_v, rows_v, sem):
        wid = lax.axis_index("s") * SC_CORES + lax.axis_index("c")

        @pl.loop(0, steps)
        def _(step):
            base = wid * per_worker + step * SC_WINDOW
            pltpu.sync_copy(h2_hbm.at[pl.ds(base, SC_WINDOW)], rows_v)
            pltpu.sync_copy(dest_hbm.at[:, pl.ds(base, SC_WINDOW)], idx_v)
            copies = [pltpu.async_copy(rows_v, xs_hbm.at[idx_v.at[kk]], sem)
                      for kk in range(TOP_K)]
            for c in copies:
                c.wait()

    return scatter_kernel(h2, dest)


def _gather_rows(table, idx):
    N = idx.shape[0]
    D = table.shape[1]
    per_worker = N // SC_WORKERS
    steps = per_worker // SC_WINDOW

    @functools.partial(
        pl.kernel, mesh=_sc_mesh(),
        out_type=jax.ShapeDtypeStruct((N, D), table.dtype),
        scratch_types=[pltpu.VMEM((SC_WINDOW,), jnp.int32),
                       pltpu.VMEM((SC_WINDOW, D), table.dtype),
                       pltpu.SemaphoreType.DMA],
    )
    def gather_kernel(table_hbm, idx_hbm, out_hbm, idx_v, rows_v, sem):
        wid = lax.axis_index("s") * SC_CORES + lax.axis_index("c")

        @pl.loop(0, steps)
        def _(step):
            base = wid * per_worker + step * SC_WINDOW
            pltpu.sync_copy(idx_hbm.at[pl.ds(base, SC_WINDOW)], idx_v)
            pltpu.async_copy(table_hbm.at[idx_v], rows_v, sem).wait()
            pltpu.sync_copy(rows_v, out_hbm.at[pl.ds(base, SC_WINDOW)])

    return gather_kernel(table, idx)


def _expert_kernel(be_ref, nvalid_ref, x_ref, wup_ref, bup_ref, wdn_ref, bdn_ref, y_ref,
                   wup_bf, wdn_bf):
    i = pl.program_id(0)
    blk = x_ref.shape[0]
    changed = jnp.logical_or(i == 0, be_ref[i] != be_ref[jnp.maximum(i - 1, 0)])

    @pl.when(changed)
    def _():
        wup_bf[...] = wup_ref[0].astype(jnp.bfloat16)
        wdn_bf[...] = wdn_ref[0].astype(jnp.bfloat16)

    nvalid = nvalid_ref[i]

    @pl.when(nvalid > 0)
    def _():
        row = lax.broadcasted_iota(jnp.int32, (blk, 1), 0)
        x_hi, x_lo = _unpack_bf16_pairs(jnp.where(row < nvalid, x_ref[...], jnp.uint32(0)))
        x = jnp.concatenate([x_hi.astype(jnp.bfloat16), x_lo.astype(jnp.bfloat16)], axis=1)
        glu = _dot(x, wup_bf[:, :D_EXPERT]) + bup_ref[0, :, :D_EXPERT]
        lin = _dot(x, wup_bf[:, D_EXPERT:]) + bup_ref[0, :, D_EXPERT:]
        glu = jnp.minimum(glu, SWIGLU_LIMIT)
        lin = jnp.clip(lin, -SWIGLU_LIMIT, SWIGLU_LIMIT)
        act = glu * (1.0 / (1.0 + jnp.exp(-SWIGLU_ALPHA * glu))) * (lin + 1.0)
        y = _dot(act.astype(jnp.bfloat16), wdn_bf[...]) + bdn_ref[0]
        y_ref[...] = _pack_bf16_pairs(y)

    @pl.when(nvalid <= 0)
    def _():
        y_ref[...] = jnp.zeros_like(y_ref)


def _expert_ffn(xs, block_e, block_valid, w_up, b_up, w_down, b_down):
    n_rows = xs.shape[0]
    D = D_MODEL
    blk = EXPERT_ROWS
    grid_spec = pltpu.PrefetchScalarGridSpec(
        num_scalar_prefetch=2,
        grid=(n_rows // blk,),
        in_specs=[
            pl.BlockSpec((blk, D // 2), lambda i, be, nv: (i, 0)),
            pl.BlockSpec((1, D, 2 * D_EXPERT), lambda i, be, nv: (be[i], 0, 0)),
            pl.BlockSpec((1, 1, 2 * D_EXPERT), lambda i, be, nv: (be[i], 0, 0)),
            pl.BlockSpec((1, D_EXPERT, D), lambda i, be, nv: (be[i], 0, 0)),
            pl.BlockSpec((1, 1, D), lambda i, be, nv: (be[i], 0, 0)),
        ],
        out_specs=pl.BlockSpec((blk, D // 2), lambda i, be, nv: (i, 0)),
        scratch_shapes=[pltpu.VMEM((D, 2 * D_EXPERT), jnp.bfloat16),
                        pltpu.VMEM((D_EXPERT, D), jnp.bfloat16)],
    )
    return pl.pallas_call(
        _expert_kernel,
        grid_spec=grid_spec,
        out_shape=jax.ShapeDtypeStruct((n_rows, D // 2), jnp.uint32),
        compiler_params=pltpu.CompilerParams(
            dimension_semantics=("arbitrary",), vmem_limit_bytes=VMEM_LIMIT),
        name="expert_ffn",
    )(block_e, block_valid, xs, w_up, b_up.reshape(N_EXPERTS, 1, 2 * D_EXPERT), w_down,
      b_down.reshape(N_EXPERTS, 1, D))


def _final_kernel(x1_ref, rows_ref, gate_ref, g_ref, o_ref):
    half = x1_ref.shape[1] // 2
    acc_hi = x1_ref[:, :half]
    acc_lo = x1_ref[:, half:]
    for kk in range(TOP_K):
        y_hi, y_lo = _unpack_bf16_pairs(rows_ref[kk])
        gate = gate_ref[:, kk:kk + 1]
        acc_hi = acc_hi + y_hi * gate
        acc_lo = acc_lo + y_lo * gate
    o_ref[...] = _rms_norm(jnp.concatenate([acc_hi, acc_lo], axis=1), g_ref[...])


def _final_alias_kernel(x1_ref, rows_ref, gate_ref, g_ref, prev_ref, o_ref):
    del prev_ref
    _final_kernel(x1_ref, rows_ref, gate_ref, g_ref, o_ref)


def _final_norm(x1, rows, gates_t, g_final, part, prev_out):
    T, D = x1.shape
    Tp = rows.shape[1]
    tm = SEQ_TILE
    t0 = part * (Tp // tm)
    row = pl.BlockSpec((tm, D), lambda t: (t + t0, 0))
    in_specs = [row, pl.BlockSpec((TOP_K, tm, D // 2), lambda t: (0, t, 0)),
                pl.BlockSpec((tm, TOP_K), lambda t: (t + t0, 0)),
                pl.BlockSpec((1, D), lambda t: (0, 0))]
    args = [x1, rows, gates_t, g_final.reshape(1, D)]
    body, aliases = _final_kernel, {}
    if prev_out is not None:
        in_specs.append(pl.BlockSpec(memory_space=pl.ANY))
        args.append(prev_out)
        body, aliases = _final_alias_kernel, {len(args) - 1: 0}
    return pl.pallas_call(
        body,
        grid=(Tp // tm,),
        in_specs=in_specs,
        out_specs=row,
        out_shape=jax.ShapeDtypeStruct((T, D), jnp.float32),
        input_output_aliases=aliases,
        compiler_params=pltpu.CompilerParams(dimension_semantics=("arbitrary",)),
        name="combine_final_norm",
    )(*args)


def kernel(x, g_mix, w_in, b_forget, w_pool, pool_scale, w_out, g_ffn, w_router, b_router,
           w_up, b_up, w_down, b_down, g_final):
    B, S, D = x.shape
    T = B * S
    ypool, qt, qaug, k, kaug, vt, stats = _in_projection(x, g_mix[0], w_in[0], b_forget[0],
                                                        w_pool[0], pool_scale[0])
    yattn = _attention(qt, qaug, k, kaug, vt, _first_needed_chunk(stats))
    x1, h2, top_idx, gates, rank, counts = _out_projection(
        ypool.reshape(T, POOL_WIDTH), yattn.reshape(T, ATTN_WIDTH), x.reshape(T, D),
        w_out[0], g_ffn[0], w_router[0], b_router[0])

    blk = EXPERT_ROWS
    counts = counts[:, 0].astype(jnp.int32)
    padded = (counts + blk - 1) // blk * blk
    pad_ends = jnp.cumsum(padded)
    pad_starts = pad_ends - padded
    dest = rank
    for e in range(N_EXPERTS):
        dest = dest + jnp.where(top_idx == e, pad_starts[e], 0)
    n_rows = T * TOP_K + N_EXPERTS * blk
    block_row0 = jnp.arange(n_rows // blk, dtype=jnp.int32) * blk
    block_e = jnp.sum((pad_ends[None, :] <= block_row0[:, None]).astype(jnp.int32), axis=1)
    block_e = jnp.minimum(block_e, N_EXPERTS - 1)
    of_block = block_e[:, None] == jnp.arange(N_EXPERTS, dtype=jnp.int32)[None, :]
    block_end = jnp.sum(jnp.where(of_block, (pad_starts + counts)[None, :], 0), axis=1)
    block_valid = jnp.clip(block_end - block_row0, 0, blk)
    block_valid = jnp.where(block_row0 < pad_ends[-1], block_valid, 0).astype(jnp.int32)

    xs = _dispatch_rows(h2, dest, n_rows)
    ys = _expert_ffn(xs, block_e, block_valid, w_up[0], b_up[0], w_down[0], b_down[0])
    gates_t = gates.T
    tp = T // COMBINE_PARTS
    out = None
    for part in range(COMBINE_PARTS):
        part_dest = dest[:, part * tp:(part + 1) * tp].reshape(-1)
        rows = _gather_rows(ys, part_dest).reshape(TOP_K, tp, D // 2)
        out = _final_norm(x1, rows, gates_t, g_final, part, out)
    return out.reshape(B, S, D)
```

```python
import functools

import jax
import jax.numpy as jnp
from jax import lax
from jax.experimental import pallas as pl
from jax.experimental.pallas import tpu as pltpu
from jax.experimental.pallas import tpu_sc as plsc

D_MODEL = 1024
POOL_WIDTH = 512
POOL_WINDOWS = (2, 4, 8, 16)
POOL_GROUP = 128
MAX_WINDOW = 16
ATTN_WIDTH = 512
HEAD_DIM = 64
N_HEADS = 8
N_EXPERTS = 32
TOP_K = 4
D_EXPERT = 1024
SWIGLU_LIMIT = 7.0
SWIGLU_ALPHA = 1.702
RMS_EPS = 1e-5

LANES = 128
SUBLANES = 8
AUG_GROUP = 16
N_SPLIT = 3
MASK_VALUE = -1e30
LOG2_E = 1.4426950408889634
SKIP_MARGIN = 160.0
NORM_SLACK = 1.02
V_ROWS = 80

SEQ_TILE = 512
Q_TILE = 512
HEADS_PER_STEP = 4
LOGITS_AHEAD = 2
KV_TILE = SEQ_TILE
ROUTE_TILE = 1024
ROUTE_PART = 512
EXPERT_ROWS = 512
COMBINE_PARTS = 8
SC_CORES = 2
SC_SUBCORES = 16
SC_WORKERS = SC_CORES * SC_SUBCORES
SC_WINDOW = 128
VMEM_LIMIT = 56 * 1024 * 1024

_NT = (((1,), (1,)), ((), ()))


def _dot(a, b):
    return jnp.dot(a, b, preferred_element_type=jnp.float32)


def _dot_nt(a, b):
    return lax.dot_general(a, b, _NT, preferred_element_type=jnp.float32)


def _rms_norm(x, g):
    return x * lax.rsqrt(jnp.mean(x * x, axis=-1, keepdims=True) + RMS_EPS) * g


def _pack_bf16_pairs(x):
    n = x.shape[1] // 2
    bits = lax.bitcast_convert_type(x.astype(jnp.bfloat16).astype(jnp.float32), jnp.uint32)
    return bits[:, :n] | (bits[:, n:] >> 16)


def _unpack_bf16_pairs(p):
    hi = lax.bitcast_convert_type(p & jnp.uint32(0xFFFF0000), jnp.float32)
    lo = lax.bitcast_convert_type(p << 16, jnp.float32)
    return hi, lo


def _split3(x):
    hi = x.astype(jnp.bfloat16)
    r1 = x - hi.astype(jnp.float32)
    mid = r1.astype(jnp.bfloat16)
    lo = (r1 - mid.astype(jnp.float32)).astype(jnp.bfloat16)
    return hi, mid, lo


def _inproj_kernel(x_ref, g_ref, wu_ref, wqt_ref, wk_ref, wvt_ref, wf_ref, bf_ref, tri_ref,
                   pqt_ref, pk_ref, oneq_ref, onek_ref, hsel_ref, wpool_ref, pscale_ref,
                   ypool_ref, qt_ref, qaug_ref, k_ref, kaug_ref, vt_ref, stats_ref, ubuf, ccarry):
    s = pl.program_id(1)
    tm = x_ref.shape[1]

    @pl.when(s == 0)
    def _():
        ubuf[...] = jnp.zeros_like(ubuf)
        ccarry[...] = jnp.zeros_like(ccarry)

    x = x_ref[0]
    h = _rms_norm(x, g_ref[...]).astype(jnp.bfloat16)
    fl = _dot(h, wf_ref[...]) + bf_ref[...]
    u = _dot(h, wu_ref[...])
    kp = _dot(h, wk_ref[...])
    qt = _dot_nt(wqt_ref[...], h)
    vt = _dot_nt(wvt_ref[...], h).astype(jnp.bfloat16)
    k_ref[0] = kp.astype(jnp.bfloat16)
    qt_ref[0] = qt.astype(jnp.bfloat16)

    logf = jnp.minimum(fl, 0.0) - jnp.log1p(jnp.exp(-jnp.abs(fl)))
    lane = lax.broadcasted_iota(jnp.int32, (tm, LANES), 1)
    logf = jnp.where(lane < N_SPLIT * N_HEADS, logf, 0.0)
    f_hi, f_mid, f_lo = _split3(logf)
    tri = tri_ref[...]
    c = _dot(tri, f_hi) + _dot(tri, f_mid) + _dot(tri, f_lo) + ccarry[...]
    ccarry[...] = c[tm - 1:tm, :]
    c2 = c * LOG2_E
    c_hi, c_mid, c_lo = _split3(c2)
    term = lane % N_SPLIT
    c3 = jnp.where(term == 0, c_hi, jnp.where(term == 1, c_mid, c_lo))
    kaug_ref[0] = (_dot(c3, pk_ref[...]) + onek_ref[...]).astype(jnp.bfloat16)
    qaug_ref[0] = (_dot_nt(pqt_ref[...], c3) + oneq_ref[...]).astype(jnp.bfloat16)
    lane_row = lax.broadcasted_iota(jnp.int32, (1, LANES), 1)
    kn2 = jnp.max(_dot((kp * kp).astype(jnp.bfloat16), hsel_ref[...]), axis=0, keepdims=True)
    qn2 = jnp.zeros((1, LANES), jnp.float32)
    for hd in range(N_HEADS):
        qs = qt[hd * HEAD_DIM:(hd + 1) * HEAD_DIM, :]
        qmax = jnp.max(jnp.sum(qs * qs, axis=0, keepdims=True), axis=1, keepdims=True)
        qn2 = jnp.where(lane_row == hd, qmax, qn2)
    stats_ref[0, 0, 0:1, :] = kn2
    stats_ref[0, 0, 1:2, :] = qn2
    stats_ref[0, 0, 2:3, :] = c2[0:1, :]
    stats_ref[0, 0, 3:4, :] = c2[tm - 1:tm, :]
    stats_ref[0, 0, 4:SUBLANES, :] = jnp.zeros((SUBLANES - 4, LANES), jnp.float32)
    ones = jnp.ones((V_ROWS - HEAD_DIM, tm), jnp.bfloat16)
    for hd in range(N_HEADS):
        vt_ref[0, 0, hd * V_ROWS:hd * V_ROWS + HEAD_DIM, :] = vt[hd * HEAD_DIM:(hd + 1) * HEAD_DIM]
        vt_ref[0, 0, hd * V_ROWS + HEAD_DIM:(hd + 1) * V_ROWS, :] = ones

    head_pos = s * tm + lax.broadcasted_iota(jnp.int32, (MAX_WINDOW, 1), 0)
    for gi, w in enumerate(POOL_WINDOWS):
        cols = slice(gi * POOL_GROUP, (gi + 1) * POOL_GROUP)
        ug = u[:, cols]
        win = jnp.concatenate([ubuf[:, cols], ug], axis=0)
        span = 1
        while span < w:
            win = win + pltpu.roll(win, span, axis=0)
            span *= 2
        acc = win[MAX_WINDOW:]
        inv_head = 1.0 / jnp.minimum(head_pos + 1, w).astype(jnp.float32)
        p = jnp.concatenate([acc[:MAX_WINDOW] * inv_head, acc[MAX_WINDOW:] * (1.0 / w)],
                            axis=0) - ug
        y = _dot(p.astype(jnp.bfloat16), wpool_ref[gi]) * pscale_ref[:, cols]
        ypool_ref[0, :, cols] = y.astype(jnp.bfloat16)
    ubuf[...] = u[tm - MAX_WINDOW:]


def _in_projection(x, g_mix, w_in, b_forget, w_pool, pool_scale):
    B, S, D = x.shape
    tm = SEQ_TILE
    bf16 = jnp.bfloat16
    o = POOL_WIDTH
    w_u = w_in[:, :o].astype(bf16)
    w_q = w_in[:, o:o + ATTN_WIDTH] * (HEAD_DIM ** -0.5 * LOG2_E)
    w_k = w_in[:, o + ATTN_WIDTH:o + 2 * ATTN_WIDTH]
    w_v = w_in[:, o + 2 * ATTN_WIDTH:o + 3 * ATTN_WIDTH]
    w_f = w_in[:, o + 3 * ATTN_WIDTH:]

    wqt = w_q.T.astype(bf16)
    wk = w_k.astype(bf16)
    wvt = w_v.T.astype(bf16)
    n_gate = N_SPLIT * N_HEADS
    wf = jnp.pad(jnp.repeat(w_f, N_SPLIT, axis=1), ((0, 0), (0, LANES - n_gate))).astype(bf16)
    bfp = jnp.pad(jnp.repeat(b_forget, N_SPLIT), (0, LANES - n_gate)).reshape(1, LANES)

    assert N_HEADS * AUG_GROUP == LANES and 2 * N_SPLIT <= AUG_GROUP
    src = jnp.arange(LANES)
    head, term = src // N_SPLIT, src % N_SPLIT
    dst = jnp.arange(LANES)
    valid = (src < n_gate)[:, None]
    q_hit = valid & (dst[None, :] == (head * AUG_GROUP + term)[:, None])
    k_hit = valid & (dst[None, :] == (head * AUG_GROUP + N_SPLIT + term)[:, None])
    pqt = q_hit.T.astype(bf16)
    pk = -k_hit.astype(bf16)
    slot_pos = dst % AUG_GROUP
    oneq = ((slot_pos >= N_SPLIT) & (slot_pos < 2 * N_SPLIT))
    onek = slot_pos < N_SPLIT
    oneq = oneq.astype(jnp.float32).reshape(-1, 1)
    onek = onek.astype(jnp.float32).reshape(1, -1)
    r = jnp.arange(tm)
    tri = (r[None, :] <= r[:, None]).astype(bf16)
    hsel = (jnp.arange(ATTN_WIDTH)[:, None] // HEAD_DIM == jnp.arange(LANES)[None, :]).astype(bf16)

    full = lambda *shape: pl.BlockSpec(shape, lambda b, s: (0,) * len(shape))
    aw = ATTN_WIDTH
    return pl.pallas_call(
        _inproj_kernel,
        grid=(B, S // tm),
        in_specs=[
            pl.BlockSpec((1, tm, D), lambda b, s: (b, s, 0)),
            full(1, D), full(D, o), full(aw, D), full(D, aw), full(aw, D),
            full(D, LANES), full(1, LANES), full(tm, tm), full(LANES, LANES), full(LANES, LANES),
            full(LANES, 1), full(1, LANES), full(aw, LANES),
            full(len(POOL_WINDOWS), POOL_GROUP, POOL_GROUP),
            full(1, o),
        ],
        out_specs=[
            pl.BlockSpec((1, tm, o), lambda b, s: (b, s, 0)),
            pl.BlockSpec((1, aw, tm), lambda b, s: (b, 0, s)),
            pl.BlockSpec((1, LANES, tm), lambda b, s: (b, 0, s)),
            pl.BlockSpec((1, tm, aw), lambda b, s: (b, s, 0)),
            pl.BlockSpec((1, tm, LANES), lambda b, s: (b, s, 0)),
            pl.BlockSpec((1, 1, N_HEADS * V_ROWS, tm), lambda b, s: (b, s, 0, 0)),
            pl.BlockSpec((1, 1, SUBLANES, LANES), lambda b, s: (b, s, 0, 0)),
        ],
        out_shape=[
            jax.ShapeDtypeStruct((B, S, o), bf16),
            jax.ShapeDtypeStruct((B, aw, S), bf16),
            jax.ShapeDtypeStruct((B, LANES, S), bf16),
            jax.ShapeDtypeStruct((B, S, aw), bf16),
            jax.ShapeDtypeStruct((B, S, LANES), bf16),
            jax.ShapeDtypeStruct((B, S // tm, N_HEADS * V_ROWS, tm), bf16),
            jax.ShapeDtypeStruct((B, S // tm, SUBLANES, LANES), jnp.float32),
        ],
        scratch_shapes=[
            pltpu.VMEM((MAX_WINDOW, o), jnp.float32),
            pltpu.VMEM((1, LANES), jnp.float32),
        ],
        compiler_params=pltpu.CompilerParams(
            dimension_semantics=("arbitrary", "arbitrary"), vmem_limit_bytes=VMEM_LIMIT),
        name="in_projection",
    )(x, g_mix.reshape(1, D), w_u, wqt, wk, wvt, wf, bfp, tri, pqt, pk, oneq, onek, hsel,
      w_pool.astype(bf16), pool_scale.reshape(1, o))


def _attn_kernel(j0_ref, qt_ref, qaug_ref, k_ref, kaug_ref, vt_ref, causal_ref, o_ref,
                 qop_ref, st_buf, mx_buf, m_ref, acc_ref):
    i = pl.program_id(2)
    tile = (pl.program_id(0) * pl.num_programs(1) + pl.program_id(1)) * pl.num_programs(2) + i
    j0 = j0_ref[tile]
    tq = qt_ref.shape[2]
    tk = vt_ref.shape[3]
    heads = qt_ref.shape[1] // HEAD_DIM

    m_ref[...] = jnp.full(m_ref.shape, MASK_VALUE, jnp.float32)
    acc_ref[...] = jnp.zeros_like(acc_ref)

    first_head = pl.program_id(1) * heads
    group = lax.broadcasted_iota(jnp.int32, (LANES, tq), 0) // AUG_GROUP
    q_aug = qaug_ref[0]
    blank = jnp.zeros((HEAD_DIM, tq), q_aug.dtype)
    for hh in range(heads):
        q_own = qt_ref[0, hh * HEAD_DIM:(hh + 1) * HEAD_DIM, :]
        pair_rows = [q_own, blank] if hh % 2 == 0 else [blank, q_own]
        qop_ref[hh, 0:2 * HEAD_DIM, :] = jnp.concatenate(pair_rows, axis=0)
        qop_ref[hh, 2 * HEAD_DIM:, :] = jnp.where(group == first_head + hh, q_aug,
                                                  jnp.zeros_like(q_aug))

    def logits(j, slot, hh, masked):
        row0 = pl.multiple_of(j * tk, tk)
        pair_lanes = slice(hh // 2 * 2 * HEAD_DIM, (hh // 2 + 1) * 2 * HEAD_DIM)
        k = jnp.concatenate([k_ref[0, pl.ds(row0, tk), pair_lanes],
                             kaug_ref[0, pl.ds(row0, tk), :]], axis=1)
        st = _dot(k, qop_ref[hh])
        if masked:
            st = st + causal_ref[...]
        st_buf[slot, hh] = st
        mx_buf[slot, hh] = jnp.max(st, axis=0, keepdims=True)

    def accumulate(j, slot, hh):
        st = st_buf[slot, hh]
        m_prev = m_ref[hh]
        m_new = jnp.maximum(m_prev, mx_buf[slot, hh])
        alpha = jnp.exp2(m_prev - m_new)
        p = jnp.exp2(st - m_new)
        vt = vt_ref[0, j, hh * V_ROWS:(hh + 1) * V_ROWS, :]
        acc_ref[hh] = alpha * acc_ref[hh] + _dot(vt, p.astype(jnp.bfloat16))
        m_ref[hh] = m_new

    ahead = LOGITS_AHEAD
    assert 1 <= ahead <= heads
    for hh in range(heads):
        logits(i, 0, hh, True)
        if hh >= ahead:
            accumulate(i, 0, hh - ahead)

    def before(t):
        return jnp.where(t == j0, i, t - 1)

    def step(t, slot):
        for hh in range(heads):
            logits(t, slot, hh, False)
            if hh >= ahead:
                accumulate(t, slot, hh - ahead)
            else:
                accumulate(before(t), 1 - slot, hh - ahead + heads)

    def drain(t, slot):
        for hh in range(heads - ahead, heads):
            accumulate(t, slot, hh)

    def body(u, carry):
        step(j0 + 2 * u, 1)
        step(j0 + 2 * u + 1, 0)
        return carry

    n_chunks = i - j0
    lax.fori_loop(0, n_chunks // 2, body, 0)

    @pl.when(n_chunks % 2 == 1)
    def _():
        step(i - 1, 1)
        drain(i - 1, 1)

    @pl.when(n_chunks % 2 == 0)
    def _():
        drain(before(i), 0)

    outs = [acc_ref[hh, :HEAD_DIM] / acc_ref[hh, HEAD_DIM:HEAD_DIM + 1] for hh in range(heads)]
    ot = jnp.concatenate(outs, axis=0)
    o_ref[0] = ot.T.astype(o_ref.dtype)


def _first_needed_chunk(stats):
    kn = jnp.sqrt(stats[:, :, 0, :N_HEADS])
    qn = jnp.sqrt(stats[:, :, 1, :N_HEADS])
    c_first = stats[:, :, 2, 0:N_SPLIT * N_HEADS:N_SPLIT]
    c_last = stats[:, :, 3, 0:N_SPLIT * N_HEADS:N_SPLIT]
    n = stats.shape[1]
    dot_bound = NORM_SLACK * qn[:, :, None, :] * kn[:, None, :, :] + 1.0
    upper = dot_bound + c_first[:, :, None, :] - c_last[:, None, :, :]
    self_bound = NORM_SLACK * qn * kn + 1.0
    ii = jnp.arange(n, dtype=jnp.int32)[None, :, None, None]
    jj = jnp.arange(n, dtype=jnp.int32)[None, None, :, None]
    needed = (upper + self_bound[:, :, None, :] >= -SKIP_MARGIN) & (jj < ii)
    first = jnp.min(jnp.where(needed, jj, ii), axis=2)
    g = HEADS_PER_STEP
    first = jnp.min(first.reshape(first.shape[0], n, N_HEADS // g, g), axis=3)
    return jnp.transpose(first, (0, 2, 1)).reshape(-1).astype(jnp.int32)


def _attention(qt, qaug, k, kaug, vt, first_chunk):
    B, _, S = qt.shape
    tq, tk = Q_TILE, KV_TILE
    assert tq == tk, "the diagonal is handled as a single masked key chunk"
    pair = HEADS_PER_STEP
    n_pairs = N_HEADS // pair
    pw = pair * HEAD_DIM
    causal = jnp.where(jnp.arange(tk)[:, None] <= jnp.arange(tq)[None, :], 0.0, MASK_VALUE)
    grid_spec = pltpu.PrefetchScalarGridSpec(
        num_scalar_prefetch=1,
        grid=(B, n_pairs, S // tq),
        in_specs=[
            pl.BlockSpec((1, pw, tq), lambda b, p, i, j0: (b, p, i)),
            pl.BlockSpec((1, LANES, tq), lambda b, p, i, j0: (b, 0, i)),
            pl.BlockSpec((1, S, pw), lambda b, p, i, j0: (b, 0, p)),
            pl.BlockSpec((1, S, LANES), lambda b, p, i, j0: (b, 0, 0)),
            pl.BlockSpec((1, S // tk, pair * V_ROWS, tk), lambda b, p, i, j0: (b, 0, p, 0)),
            pl.BlockSpec((tk, tq), lambda b, p, i, j0: (0, 0)),
        ],
        out_specs=pl.BlockSpec((1, tq, pw), lambda b, p, i, j0: (b, i, p)),
        scratch_shapes=[
            pltpu.VMEM((pair, 2 * HEAD_DIM + LANES, tq), jnp.bfloat16),
            pltpu.VMEM((2, pair, tk, tq), jnp.float32),
            pltpu.VMEM((2, pair, 1, tq), jnp.float32),
            pltpu.VMEM((pair, 1, tq), jnp.float32),
            pltpu.VMEM((pair, V_ROWS, tq), jnp.float32),
        ],
    )
    return pl.pallas_call(
        _attn_kernel,
        grid_spec=grid_spec,
        out_shape=jax.ShapeDtypeStruct((B, S, ATTN_WIDTH), jnp.bfloat16),
        compiler_params=pltpu.CompilerParams(
            dimension_semantics=("arbitrary", "arbitrary", "arbitrary"),
            vmem_limit_bytes=VMEM_LIMIT),
        name="forgetting_attention",
    )(first_chunk, qt, qaug, k, kaug, vt, causal.astype(jnp.float32))


def _outproj_kernel(yp_ref, ya_ref, x_ref, wo1_ref, wo2_ref, g_ref, wr2_ref, wrhi_ref, br_ref,
                    upper_ref, x1_ref, h2_ref, idx_ref, gate_ref, rank_ref, cnt_ref, carry):
    t = pl.program_id(0)
    tm = x_ref.shape[0]

    @pl.when(t == 0)
    def _():
        carry[...] = jnp.zeros_like(carry)

    hm = upper_ref.shape[0]
    parts = [slice(r * hm, (r + 1) * hm) for r in range(tm // hm)]

    def project(rows):
        x1 = x_ref[rows] + _dot(yp_ref[rows], wo1_ref[...]) + _dot(ya_ref[rows], wo2_ref[...])
        x1_ref[rows] = x1
        return x1

    def route(rows, x1):
        h2 = _rms_norm(x1, g_ref[...])
        h2_hi = h2.astype(jnp.bfloat16)
        h2_lo = (h2 - h2_hi.astype(jnp.float32)).astype(jnp.bfloat16)
        h2_ref[rows] = _pack_bf16_pairs(h2)

        lg2 = _dot_nt(wr2_ref[...], h2_hi)
        logits = (lg2[:N_EXPERTS] + lg2[N_EXPERTS:] + _dot_nt(wrhi_ref[...], h2_lo)
                  + br_ref[...])

        eio = lax.broadcasted_iota(jnp.int32, (N_EXPERTS, hm), 0)
        vals, idxs = [], []
        cur = logits
        for _ in range(TOP_K):
            m = jnp.max(cur, axis=0, keepdims=True)
            ix = jnp.min(jnp.where(cur == m, eio, N_EXPERTS), axis=0, keepdims=True)
            vals.append(m)
            idxs.append(ix)
            cur = jnp.where(eio == ix, -jnp.inf, cur)
        exps = [jnp.exp(v - vals[0]) for v in vals]
        denom = exps[0] + exps[1] + exps[2] + exps[3]
        hits = [eio == ix for ix in idxs]
        cnt = jnp.zeros((N_EXPERTS, hm), jnp.float32)
        for hit in hits:
            cnt = cnt + hit.astype(jnp.float32)
        base = _dot(cnt.astype(jnp.bfloat16), upper_ref[...]) + carry[...]
        for kk in range(TOP_K):
            idx_ref[kk:kk + 1, rows] = idxs[kk]
            gate_ref[kk:kk + 1, rows] = exps[kk] / denom
            rank = jnp.sum(jnp.where(hits[kk], base, 0.0), axis=0, keepdims=True)
            rank_ref[kk:kk + 1, rows] = rank.astype(jnp.int32)
        carry[...] = carry[...] + jnp.sum(cnt, axis=1, keepdims=True)

    pending = project(parts[0])
    for r, rows in enumerate(parts):
        ahead = project(parts[r + 1]) if r + 1 < len(parts) else None
        route(rows, pending)
        pending = ahead
    cnt_ref[...] = carry[...]


def _out_projection(ypool, yattn, x, w_out, g_ffn, w_router, b_router):
    T, D = x.shape
    tm = ROUTE_TILE
    bf16 = jnp.bfloat16
    wo = w_out.astype(bf16)
    wr_t = w_router.T
    wr_hi = wr_t.astype(bf16)
    wr_lo = (wr_t - wr_hi.astype(jnp.float32)).astype(bf16)
    wr2 = jnp.concatenate([wr_hi, wr_lo], axis=0)
    r = jnp.arange(ROUTE_PART)
    upper = (r[:, None] < r[None, :]).astype(bf16)
    full = lambda *shape: pl.BlockSpec(shape, lambda t: (0,) * len(shape))
    row = lambda w: pl.BlockSpec((tm, w), lambda t: (t, 0))
    col = pl.BlockSpec((TOP_K, tm), lambda t: (0, t))
    return pl.pallas_call(
        _outproj_kernel,
        grid=(T // tm,),
        in_specs=[row(POOL_WIDTH), row(ATTN_WIDTH), row(D), full(POOL_WIDTH, D),
                  full(ATTN_WIDTH, D), full(1, D), full(2 * N_EXPERTS, D), full(N_EXPERTS, D),
                  full(N_EXPERTS, 1), full(ROUTE_PART, ROUTE_PART)],
        out_specs=[row(D), row(D // 2), col, col, col, full(N_EXPERTS, 1)],
        out_shape=[
            jax.ShapeDtypeStruct((T, D), jnp.float32),
            jax.ShapeDtypeStruct((T, D // 2), jnp.uint32),
            jax.ShapeDtypeStruct((TOP_K, T), jnp.int32),
            jax.ShapeDtypeStruct((TOP_K, T), jnp.float32),
            jax.ShapeDtypeStruct((TOP_K, T), jnp.int32),
            jax.ShapeDtypeStruct((N_EXPERTS, 1), jnp.float32),
        ],
        scratch_shapes=[pltpu.VMEM((N_EXPERTS, 1), jnp.float32)],
        compiler_params=pltpu.CompilerParams(
            dimension_semantics=("arbitrary",), vmem_limit_bytes=VMEM_LIMIT),
        name="out_projection_router",
    )(ypool, yattn, x, wo[:POOL_WIDTH], wo[POOL_WIDTH:], g_ffn.reshape(1, D), wr2, wr_hi,
      b_router.reshape(N_EXPERTS, 1), upper)


def _sc_mesh():
    return plsc.VectorSubcoreMesh(core_axis_name="c", subcore_axis_name="s")


def _dispatch_rows(h2, dest, n_rows):
    T, D = h2.shape
    per_worker = T // SC_WORKERS
    steps = per_worker // SC_WINDOW

    @functools.partial(
        pl.kernel, mesh=_sc_mesh(),
        out_type=jax.ShapeDtypeStruct((n_rows, D), h2.dtype),
        scratch_types=[pltpu.VMEM((TOP_K, SC_WINDOW), jnp.int32),
                       pltpu.VMEM((SC_WINDOW, D), h2.dtype),
                       pltpu.SemaphoreType.DMA],
    )
    def scatter_kernel(h2_hbm, dest_hbm, xs_hbm, idx_v, rows_v, sem):
        wid = lax.axis_index("s") * SC_CORES + lax.axis_index("c")

        @pl.loop(0, steps)
        def _(step):
            base = wid * per_worker + step * SC_WINDOW
            pltpu.sync_copy(h2_hbm.at[pl.ds(base, SC_WINDOW)], rows_v)
            pltpu.sync_copy(dest_hbm.at[:, pl.ds(base, SC_WINDOW)], idx_v)
            copies = [pltpu.async_copy(rows_v, xs_hbm.at[idx_v.at[kk]], sem)
                      for kk in range(TOP_K)]
            for c in copies:
                c.wait()

    return scatter_kernel(h2, dest)


def _gather_rows(table, idx):
    N = idx.shape[0]
    D = table.shape[1]
    per_worker = N // SC_WORKERS
    steps = per_worker // SC_WINDOW

    @functools.partial(
        pl.kernel, mesh=_sc_mesh(),
        out_type=jax.ShapeDtypeStruct((N, D), table.dtype),
        scratch_types=[pltpu.VMEM((SC_WINDOW,), jnp.int32),
                       pltpu.VMEM((SC_WINDOW, D), table.dtype),
                       pltpu.SemaphoreType.DMA],
    )
    def gather_kernel(table_hbm, idx_hbm, out_hbm, idx_v, rows_v, sem):
        wid = lax.axis_index("s") * SC_CORES + lax.axis_index("c")

        @pl.loop(0, steps)
        def _(step):
            base = wid * per_worker + step * SC_WINDOW
            pltpu.sync_copy(idx_hbm.at[pl.ds(base, SC_WINDOW)], idx_v)
            pltpu.async_copy(table_hbm.at[idx_v], rows_v, sem).wait()
            pltpu.sync_copy(rows_v, out_hbm.at[pl.ds(base, SC_WINDOW)])

    return gather_kernel(table, idx)


def _expert_kernel(be_ref, nvalid_ref, x_ref, wup_ref, bup_ref, wdn_ref, bdn_ref, y_ref,
                   wup_bf, wdn_bf):
    i = pl.program_id(0)
    blk = x_ref.shape[0]
    changed = jnp.logical_or(i == 0, be_ref[i] != be_ref[jnp.maximum(i - 1, 0)])

    @pl.when(changed)
    def _():
        wup_bf[...] = wup_ref[0].astype(jnp.bfloat16)
        wdn_bf[...] = wdn_ref[0].astype(jnp.bfloat16)

    nvalid = nvalid_ref[i]

    @pl.when(nvalid > 0)
    def _():
        row = lax.broadcasted_iota(jnp.int32, (blk, 1), 0)
        x_hi, x_lo = _unpack_bf16_pairs(jnp.where(row < nvalid, x_ref[...], jnp.uint32(0)))
        x = jnp.concatenate([x_hi.astype(jnp.bfloat16), x_lo.astype(jnp.bfloat16)], axis=1)
        glu = _dot(x, wup_bf[:, :D_EXPERT]) + bup_ref[0, :, :D_EXPERT]
        lin = _dot(x, wup_bf[:, D_EXPERT:]) + bup_ref[0, :, D_EXPERT:]
        glu = jnp.minimum(glu, SWIGLU_LIMIT)
        lin = jnp.clip(lin, -SWIGLU_LIMIT, SWIGLU_LIMIT)
        act = glu * (1.0 / (1.0 + jnp.exp(-SWIGLU_ALPHA * glu))) * (lin + 1.0)
        y = _dot(act.astype(jnp.bfloat16), wdn_bf[...]) + bdn_ref[0]
        y_ref[...] = _pack_bf16_pairs(y)

    @pl.when(nvalid <= 0)
    def _():
        y_ref[...] = jnp.zeros_like(y_ref)


def _expert_ffn(xs, block_e, block_valid, w_up, b_up, w_down, b_down):
    n_rows = xs.shape[0]
    D = D_MODEL
    blk = EXPERT_ROWS
    grid_spec = pltpu.PrefetchScalarGridSpec(
        num_scalar_prefetch=2,
        grid=(n_rows // blk,),
        in_specs=[
            pl.BlockSpec((blk, D // 2), lambda i, be, nv: (i, 0)),
            pl.BlockSpec((1, D, 2 * D_EXPERT), lambda i, be, nv: (be[i], 0, 0)),
            pl.BlockSpec((1, 1, 2 * D_EXPERT), lambda i, be, nv: (be[i], 0, 0)),
            pl.BlockSpec((1, D_EXPERT, D), lambda i, be, nv: (be[i], 0, 0)),
            pl.BlockSpec((1, 1, D), lambda i, be, nv: (be[i], 0, 0)),
        ],
        out_specs=pl.BlockSpec((blk, D // 2), lambda i, be, nv: (i, 0)),
        scratch_shapes=[pltpu.VMEM((D, 2 * D_EXPERT), jnp.bfloat16),
                        pltpu.VMEM((D_EXPERT, D), jnp.bfloat16)],
    )
    return pl.pallas_call(
        _expert_kernel,
        grid_spec=grid_spec,
        out_shape=jax.ShapeDtypeStruct((n_rows, D // 2), jnp.uint32),
        compiler_params=pltpu.CompilerParams(
            dimension_semantics=("arbitrary",), vmem_limit_bytes=VMEM_LIMIT),
        name="expert_ffn",
    )(block_e, block_valid, xs, w_up, b_up.reshape(N_EXPERTS, 1, 2 * D_EXPERT), w_down,
      b_down.reshape(N_EXPERTS, 1, D))


def _final_kernel(x1_ref, rows_ref, gate_ref, g_ref, o_ref):
    half = x1_ref.shape[1] // 2
    acc_hi = x1_ref[:, :half]
    acc_lo = x1_ref[:, half:]
    for kk in range(TOP_K):
        y_hi, y_lo = _unpack_bf16_pairs(rows_ref[kk])
        gate = gate_ref[:, kk:kk + 1]
        acc_hi = acc_hi + y_hi * gate
        acc_lo = acc_lo + y_lo * gate
    o_ref[...] = _rms_norm(jnp.concatenate([acc_hi, acc_lo], axis=1), g_ref[...])


def _final_alias_kernel(x1_ref, rows_ref, gate_ref, g_ref, prev_ref, o_ref):
    del prev_ref
    _final_kernel(x1_ref, rows_ref, gate_ref, g_ref, o_ref)


def _final_norm(x1, rows, gates_t, g_final, part, prev_out):
    T, D = x1.shape
    Tp = rows.shape[1]
    tm = SEQ_TILE
    t0 = part * (Tp // tm)
    row = pl.BlockSpec((tm, D), lambda t: (t + t0, 0))
    in_specs = [row, pl.BlockSpec((TOP_K, tm, D // 2), lambda t: (0, t, 0)),
                pl.BlockSpec((tm, TOP_K), lambda t: (t + t0, 0)),
                pl.BlockSpec((1, D), lambda t: (0, 0))]
    args = [x1, rows, gates_t, g_final.reshape(1, D)]
    body, aliases = _final_kernel, {}
    if prev_out is not None:
        in_specs.append(pl.BlockSpec(memory_space=pl.ANY))
        args.append(prev_out)
        body, aliases = _final_alias_kernel, {len(args) - 1: 0}
    return pl.pallas_call(
        body,
        grid=(Tp // tm,),
        in_specs=in_specs,
        out_specs=row,
        out_shape=jax.ShapeDtypeStruct((T, D), jnp.float32),
        input_output_aliases=aliases,
        compiler_params=pltpu.CompilerParams(dimension_semantics=("arbitrary",)),
        name="combine_final_norm",
    )(*args)


def kernel(x, g_mix, w_in, b_forget, w_pool, pool_scale, w_out, g_ffn, w_router, b_router,
           w_up, b_up, w_down, b_down, g_final):
    B, S, D = x.shape
    T = B * S
    ypool, qt, qaug, k, kaug, vt, stats = _in_projection(x, g_mix[0], w_in[0], b_forget[0],
                                                        w_pool[0], pool_scale[0])
    yattn = _attention(qt, qaug, k, kaug, vt, _first_needed_chunk(stats))
    x1, h2, top_idx, gates, rank, counts = _out_projection(
        ypool.reshape(T, POOL_WIDTH), yattn.reshape(T, ATTN_WIDTH), x.reshape(T, D),
        w_out[0], g_ffn[0], w_router[0], b_router[0])

    blk = EXPERT_ROWS
    counts = counts[:, 0].astype(jnp.int32)
    padded = (counts + blk - 1) // blk * blk
    pad_ends = jnp.cumsum(padded)
    pad_starts = pad_ends - padded
    dest = rank
    for e in range(N_EXPERTS):
        dest = dest + jnp.where(top_idx == e, pad_starts[e], 0)
    n_rows = T * TOP_K + N_EXPERTS * blk
    block_row0 = jnp.arange(n_rows // blk, dtype=jnp.int32) * blk
    block_e = jnp.sum((pad_ends[None, :] <= block_row0[:, None]).astype(jnp.int32), axis=1)
    block_e = jnp.minimum(block_e, N_EXPERTS - 1)
    of_block = block_e[:, None] == jnp.arange(N_EXPERTS, dtype=jnp.int32)[None, :]
    block_end = jnp.sum(jnp.where(of_block, (pad_starts + counts)[None, :], 0), axis=1)
    block_valid = jnp.clip(block_end - block_row0, 0, blk)
    block_valid = jnp.where(block_row0 < pad_ends[-1], block_valid, 0).astype(jnp.int32)

    xs = _dispatch_rows(h2, dest, n_rows)
    ys = _expert_ffn(xs, block_e, block_valid, w_up[0], b_up[0], w_down[0], b_down[0])
    gates_t = gates.T
    tp = T // COMBINE_PARTS
    out = None
    for part in range(COMBINE_PARTS):
        part_dest = dest[:, part * tp:(part + 1) * tp].reshape(-1)
        rows = _gather_rows(ys, part_dest).reshape(TOP_K, tp, D // 2)
        out = _final_norm(x1, rows, gates_t, g_final, part, out)
    return out.reshape(B, S, D)
```

```python
import functools

import jax
import jax.numpy as jnp
from jax import lax
from jax.experimental import pallas as pl
from jax.experimental.pallas import tpu as pltpu
from jax.experimental.pallas import tpu_sc as plsc

D_MODEL = 1024
POOL_WIDTH = 512
POOL_WINDOWS = (2, 4, 8, 16)
POOL_GROUP = 128
MAX_WINDOW = 16
ATTN_WIDTH = 512
HEAD_DIM = 64
N_HEADS = 8
N_EXPERTS = 32
TOP_K = 4
D_EXPERT = 1024
SWIGLU_LIMIT = 7.0
SWIGLU_ALPHA = 1.702
RMS_EPS = 1e-5

LANES = 128
SUBLANES = 8
AUG_GROUP = 16
N_SPLIT = 3
MASK_VALUE = -1e30
LOG2_E = 1.4426950408889634
SKIP_MARGIN = 160.0
NORM_SLACK = 1.02
V_ROWS = 80

SEQ_TILE = 512
Q_TILE = 512
HEADS_PER_STEP = 4
LOGITS_AHEAD = 2
KV_TILE = SEQ_TILE
ROUTE_TILE = 1024
ROUTE_PART = 512
EXPERT_ROWS = 512
COMBINE_PARTS = 8
SC_CORES = 2
SC_SUBCORES = 16
SC_WORKERS = SC_CORES * SC_SUBCORES
SC_WINDOW = 128
VMEM_LIMIT = 56 * 1024 * 1024

_NT = (((1,), (1,)), ((), ()))


def _dot(a, b):
    return jnp.dot(a, b, preferred_element_type=jnp.float32)


def _dot_nt(a, b):
    return lax.dot_general(a, b, _NT, preferred_element_type=jnp.float32)


def _rms_norm(x, g):
    return x * lax.rsqrt(jnp.mean(x * x, axis=-1, keepdims=True) + RMS_EPS) * g


def _pack_bf16_pairs(x):
    n = x.shape[1] // 2
    bits = lax.bitcast_convert_type(x.astype(jnp.bfloat16).astype(jnp.float32), jnp.uint32)
    return bits[:, :n] | (bits[:, n:] >> 16)


def _unpack_bf16_pairs(p):
    hi = lax.bitcast_convert_type(p & jnp.uint32(0xFFFF0000), jnp.float32)
    lo = lax.bitcast_convert_type(p << 16, jnp.float32)
    return hi, lo


def _split3(x):
    hi = x.astype(jnp.bfloat16)
    r1 = x - hi.astype(jnp.float32)
    mid = r1.astype(jnp.bfloat16)
    lo = (r1 - mid.astype(jnp.float32)).astype(jnp.bfloat16)
    return hi, mid, lo


def _inproj_kernel(x_ref, g_ref, wu_ref, wqt_ref, wk_ref, wvt_ref, wf_ref, bf_ref, tri_ref,
                   pqt_ref, pk_ref, oneq_ref, onek_ref, hsel_ref, wpool_ref, pscale_ref,
                   ypool_ref, qt_ref, qaug_ref, k_ref, kaug_ref, vt_ref, stats_ref, ubuf, ccarry):
    s = pl.program_id(1)
    tm = x_ref.shape[1]

    @pl.when(s == 0)
    def _():
        ubuf[...] = jnp.zeros_like(ubuf)
        ccarry[...] = jnp.zeros_like(ccarry)

    x = x_ref[0]
    h = _rms_norm(x, g_ref[...]).astype(jnp.bfloat16)
    fl = _dot(h, wf_ref[...]) + bf_ref[...]
    u = _dot(h, wu_ref[...])
    kp = _dot(h, wk_ref[...])
    qt = _dot_nt(wqt_ref[...], h)
    vt = _dot_nt(wvt_ref[...], h).astype(jnp.bfloat16)
    k_ref[0] = kp.astype(jnp.bfloat16)
    qt_ref[0] = qt.astype(jnp.bfloat16)

    logf = jnp.minimum(fl, 0.0) - jnp.log1p(jnp.exp(-jnp.abs(fl)))
    lane = lax.broadcasted_iota(jnp.int32, (tm, LANES), 1)
    logf = jnp.where(lane < N_SPLIT * N_HEADS, logf, 0.0)
    f_hi, f_mid, f_lo = _split3(logf)
    tri = tri_ref[...]
    c = _dot(tri, f_hi) + _dot(tri, f_mid) + _dot(tri, f_lo) + ccarry[...]
    ccarry[...] = c[tm - 1:tm, :]
    c2 = c * LOG2_E
    c_hi, c_mid, c_lo = _split3(c2)
    term = lane % N_SPLIT
    c3 = jnp.where(term == 0, c_hi, jnp.where(term == 1, c_mid, c_lo))
    kaug_ref[0] = (_dot(c3, pk_ref[...]) + onek_ref[...]).astype(jnp.bfloat16)
    qaug_ref[0] = (_dot_nt(pqt_ref[...], c3) + oneq_ref[...]).astype(jnp.bfloat16)
    lane_row = lax.broadcasted_iota(jnp.int32, (1, LANES), 1)
    kn2 = jnp.max(_dot((kp * kp).astype(jnp.bfloat16), hsel_ref[...]), axis=0, keepdims=True)
    qn2 = jnp.zeros((1, LANES), jnp.float32)
    for hd in range(N_HEADS):
        qs = qt[hd * HEAD_DIM:(hd + 1) * HEAD_DIM, :]
        qmax = jnp.max(jnp.sum(qs * qs, axis=0, keepdims=True), axis=1, keepdims=True)
        qn2 = jnp.where(lane_row == hd, qmax, qn2)
    stats_ref[0, 0, 0:1, :] = kn2
    stats_ref[0, 0, 1:2, :] = qn2
    stats_ref[0, 0, 2:3, :] = c2[0:1, :]
    stats_ref[0, 0, 3:4, :] = c2[tm - 1:tm, :]
    stats_ref[0, 0, 4:SUBLANES, :] = jnp.zeros((SUBLANES - 4, LANES), jnp.float32)
    ones = jnp.ones((V_ROWS - HEAD_DIM, tm), jnp.bfloat16)
    for hd in range(N_HEADS):
        vt_ref[0, 0, hd * V_ROWS:hd * V_ROWS + HEAD_DIM, :] = vt[hd * HEAD_DIM:(hd + 1) * HEAD_DIM]
        vt_ref[0, 0, hd * V_ROWS + HEAD_DIM:(hd + 1) * V_ROWS, :] = ones

    head_pos = s * tm + lax.broadcasted_iota(jnp.int32, (MAX_WINDOW, 1), 0)
    for gi, w in enumerate(POOL_WINDOWS):
        cols = slice(gi * POOL_GROUP, (gi + 1) * POOL_GROUP)
        ug = u[:, cols]
        win = jnp.concatenate([ubuf[:, cols], ug], axis=0)
        span = 1
        while span < w:
            win = win + pltpu.roll(win, span, axis=0)
            span *= 2
        acc = win[MAX_WINDOW:]
        inv_head = 1.0 / jnp.minimum(head_pos + 1, w).astype(jnp.float32)
        p = jnp.concatenate([acc[:MAX_WINDOW] * inv_head, acc[MAX_WINDOW:] * (1.0 / w)],
                            axis=0) - ug
        y = _dot(p.astype(jnp.bfloat16), wpool_ref[gi]) * pscale_ref[:, cols]
        ypool_ref[0, :, cols] = y.astype(jnp.bfloat16)
    ubuf[...] = u[tm - MAX_WINDOW:]


def _in_projection(x, g_mix, w_in, b_forget, w_pool, pool_scale, head_order):
    B, S, D = x.shape
    tm = SEQ_TILE
    bf16 = jnp.bfloat16
    o = POOL_WIDTH
    w_u = w_in[:, :o].astype(bf16)
    w_q = w_in[:, o:o + ATTN_WIDTH] * (HEAD_DIM ** -0.5 * LOG2_E)
    w_k = w_in[:, o + ATTN_WIDTH:o + 2 * ATTN_WIDTH]
    w_v = w_in[:, o + 2 * ATTN_WIDTH:o + 3 * ATTN_WIDTH]
    w_f = w_in[:, o + 3 * ATTN_WIDTH:]
    by_head = lambda w: jnp.take(w.reshape(D, N_HEADS, HEAD_DIM), head_order,
                                 axis=1).reshape(D, ATTN_WIDTH)
    w_q, w_k, w_v = by_head(w_q), by_head(w_k), by_head(w_v)
    w_f = jnp.take(w_f, head_order, axis=1)
    b_forget = jnp.take(b_forget, head_order)

    wqt = w_q.T.astype(bf16)
    wk = w_k.astype(bf16)
    wvt = w_v.T.astype(bf16)
    n_gate = N_SPLIT * N_HEADS
    wf = jnp.pad(jnp.repeat(w_f, N_SPLIT, axis=1), ((0, 0), (0, LANES - n_gate))).astype(bf16)
    bfp = jnp.pad(jnp.repeat(b_forget, N_SPLIT), (0, LANES - n_gate)).reshape(1, LANES)

    assert N_HEADS * AUG_GROUP == LANES and 2 * N_SPLIT <= AUG_GROUP
    src = jnp.arange(LANES)
    head, term = src // N_SPLIT, src % N_SPLIT
    dst = jnp.arange(LANES)
    valid = (src < n_gate)[:, None]
    q_hit = valid & (dst[None, :] == (head * AUG_GROUP + term)[:, None])
    k_hit = valid & (dst[None, :] == (head * AUG_GROUP + N_SPLIT + term)[:, None])
    pqt = q_hit.T.astype(bf16)
    pk = -k_hit.astype(bf16)
    slot_pos = dst % AUG_GROUP
    oneq = ((slot_pos >= N_SPLIT) & (slot_pos < 2 * N_SPLIT))
    onek = slot_pos < N_SPLIT
    oneq = oneq.astype(jnp.float32).reshape(-1, 1)
    onek = onek.astype(jnp.float32).reshape(1, -1)
    r = jnp.arange(tm)
    tri = (r[None, :] <= r[:, None]).astype(bf16)
    hsel = (jnp.arange(ATTN_WIDTH)[:, None] // HEAD_DIM == jnp.arange(LANES)[None, :]).astype(bf16)

    full = lambda *shape: pl.BlockSpec(shape, lambda b, s: (0,) * len(shape))
    aw = ATTN_WIDTH
    return pl.pallas_call(
        _inproj_kernel,
        grid=(B, S // tm),
        in_specs=[
            pl.BlockSpec((1, tm, D), lambda b, s: (b, s, 0)),
            full(1, D), full(D, o), full(aw, D), full(D, aw), full(aw, D),
            full(D, LANES), full(1, LANES), full(tm, tm), full(LANES, LANES), full(LANES, LANES),
            full(LANES, 1), full(1, LANES), full(aw, LANES),
            full(len(POOL_WINDOWS), POOL_GROUP, POOL_GROUP),
            full(1, o),
        ],
        out_specs=[
            pl.BlockSpec((1, tm, o), lambda b, s: (b, s, 0)),
            pl.BlockSpec((1, aw, tm), lambda b, s: (b, 0, s)),
            pl.BlockSpec((1, LANES, tm), lambda b, s: (b, 0, s)),
            pl.BlockSpec((1, tm, aw), lambda b, s: (b, s, 0)),
            pl.BlockSpec((1, tm, LANES), lambda b, s: (b, s, 0)),
            pl.BlockSpec((1, 1, N_HEADS * V_ROWS, tm), lambda b, s: (b, s, 0, 0)),
            pl.BlockSpec((1, 1, SUBLANES, LANES), lambda b, s: (b, s, 0, 0)),
        ],
        out_shape=[
            jax.ShapeDtypeStruct((B, S, o), bf16),
            jax.ShapeDtypeStruct((B, aw, S), bf16),
            jax.ShapeDtypeStruct((B, LANES, S), bf16),
            jax.ShapeDtypeStruct((B, S, aw), bf16),
            jax.ShapeDtypeStruct((B, S, LANES), bf16),
            jax.ShapeDtypeStruct((B, S // tm, N_HEADS * V_ROWS, tm), bf16),
            jax.ShapeDtypeStruct((B, S // tm, SUBLANES, LANES), jnp.float32),
        ],
        scratch_shapes=[
            pltpu.VMEM((MAX_WINDOW, o), jnp.float32),
            pltpu.VMEM((1, LANES), jnp.float32),
        ],
        compiler_params=pltpu.CompilerParams(
            dimension_semantics=("arbitrary", "arbitrary"), vmem_limit_bytes=VMEM_LIMIT),
        name="in_projection",
    )(x, g_mix.reshape(1, D), w_u, wqt, wk, wvt, wf, bfp, tri, pqt, pk, oneq, onek, hsel,
      w_pool.astype(bf16), pool_scale.reshape(1, o))


def _attn_kernel(j0_ref, qt_ref, qaug_ref, k_ref, kaug_ref, vt_ref, causal_ref, o_ref,
                 qop_ref, st_buf, mx_buf, m_ref, acc_ref):
    i = pl.program_id(2)
    tile = (pl.program_id(0) * pl.num_programs(1) + pl.program_id(1)) * pl.num_programs(2) + i
    j0 = j0_ref[tile]
    tq = qt_ref.shape[2]
    tk = vt_ref.shape[3]
    heads = qt_ref.shape[1] // HEAD_DIM

    m_ref[...] = jnp.full(m_ref.shape, MASK_VALUE, jnp.float32)
    acc_ref[...] = jnp.zeros_like(acc_ref)

    first_head = pl.program_id(1) * heads
    group = lax.broadcasted_iota(jnp.int32, (LANES, tq), 0) // AUG_GROUP
    q_aug = qaug_ref[0]
    blank = jnp.zeros((HEAD_DIM, tq), q_aug.dtype)
    for hh in range(heads):
        q_own = qt_ref[0, hh * HEAD_DIM:(hh + 1) * HEAD_DIM, :]
        pair_rows = [q_own, blank] if hh % 2 == 0 else [blank, q_own]
        qop_ref[hh, 0:2 * HEAD_DIM, :] = jnp.concatenate(pair_rows, axis=0)
        qop_ref[hh, 2 * HEAD_DIM:, :] = jnp.where(group == first_head + hh, q_aug,
                                                  jnp.zeros_like(q_aug))

    def logits(j, slot, hh, masked):
        row0 = pl.multiple_of(j * tk, tk)
        pair_lanes = slice(hh // 2 * 2 * HEAD_DIM, (hh // 2 + 1) * 2 * HEAD_DIM)
        k = jnp.concatenate([k_ref[0, pl.ds(row0, tk), pair_lanes],
                             kaug_ref[0, pl.ds(row0, tk), :]], axis=1)
        st = _dot(k, qop_ref[hh])
        if masked:
            st = st + causal_ref[...]
        st_buf[slot, hh] = st
        mx_buf[slot, hh] = jnp.max(st, axis=0, keepdims=True)

    def accumulate(j, slot, hh):
        st = st_buf[slot, hh]
        m_prev = m_ref[hh]
        m_new = jnp.maximum(m_prev, mx_buf[slot, hh])
        alpha = jnp.exp2(m_prev - m_new)
        p = jnp.exp2(st - m_new)
        vt = vt_ref[0, j, hh * V_ROWS:(hh + 1) * V_ROWS, :]
        acc_ref[hh] = alpha * acc_ref[hh] + _dot(vt, p.astype(jnp.bfloat16))
        m_ref[hh] = m_new

    ahead = LOGITS_AHEAD
    assert 1 <= ahead <= heads
    for hh in range(heads):
        logits(i, 0, hh, True)
        if hh >= ahead:
            accumulate(i, 0, hh - ahead)

    def before(t):
        return jnp.where(t == j0, i, t - 1)

    def step(t, slot):
        for hh in range(heads):
            logits(t, slot, hh, False)
            if hh >= ahead:
                accumulate(t, slot, hh - ahead)
            else:
                accumulate(before(t), 1 - slot, hh - ahead + heads)

    def drain(t, slot):
        for hh in range(heads - ahead, heads):
            accumulate(t, slot, hh)

    def body(u, carry):
        step(j0 + 2 * u, 1)
        step(j0 + 2 * u + 1, 0)
        return carry

    n_chunks = i - j0
    lax.fori_loop(0, n_chunks // 2, body, 0)

    @pl.when(n_chunks % 2 == 1)
    def _():
        step(i - 1, 1)
        drain(i - 1, 1)

    @pl.when(n_chunks % 2 == 0)
    def _():
        drain(before(i), 0)

    outs = [acc_ref[hh, :HEAD_DIM] / acc_ref[hh, HEAD_DIM:HEAD_DIM + 1] for hh in range(heads)]
    ot = jnp.concatenate(outs, axis=0)
    o_ref[0] = ot.T.astype(o_ref.dtype)


def _decay_head_order(b_forget):
    return jnp.argsort(b_forget).astype(jnp.int32)


def _first_needed_chunk(stats):
    kn = jnp.sqrt(stats[:, :, 0, :N_HEADS])
    qn = jnp.sqrt(stats[:, :, 1, :N_HEADS])
    c_first = stats[:, :, 2, 0:N_SPLIT * N_HEADS:N_SPLIT]
    c_last = stats[:, :, 3, 0:N_SPLIT * N_HEADS:N_SPLIT]
    n = stats.shape[1]
    dot_bound = NORM_SLACK * qn[:, :, None, :] * kn[:, None, :, :] + 1.0
    upper = dot_bound + c_first[:, :, None, :] - c_last[:, None, :, :]
    self_bound = NORM_SLACK * qn * kn + 1.0
    ii = jnp.arange(n, dtype=jnp.int32)[None, :, None, None]
    jj = jnp.arange(n, dtype=jnp.int32)[None, None, :, None]
    needed = (upper + self_bound[:, :, None, :] >= -SKIP_MARGIN) & (jj < ii)
    first = jnp.min(jnp.where(needed, jj, ii), axis=2)
    g = HEADS_PER_STEP
    first = jnp.min(first.reshape(first.shape[0], n, N_HEADS // g, g), axis=3)
    return jnp.transpose(first, (0, 2, 1)).reshape(-1).astype(jnp.int32)


def _attention(qt, qaug, k, kaug, vt, first_chunk):
    B, _, S = qt.shape
    tq, tk = Q_TILE, KV_TILE
    assert tq == tk, "the diagonal is handled as a single masked key chunk"
    pair = HEADS_PER_STEP
    n_pairs = N_HEADS // pair
    pw = pair * HEAD_DIM
    causal = jnp.where(jnp.arange(tk)[:, None] <= jnp.arange(tq)[None, :], 0.0, MASK_VALUE)
    grid_spec = pltpu.PrefetchScalarGridSpec(
        num_scalar_prefetch=1,
        grid=(B, n_pairs, S // tq),
        in_specs=[
            pl.BlockSpec((1, pw, tq), lambda b, p, i, j0: (b, p, i)),
            pl.BlockSpec((1, LANES, tq), lambda b, p, i, j0: (b, 0, i)),
            pl.BlockSpec((1, S, pw), lambda b, p, i, j0: (b, 0, p)),
            pl.BlockSpec((1, S, LANES), lambda b, p, i, j0: (b, 0, 0)),
            pl.BlockSpec((1, S // tk, pair * V_ROWS, tk), lambda b, p, i, j0: (b, 0, p, 0)),
            pl.BlockSpec((tk, tq), lambda b, p, i, j0: (0, 0)),
        ],
        out_specs=pl.BlockSpec((1, tq, pw), lambda b, p, i, j0: (b, i, p)),
        scratch_shapes=[
            pltpu.VMEM((pair, 2 * HEAD_DIM + LANES, tq), jnp.bfloat16),
            pltpu.VMEM((2, pair, tk, tq), jnp.float32),
            pltpu.VMEM((2, pair, 1, tq), jnp.float32),
            pltpu.VMEM((pair, 1, tq), jnp.float32),
            pltpu.VMEM((pair, V_ROWS, tq), jnp.float32),
        ],
    )
    return pl.pallas_call(
        _attn_kernel,
        grid_spec=grid_spec,
        out_shape=jax.ShapeDtypeStruct((B, S, ATTN_WIDTH), jnp.bfloat16),
        compiler_params=pltpu.CompilerParams(
            dimension_semantics=("arbitrary", "arbitrary", "arbitrary"),
            vmem_limit_bytes=VMEM_LIMIT),
        name="forgetting_attention",
    )(first_chunk, qt, qaug, k, kaug, vt, causal.astype(jnp.float32))


def _outproj_kernel(yp_ref, ya_ref, x_ref, wo1_ref, wo2_ref, g_ref, wr2_ref, wrhi_ref, br_ref,
                    upper_ref, x1_ref, h2_ref, idx_ref, gate_ref, rank_ref, cnt_ref, carry):
    t = pl.program_id(0)
    tm = x_ref.shape[0]

    @pl.when(t == 0)
    def _():
        carry[...] = jnp.zeros_like(carry)

    hm = upper_ref.shape[0]
    parts = [slice(r * hm, (r + 1) * hm) for r in range(tm // hm)]

    def project(rows):
        x1 = x_ref[rows] + _dot(yp_ref[rows], wo1_ref[...]) + _dot(ya_ref[rows], wo2_ref[...])
        x1_ref[rows] = x1
        return x1

    def route(rows, x1):
        h2 = _rms_norm(x1, g_ref[...])
        h2_hi = h2.astype(jnp.bfloat16)
        h2_lo = (h2 - h2_hi.astype(jnp.float32)).astype(jnp.bfloat16)
        h2_ref[rows] = _pack_bf16_pairs(h2)

        lg2 = _dot_nt(wr2_ref[...], h2_hi)
        logits = (lg2[:N_EXPERTS] + lg2[N_EXPERTS:] + _dot_nt(wrhi_ref[...], h2_lo)
                  + br_ref[...])

        eio = lax.broadcasted_iota(jnp.int32, (N_EXPERTS, hm), 0)
        vals, idxs = [], []
        cur = logits
        for _ in range(TOP_K):
            m = jnp.max(cur, axis=0, keepdims=True)
            ix = jnp.min(jnp.where(cur == m, eio, N_EXPERTS), axis=0, keepdims=True)
            vals.append(m)
            idxs.append(ix)
            cur = jnp.where(eio == ix, -jnp.inf, cur)
        exps = [jnp.exp(v - vals[0]) for v in vals]
        denom = exps[0] + exps[1] + exps[2] + exps[3]
        hits = [eio == ix for ix in idxs]
        cnt = jnp.zeros((N_EXPERTS, hm), jnp.float32)
        for hit in hits:
            cnt = cnt + hit.astype(jnp.float32)
        base = _dot(cnt.astype(jnp.bfloat16), upper_ref[...]) + carry[...]
        for kk in range(TOP_K):
            idx_ref[kk:kk + 1, rows] = idxs[kk]
            gate_ref[kk:kk + 1, rows] = exps[kk] / denom
            rank = jnp.sum(jnp.where(hits[kk], base, 0.0), axis=0, keepdims=True)
            rank_ref[kk:kk + 1, rows] = rank.astype(jnp.int32)
        carry[...] = carry[...] + jnp.sum(cnt, axis=1, keepdims=True)

    pending = project(parts[0])
    for r, rows in enumerate(parts):
        ahead = project(parts[r + 1]) if r + 1 < len(parts) else None
        route(rows, pending)
        pending = ahead
    cnt_ref[...] = carry[...]


def _out_projection(ypool, yattn, x, w_out, g_ffn, w_router, b_router, head_order):
    T, D = x.shape
    tm = ROUTE_TILE
    bf16 = jnp.bfloat16
    w_attn = jnp.take(w_out[POOL_WIDTH:].reshape(N_HEADS, HEAD_DIM, D), head_order, axis=0)
    wo = jnp.concatenate([w_out[:POOL_WIDTH], w_attn.reshape(ATTN_WIDTH, D)], axis=0).astype(bf16)
    wr_t = w_router.T
    wr_hi = wr_t.astype(bf16)
    wr_lo = (wr_t - wr_hi.astype(jnp.float32)).astype(bf16)
    wr2 = jnp.concatenate([wr_hi, wr_lo], axis=0)
    r = jnp.arange(ROUTE_PART)
    upper = (r[:, None] < r[None, :]).astype(bf16)
    full = lambda *shape: pl.BlockSpec(shape, lambda t: (0,) * len(shape))
    row = lambda w: pl.BlockSpec((tm, w), lambda t: (t, 0))
    col = pl.BlockSpec((TOP_K, tm), lambda t: (0, t))
    return pl.pallas_call(
        _outproj_kernel,
        grid=(T // tm,),
        in_specs=[row(POOL_WIDTH), row(ATTN_WIDTH), row(D), full(POOL_WIDTH, D),
                  full(ATTN_WIDTH, D), full(1, D), full(2 * N_EXPERTS, D), full(N_EXPERTS, D),
                  full(N_EXPERTS, 1), full(ROUTE_PART, ROUTE_PART)],
        out_specs=[row(D), row(D // 2), col, col, col, full(N_EXPERTS, 1)],
        out_shape=[
            jax.ShapeDtypeStruct((T, D), jnp.float32),
            jax.ShapeDtypeStruct((T, D // 2), jnp.uint32),
            jax.ShapeDtypeStruct((TOP_K, T), jnp.int32),
            jax.ShapeDtypeStruct((TOP_K, T), jnp.float32),
            jax.ShapeDtypeStruct((TOP_K, T), jnp.int32),
            jax.ShapeDtypeStruct((N_EXPERTS, 1), jnp.float32),
        ],
        scratch_shapes=[pltpu.VMEM((N_EXPERTS, 1), jnp.float32)],
        compiler_params=pltpu.CompilerParams(
            dimension_semantics=("arbitrary",), vmem_limit_bytes=VMEM_LIMIT),
        name="out_projection_router",
    )(ypool, yattn, x, wo[:POOL_WIDTH], wo[POOL_WIDTH:], g_ffn.reshape(1, D), wr2, wr_hi,
      b_router.reshape(N_EXPERTS, 1), upper)


def _sc_mesh():
    return plsc.VectorSubcoreMesh(core_axis_name="c", subcore_axis_name="s")


def _dispatch_rows(h2, dest, n_rows):
    T, D = h2.shape
    per_worker = T // SC_WORKERS
    steps = per_worker // SC_WINDOW

    @functools.partial(
        pl.kernel, mesh=_sc_mesh(),
        out_type=jax.ShapeDtypeStruct((n_rows, D), h2.dtype),
        scratch_types=[pltpu.VMEM((TOP_K, SC_WINDOW), jnp.int32),
                       pltpu.VMEM((SC_WINDOW, D), h2.dtype),
                       pltpu.SemaphoreType.DMA],
    )
    def scatter_kernel(h2_hbm, dest_hbm, xs_hbm, idx_v, rows_v, sem):
        wid = lax.axis_index("s") * SC_CORES + lax.axis_index("c")

        @pl.loop(0, steps)
        def _(step):
            base = wid * per_worker + step * SC_WINDOW
            pltpu.sync_copy(h2_hbm.at[pl.ds(base, SC_WINDOW)], rows_v)
            pltpu.sync_copy(dest_hbm.at[:, pl.ds(base, SC_WINDOW)], idx_v)
            copies = [pltpu.async_copy(rows_v, xs_hbm.at[idx_v.at[kk]], sem)
                      for kk in range(TOP_K)]
            for c in copies:
                c.wait()

    return scatter_kernel(h2, dest)


def _gather_rows(table, idx):
    N = idx.shape[0]
    D = table.shape[1]
    per_worker = N // SC_WORKERS
    steps = per_worker // SC_WINDOW

    @functools.partial(
        pl.kernel, mesh=_sc_mesh(),
        out_type=jax.ShapeDtypeStruct((N, D), table.dtype),
        scratch_types=[pltpu.VMEM((SC_WINDOW,), jnp.int32),
                       pltpu.VMEM((SC_WINDOW, D), table.dtype),
                       pltpu.SemaphoreType.DMA],
    )
    def gather_kernel(table_hbm, idx_hbm, out_hbm, idx_v, rows_v, sem):
        wid = lax.axis_index("s") * SC_CORES + lax.axis_index("c")

        @pl.loop(0, steps)
        def _(step):
            base = wid * per_worker + step * SC_WINDOW
            pltpu.sync_copy(idx_hbm.at[pl.ds(base, SC_WINDOW)], idx_v)
            pltpu.async_copy(table_hbm.at[idx_v], rows_v, sem).wait()
            pltpu.sync_copy(rows_v, out_hbm.at[pl.ds(base, SC_WINDOW)])

    return gather_kernel(table, idx)


def _expert_kernel(be_ref, nvalid_ref, x_ref, wup_ref, bup_ref, wdn_ref, bdn_ref, y_ref,
                   wup_bf, wdn_bf):
    i = pl.program_id(0)
    blk = x_ref.shape[0]
    changed = jnp.logical_or(i == 0, be_ref[i] != be_ref[jnp.maximum(i - 1, 0)])

    @pl.when(changed)
    def _():
        wup_bf[...] = wup_ref[0].astype(jnp.bfloat16)
        wdn_bf[...] = wdn_ref[0].astype(jnp.bfloat16)

    nvalid = nvalid_ref[i]

    @pl.when(nvalid > 0)
    def _():
        row = lax.broadcasted_iota(jnp.int32, (blk, 1), 0)
        x_hi, x_lo = _unpack_bf16_pairs(jnp.where(row < nvalid, x_ref[...], jnp.uint32(0)))
        x = jnp.concatenate([x_hi.astype(jnp.bfloat16), x_lo.astype(jnp.bfloat16)], axis=1)
        glu = _dot(x, wup_bf[:, :D_EXPERT]) + bup_ref[0, :, :D_EXPERT]
        lin = _dot(x, wup_bf[:, D_EXPERT:]) + bup_ref[0, :, D_EXPERT:]
        glu = jnp.minimum(glu, SWIGLU_LIMIT)
        lin = jnp.clip(lin, -SWIGLU_LIMIT, SWIGLU_LIMIT)
        act = glu * (1.0 / (1.0 + jnp.exp(-SWIGLU_ALPHA * glu))) * (lin + 1.0)
        y = _dot(act.astype(jnp.bfloat16), wdn_bf[...]) + bdn_ref[0]
        y_ref[...] = _pack_bf16_pairs(y)

    @pl.when(nvalid <= 0)
    def _():
        y_ref[...] = jnp.zeros_like(y_ref)


def _expert_ffn(xs, block_e, block_valid, w_up, b_up, w_down, b_down):
    n_rows = xs.shape[0]
    D = D_MODEL
    blk = EXPERT_ROWS
    grid_spec = pltpu.PrefetchScalarGridSpec(
        num_scalar_prefetch=2,
        grid=(n_rows // blk,),
        in_specs=[
            pl.BlockSpec((blk, D // 2), lambda i, be, nv: (i, 0)),
            pl.BlockSpec((1, D, 2 * D_EXPERT), lambda i, be, nv: (be[i], 0, 0)),
            pl.BlockSpec((1, 1, 2 * D_EXPERT), lambda i, be, nv: (be[i], 0, 0)),
            pl.BlockSpec((1, D_EXPERT, D), lambda i, be, nv: (be[i], 0, 0)),
            pl.BlockSpec((1, 1, D), lambda i, be, nv: (be[i], 0, 0)),
        ],
        out_specs=pl.BlockSpec((blk, D // 2), lambda i, be, nv: (i, 0)),
        scratch_shapes=[pltpu.VMEM((D, 2 * D_EXPERT), jnp.bfloat16),
                        pltpu.VMEM((D_EXPERT, D), jnp.bfloat16)],
    )
    return pl.pallas_call(
        _expert_kernel,
        grid_spec=grid_spec,
        out_shape=jax.ShapeDtypeStruct((n_rows, D // 2), jnp.uint32),
        compiler_params=pltpu.CompilerParams(
            dimension_semantics=("arbitrary",), vmem_limit_bytes=VMEM_LIMIT),
        name="expert_ffn",
    )(block_e, block_valid, xs, w_up, b_up.reshape(N_EXPERTS, 1, 2 * D_EXPERT), w_down,
      b_down.reshape(N_EXPERTS, 1, D))


def _final_kernel(x1_ref, rows_ref, gate_ref, g_ref, o_ref):
    half = x1_ref.shape[1] // 2
    acc_hi = x1_ref[:, :half]
    acc_lo = x1_ref[:, half:]
    for kk in range(TOP_K):
        y_hi, y_lo = _unpack_bf16_pairs(rows_ref[kk])
        gate = gate_ref[:, kk:kk + 1]
        acc_hi = acc_hi + y_hi * gate
        acc_lo = acc_lo + y_lo * gate
    o_ref[...] = _rms_norm(jnp.concatenate([acc_hi, acc_lo], axis=1), g_ref[...])


def _final_alias_kernel(x1_ref, rows_ref, gate_ref, g_ref, prev_ref, o_ref):
    del prev_ref
    _final_kernel(x1_ref, rows_ref, gate_ref, g_ref, o_ref)


def _final_norm(x1, rows, gates_t, g_final, part, prev_out):
    T, D = x1.shape
    Tp = rows.shape[1]
    tm = SEQ_TILE
    t0 = part * (Tp // tm)
    row = pl.BlockSpec((tm, D), lambda t: (t + t0, 0))
    in_specs = [row, pl.BlockSpec((TOP_K, tm, D // 2), lambda t: (0, t, 0)),
                pl.BlockSpec((tm, TOP_K), lambda t: (t + t0, 0)),
                pl.BlockSpec((1, D), lambda t: (0, 0))]
    args = [x1, rows, gates_t, g_final.reshape(1, D)]
    body, aliases = _final_kernel, {}
    if prev_out is not None:
        in_specs.append(pl.BlockSpec(memory_space=pl.ANY))
        args.append(prev_out)
        body, aliases = _final_alias_kernel, {len(args) - 1: 0}
    return pl.pallas_call(
        body,
        grid=(Tp // tm,),
        in_specs=in_specs,
        out_specs=row,
        out_shape=jax.ShapeDtypeStruct((T, D), jnp.float32),
        input_output_aliases=aliases,
        compiler_params=pltpu.CompilerParams(dimension_semantics=("arbitrary",)),
        name="combine_final_norm",
    )(*args)


def kernel(x, g_mix, w_in, b_forget, w_pool, pool_scale, w_out, g_ffn, w_router, b_router,
           w_up, b_up, w_down, b_down, g_final):
    B, S, D = x.shape
    T = B * S
    head_order = _decay_head_order(b_forget[0])
    ypool, qt, qaug, k, kaug, vt, stats = _in_projection(x, g_mix[0], w_in[0], b_forget[0],
                                                        w_pool[0], pool_scale[0], head_order)
    yattn = _attention(qt, qaug, k, kaug, vt, _first_needed_chunk(stats))
    x1, h2, top_idx, gates, rank, counts = _out_projection(
        ypool.reshape(T, POOL_WIDTH), yattn.reshape(T, ATTN_WIDTH), x.reshape(T, D),
        w_out[0], g_ffn[0], w_router[0], b_router[0], head_order)

    blk = EXPERT_ROWS
    counts = counts[:, 0].astype(jnp.int32)
    padded = (counts + blk - 1) // blk * blk
    pad_ends = jnp.cumsum(padded)
    pad_starts = pad_ends - padded
    dest = rank
    for e in range(N_EXPERTS):
        dest = dest + jnp.where(top_idx == e, pad_starts[e], 0)
    n_rows = T * TOP_K + N_EXPERTS * blk
    block_row0 = jnp.arange(n_rows // blk, dtype=jnp.int32) * blk
    block_e = jnp.sum((pad_ends[None, :] <= block_row0[:, None]).astype(jnp.int32), axis=1)
    block_e = jnp.minimum(block_e, N_EXPERTS - 1)
    of_block = block_e[:, None] == jnp.arange(N_EXPERTS, dtype=jnp.int32)[None, :]
    block_end = jnp.sum(jnp.where(of_block, (pad_starts + counts)[None, :], 0), axis=1)
    block_valid = jnp.clip(block_end - block_row0, 0, blk)
    block_valid = jnp.where(block_row0 < pad_ends[-1], block_valid, 0).astype(jnp.int32)

    xs = _dispatch_rows(h2, dest, n_rows)
    ys = _expert_ffn(xs, block_e, block_valid, w_up[0], b_up[0], w_down[0], b_down[0])
    gates_t = gates.T
    tp = T // COMBINE_PARTS
    out = None
    for part in range(COMBINE_PARTS):
        part_dest = dest[:, part * tp:(part + 1) * tp].reshape(-1)
        rows = _gather_rows(ys, part_dest).reshape(TOP_K, tp, D // 2)
        out = _final_norm(x1, rows, gates_t, g_final, part, out)
    return out.reshape(B, S, D)
```

```python
import functools

import jax
import jax.numpy as jnp
import numpy as np
from jax import lax
from jax.experimental import pallas as pl
from jax.experimental.pallas import tpu as pltpu
from jax.experimental.pallas import tpu_sc as plsc

D_MODEL = 1024
POOL_WIDTH = 512
POOL_WINDOWS = (2, 4, 8, 16)
POOL_GROUP = 128
MAX_WINDOW = 16
ATTN_WIDTH = 512
HEAD_DIM = 64
N_HEADS = 8
N_EXPERTS = 32
TOP_K = 4
D_EXPERT = 1024
SWIGLU_LIMIT = 7.0
SWIGLU_ALPHA = 1.702
RMS_EPS = 1e-5

LANES = 128
SUBLANES = 8
AUG_GROUP = 16
N_SPLIT = 3
MASK_VALUE = -1e30
LOG2_E = 1.4426950408889634
SKIP_MARGIN = 160.0
NORM_SLACK = 1.02
V_ROWS = 80

SEQ_TILE = 512
Q_TILE = 512
HEADS_PER_STEP = 4
LOGITS_AHEAD = 2
KV_TILE = SEQ_TILE
ROUTE_TILE = 1024
ROUTE_PART = 512
EXPERT_ROWS = 512
COMBINE_PARTS = 8
SC_CORES = 2
SC_SUBCORES = 16
SC_WORKERS = SC_CORES * SC_SUBCORES
SC_WINDOW = 128
VMEM_LIMIT = 56 * 1024 * 1024

_NT = (((1,), (1,)), ((), ()))


def _dot(a, b):
    return jnp.dot(a, b, preferred_element_type=jnp.float32)


def _dot_nt(a, b):
    return lax.dot_general(a, b, _NT, preferred_element_type=jnp.float32)


def _rms_norm(x, g):
    return x * lax.rsqrt(jnp.mean(x * x, axis=-1, keepdims=True) + RMS_EPS) * g


def _pack_bf16_pairs(x):
    n = x.shape[1] // 2
    bits = lax.bitcast_convert_type(x.astype(jnp.bfloat16).astype(jnp.float32), jnp.uint32)
    return bits[:, :n] | (bits[:, n:] >> 16)


def _unpack_bf16_pairs(p):
    hi = lax.bitcast_convert_type(p & jnp.uint32(0xFFFF0000), jnp.float32)
    lo = lax.bitcast_convert_type(p << 16, jnp.float32)
    return hi, lo


def _split3(x):
    hi = x.astype(jnp.bfloat16)
    r1 = x - hi.astype(jnp.float32)
    mid = r1.astype(jnp.bfloat16)
    lo = (r1 - mid.astype(jnp.float32)).astype(jnp.bfloat16)
    return hi, mid, lo


def _inproj_kernel(x_ref, g_ref, wu_ref, wqt_ref, wk_ref, wvt_ref, wf_ref, bf_ref, tri_ref,
                   pqt_ref, pk_ref, oneq_ref, onek_ref, hsel_ref, wpool_ref, pscale_ref,
                   ypool_ref, qt_ref, qaug_ref, k_ref, kaug_ref, vt_ref, stats_ref, ubuf, ccarry):
    s = pl.program_id(1)
    tm = x_ref.shape[1]

    @pl.when(s == 0)
    def _():
        ubuf[...] = jnp.zeros_like(ubuf)
        ccarry[...] = jnp.zeros_like(ccarry)

    x = x_ref[0]
    h = _rms_norm(x, g_ref[...]).astype(jnp.bfloat16)
    fl = _dot(h, wf_ref[...]) + bf_ref[...]
    u = _dot(h, wu_ref[...])
    kp = _dot(h, wk_ref[...])
    qt = _dot_nt(wqt_ref[...], h)
    vt = _dot_nt(wvt_ref[...], h).astype(jnp.bfloat16)
    k_ref[0] = kp.astype(jnp.bfloat16)
    qt_ref[0] = qt.astype(jnp.bfloat16)

    logf = jnp.minimum(fl, 0.0) - jnp.log1p(jnp.exp(-jnp.abs(fl)))
    lane = lax.broadcasted_iota(jnp.int32, (tm, LANES), 1)
    logf = jnp.where(lane < N_SPLIT * N_HEADS, logf, 0.0)
    f_hi, f_mid, f_lo = _split3(logf)
    tri = tri_ref[...]
    c = _dot(tri, f_hi) + _dot(tri, f_mid) + _dot(tri, f_lo) + ccarry[...]
    ccarry[...] = c[tm - 1:tm, :]
    c2 = c * LOG2_E
    c_hi, c_mid, c_lo = _split3(c2)
    term = lane % N_SPLIT
    c3 = jnp.where(term == 0, c_hi, jnp.where(term == 1, c_mid, c_lo))
    kaug_ref[0] = (_dot(c3, pk_ref[...]) + onek_ref[...]).astype(jnp.bfloat16)
    qaug_ref[0] = (_dot_nt(pqt_ref[...], c3) + oneq_ref[...]).astype(jnp.bfloat16)
    lane_row = lax.broadcasted_iota(jnp.int32, (1, LANES), 1)
    kn2 = jnp.max(_dot((kp * kp).astype(jnp.bfloat16), hsel_ref[...]), axis=0, keepdims=True)
    qn2 = jnp.zeros((1, LANES), jnp.float32)
    for hd in range(N_HEADS):
        qs = qt[hd * HEAD_DIM:(hd + 1) * HEAD_DIM, :]
        qmax = jnp.max(jnp.sum(qs * qs, axis=0, keepdims=True), axis=1, keepdims=True)
        qn2 = jnp.where(lane_row == hd, qmax, qn2)
    stats_ref[0, 0, 0:1, :] = kn2
    stats_ref[0, 0, 1:2, :] = qn2
    stats_ref[0, 0, 2:3, :] = c2[0:1, :]
    stats_ref[0, 0, 3:4, :] = c2[tm - 1:tm, :]
    stats_ref[0, 0, 4:SUBLANES, :] = jnp.zeros((SUBLANES - 4, LANES), jnp.float32)
    ones = jnp.ones((V_ROWS - HEAD_DIM, tm), jnp.bfloat16)
    for hd in range(N_HEADS):
        vt_ref[0, 0, hd * V_ROWS:hd * V_ROWS + HEAD_DIM, :] = vt[hd * HEAD_DIM:(hd + 1) * HEAD_DIM]
        vt_ref[0, 0, hd * V_ROWS + HEAD_DIM:(hd + 1) * V_ROWS, :] = ones

    head_pos = s * tm + lax.broadcasted_iota(jnp.int32, (MAX_WINDOW, 1), 0)
    for gi, w in enumerate(POOL_WINDOWS):
        cols = slice(gi * POOL_GROUP, (gi + 1) * POOL_GROUP)
        ug = u[:, cols]
        win = jnp.concatenate([ubuf[:, cols], ug], axis=0)
        span = 1
        while span < w:
            win = win + pltpu.roll(win, span, axis=0)
            span *= 2
        acc = win[MAX_WINDOW:]
        inv_head = 1.0 / jnp.minimum(head_pos + 1, w).astype(jnp.float32)
        p = jnp.concatenate([acc[:MAX_WINDOW] * inv_head, acc[MAX_WINDOW:] * (1.0 / w)],
                            axis=0) - ug
        y = _dot(p.astype(jnp.bfloat16), wpool_ref[gi]) * pscale_ref[:, cols]
        ypool_ref[0, :, cols] = y.astype(jnp.bfloat16)
    ubuf[...] = u[tm - MAX_WINDOW:]


def _in_projection(x, g_mix, w_in, b_forget, w_pool, pool_scale, head_order):
    B, S, D = x.shape
    tm = SEQ_TILE
    bf16 = jnp.bfloat16
    o = POOL_WIDTH
    w_u = w_in[:, :o].astype(bf16)
    w_q = w_in[:, o:o + ATTN_WIDTH] * (HEAD_DIM ** -0.5 * LOG2_E)
    w_k = w_in[:, o + ATTN_WIDTH:o + 2 * ATTN_WIDTH]
    w_v = w_in[:, o + 2 * ATTN_WIDTH:o + 3 * ATTN_WIDTH]
    w_f = w_in[:, o + 3 * ATTN_WIDTH:]
    by_head = lambda w: jnp.take(w.reshape(D, N_HEADS, HEAD_DIM), head_order,
                                 axis=1).reshape(D, ATTN_WIDTH)
    w_q, w_k, w_v = by_head(w_q), by_head(w_k), by_head(w_v)
    w_f = jnp.take(w_f, head_order, axis=1)
    b_forget = jnp.take(b_forget, head_order)

    wqt = w_q.T.astype(bf16)
    wk = w_k.astype(bf16)
    wvt = w_v.T.astype(bf16)
    n_gate = N_SPLIT * N_HEADS
    wf = jnp.pad(jnp.repeat(w_f, N_SPLIT, axis=1), ((0, 0), (0, LANES - n_gate))).astype(bf16)
    bfp = jnp.pad(jnp.repeat(b_forget, N_SPLIT), (0, LANES - n_gate)).reshape(1, LANES)

    assert N_HEADS * AUG_GROUP == LANES and 2 * N_SPLIT <= AUG_GROUP
    src = np.arange(LANES)
    head, term = src // N_SPLIT, src % N_SPLIT
    dst = np.arange(LANES)
    valid = (src < n_gate)[:, None]
    q_hit = valid & (dst[None, :] == (head * AUG_GROUP + term)[:, None])
    k_hit = valid & (dst[None, :] == (head * AUG_GROUP + N_SPLIT + term)[:, None])
    pqt = jnp.asarray(q_hit.T.astype(np.float32), bf16)
    pk = jnp.asarray(-k_hit.astype(np.float32), bf16)
    slot_pos = dst % AUG_GROUP
    oneq = ((slot_pos >= N_SPLIT) & (slot_pos < 2 * N_SPLIT))
    onek = slot_pos < N_SPLIT
    oneq = jnp.asarray(oneq.astype(np.float32).reshape(-1, 1))
    onek = jnp.asarray(onek.astype(np.float32).reshape(1, -1))
    r = np.arange(tm)
    tri = jnp.asarray((r[None, :] <= r[:, None]).astype(np.float32), bf16)
    hsel = jnp.asarray((np.arange(ATTN_WIDTH)[:, None] // HEAD_DIM
                        == np.arange(LANES)[None, :]).astype(np.float32), bf16)

    full = lambda *shape: pl.BlockSpec(shape, lambda b, s: (0,) * len(shape))
    aw = ATTN_WIDTH
    return pl.pallas_call(
        _inproj_kernel,
        grid=(B, S // tm),
        in_specs=[
            pl.BlockSpec((1, tm, D), lambda b, s: (b, s, 0)),
            full(1, D), full(D, o), full(aw, D), full(D, aw), full(aw, D),
            full(D, LANES), full(1, LANES), full(tm, tm), full(LANES, LANES), full(LANES, LANES),
            full(LANES, 1), full(1, LANES), full(aw, LANES),
            full(len(POOL_WINDOWS), POOL_GROUP, POOL_GROUP),
            full(1, o),
        ],
        out_specs=[
            pl.BlockSpec((1, tm, o), lambda b, s: (b, s, 0)),
            pl.BlockSpec((1, aw, tm), lambda b, s: (b, 0, s)),
            pl.BlockSpec((1, LANES, tm), lambda b, s: (b, 0, s)),
            pl.BlockSpec((1, tm, aw), lambda b, s: (b, s, 0)),
            pl.BlockSpec((1, tm, LANES), lambda b, s: (b, s, 0)),
            pl.BlockSpec((1, 1, N_HEADS * V_ROWS, tm), lambda b, s: (b, s, 0, 0)),
            pl.BlockSpec((1, 1, SUBLANES, LANES), lambda b, s: (b, s, 0, 0)),
        ],
        out_shape=[
            jax.ShapeDtypeStruct((B, S, o), bf16),
            jax.ShapeDtypeStruct((B, aw, S), bf16),
            jax.ShapeDtypeStruct((B, LANES, S), bf16),
            jax.ShapeDtypeStruct((B, S, aw), bf16),
            jax.ShapeDtypeStruct((B, S, LANES), bf16),
            jax.ShapeDtypeStruct((B, S // tm, N_HEADS * V_ROWS, tm), bf16),
            jax.ShapeDtypeStruct((B, S // tm, SUBLANES, LANES), jnp.float32),
        ],
        scratch_shapes=[
            pltpu.VMEM((MAX_WINDOW, o), jnp.float32),
            pltpu.VMEM((1, LANES), jnp.float32),
        ],
        compiler_params=pltpu.CompilerParams(
            dimension_semantics=("arbitrary", "arbitrary"), vmem_limit_bytes=VMEM_LIMIT),
        name="in_projection",
    )(x, g_mix.reshape(1, D), w_u, wqt, wk, wvt, wf, bfp, tri, pqt, pk, oneq, onek, hsel,
      w_pool.astype(bf16), pool_scale.reshape(1, o))


def _attn_kernel(j0_ref, qt_ref, qaug_ref, k_ref, kaug_ref, vt_ref, causal_ref, o_ref,
                 qop_ref, st_buf, mx_buf, m_ref, acc_ref):
    i = pl.program_id(2)
    tile = (pl.program_id(0) * pl.num_programs(1) + pl.program_id(1)) * pl.num_programs(2) + i
    j0 = j0_ref[tile]
    tq = qt_ref.shape[2]
    tk = vt_ref.shape[3]
    heads = qt_ref.shape[1] // HEAD_DIM

    m_ref[...] = jnp.full(m_ref.shape, MASK_VALUE, jnp.float32)
    acc_ref[...] = jnp.zeros_like(acc_ref)

    first_head = pl.program_id(1) * heads
    group = lax.broadcasted_iota(jnp.int32, (LANES, tq), 0) // AUG_GROUP
    q_aug = qaug_ref[0]
    blank = jnp.zeros((HEAD_DIM, tq), q_aug.dtype)
    for hh in range(heads):
        q_own = qt_ref[0, hh * HEAD_DIM:(hh + 1) * HEAD_DIM, :]
        pair_rows = [q_own, blank] if hh % 2 == 0 else [blank, q_own]
        qop_ref[hh, 0:2 * HEAD_DIM, :] = jnp.concatenate(pair_rows, axis=0)
        qop_ref[hh, 2 * HEAD_DIM:, :] = jnp.where(group == first_head + hh, q_aug,
                                                  jnp.zeros_like(q_aug))

    def logits(j, slot, hh, masked):
        row0 = pl.multiple_of(j * tk, tk)
        pair_lanes = slice(hh // 2 * 2 * HEAD_DIM, (hh // 2 + 1) * 2 * HEAD_DIM)
        k = jnp.concatenate([k_ref[0, pl.ds(row0, tk), pair_lanes],
                             kaug_ref[0, pl.ds(row0, tk), :]], axis=1)
        st = _dot(k, qop_ref[hh])
        if masked:
            st = st + causal_ref[...]
        st_buf[slot, hh] = st
        mx_buf[slot, hh] = jnp.max(st, axis=0, keepdims=True)

    def accumulate(j, slot, hh):
        st = st_buf[slot, hh]
        m_prev = m_ref[hh]
        m_new = jnp.maximum(m_prev, mx_buf[slot, hh])
        alpha = jnp.exp2(m_prev - m_new)
        p = jnp.exp2(st - m_new)
        vt = vt_ref[0, j, hh * V_ROWS:(hh + 1) * V_ROWS, :]
        acc_ref[hh] = alpha * acc_ref[hh] + _dot(vt, p.astype(jnp.bfloat16))
        m_ref[hh] = m_new

    ahead = LOGITS_AHEAD
    assert 1 <= ahead <= heads
    for hh in range(heads):
        logits(i, 0, hh, True)
        if hh >= ahead:
            accumulate(i, 0, hh - ahead)

    def before(t):
        return jnp.where(t == j0, i, t - 1)

    def step(t, slot):
        for hh in range(heads):
            logits(t, slot, hh, False)
            if hh >= ahead:
                accumulate(t, slot, hh - ahead)
            else:
                accumulate(before(t), 1 - slot, hh - ahead + heads)

    def drain(t, slot):
        for hh in range(heads - ahead, heads):
            accumulate(t, slot, hh)

    def body(u, carry):
        step(j0 + 2 * u, 1)
        step(j0 + 2 * u + 1, 0)
        return carry

    n_chunks = i - j0
    lax.fori_loop(0, n_chunks // 2, body, 0)

    @pl.when(n_chunks % 2 == 1)
    def _():
        step(i - 1, 1)
        drain(i - 1, 1)

    @pl.when(n_chunks % 2 == 0)
    def _():
        drain(before(i), 0)

    outs = [acc_ref[hh, :HEAD_DIM] / acc_ref[hh, HEAD_DIM:HEAD_DIM + 1] for hh in range(heads)]
    ot = jnp.concatenate(outs, axis=0)
    o_ref[0] = ot.T.astype(o_ref.dtype)


def _decay_head_order(b_forget):
    return jnp.argsort(b_forget).astype(jnp.int32)


def _first_needed_chunk(stats):
    kn = jnp.sqrt(stats[:, :, 0, :N_HEADS])
    qn = jnp.sqrt(stats[:, :, 1, :N_HEADS])
    c_first = stats[:, :, 2, 0:N_SPLIT * N_HEADS:N_SPLIT]
    c_last = stats[:, :, 3, 0:N_SPLIT * N_HEADS:N_SPLIT]
    n = stats.shape[1]
    dot_bound = NORM_SLACK * qn[:, :, None, :] * kn[:, None, :, :] + 1.0
    upper = dot_bound + c_first[:, :, None, :] - c_last[:, None, :, :]
    self_bound = NORM_SLACK * qn * kn + 1.0
    ii = jnp.arange(n, dtype=jnp.int32)[None, :, None, None]
    jj = jnp.arange(n, dtype=jnp.int32)[None, None, :, None]
    needed = (upper + self_bound[:, :, None, :] >= -SKIP_MARGIN) & (jj < ii)
    first = jnp.min(jnp.where(needed, jj, ii), axis=2)
    g = HEADS_PER_STEP
    first = jnp.min(first.reshape(first.shape[0], n, N_HEADS // g, g), axis=3)
    return jnp.transpose(first, (0, 2, 1)).reshape(-1).astype(jnp.int32)


def _attention(qt, qaug, k, kaug, vt, first_chunk):
    B, _, S = qt.shape
    tq, tk = Q_TILE, KV_TILE
    assert tq == tk, "the diagonal is handled as a single masked key chunk"
    pair = HEADS_PER_STEP
    n_pairs = N_HEADS // pair
    pw = pair * HEAD_DIM
    causal = np.where(np.arange(tk)[:, None] <= np.arange(tq)[None, :], 0.0, MASK_VALUE)
    grid_spec = pltpu.PrefetchScalarGridSpec(
        num_scalar_prefetch=1,
        grid=(B, n_pairs, S // tq),
        in_specs=[
            pl.BlockSpec((1, pw, tq), lambda b, p, i, j0: (b, p, i)),
            pl.BlockSpec((1, LANES, tq), lambda b, p, i, j0: (b, 0, i)),
            pl.BlockSpec((1, S, pw), lambda b, p, i, j0: (b, 0, p)),
            pl.BlockSpec((1, S, LANES), lambda b, p, i, j0: (b, 0, 0)),
            pl.BlockSpec((1, S // tk, pair * V_ROWS, tk), lambda b, p, i, j0: (b, 0, p, 0)),
            pl.BlockSpec((tk, tq), lambda b, p, i, j0: (0, 0)),
        ],
        out_specs=pl.BlockSpec((1, tq, pw), lambda b, p, i, j0: (b, i, p)),
        scratch_shapes=[
            pltpu.VMEM((pair, 2 * HEAD_DIM + LANES, tq), jnp.bfloat16),
            pltpu.VMEM((2, pair, tk, tq), jnp.float32),
            pltpu.VMEM((2, pair, 1, tq), jnp.float32),
            pltpu.VMEM((pair, 1, tq), jnp.float32),
            pltpu.VMEM((pair, V_ROWS, tq), jnp.float32),
        ],
    )
    return pl.pallas_call(
        _attn_kernel,
        grid_spec=grid_spec,
        out_shape=jax.ShapeDtypeStruct((B, S, ATTN_WIDTH), jnp.bfloat16),
        compiler_params=pltpu.CompilerParams(
            dimension_semantics=("arbitrary", "arbitrary", "arbitrary"),
            vmem_limit_bytes=VMEM_LIMIT),
        name="forgetting_attention",
    )(first_chunk, qt, qaug, k, kaug, vt, jnp.asarray(causal, jnp.float32))


def _outproj_kernel(yp_ref, ya_ref, x_ref, wo1_ref, wo2_ref, g_ref, wr2_ref, wrhi_ref, br_ref,
                    upper_ref, x1_ref, h2_ref, idx_ref, gate_ref, rank_ref, cnt_ref, carry):
    t = pl.program_id(0)
    tm = x_ref.shape[0]

    @pl.when(t == 0)
    def _():
        carry[...] = jnp.zeros_like(carry)

    hm = upper_ref.shape[0]
    parts = [slice(r * hm, (r + 1) * hm) for r in range(tm // hm)]

    def project(rows):
        x1 = x_ref[rows] + _dot(yp_ref[rows], wo1_ref[...]) + _dot(ya_ref[rows], wo2_ref[...])
        x1_ref[rows] = x1
        return x1

    def route(rows, x1):
        h2 = _rms_norm(x1, g_ref[...])
        h2_hi = h2.astype(jnp.bfloat16)
        h2_lo = (h2 - h2_hi.astype(jnp.float32)).astype(jnp.bfloat16)
        h2_ref[rows] = _pack_bf16_pairs(h2)

        lg2 = _dot_nt(wr2_ref[...], h2_hi)
        logits = (lg2[:N_EXPERTS] + lg2[N_EXPERTS:] + _dot_nt(wrhi_ref[...], h2_lo)
                  + br_ref[...])

        eio = lax.broadcasted_iota(jnp.int32, (N_EXPERTS, hm), 0)
        vals, idxs = [], []
        cur = logits
        for _ in range(TOP_K):
            m = jnp.max(cur, axis=0, keepdims=True)
            ix = jnp.min(jnp.where(cur == m, eio, N_EXPERTS), axis=0, keepdims=True)
            vals.append(m)
            idxs.append(ix)
            cur = jnp.where(eio == ix, -jnp.inf, cur)
        exps = [jnp.exp(v - vals[0]) for v in vals]
        denom = exps[0] + exps[1] + exps[2] + exps[3]
        hits = [eio == ix for ix in idxs]
        cnt = jnp.zeros((N_EXPERTS, hm), jnp.float32)
        for hit in hits:
            cnt = cnt + hit.astype(jnp.float32)
        base = _dot(cnt.astype(jnp.bfloat16), upper_ref[...]) + carry[...]
        for kk in range(TOP_K):
            idx_ref[kk:kk + 1, rows] = idxs[kk]
            gate_ref[kk:kk + 1, rows] = exps[kk] / denom
            rank = jnp.sum(jnp.where(hits[kk], base, 0.0), axis=0, keepdims=True)
            rank_ref[kk:kk + 1, rows] = rank.astype(jnp.int32)
        carry[...] = carry[...] + jnp.sum(cnt, axis=1, keepdims=True)

    pending = project(parts[0])
    for r, rows in enumerate(parts):
        ahead = project(parts[r + 1]) if r + 1 < len(parts) else None
        route(rows, pending)
        pending = ahead
    cnt_ref[...] = carry[...]


def _out_projection(ypool, yattn, x, w_out, g_ffn, w_router, b_router, head_order):
    T, D = x.shape
    tm = ROUTE_TILE
    bf16 = jnp.bfloat16
    w_attn = jnp.take(w_out[POOL_WIDTH:].reshape(N_HEADS, HEAD_DIM, D), head_order, axis=0)
    wo = jnp.concatenate([w_out[:POOL_WIDTH], w_attn.reshape(ATTN_WIDTH, D)], axis=0).astype(bf16)
    wr_t = w_router.T
    wr_hi = wr_t.astype(bf16)
    wr_lo = (wr_t - wr_hi.astype(jnp.float32)).astype(bf16)
    wr2 = jnp.concatenate([wr_hi, wr_lo], axis=0)
    r = np.arange(ROUTE_PART)
    upper = jnp.asarray((r[:, None] < r[None, :]).astype(np.float32), bf16)
    full = lambda *shape: pl.BlockSpec(shape, lambda t: (0,) * len(shape))
    row = lambda w: pl.BlockSpec((tm, w), lambda t: (t, 0))
    col = pl.BlockSpec((TOP_K, tm), lambda t: (0, t))
    return pl.pallas_call(
        _outproj_kernel,
        grid=(T // tm,),
        in_specs=[row(POOL_WIDTH), row(ATTN_WIDTH), row(D), full(POOL_WIDTH, D),
                  full(ATTN_WIDTH, D), full(1, D), full(2 * N_EXPERTS, D), full(N_EXPERTS, D),
                  full(N_EXPERTS, 1), full(ROUTE_PART, ROUTE_PART)],
        out_specs=[row(D), row(D // 2), col, col, col, full(N_EXPERTS, 1)],
        out_shape=[
            jax.ShapeDtypeStruct((T, D), jnp.float32),
            jax.ShapeDtypeStruct((T, D // 2), jnp.uint32),
            jax.ShapeDtypeStruct((TOP_K, T), jnp.int32),
            jax.ShapeDtypeStruct((TOP_K, T), jnp.float32),
            jax.ShapeDtypeStruct((TOP_K, T), jnp.int32),
            jax.ShapeDtypeStruct((N_EXPERTS, 1), jnp.float32),
        ],
        scratch_shapes=[pltpu.VMEM((N_EXPERTS, 1), jnp.float32)],
        compiler_params=pltpu.CompilerParams(
            dimension_semantics=("arbitrary",), vmem_limit_bytes=VMEM_LIMIT),
        name="out_projection_router",
    )(ypool, yattn, x, wo[:POOL_WIDTH], wo[POOL_WIDTH:], g_ffn.reshape(1, D), wr2, wr_hi,
      b_router.reshape(N_EXPERTS, 1), upper)


def _sc_mesh():
    return plsc.VectorSubcoreMesh(core_axis_name="c", subcore_axis_name="s")


def _dispatch_rows(h2, dest, n_rows):
    T, D = h2.shape
    per_worker = T // SC_WORKERS
    steps = per_worker // SC_WINDOW

    @functools.partial(
        pl.kernel, mesh=_sc_mesh(),
        out_type=jax.ShapeDtypeStruct((n_rows, D), h2.dtype),
        scratch_types=[pltpu.VMEM((TOP_K, SC_WINDOW), jnp.int32),
                       pltpu.VMEM((SC_WINDOW, D), h2.dtype),
                       pltpu.SemaphoreType.DMA],
    )
    def scatter_kernel(h2_hbm, dest_hbm, xs_hbm, idx_v, rows_v, sem):
        wid = lax.axis_index("s") * SC_CORES + lax.axis_index("c")

        @pl.loop(0, steps)
        def _(step):
            base = wid * per_worker + step * SC_WINDOW
            pltpu.sync_copy(h2_hbm.at[pl.ds(base, SC_WINDOW)], rows_v)
            pltpu.sync_copy(dest_hbm.at[:, pl.ds(base, SC_WINDOW)], idx_v)
            copies = [pltpu.async_copy(rows_v, xs_hbm.at[idx_v.at[kk]], sem)
                      for kk in range(TOP_K)]
            for c in copies:
                c.wait()

    return scatter_kernel(h2, dest)


def _gather_rows(table, idx):
    N = idx.shape[0]
    D = table.shape[1]
    per_worker = N // SC_WORKERS
    steps = per_worker // SC_WINDOW

    @functools.partial(
        pl.kernel, mesh=_sc_mesh(),
        out_type=jax.ShapeDtypeStruct((N, D), table.dtype),
        scratch_types=[pltpu.VMEM((SC_WINDOW,), jnp.int32),
                       pltpu.VMEM((SC_WINDOW, D), table.dtype),
                       pltpu.SemaphoreType.DMA],
    )
    def gather_kernel(table_hbm, idx_hbm, out_hbm, idx_v, rows_v, sem):
        wid = lax.axis_index("s") * SC_CORES + lax.axis_index("c")

        @pl.loop(0, steps)
        def _(step):
            base = wid * per_worker + step * SC_WINDOW
            pltpu.sync_copy(idx_hbm.at[pl.ds(base, SC_WINDOW)], idx_v)
            pltpu.async_copy(table_hbm.at[idx_v], rows_v, sem).wait()
            pltpu.sync_copy(rows_v, out_hbm.at[pl.ds(base, SC_WINDOW)])

    return gather_kernel(table, idx)


def _expert_kernel(be_ref, nvalid_ref, x_ref, wup_ref, bup_ref, wdn_ref, bdn_ref, y_ref,
                   wup_bf, wdn_bf):
    i = pl.program_id(0)
    blk = x_ref.shape[0]
    changed = jnp.logical_or(i == 0, be_ref[i] != be_ref[jnp.maximum(i - 1, 0)])

    @pl.when(changed)
    def _():
        wup_bf[...] = wup_ref[0].astype(jnp.bfloat16)
        wdn_bf[...] = wdn_ref[0].astype(jnp.bfloat16)

    nvalid = nvalid_ref[i]

    @pl.when(nvalid > 0)
    def _():
        row = lax.broadcasted_iota(jnp.int32, (blk, 1), 0)
        x_hi, x_lo = _unpack_bf16_pairs(jnp.where(row < nvalid, x_ref[...], jnp.uint32(0)))
        x = jnp.concatenate([x_hi.astype(jnp.bfloat16), x_lo.astype(jnp.bfloat16)], axis=1)
        glu = _dot(x, wup_bf[:, :D_EXPERT]) + bup_ref[0, :, :D_EXPERT]
        lin = _dot(x, wup_bf[:, D_EXPERT:]) + bup_ref[0, :, D_EXPERT:]
        glu = jnp.minimum(glu, SWIGLU_LIMIT)
        lin = jnp.clip(lin, -SWIGLU_LIMIT, SWIGLU_LIMIT)
        act = glu * (1.0 / (1.0 + jnp.exp(-SWIGLU_ALPHA * glu))) * (lin + 1.0)
        y = _dot(act.astype(jnp.bfloat16), wdn_bf[...]) + bdn_ref[0]
        y_ref[...] = _pack_bf16_pairs(y)

    @pl.when(nvalid <= 0)
    def _():
        y_ref[...] = jnp.zeros_like(y_ref)


def _expert_ffn(xs, block_e, block_valid, w_up, b_up, w_down, b_down):
    n_rows = xs.shape[0]
    D = D_MODEL
    blk = EXPERT_ROWS
    grid_spec = pltpu.PrefetchScalarGridSpec(
        num_scalar_prefetch=2,
        grid=(n_rows // blk,),
        in_specs=[
            pl.BlockSpec((blk, D // 2), lambda i, be, nv: (i, 0)),
            pl.BlockSpec((1, D, 2 * D_EXPERT), lambda i, be, nv: (be[i], 0, 0)),
            pl.BlockSpec((1, 1, 2 * D_EXPERT), lambda i, be, nv: (be[i], 0, 0)),
            pl.BlockSpec((1, D_EXPERT, D), lambda i, be, nv: (be[i], 0, 0)),
            pl.BlockSpec((1, 1, D), lambda i, be, nv: (be[i], 0, 0)),
        ],
        out_specs=pl.BlockSpec((blk, D // 2), lambda i, be, nv: (i, 0)),
        scratch_shapes=[pltpu.VMEM((D, 2 * D_EXPERT), jnp.bfloat16),
                        pltpu.VMEM((D_EXPERT, D), jnp.bfloat16)],
    )
    return pl.pallas_call(
        _expert_kernel,
        grid_spec=grid_spec,
        out_shape=jax.ShapeDtypeStruct((n_rows, D // 2), jnp.uint32),
        compiler_params=pltpu.CompilerParams(
            dimension_semantics=("arbitrary",), vmem_limit_bytes=VMEM_LIMIT),
        name="expert_ffn",
    )(block_e, block_valid, xs, w_up, b_up.reshape(N_EXPERTS, 1, 2 * D_EXPERT), w_down,
      b_down.reshape(N_EXPERTS, 1, D))


def _final_kernel(x1_ref, rows_ref, gate_ref, g_ref, o_ref):
    half = x1_ref.shape[1] // 2
    acc_hi = x1_ref[:, :half]
    acc_lo = x1_ref[:, half:]
    for kk in range(TOP_K):
        y_hi, y_lo = _unpack_bf16_pairs(rows_ref[kk])
        gate = gate_ref[:, kk:kk + 1]
        acc_hi = acc_hi + y_hi * gate
        acc_lo = acc_lo + y_lo * gate
    o_ref[...] = _rms_norm(jnp.concatenate([acc_hi, acc_lo], axis=1), g_ref[...])


def _final_alias_kernel(x1_ref, rows_ref, gate_ref, g_ref, prev_ref, o_ref):
    del prev_ref
    _final_kernel(x1_ref, rows_ref, gate_ref, g_ref, o_ref)


def _final_norm(x1, rows, gates_t, g_final, part, prev_out):
    T, D = x1.shape
    Tp = rows.shape[1]
    tm = SEQ_TILE
    t0 = part * (Tp // tm)
    row = pl.BlockSpec((tm, D), lambda t: (t + t0, 0))
    in_specs = [row, pl.BlockSpec((TOP_K, tm, D // 2), lambda t: (0, t, 0)),
                pl.BlockSpec((tm, TOP_K), lambda t: (t + t0, 0)),
                pl.BlockSpec((1, D), lambda t: (0, 0))]
    args = [x1, rows, gates_t, g_final.reshape(1, D)]
    body, aliases = _final_kernel, {}
    if prev_out is not None:
        in_specs.append(pl.BlockSpec(memory_space=pl.ANY))
        args.append(prev_out)
        body, aliases = _final_alias_kernel, {len(args) - 1: 0}
    return pl.pallas_call(
        body,
        grid=(Tp // tm,),
        in_specs=in_specs,
        out_specs=row,
        out_shape=jax.ShapeDtypeStruct((T, D), jnp.float32),
        input_output_aliases=aliases,
        compiler_params=pltpu.CompilerParams(dimension_semantics=("arbitrary",)),
        name="combine_final_norm",
    )(*args)


def kernel(x, g_mix, w_in, b_forget, w_pool, pool_scale, w_out, g_ffn, w_router, b_router,
           w_up, b_up, w_down, b_down, g_final):
    B, S, D = x.shape
    T = B * S
    head_order = _decay_head_order(b_forget[0])
    ypool, qt, qaug, k, kaug, vt, stats = _in_projection(x, g_mix[0], w_in[0], b_forget[0],
                                                        w_pool[0], pool_scale[0], head_order)
    yattn = _attention(qt, qaug, k, kaug, vt, _first_needed_chunk(stats))
    x1, h2, top_idx, gates, rank, counts = _out_projection(
        ypool.reshape(T, POOL_WIDTH), yattn.reshape(T, ATTN_WIDTH), x.reshape(T, D),
        w_out[0], g_ffn[0], w_router[0], b_router[0], head_order)

    blk = EXPERT_ROWS
    counts = counts[:, 0].astype(jnp.int32)
    padded = (counts + blk - 1) // blk * blk
    pad_ends = jnp.cumsum(padded)
    pad_starts = pad_ends - padded
    dest = rank
    for e in range(N_EXPERTS):
        dest = dest + jnp.where(top_idx == e, pad_starts[e], 0)
    n_rows = T * TOP_K + N_EXPERTS * blk
    block_row0 = jnp.arange(n_rows // blk, dtype=jnp.int32) * blk
    block_e = jnp.sum((pad_ends[None, :] <= block_row0[:, None]).astype(jnp.int32), axis=1)
    block_e = jnp.minimum(block_e, N_EXPERTS - 1)
    of_block = block_e[:, None] == jnp.arange(N_EXPERTS, dtype=jnp.int32)[None, :]
    block_end = jnp.sum(jnp.where(of_block, (pad_starts + counts)[None, :], 0), axis=1)
    block_valid = jnp.clip(block_end - block_row0, 0, blk)
    block_valid = jnp.where(block_row0 < pad_ends[-1], block_valid, 0).astype(jnp.int32)

    xs = _dispatch_rows(h2, dest, n_rows)
    ys = _expert_ffn(xs, block_e, block_valid, w_up[0], b_up[0], w_down[0], b_down[0])
    gates_t = gates.T
    tp = T // COMBINE_PARTS
    out = None
    for part in range(COMBINE_PARTS):
        part_dest = dest[:, part * tp:(part + 1) * tp].reshape(-1)
        rows = _gather_rows(ys, part_dest).reshape(TOP_K, tp, D // 2)
        out = _final_norm(x1, rows, gates_t, g_final, part, out)
    return out.reshape(B, S, D)
```

```python
import functools

import jax
import jax.numpy as jnp
import numpy as np
from jax import lax
from jax.experimental import pallas as pl
from jax.experimental.pallas import tpu as pltpu
from jax.experimental.pallas import tpu_sc as plsc

D_MODEL = 1024
POOL_WIDTH = 512
POOL_WINDOWS = (2, 4, 8, 16)
POOL_GROUP = 128
MAX_WINDOW = 16
ATTN_WIDTH = 512
HEAD_DIM = 64
N_HEADS = 8
N_EXPERTS = 32
TOP_K = 4
D_EXPERT = 1024
SWIGLU_LIMIT = 7.0
SWIGLU_ALPHA = 1.702
RMS_EPS = 1e-5

LANES = 128
SUBLANES = 8
AUG_GROUP = 16
N_SPLIT = 3
MASK_VALUE = -1e30
LOG2_E = 1.4426950408889634
SKIP_MARGIN = 160.0
NORM_SLACK = 1.02
V_ROWS = 80

SEQ_TILE = 512
Q_TILE = 512
HEADS_PER_STEP = 4
LOGITS_AHEAD = 2
KV_TILE = SEQ_TILE
ROUTE_TILE = 1024
ROUTE_PART = 512
EXPERT_ROWS = 512
COMBINE_PARTS = 16
SC_CORES = 2
SC_SUBCORES = 16
SC_WORKERS = SC_CORES * SC_SUBCORES
SC_WINDOW = 128
VMEM_LIMIT = 56 * 1024 * 1024

_NT = (((1,), (1,)), ((), ()))


def _dot(a, b):
    return jnp.dot(a, b, preferred_element_type=jnp.float32)


def _dot_nt(a, b):
    return lax.dot_general(a, b, _NT, preferred_element_type=jnp.float32)


def _rms_norm(x, g):
    return x * lax.rsqrt(jnp.mean(x * x, axis=-1, keepdims=True) + RMS_EPS) * g


def _pack_bf16_pairs(x):
    n = x.shape[1] // 2
    bits = lax.bitcast_convert_type(x.astype(jnp.bfloat16).astype(jnp.float32), jnp.uint32)
    return bits[:, :n] | (bits[:, n:] >> 16)


def _unpack_bf16_pairs(p):
    hi = lax.bitcast_convert_type(p & jnp.uint32(0xFFFF0000), jnp.float32)
    lo = lax.bitcast_convert_type(p << 16, jnp.float32)
    return hi, lo


def _split3(x):
    hi = x.astype(jnp.bfloat16)
    r1 = x - hi.astype(jnp.float32)
    mid = r1.astype(jnp.bfloat16)
    lo = (r1 - mid.astype(jnp.float32)).astype(jnp.bfloat16)
    return hi, mid, lo


def _inproj_kernel(x_ref, g_ref, wu_ref, wqt_ref, wk_ref, wvt_ref, wf_ref, bf_ref, tri_ref,
                   pqt_ref, pk_ref, oneq_ref, onek_ref, hsel_ref, wpool_ref, pscale_ref,
                   ypool_ref, qt_ref, qaug_ref, k_ref, kaug_ref, vt_ref, stats_ref, ubuf, ccarry):
    s = pl.program_id(1)
    tm = x_ref.shape[1]

    @pl.when(s == 0)
    def _():
        ubuf[...] = jnp.zeros_like(ubuf)
        ccarry[...] = jnp.zeros_like(ccarry)

    x = x_ref[0]
    h = _rms_norm(x, g_ref[...]).astype(jnp.bfloat16)
    fl = _dot(h, wf_ref[...]) + bf_ref[...]
    u = _dot(h, wu_ref[...])
    kp = _dot(h, wk_ref[...])
    qt = _dot_nt(wqt_ref[...], h)
    vt = _dot_nt(wvt_ref[...], h).astype(jnp.bfloat16)
    k_ref[0] = kp.astype(jnp.bfloat16)
    qt_ref[0] = qt.astype(jnp.bfloat16)

    logf = jnp.minimum(fl, 0.0) - jnp.log1p(jnp.exp(-jnp.abs(fl)))
    lane = lax.broadcasted_iota(jnp.int32, (tm, LANES), 1)
    logf = jnp.where(lane < N_SPLIT * N_HEADS, logf, 0.0)
    f_hi, f_mid, f_lo = _split3(logf)
    tri = tri_ref[...]
    c = _dot(tri, f_hi) + _dot(tri, f_mid) + _dot(tri, f_lo) + ccarry[...]
    ccarry[...] = c[tm - 1:tm, :]
    c2 = c * LOG2_E
    c_hi, c_mid, c_lo = _split3(c2)
    term = lane % N_SPLIT
    c3 = jnp.where(term == 0, c_hi, jnp.where(term == 1, c_mid, c_lo))
    kaug_ref[0] = (_dot(c3, pk_ref[...]) + onek_ref[...]).astype(jnp.bfloat16)
    qaug_ref[0] = (_dot_nt(pqt_ref[...], c3) + oneq_ref[...]).astype(jnp.bfloat16)
    lane_row = lax.broadcasted_iota(jnp.int32, (1, LANES), 1)
    kn2 = jnp.max(_dot((kp * kp).astype(jnp.bfloat16), hsel_ref[...]), axis=0, keepdims=True)
    qn2 = jnp.zeros((1, LANES), jnp.float32)
    for hd in range(N_HEADS):
        qs = qt[hd * HEAD_DIM:(hd + 1) * HEAD_DIM, :]
        qmax = jnp.max(jnp.sum(qs * qs, axis=0, keepdims=True), axis=1, keepdims=True)
        qn2 = jnp.where(lane_row == hd, qmax, qn2)
    stats_ref[0, 0, 0:1, :] = kn2
    stats_ref[0, 0, 1:2, :] = qn2
    stats_ref[0, 0, 2:3, :] = c2[0:1, :]
    stats_ref[0, 0, 3:4, :] = c2[tm - 1:tm, :]
    stats_ref[0, 0, 4:SUBLANES, :] = jnp.zeros((SUBLANES - 4, LANES), jnp.float32)
    ones = jnp.ones((V_ROWS - HEAD_DIM, tm), jnp.bfloat16)
    for hd in range(N_HEADS):
        vt_ref[0, 0, hd * V_ROWS:hd * V_ROWS + HEAD_DIM, :] = vt[hd * HEAD_DIM:(hd + 1) * HEAD_DIM]
        vt_ref[0, 0, hd * V_ROWS + HEAD_DIM:(hd + 1) * V_ROWS, :] = ones

    head_pos = s * tm + lax.broadcasted_iota(jnp.int32, (MAX_WINDOW, 1), 0)
    for gi, w in enumerate(POOL_WINDOWS):
        cols = slice(gi * POOL_GROUP, (gi + 1) * POOL_GROUP)
        ug = u[:, cols]
        win = jnp.concatenate([ubuf[:, cols], ug], axis=0)
        span = 1
        while span < w:
            win = win + pltpu.roll(win, span, axis=0)
            span *= 2
        acc = win[MAX_WINDOW:]
        inv_head = 1.0 / jnp.minimum(head_pos + 1, w).astype(jnp.float32)
        p = jnp.concatenate([acc[:MAX_WINDOW] * inv_head, acc[MAX_WINDOW:] * (1.0 / w)],
                            axis=0) - ug
        y = _dot(p.astype(jnp.bfloat16), wpool_ref[gi]) * pscale_ref[:, cols]
        ypool_ref[0, :, cols] = y.astype(jnp.bfloat16)
    ubuf[...] = u[tm - MAX_WINDOW:]


def _in_projection(x, g_mix, w_in, b_forget, w_pool, pool_scale, head_order):
    B, S, D = x.shape
    tm = SEQ_TILE
    bf16 = jnp.bfloat16
    o = POOL_WIDTH
    w_u = w_in[:, :o].astype(bf16)
    w_q = w_in[:, o:o + ATTN_WIDTH] * (HEAD_DIM ** -0.5 * LOG2_E)
    w_k = w_in[:, o + ATTN_WIDTH:o + 2 * ATTN_WIDTH]
    w_v = w_in[:, o + 2 * ATTN_WIDTH:o + 3 * ATTN_WIDTH]
    w_f = w_in[:, o + 3 * ATTN_WIDTH:]
    by_head = lambda w: jnp.take(w.reshape(D, N_HEADS, HEAD_DIM), head_order,
                                 axis=1).reshape(D, ATTN_WIDTH)
    w_q, w_k, w_v = by_head(w_q), by_head(w_k), by_head(w_v)
    w_f = jnp.take(w_f, head_order, axis=1)
    b_forget = jnp.take(b_forget, head_order)

    wqt = w_q.T.astype(bf16)
    wk = w_k.astype(bf16)
    wvt = w_v.T.astype(bf16)
    n_gate = N_SPLIT * N_HEADS
    wf = jnp.pad(jnp.repeat(w_f, N_SPLIT, axis=1), ((0, 0), (0, LANES - n_gate))).astype(bf16)
    bfp = jnp.pad(jnp.repeat(b_forget, N_SPLIT), (0, LANES - n_gate)).reshape(1, LANES)

    assert N_HEADS * AUG_GROUP == LANES and 2 * N_SPLIT <= AUG_GROUP
    src = np.arange(LANES)
    head, term = src // N_SPLIT, src % N_SPLIT
    dst = np.arange(LANES)
    valid = (src < n_gate)[:, None]
    q_hit = valid & (dst[None, :] == (head * AUG_GROUP + term)[:, None])
    k_hit = valid & (dst[None, :] == (head * AUG_GROUP + N_SPLIT + term)[:, None])
    pqt = jnp.asarray(q_hit.T.astype(np.float32), bf16)
    pk = jnp.asarray(-k_hit.astype(np.float32), bf16)
    slot_pos = dst % AUG_GROUP
    oneq = ((slot_pos >= N_SPLIT) & (slot_pos < 2 * N_SPLIT))
    onek = slot_pos < N_SPLIT
    oneq = jnp.asarray(oneq.astype(np.float32).reshape(-1, 1))
    onek = jnp.asarray(onek.astype(np.float32).reshape(1, -1))
    r = np.arange(tm)
    tri = jnp.asarray((r[None, :] <= r[:, None]).astype(np.float32), bf16)
    hsel = jnp.asarray((np.arange(ATTN_WIDTH)[:, None] // HEAD_DIM
                        == np.arange(LANES)[None, :]).astype(np.float32), bf16)

    full = lambda *shape: pl.BlockSpec(shape, lambda b, s: (0,) * len(shape))
    aw = ATTN_WIDTH
    return pl.pallas_call(
        _inproj_kernel,
        grid=(B, S // tm),
        in_specs=[
            pl.BlockSpec((1, tm, D), lambda b, s: (b, s, 0)),
            full(1, D), full(D, o), full(aw, D), full(D, aw), full(aw, D),
            full(D, LANES), full(1, LANES), full(tm, tm), full(LANES, LANES), full(LANES, LANES),
            full(LANES, 1), full(1, LANES), full(aw, LANES),
            full(len(POOL_WINDOWS), POOL_GROUP, POOL_GROUP),
            full(1, o),
        ],
        out_specs=[
            pl.BlockSpec((1, tm, o), lambda b, s: (b, s, 0)),
            pl.BlockSpec((1, aw, tm), lambda b, s: (b, 0, s)),
            pl.BlockSpec((1, LANES, tm), lambda b, s: (b, 0, s)),
            pl.BlockSpec((1, tm, aw), lambda b, s: (b, s, 0)),
            pl.BlockSpec((1, tm, LANES), lambda b, s: (b, s, 0)),
            pl.BlockSpec((1, 1, N_HEADS * V_ROWS, tm), lambda b, s: (b, s, 0, 0)),
            pl.BlockSpec((1, 1, SUBLANES, LANES), lambda b, s: (b, s, 0, 0)),
        ],
        out_shape=[
            jax.ShapeDtypeStruct((B, S, o), bf16),
            jax.ShapeDtypeStruct((B, aw, S), bf16),
            jax.ShapeDtypeStruct((B, LANES, S), bf16),
            jax.ShapeDtypeStruct((B, S, aw), bf16),
            jax.ShapeDtypeStruct((B, S, LANES), bf16),
            jax.ShapeDtypeStruct((B, S // tm, N_HEADS * V_ROWS, tm), bf16),
            jax.ShapeDtypeStruct((B, S // tm, SUBLANES, LANES), jnp.float32),
        ],
        scratch_shapes=[
            pltpu.VMEM((MAX_WINDOW, o), jnp.float32),
            pltpu.VMEM((1, LANES), jnp.float32),
        ],
        compiler_params=pltpu.CompilerParams(
            dimension_semantics=("arbitrary", "arbitrary"), vmem_limit_bytes=VMEM_LIMIT),
        name="in_projection",
    )(x, g_mix.reshape(1, D), w_u, wqt, wk, wvt, wf, bfp, tri, pqt, pk, oneq, onek, hsel,
      w_pool.astype(bf16), pool_scale.reshape(1, o))


def _attn_kernel(j0_ref, qt_ref, qaug_ref, k_ref, kaug_ref, vt_ref, causal_ref, o_ref,
                 qop_ref, st_buf, mx_buf, m_ref, acc_ref):
    i = pl.program_id(2)
    tile = (pl.program_id(0) * pl.num_programs(1) + pl.program_id(1)) * pl.num_programs(2) + i
    j0 = j0_ref[tile]
    tq = qt_ref.shape[2]
    tk = vt_ref.shape[3]
    heads = qt_ref.shape[1] // HEAD_DIM

    m_ref[...] = jnp.full(m_ref.shape, MASK_VALUE, jnp.float32)
    acc_ref[...] = jnp.zeros_like(acc_ref)

    first_head = pl.program_id(1) * heads
    group = lax.broadcasted_iota(jnp.int32, (LANES, tq), 0) // AUG_GROUP
    q_aug = qaug_ref[0]
    blank = jnp.zeros((HEAD_DIM, tq), q_aug.dtype)
    for hh in range(heads):
        q_own = qt_ref[0, hh * HEAD_DIM:(hh + 1) * HEAD_DIM, :]
        pair_rows = [q_own, blank] if hh % 2 == 0 else [blank, q_own]
        qop_ref[hh, 0:2 * HEAD_DIM, :] = jnp.concatenate(pair_rows, axis=0)
        qop_ref[hh, 2 * HEAD_DIM:, :] = jnp.where(group == first_head + hh, q_aug,
                                                  jnp.zeros_like(q_aug))

    def logits(j, slot, hh, masked):
        row0 = pl.multiple_of(j * tk, tk)
        pair_lanes = slice(hh // 2 * 2 * HEAD_DIM, (hh // 2 + 1) * 2 * HEAD_DIM)
        k = jnp.concatenate([k_ref[0, pl.ds(row0, tk), pair_lanes],
                             kaug_ref[0, pl.ds(row0, tk), :]], axis=1)
        st = _dot(k, qop_ref[hh])
        if masked:
            st = st + causal_ref[...]
        st_buf[slot, hh] = st
        mx_buf[slot, hh] = jnp.max(st, axis=0, keepdims=True)

    def accumulate(j, slot, hh):
        st = st_buf[slot, hh]
        m_prev = m_ref[hh]
        m_new = jnp.maximum(m_prev, mx_buf[slot, hh])
        alpha = jnp.exp2(m_prev - m_new)
        p = jnp.exp2(st - m_new)
        vt = vt_ref[0, j, hh * V_ROWS:(hh + 1) * V_ROWS, :]
        acc_ref[hh] = alpha * acc_ref[hh] + _dot(vt, p.astype(jnp.bfloat16))
        m_ref[hh] = m_new

    ahead = LOGITS_AHEAD
    assert 1 <= ahead <= heads
    for hh in range(heads):
        logits(i, 0, hh, True)
        if hh >= ahead:
            accumulate(i, 0, hh - ahead)

    def before(t):
        return jnp.where(t == j0, i, t - 1)

    def step(t, slot):
        for hh in range(heads):
            logits(t, slot, hh, False)
            if hh >= ahead:
                accumulate(t, slot, hh - ahead)
            else:
                accumulate(before(t), 1 - slot, hh - ahead + heads)

    def drain(t, slot):
        for hh in range(heads - ahead, heads):
            accumulate(t, slot, hh)

    def body(u, carry):
        step(j0 + 2 * u, 1)
        step(j0 + 2 * u + 1, 0)
        return carry

    n_chunks = i - j0
    lax.fori_loop(0, n_chunks // 2, body, 0)

    @pl.when(n_chunks % 2 == 1)
    def _():
        step(i - 1, 1)
        drain(i - 1, 1)

    @pl.when(n_chunks % 2 == 0)
    def _():
        drain(before(i), 0)

    outs = [acc_ref[hh, :HEAD_DIM] / acc_ref[hh, HEAD_DIM:HEAD_DIM + 1] for hh in range(heads)]
    ot = jnp.concatenate(outs, axis=0)
    o_ref[0] = ot.T.astype(o_ref.dtype)


def _decay_head_order(b_forget):
    return jnp.argsort(b_forget).astype(jnp.int32)


def _first_needed_chunk(stats):
    kn = jnp.sqrt(stats[:, :, 0, :N_HEADS])
    qn = jnp.sqrt(stats[:, :, 1, :N_HEADS])
    c_first = stats[:, :, 2, 0:N_SPLIT * N_HEADS:N_SPLIT]
    c_last = stats[:, :, 3, 0:N_SPLIT * N_HEADS:N_SPLIT]
    n = stats.shape[1]
    dot_bound = NORM_SLACK * qn[:, :, None, :] * kn[:, None, :, :] + 1.0
    upper = dot_bound + c_first[:, :, None, :] - c_last[:, None, :, :]
    self_bound = NORM_SLACK * qn * kn + 1.0
    ii = jnp.arange(n, dtype=jnp.int32)[None, :, None, None]
    jj = jnp.arange(n, dtype=jnp.int32)[None, None, :, None]
    needed = (upper + self_bound[:, :, None, :] >= -SKIP_MARGIN) & (jj < ii)
    first = jnp.min(jnp.where(needed, jj, ii), axis=2)
    g = HEADS_PER_STEP
    first = jnp.min(first.reshape(first.shape[0], n, N_HEADS // g, g), axis=3)
    return jnp.transpose(first, (0, 2, 1)).reshape(-1).astype(jnp.int32)


def _attention(qt, qaug, k, kaug, vt, first_chunk):
    B, _, S = qt.shape
    tq, tk = Q_TILE, KV_TILE
    assert tq == tk, "the diagonal is handled as a single masked key chunk"
    pair = HEADS_PER_STEP
    n_pairs = N_HEADS // pair
    pw = pair * HEAD_DIM
    causal = np.where(np.arange(tk)[:, None] <= np.arange(tq)[None, :], 0.0, MASK_VALUE)
    grid_spec = pltpu.PrefetchScalarGridSpec(
        num_scalar_prefetch=1,
        grid=(B, n_pairs, S // tq),
        in_specs=[
            pl.BlockSpec((1, pw, tq), lambda b, p, i, j0: (b, p, i)),
            pl.BlockSpec((1, LANES, tq), lambda b, p, i, j0: (b, 0, i)),
            pl.BlockSpec((1, S, pw), lambda b, p, i, j0: (b, 0, p)),
            pl.BlockSpec((1, S, LANES), lambda b, p, i, j0: (b, 0, 0)),
            pl.BlockSpec((1, S // tk, pair * V_ROWS, tk), lambda b, p, i, j0: (b, 0, p, 0)),
            pl.BlockSpec((tk, tq), lambda b, p, i, j0: (0, 0)),
        ],
        out_specs=pl.BlockSpec((1, tq, pw), lambda b, p, i, j0: (b, i, p)),
        scratch_shapes=[
            pltpu.VMEM((pair, 2 * HEAD_DIM + LANES, tq), jnp.bfloat16),
            pltpu.VMEM((2, pair, tk, tq), jnp.float32),
            pltpu.VMEM((2, pair, 1, tq), jnp.float32),
            pltpu.VMEM((pair, 1, tq), jnp.float32),
            pltpu.VMEM((pair, V_ROWS, tq), jnp.float32),
        ],
    )
    return pl.pallas_call(
        _attn_kernel,
        grid_spec=grid_spec,
        out_shape=jax.ShapeDtypeStruct((B, S, ATTN_WIDTH), jnp.bfloat16),
        compiler_params=pltpu.CompilerParams(
            dimension_semantics=("arbitrary", "arbitrary", "arbitrary"),
            vmem_limit_bytes=VMEM_LIMIT),
        name="forgetting_attention",
    )(first_chunk, qt, qaug, k, kaug, vt, jnp.asarray(causal, jnp.float32))


def _outproj_kernel(yp_ref, ya_ref, x_ref, wo1_ref, wo2_ref, g_ref, wr2_ref, wrhi_ref, br_ref,
                    upper_ref, x1_ref, h2_ref, idx_ref, gate_ref, rank_ref, cnt_ref, carry):
    t = pl.program_id(0)
    tm = x_ref.shape[0]

    @pl.when(t == 0)
    def _():
        carry[...] = jnp.zeros_like(carry)

    hm = upper_ref.shape[0]
    parts = [slice(r * hm, (r + 1) * hm) for r in range(tm // hm)]

    def project(rows):
        x1 = x_ref[rows] + _dot(yp_ref[rows], wo1_ref[...]) + _dot(ya_ref[rows], wo2_ref[...])
        x1_ref[rows] = x1
        return x1

    def route(rows, x1):
        h2 = _rms_norm(x1, g_ref[...])
        h2_hi = h2.astype(jnp.bfloat16)
        h2_lo = (h2 - h2_hi.astype(jnp.float32)).astype(jnp.bfloat16)
        h2_ref[rows] = _pack_bf16_pairs(h2)

        lg2 = _dot_nt(wr2_ref[...], h2_hi)
        logits = (lg2[:N_EXPERTS] + lg2[N_EXPERTS:] + _dot_nt(wrhi_ref[...], h2_lo)
                  + br_ref[...])

        eio = lax.broadcasted_iota(jnp.int32, (N_EXPERTS, hm), 0)
        vals, idxs = [], []
        cur = logits
        for _ in range(TOP_K):
            m = jnp.max(cur, axis=0, keepdims=True)
            ix = jnp.min(jnp.where(cur == m, eio, N_EXPERTS), axis=0, keepdims=True)
            vals.append(m)
            idxs.append(ix)
            cur = jnp.where(eio == ix, -jnp.inf, cur)
        exps = [jnp.exp(v - vals[0]) for v in vals]
        denom = exps[0] + exps[1] + exps[2] + exps[3]
        hits = [eio == ix for ix in idxs]
        cnt = jnp.zeros((N_EXPERTS, hm), jnp.float32)
        for hit in hits:
            cnt = cnt + hit.astype(jnp.float32)
        base = _dot(cnt.astype(jnp.bfloat16), upper_ref[...]) + carry[...]
        for kk in range(TOP_K):
            idx_ref[kk:kk + 1, rows] = idxs[kk]
            gate_ref[kk:kk + 1, rows] = exps[kk] / denom
            rank = jnp.sum(jnp.where(hits[kk], base, 0.0), axis=0, keepdims=True)
            rank_ref[kk:kk + 1, rows] = rank.astype(jnp.int32)
        carry[...] = carry[...] + jnp.sum(cnt, axis=1, keepdims=True)

    pending = project(parts[0])
    for r, rows in enumerate(parts):
        ahead = project(parts[r + 1]) if r + 1 < len(parts) else None
        route(rows, pending)
        pending = ahead
    cnt_ref[...] = carry[...]


def _out_projection(ypool, yattn, x, w_out, g_ffn, w_router, b_router, head_order):
    T, D = x.shape
    tm = ROUTE_TILE
    bf16 = jnp.bfloat16
    w_attn = jnp.take(w_out[POOL_WIDTH:].reshape(N_HEADS, HEAD_DIM, D), head_order, axis=0)
    wo = jnp.concatenate([w_out[:POOL_WIDTH], w_attn.reshape(ATTN_WIDTH, D)], axis=0).astype(bf16)
    wr_t = w_router.T
    wr_hi = wr_t.astype(bf16)
    wr_lo = (wr_t - wr_hi.astype(jnp.float32)).astype(bf16)
    wr2 = jnp.concatenate([wr_hi, wr_lo], axis=0)
    r = np.arange(ROUTE_PART)
    upper = jnp.asarray((r[:, None] < r[None, :]).astype(np.float32), bf16)
    full = lambda *shape: pl.BlockSpec(shape, lambda t: (0,) * len(shape))
    row = lambda w: pl.BlockSpec((tm, w), lambda t: (t, 0))
    col = pl.BlockSpec((TOP_K, tm), lambda t: (0, t))
    return pl.pallas_call(
        _outproj_kernel,
        grid=(T // tm,),
        in_specs=[row(POOL_WIDTH), row(ATTN_WIDTH), row(D), full(POOL_WIDTH, D),
                  full(ATTN_WIDTH, D), full(1, D), full(2 * N_EXPERTS, D), full(N_EXPERTS, D),
                  full(N_EXPERTS, 1), full(ROUTE_PART, ROUTE_PART)],
        out_specs=[row(D), row(D // 2), col, col, col, full(N_EXPERTS, 1)],
        out_shape=[
            jax.ShapeDtypeStruct((T, D), jnp.float32),
            jax.ShapeDtypeStruct((T, D // 2), jnp.uint32),
            jax.ShapeDtypeStruct((TOP_K, T), jnp.int32),
            jax.ShapeDtypeStruct((TOP_K, T), jnp.float32),
            jax.ShapeDtypeStruct((TOP_K, T), jnp.int32),
            jax.ShapeDtypeStruct((N_EXPERTS, 1), jnp.float32),
        ],
        scratch_shapes=[pltpu.VMEM((N_EXPERTS, 1), jnp.float32)],
        compiler_params=pltpu.CompilerParams(
            dimension_semantics=("arbitrary",), vmem_limit_bytes=VMEM_LIMIT),
        name="out_projection_router",
    )(ypool, yattn, x, wo[:POOL_WIDTH], wo[POOL_WIDTH:], g_ffn.reshape(1, D), wr2, wr_hi,
      b_router.reshape(N_EXPERTS, 1), upper)


def _sc_mesh():
    return plsc.VectorSubcoreMesh(core_axis_name="c", subcore_axis_name="s")


def _dispatch_rows(h2, dest, n_rows):
    T, D = h2.shape
    per_worker = T // SC_WORKERS
    steps = per_worker // SC_WINDOW

    @functools.partial(
        pl.kernel, mesh=_sc_mesh(),
        out_type=jax.ShapeDtypeStruct((n_rows, D), h2.dtype),
        scratch_types=[pltpu.VMEM((TOP_K, SC_WINDOW), jnp.int32),
                       pltpu.VMEM((SC_WINDOW, D), h2.dtype),
                       pltpu.SemaphoreType.DMA],
    )
    def scatter_kernel(h2_hbm, dest_hbm, xs_hbm, idx_v, rows_v, sem):
        wid = lax.axis_index("s") * SC_CORES + lax.axis_index("c")

        @pl.loop(0, steps)
        def _(step):
            base = wid * per_worker + step * SC_WINDOW
            pltpu.sync_copy(h2_hbm.at[pl.ds(base, SC_WINDOW)], rows_v)
            pltpu.sync_copy(dest_hbm.at[:, pl.ds(base, SC_WINDOW)], idx_v)
            copies = [pltpu.async_copy(rows_v, xs_hbm.at[idx_v.at[kk]], sem)
                      for kk in range(TOP_K)]
            for c in copies:
                c.wait()

    return scatter_kernel(h2, dest)


def _gather_rows(table, idx):
    N = idx.shape[0]
    D = table.shape[1]
    per_worker = N // SC_WORKERS
    steps = per_worker // SC_WINDOW

    @functools.partial(
        pl.kernel, mesh=_sc_mesh(),
        out_type=jax.ShapeDtypeStruct((N, D), table.dtype),
        scratch_types=[pltpu.VMEM((SC_WINDOW,), jnp.int32),
                       pltpu.VMEM((SC_WINDOW, D), table.dtype),
                       pltpu.SemaphoreType.DMA],
    )
    def gather_kernel(table_hbm, idx_hbm, out_hbm, idx_v, rows_v, sem):
        wid = lax.axis_index("s") * SC_CORES + lax.axis_index("c")

        @pl.loop(0, steps)
        def _(step):
            base = wid * per_worker + step * SC_WINDOW
            pltpu.sync_copy(idx_hbm.at[pl.ds(base, SC_WINDOW)], idx_v)
            pltpu.async_copy(table_hbm.at[idx_v], rows_v, sem).wait()
            pltpu.sync_copy(rows_v, out_hbm.at[pl.ds(base, SC_WINDOW)])

    return gather_kernel(table, idx)


def _expert_kernel(be_ref, nvalid_ref, x_ref, wup_ref, bup_ref, wdn_ref, bdn_ref, y_ref,
                   wup_bf, wdn_bf):
    i = pl.program_id(0)
    blk = x_ref.shape[0]
    changed = jnp.logical_or(i == 0, be_ref[i] != be_ref[jnp.maximum(i - 1, 0)])

    @pl.when(changed)
    def _():
        wup_bf[...] = wup_ref[0].astype(jnp.bfloat16)
        wdn_bf[...] = wdn_ref[0].astype(jnp.bfloat16)

    nvalid = nvalid_ref[i]

    @pl.when(nvalid > 0)
    def _():
        row = lax.broadcasted_iota(jnp.int32, (blk, 1), 0)
        x_hi, x_lo = _unpack_bf16_pairs(jnp.where(row < nvalid, x_ref[...], jnp.uint32(0)))
        x = jnp.concatenate([x_hi.astype(jnp.bfloat16), x_lo.astype(jnp.bfloat16)], axis=1)
        glu = _dot(x, wup_bf[:, :D_EXPERT]) + bup_ref[0, :, :D_EXPERT]
        lin = _dot(x, wup_bf[:, D_EXPERT:]) + bup_ref[0, :, D_EXPERT:]
        glu = jnp.minimum(glu, SWIGLU_LIMIT)
        lin = jnp.clip(lin, -SWIGLU_LIMIT, SWIGLU_LIMIT)
        act = glu * (1.0 / (1.0 + jnp.exp(-SWIGLU_ALPHA * glu))) * (lin + 1.0)
        y = _dot(act.astype(jnp.bfloat16), wdn_bf[...]) + bdn_ref[0]
        y_ref[...] = _pack_bf16_pairs(y)

    @pl.when(nvalid <= 0)
    def _():
        y_ref[...] = jnp.zeros_like(y_ref)


def _expert_ffn(xs, block_e, block_valid, w_up, b_up, w_down, b_down):
    n_rows = xs.shape[0]
    D = D_MODEL
    blk = EXPERT_ROWS
    grid_spec = pltpu.PrefetchScalarGridSpec(
        num_scalar_prefetch=2,
        grid=(n_rows // blk,),
        in_specs=[
            pl.BlockSpec((blk, D // 2), lambda i, be, nv: (i, 0)),
            pl.BlockSpec((1, D, 2 * D_EXPERT), lambda i, be, nv: (be[i], 0, 0)),
            pl.BlockSpec((1, 1, 2 * D_EXPERT), lambda i, be, nv: (be[i], 0, 0)),
            pl.BlockSpec((1, D_EXPERT, D), lambda i, be, nv: (be[i], 0, 0)),
            pl.BlockSpec((1, 1, D), lambda i, be, nv: (be[i], 0, 0)),
        ],
        out_specs=pl.BlockSpec((blk, D // 2), lambda i, be, nv: (i, 0)),
        scratch_shapes=[pltpu.VMEM((D, 2 * D_EXPERT), jnp.bfloat16),
                        pltpu.VMEM((D_EXPERT, D), jnp.bfloat16)],
    )
    return pl.pallas_call(
        _expert_kernel,
        grid_spec=grid_spec,
        out_shape=jax.ShapeDtypeStruct((n_rows, D // 2), jnp.uint32),
        compiler_params=pltpu.CompilerParams(
            dimension_semantics=("arbitrary",), vmem_limit_bytes=VMEM_LIMIT),
        name="expert_ffn",
    )(block_e, block_valid, xs, w_up, b_up.reshape(N_EXPERTS, 1, 2 * D_EXPERT), w_down,
      b_down.reshape(N_EXPERTS, 1, D))


def _final_kernel(x1_ref, rows_ref, gate_ref, g_ref, o_ref):
    half = x1_ref.shape[1] // 2
    acc_hi = x1_ref[:, :half]
    acc_lo = x1_ref[:, half:]
    for kk in range(TOP_K):
        y_hi, y_lo = _unpack_bf16_pairs(rows_ref[kk])
        gate = gate_ref[:, kk:kk + 1]
        acc_hi = acc_hi + y_hi * gate
        acc_lo = acc_lo + y_lo * gate
    o_ref[...] = _rms_norm(jnp.concatenate([acc_hi, acc_lo], axis=1), g_ref[...])


def _final_alias_kernel(x1_ref, rows_ref, gate_ref, g_ref, prev_ref, o_ref):
    del prev_ref
    _final_kernel(x1_ref, rows_ref, gate_ref, g_ref, o_ref)


def _final_norm(x1, rows, gates_t, g_final, part, prev_out):
    T, D = x1.shape
    Tp = rows.shape[1]
    tm = SEQ_TILE
    t0 = part * (Tp // tm)
    row = pl.BlockSpec((tm, D), lambda t: (t + t0, 0))
    in_specs = [row, pl.BlockSpec((TOP_K, tm, D // 2), lambda t: (0, t, 0)),
                pl.BlockSpec((tm, TOP_K), lambda t: (t + t0, 0)),
                pl.BlockSpec((1, D), lambda t: (0, 0))]
    args = [x1, rows, gates_t, g_final.reshape(1, D)]
    body, aliases = _final_kernel, {}
    if prev_out is not None:
        in_specs.append(pl.BlockSpec(memory_space=pl.ANY))
        args.append(prev_out)
        body, aliases = _final_alias_kernel, {len(args) - 1: 0}
    return pl.pallas_call(
        body,
        grid=(Tp // tm,),
        in_specs=in_specs,
        out_specs=row,
        out_shape=jax.ShapeDtypeStruct((T, D), jnp.float32),
        input_output_aliases=aliases,
        compiler_params=pltpu.CompilerParams(dimension_semantics=("arbitrary",)),
        name="combine_final_norm",
    )(*args)


def kernel(x, g_mix, w_in, b_forget, w_pool, pool_scale, w_out, g_ffn, w_router, b_router,
           w_up, b_up, w_down, b_down, g_final):
    B, S, D = x.shape
    T = B * S
    head_order = _decay_head_order(b_forget[0])
    ypool, qt, qaug, k, kaug, vt, stats = _in_projection(x, g_mix[0], w_in[0], b_forget[0],
                                                        w_pool[0], pool_scale[0], head_order)
    yattn = _attention(qt, qaug, k, kaug, vt, _first_needed_chunk(stats))
    x1, h2, top_idx, gates, rank, counts = _out_projection(
        ypool.reshape(T, POOL_WIDTH), yattn.reshape(T, ATTN_WIDTH), x.reshape(T, D),
        w_out[0], g_ffn[0], w_router[0], b_router[0], head_order)

    blk = EXPERT_ROWS
    counts = counts[:, 0].astype(jnp.int32)
    padded = (counts + blk - 1) // blk * blk
    pad_ends = jnp.cumsum(padded)
    pad_starts = pad_ends - padded
    dest = rank
    for e in range(N_EXPERTS):
        dest = dest + jnp.where(top_idx == e, pad_starts[e], 0)
    n_rows = T * TOP_K + N_EXPERTS * blk
    block_row0 = jnp.arange(n_rows // blk, dtype=jnp.int32) * blk
    block_e = jnp.sum((pad_ends[None, :] <= block_row0[:, None]).astype(jnp.int32), axis=1)
    block_e = jnp.minimum(block_e, N_EXPERTS - 1)
    of_block = block_e[:, None] == jnp.arange(N_EXPERTS, dtype=jnp.int32)[None, :]
    block_end = jnp.sum(jnp.where(of_block, (pad_starts + counts)[None, :], 0), axis=1)
    block_valid = jnp.clip(block_end - block_row0, 0, blk)
    block_valid = jnp.where(block_row0 < pad_ends[-1], block_valid, 0).astype(jnp.int32)

    xs = _dispatch_rows(h2, dest, n_rows)
    ys = _expert_ffn(xs, block_e, block_valid, w_up[0], b_up[0], w_down[0], b_down[0])
    gates_t = gates.T
    tp = T // COMBINE_PARTS
    out = None
    for part in range(COMBINE_PARTS):
        part_dest = dest[:, part * tp:(part + 1) * tp].reshape(-1)
        rows = _gather_rows(ys, part_dest).reshape(TOP_K, tp, D // 2)
        out = _final_norm(x1, rows, gates_t, g_final, part, out)
    return out.reshape(B, S, D)
```

```python
import functools

import jax
import jax.numpy as jnp
import numpy as np
from jax import lax
from jax.experimental import pallas as pl
from jax.experimental.pallas import tpu as pltpu
from jax.experimental.pallas import tpu_sc as plsc

D_MODEL = 1024
POOL_WIDTH = 512
POOL_WINDOWS = (2, 4, 8, 16)
POOL_GROUP = 128
MAX_WINDOW = 16
ATTN_WIDTH = 512
HEAD_DIM = 64
N_HEADS = 8
N_EXPERTS = 32
TOP_K = 4
D_EXPERT = 1024
SWIGLU_LIMIT = 7.0
SWIGLU_ALPHA = 1.702
RMS_EPS = 1e-5

LANES = 128
SUBLANES = 8
AUG_GROUP = 16
N_SPLIT = 3
MASK_VALUE = -1e30
LOG2_E = 1.4426950408889634
SKIP_MARGIN = 160.0
NORM_SLACK = 1.02
V_ROWS = 80

SEQ_TILE = 512
Q_TILE = 512
HEADS_PER_STEP = 4
LOGITS_AHEAD = 2
KV_TILE = SEQ_TILE
ROUTE_TILE = 1024
ROUTE_PART = 512
EXPERT_ROWS = 512
COMBINE_PARTS = 8
SC_CORES = 2
SC_SUBCORES = 16
SC_WORKERS = SC_CORES * SC_SUBCORES
SC_WINDOW = 128
VMEM_LIMIT = 56 * 1024 * 1024

_NT = (((1,), (1,)), ((), ()))


def _dot(a, b):
    return jnp.dot(a, b, preferred_element_type=jnp.float32)


def _dot_nt(a, b):
    return lax.dot_general(a, b, _NT, preferred_element_type=jnp.float32)


def _rms_norm(x, g):
    return x * lax.rsqrt(jnp.mean(x * x, axis=-1, keepdims=True) + RMS_EPS) * g


def _pack_bf16_pairs(x):
    n = x.shape[1] // 2
    bits = lax.bitcast_convert_type(x.astype(jnp.bfloat16).astype(jnp.float32), jnp.uint32)
    return bits[:, :n] | (bits[:, n:] >> 16)


def _unpack_bf16_pairs(p):
    hi = lax.bitcast_convert_type(p & jnp.uint32(0xFFFF0000), jnp.float32)
    lo = lax.bitcast_convert_type(p << 16, jnp.float32)
    return hi, lo


def _split3(x):
    hi = x.astype(jnp.bfloat16)
    r1 = x - hi.astype(jnp.float32)
    mid = r1.astype(jnp.bfloat16)
    lo = (r1 - mid.astype(jnp.float32)).astype(jnp.bfloat16)
    return hi, mid, lo


def _inproj_kernel(x_ref, g_ref, wu_ref, wqt_ref, wk_ref, wvt_ref, wf_ref, bf_ref, tri_ref,
                   pqt_ref, pk_ref, oneq_ref, onek_ref, hsel_ref, wpool_ref, pscale_ref,
                   ypool_ref, qt_ref, qaug_ref, k_ref, kaug_ref, vt_ref, stats_ref, ubuf, ccarry):
    s = pl.program_id(1)
    tm = x_ref.shape[1]

    @pl.when(s == 0)
    def _():
        ubuf[...] = jnp.zeros_like(ubuf)
        ccarry[...] = jnp.zeros_like(ccarry)

    x = x_ref[0]
    h = _rms_norm(x, g_ref[...]).astype(jnp.bfloat16)
    fl = _dot(h, wf_ref[...]) + bf_ref[...]
    u = _dot(h, wu_ref[...])
    kp = _dot(h, wk_ref[...])
    qt = _dot_nt(wqt_ref[...], h)
    vt = _dot_nt(wvt_ref[...], h).astype(jnp.bfloat16)
    k_ref[0] = kp.astype(jnp.bfloat16)
    qt_ref[0] = qt.astype(jnp.bfloat16)

    logf = jnp.minimum(fl, 0.0) - jnp.log1p(jnp.exp(-jnp.abs(fl)))
    lane = lax.broadcasted_iota(jnp.int32, (tm, LANES), 1)
    logf = jnp.where(lane < N_SPLIT * N_HEADS, logf, 0.0)
    f_hi, f_mid, f_lo = _split3(logf)
    tri = tri_ref[...]
    c = _dot(tri, f_hi) + _dot(tri, f_mid) + _dot(tri, f_lo) + ccarry[...]
    ccarry[...] = c[tm - 1:tm, :]
    c2 = c * LOG2_E
    c_hi, c_mid, c_lo = _split3(c2)
    term = lane % N_SPLIT
    c3 = jnp.where(term == 0, c_hi, jnp.where(term == 1, c_mid, c_lo))
    kaug_ref[0] = (_dot(c3, pk_ref[...]) + onek_ref[...]).astype(jnp.bfloat16)
    qaug_ref[0] = (_dot_nt(pqt_ref[...], c3) + oneq_ref[...]).astype(jnp.bfloat16)
    lane_row = lax.broadcasted_iota(jnp.int32, (1, LANES), 1)
    kn2 = jnp.max(_dot((kp * kp).astype(jnp.bfloat16), hsel_ref[...]), axis=0, keepdims=True)
    qn2 = jnp.zeros((1, LANES), jnp.float32)
    for hd in range(N_HEADS):
        qs = qt[hd * HEAD_DIM:(hd + 1) * HEAD_DIM, :]
        qmax = jnp.max(jnp.sum(qs * qs, axis=0, keepdims=True), axis=1, keepdims=True)
        qn2 = jnp.where(lane_row == hd, qmax, qn2)
    stats_ref[0, 0, 0:1, :] = kn2
    stats_ref[0, 0, 1:2, :] = qn2
    stats_ref[0, 0, 2:3, :] = c2[0:1, :]
    stats_ref[0, 0, 3:4, :] = c2[tm - 1:tm, :]
    stats_ref[0, 0, 4:SUBLANES, :] = jnp.zeros((SUBLANES - 4, LANES), jnp.float32)
    ones = jnp.ones((V_ROWS - HEAD_DIM, tm), jnp.bfloat16)
    for hd in range(N_HEADS):
        vt_ref[0, 0, hd * V_ROWS:hd * V_ROWS + HEAD_DIM, :] = vt[hd * HEAD_DIM:(hd + 1) * HEAD_DIM]
        vt_ref[0, 0, hd * V_ROWS + HEAD_DIM:(hd + 1) * V_ROWS, :] = ones

    head_pos = s * tm + lax.broadcasted_iota(jnp.int32, (MAX_WINDOW, 1), 0)
    for gi, w in enumerate(POOL_WINDOWS):
        cols = slice(gi * POOL_GROUP, (gi + 1) * POOL_GROUP)
        ug = u[:, cols]
        win = jnp.concatenate([ubuf[:, cols], ug], axis=0)
        span = 1
        while span < w:
            win = win + pltpu.roll(win, span, axis=0)
            span *= 2
        acc = win[MAX_WINDOW:]
        inv_head = 1.0 / jnp.minimum(head_pos + 1, w).astype(jnp.float32)
        p = jnp.concatenate([acc[:MAX_WINDOW] * inv_head, acc[MAX_WINDOW:] * (1.0 / w)],
                            axis=0) - ug
        y = _dot(p.astype(jnp.bfloat16), wpool_ref[gi]) * pscale_ref[:, cols]
        ypool_ref[0, :, cols] = y.astype(jnp.bfloat16)
    ubuf[...] = u[tm - MAX_WINDOW:]


def _in_projection(x, g_mix, w_in, b_forget, w_pool, pool_scale, head_order):
    B, S, D = x.shape
    tm = SEQ_TILE
    bf16 = jnp.bfloat16
    o = POOL_WIDTH
    w_u = w_in[:, :o].astype(bf16)
    w_q = w_in[:, o:o + ATTN_WIDTH] * (HEAD_DIM ** -0.5 * LOG2_E)
    w_k = w_in[:, o + ATTN_WIDTH:o + 2 * ATTN_WIDTH]
    w_v = w_in[:, o + 2 * ATTN_WIDTH:o + 3 * ATTN_WIDTH]
    w_f = w_in[:, o + 3 * ATTN_WIDTH:]
    by_head = lambda w: jnp.take(w.reshape(D, N_HEADS, HEAD_DIM), head_order,
                                 axis=1).reshape(D, ATTN_WIDTH)
    w_q, w_k, w_v = by_head(w_q), by_head(w_k), by_head(w_v)
    w_f = jnp.take(w_f, head_order, axis=1)
    b_forget = jnp.take(b_forget, head_order)

    wqt = w_q.T.astype(bf16)
    wk = w_k.astype(bf16)
    wvt = w_v.T.astype(bf16)
    n_gate = N_SPLIT * N_HEADS
    wf = jnp.pad(jnp.repeat(w_f, N_SPLIT, axis=1), ((0, 0), (0, LANES - n_gate))).astype(bf16)
    bfp = jnp.pad(jnp.repeat(b_forget, N_SPLIT), (0, LANES - n_gate)).reshape(1, LANES)

    assert N_HEADS * AUG_GROUP == LANES and 2 * N_SPLIT <= AUG_GROUP
    src = np.arange(LANES)
    head, term = src // N_SPLIT, src % N_SPLIT
    dst = np.arange(LANES)
    valid = (src < n_gate)[:, None]
    q_hit = valid & (dst[None, :] == (head * AUG_GROUP + term)[:, None])
    k_hit = valid & (dst[None, :] == (head * AUG_GROUP + N_SPLIT + term)[:, None])
    pqt = jnp.asarray(q_hit.T.astype(np.float32), bf16)
    pk = jnp.asarray(-k_hit.astype(np.float32), bf16)
    slot_pos = dst % AUG_GROUP
    oneq = ((slot_pos >= N_SPLIT) & (slot_pos < 2 * N_SPLIT))
    onek = slot_pos < N_SPLIT
    oneq = jnp.asarray(oneq.astype(np.float32).reshape(-1, 1))
    onek = jnp.asarray(onek.astype(np.float32).reshape(1, -1))
    r = np.arange(tm)
    tri = jnp.asarray((r[None, :] <= r[:, None]).astype(np.float32), bf16)
    hsel = jnp.asarray((np.arange(ATTN_WIDTH)[:, None] // HEAD_DIM
                        == np.arange(LANES)[None, :]).astype(np.float32), bf16)

    full = lambda *shape: pl.BlockSpec(shape, lambda b, s: (0,) * len(shape))
    aw = ATTN_WIDTH
    return pl.pallas_call(
        _inproj_kernel,
        grid=(B, S // tm),
        in_specs=[
            pl.BlockSpec((1, tm, D), lambda b, s: (b, s, 0)),
            full(1, D), full(D, o), full(aw, D), full(D, aw), full(aw, D),
            full(D, LANES), full(1, LANES), full(tm, tm), full(LANES, LANES), full(LANES, LANES),
            full(LANES, 1), full(1, LANES), full(aw, LANES),
            full(len(POOL_WINDOWS), POOL_GROUP, POOL_GROUP),
            full(1, o),
        ],
        out_specs=[
            pl.BlockSpec((1, tm, o), lambda b, s: (b, s, 0)),
            pl.BlockSpec((1, aw, tm), lambda b, s: (b, 0, s)),
            pl.BlockSpec((1, LANES, tm), lambda b, s: (b, 0, s)),
            pl.BlockSpec((1, tm, aw), lambda b, s: (b, s, 0)),
            pl.BlockSpec((1, tm, LANES), lambda b, s: (b, s, 0)),
            pl.BlockSpec((1, 1, N_HEADS * V_ROWS, tm), lambda b, s: (b, s, 0, 0)),
            pl.BlockSpec((1, 1, SUBLANES, LANES), lambda b, s: (b, s, 0, 0)),
        ],
        out_shape=[
            jax.ShapeDtypeStruct((B, S, o), bf16),
            jax.ShapeDtypeStruct((B, aw, S), bf16),
            jax.ShapeDtypeStruct((B, LANES, S), bf16),
            jax.ShapeDtypeStruct((B, S, aw), bf16),
            jax.ShapeDtypeStruct((B, S, LANES), bf16),
            jax.ShapeDtypeStruct((B, S // tm, N_HEADS * V_ROWS, tm), bf16),
            jax.ShapeDtypeStruct((B, S // tm, SUBLANES, LANES), jnp.float32),
        ],
        scratch_shapes=[
            pltpu.VMEM((MAX_WINDOW, o), jnp.float32),
            pltpu.VMEM((1, LANES), jnp.float32),
        ],
        compiler_params=pltpu.CompilerParams(
            dimension_semantics=("arbitrary", "arbitrary"), vmem_limit_bytes=VMEM_LIMIT),
        name="in_projection",
    )(x, g_mix.reshape(1, D), w_u, wqt, wk, wvt, wf, bfp, tri, pqt, pk, oneq, onek, hsel,
      w_pool.astype(bf16), pool_scale.reshape(1, o))


def _attn_kernel(j0_ref, qt_ref, qaug_ref, k_ref, kaug_ref, vt_ref, causal_ref, o_ref,
                 qop_ref, st_buf, mx_buf, m_ref, acc_ref):
    i = pl.program_id(2)
    tile = (pl.program_id(0) * pl.num_programs(1) + pl.program_id(1)) * pl.num_programs(2) + i
    j0 = j0_ref[tile]
    tq = qt_ref.shape[2]
    tk = vt_ref.shape[3]
    heads = qt_ref.shape[1] // HEAD_DIM

    m_ref[...] = jnp.full(m_ref.shape, MASK_VALUE, jnp.float32)
    acc_ref[...] = jnp.zeros_like(acc_ref)

    first_head = pl.program_id(1) * heads
    group = lax.broadcasted_iota(jnp.int32, (LANES, tq), 0) // AUG_GROUP
    q_aug = qaug_ref[0]
    blank = jnp.zeros((HEAD_DIM, tq), q_aug.dtype)
    for hh in range(heads):
        q_own = qt_ref[0, hh * HEAD_DIM:(hh + 1) * HEAD_DIM, :]
        pair_rows = [q_own, blank] if hh % 2 == 0 else [blank, q_own]
        qop_ref[hh, 0:2 * HEAD_DIM, :] = jnp.concatenate(pair_rows, axis=0)
        qop_ref[hh, 2 * HEAD_DIM:, :] = jnp.where(group == first_head + hh, q_aug,
                                                  jnp.zeros_like(q_aug))

    def logits(j, slot, hh, masked):
        row0 = pl.multiple_of(j * tk, tk)
        pair_lanes = slice(hh // 2 * 2 * HEAD_DIM, (hh // 2 + 1) * 2 * HEAD_DIM)
        k = jnp.concatenate([k_ref[0, pl.ds(row0, tk), pair_lanes],
                             kaug_ref[0, pl.ds(row0, tk), :]], axis=1)
        st = _dot(k, qop_ref[hh])
        if masked:
            st = st + causal_ref[...]
        st_buf[slot, hh] = st
        mx_buf[slot, hh] = jnp.max(st, axis=0, keepdims=True)

    def accumulate(j, slot, hh):
        st = st_buf[slot, hh]
        m_prev = m_ref[hh]
        m_new = jnp.maximum(m_prev, mx_buf[slot, hh])
        alpha = jnp.exp2(m_prev - m_new)
        p = jnp.exp2(st - m_new)
        vt = vt_ref[0, j, hh * V_ROWS:(hh + 1) * V_ROWS, :]
        acc_ref[hh] = alpha * acc_ref[hh] + _dot(vt, p.astype(jnp.bfloat16))
        m_ref[hh] = m_new

    ahead = LOGITS_AHEAD
    assert 1 <= ahead <= heads
    for hh in range(heads):
        logits(i, 0, hh, True)
        if hh >= ahead:
            accumulate(i, 0, hh - ahead)

    def before(t):
        return jnp.where(t == j0, i, t - 1)

    def step(t, slot):
        for hh in range(heads):
            logits(t, slot, hh, False)
            if hh >= ahead:
                accumulate(t, slot, hh - ahead)
            else:
                accumulate(before(t), 1 - slot, hh - ahead + heads)

    def drain(t, slot):
        for hh in range(heads - ahead, heads):
            accumulate(t, slot, hh)

    def body(u, carry):
        step(j0 + 2 * u, 1)
        step(j0 + 2 * u + 1, 0)
        return carry

    n_chunks = i - j0
    lax.fori_loop(0, n_chunks // 2, body, 0)

    @pl.when(n_chunks % 2 == 1)
    def _():
        step(i - 1, 1)
        drain(i - 1, 1)

    @pl.when(n_chunks % 2 == 0)
    def _():
        drain(before(i), 0)

    outs = [acc_ref[hh, :HEAD_DIM] / acc_ref[hh, HEAD_DIM:HEAD_DIM + 1] for hh in range(heads)]
    ot = jnp.concatenate(outs, axis=0)
    o_ref[0] = ot.T.astype(o_ref.dtype)


def _decay_head_order(b_forget):
    return jnp.argsort(b_forget).astype(jnp.int32)


def _first_needed_chunk(stats):
    kn = jnp.sqrt(stats[:, :, 0, :N_HEADS])
    qn = jnp.sqrt(stats[:, :, 1, :N_HEADS])
    c_first = stats[:, :, 2, 0:N_SPLIT * N_HEADS:N_SPLIT]
    c_last = stats[:, :, 3, 0:N_SPLIT * N_HEADS:N_SPLIT]
    n = stats.shape[1]
    dot_bound = NORM_SLACK * qn[:, :, None, :] * kn[:, None, :, :] + 1.0
    upper = dot_bound + c_first[:, :, None, :] - c_last[:, None, :, :]
    self_bound = NORM_SLACK * qn * kn + 1.0
    ii = jnp.arange(n, dtype=jnp.int32)[None, :, None, None]
    jj = jnp.arange(n, dtype=jnp.int32)[None, None, :, None]
    needed = (upper + self_bound[:, :, None, :] >= -SKIP_MARGIN) & (jj < ii)
    first = jnp.min(jnp.where(needed, jj, ii), axis=2)
    g = HEADS_PER_STEP
    first = jnp.min(first.reshape(first.shape[0], n, N_HEADS // g, g), axis=3)
    return jnp.transpose(first, (0, 2, 1)).reshape(-1).astype(jnp.int32)


def _attention(qt, qaug, k, kaug, vt, first_chunk):
    B, _, S = qt.shape
    tq, tk = Q_TILE, KV_TILE
    assert tq == tk, "the diagonal is handled as a single masked key chunk"
    pair = HEADS_PER_STEP
    n_pairs = N_HEADS // pair
    pw = pair * HEAD_DIM
    causal = np.where(np.arange(tk)[:, None] <= np.arange(tq)[None, :], 0.0, MASK_VALUE)
    grid_spec = pltpu.PrefetchScalarGridSpec(
        num_scalar_prefetch=1,
        grid=(B, n_pairs, S // tq),
        in_specs=[
            pl.BlockSpec((1, pw, tq), lambda b, p, i, j0: (b, p, i)),
            pl.BlockSpec((1, LANES, tq), lambda b, p, i, j0: (b, 0, i)),
            pl.BlockSpec((1, S, pw), lambda b, p, i, j0: (b, 0, p)),
            pl.BlockSpec((1, S, LANES), lambda b, p, i, j0: (b, 0, 0)),
            pl.BlockSpec((1, S // tk, pair * V_ROWS, tk), lambda b, p, i, j0: (b, 0, p, 0)),
            pl.BlockSpec((tk, tq), lambda b, p, i, j0: (0, 0)),
        ],
        out_specs=pl.BlockSpec((1, tq, pw), lambda b, p, i, j0: (b, i, p)),
        scratch_shapes=[
            pltpu.VMEM((pair, 2 * HEAD_DIM + LANES, tq), jnp.bfloat16),
            pltpu.VMEM((2, pair, tk, tq), jnp.float32),
            pltpu.VMEM((2, pair, 1, tq), jnp.float32),
            pltpu.VMEM((pair, 1, tq), jnp.float32),
            pltpu.VMEM((pair, V_ROWS, tq), jnp.float32),
        ],
    )
    return pl.pallas_call(
        _attn_kernel,
        grid_spec=grid_spec,
        out_shape=jax.ShapeDtypeStruct((B, S, ATTN_WIDTH), jnp.bfloat16),
        compiler_params=pltpu.CompilerParams(
            dimension_semantics=("arbitrary", "arbitrary", "arbitrary"),
            vmem_limit_bytes=VMEM_LIMIT),
        name="forgetting_attention",
    )(first_chunk, qt, qaug, k, kaug, vt, jnp.asarray(causal, jnp.float32))


def _outproj_kernel(yp_ref, ya_ref, x_ref, wo1_ref, wo2_ref, g_ref, wr2_ref, wrhi_ref, br_ref,
                    upper_ref, x1_ref, h2_ref, idx_ref, gate_ref, rank_ref, cnt_ref, carry):
    t = pl.program_id(0)
    tm = x_ref.shape[0]

    @pl.when(t == 0)
    def _():
        carry[...] = jnp.zeros_like(carry)

    hm = upper_ref.shape[0]
    parts = [slice(r * hm, (r + 1) * hm) for r in range(tm // hm)]

    def project(rows):
        x1 = x_ref[rows] + _dot(yp_ref[rows], wo1_ref[...]) + _dot(ya_ref[rows], wo2_ref[...])
        x1_ref[rows] = x1
        return x1

    def route(rows, x1):
        h2 = _rms_norm(x1, g_ref[...])
        h2_hi = h2.astype(jnp.bfloat16)
        h2_lo = (h2 - h2_hi.astype(jnp.float32)).astype(jnp.bfloat16)
        h2_ref[rows] = _pack_bf16_pairs(h2)

        lg2 = _dot_nt(wr2_ref[...], h2_hi)
        logits = (lg2[:N_EXPERTS] + lg2[N_EXPERTS:] + _dot_nt(wrhi_ref[...], h2_lo)
                  + br_ref[...])

        eio = lax.broadcasted_iota(jnp.int32, (N_EXPERTS, hm), 0)
        vals, idxs = [], []
        cur = logits
        for _ in range(TOP_K):
            m = jnp.max(cur, axis=0, keepdims=True)
            ix = jnp.min(jnp.where(cur == m, eio, N_EXPERTS), axis=0, keepdims=True)
            vals.append(m)
            idxs.append(ix)
            cur = jnp.where(eio == ix, -jnp.inf, cur)
        exps = [jnp.exp(v - vals[0]) for v in vals]
        denom = exps[0] + exps[1] + exps[2] + exps[3]
        hits = [eio == ix for ix in idxs]
        cnt = jnp.zeros((N_EXPERTS, hm), jnp.float32)
        for hit in hits:
            cnt = cnt + hit.astype(jnp.float32)
        base = _dot(cnt.astype(jnp.bfloat16), upper_ref[...]) + carry[...]
        for kk in range(TOP_K):
            idx_ref[kk:kk + 1, rows] = idxs[kk]
            gate_ref[kk:kk + 1, rows] = exps[kk] / denom
            rank = jnp.sum(jnp.where(hits[kk], base, 0.0), axis=0, keepdims=True)
            rank_ref[kk:kk + 1, rows] = rank.astype(jnp.int32)
        carry[...] = carry[...] + jnp.sum(cnt, axis=1, keepdims=True)

    pending = project(parts[0])
    for r, rows in enumerate(parts):
        ahead = project(parts[r + 1]) if r + 1 < len(parts) else None
        route(rows, pending)
        pending = ahead
    cnt_ref[...] = carry[...]


def _out_projection(ypool, yattn, x, w_out, g_ffn, w_router, b_router, head_order):
    T, D = x.shape
    tm = ROUTE_TILE
    bf16 = jnp.bfloat16
    w_attn = jnp.take(w_out[POOL_WIDTH:].reshape(N_HEADS, HEAD_DIM, D), head_order, axis=0)
    wo = jnp.concatenate([w_out[:POOL_WIDTH], w_attn.reshape(ATTN_WIDTH, D)], axis=0).astype(bf16)
    wr_t = w_router.T
    wr_hi = wr_t.astype(bf16)
    wr_lo = (wr_t - wr_hi.astype(jnp.float32)).astype(bf16)
    wr2 = jnp.concatenate([wr_hi, wr_lo], axis=0)
    r = np.arange(ROUTE_PART)
    upper = jnp.asarray((r[:, None] < r[None, :]).astype(np.float32), bf16)
    full = lambda *shape: pl.BlockSpec(shape, lambda t: (0,) * len(shape))
    row = lambda w: pl.BlockSpec((tm, w), lambda t: (t, 0))
    col = pl.BlockSpec((TOP_K, tm), lambda t: (0, t))
    return pl.pallas_call(
        _outproj_kernel,
        grid=(T // tm,),
        in_specs=[row(POOL_WIDTH), row(ATTN_WIDTH), row(D), full(POOL_WIDTH, D),
                  full(ATTN_WIDTH, D), full(1, D), full(2 * N_EXPERTS, D), full(N_EXPERTS, D),
                  full(N_EXPERTS, 1), full(ROUTE_PART, ROUTE_PART)],
        out_specs=[row(D), row(D // 2), col, col, col, full(N_EXPERTS, 1)],
        out_shape=[
            jax.ShapeDtypeStruct((T, D), jnp.float32),
            jax.ShapeDtypeStruct((T, D // 2), jnp.uint32),
            jax.ShapeDtypeStruct((TOP_K, T), jnp.int32),
            jax.ShapeDtypeStruct((TOP_K, T), jnp.float32),
            jax.ShapeDtypeStruct((TOP_K, T), jnp.int32),
            jax.ShapeDtypeStruct((N_EXPERTS, 1), jnp.float32),
        ],
        scratch_shapes=[pltpu.VMEM((N_EXPERTS, 1), jnp.float32)],
        compiler_params=pltpu.CompilerParams(
            dimension_semantics=("arbitrary",), vmem_limit_bytes=VMEM_LIMIT),
        name="out_projection_router",
    )(ypool, yattn, x, wo[:POOL_WIDTH], wo[POOL_WIDTH:], g_ffn.reshape(1, D), wr2, wr_hi,
      b_router.reshape(N_EXPERTS, 1), upper)


def _sc_mesh():
    return plsc.VectorSubcoreMesh(core_axis_name="c", subcore_axis_name="s")


def _dispatch_rows(h2, dest, n_rows):
    T, D = h2.shape
    per_worker = T // SC_WORKERS
    steps = per_worker // SC_WINDOW

    @functools.partial(
        pl.kernel, mesh=_sc_mesh(),
        out_type=jax.ShapeDtypeStruct((n_rows, D), h2.dtype),
        scratch_types=[pltpu.VMEM((TOP_K, SC_WINDOW), jnp.int32),
                       pltpu.VMEM((SC_WINDOW, D), h2.dtype),
                       pltpu.SemaphoreType.DMA],
    )
    def scatter_kernel(h2_hbm, dest_hbm, xs_hbm, idx_v, rows_v, sem):
        wid = lax.axis_index("s") * SC_CORES + lax.axis_index("c")

        @pl.loop(0, steps)
        def _(step):
            base = wid * per_worker + step * SC_WINDOW
            pltpu.sync_copy(h2_hbm.at[pl.ds(base, SC_WINDOW)], rows_v)
            pltpu.sync_copy(dest_hbm.at[:, pl.ds(base, SC_WINDOW)], idx_v)
            copies = [pltpu.async_copy(rows_v, xs_hbm.at[idx_v.at[kk]], sem)
                      for kk in range(TOP_K)]
            for c in copies:
                c.wait()

    return scatter_kernel(h2, dest)


def _gather_rows(table, idx):
    N = idx.shape[0]
    D = table.shape[1]
    per_worker = N // SC_WORKERS
    steps = per_worker // SC_WINDOW

    @functools.partial(
        pl.kernel, mesh=_sc_mesh(),
        out_type=jax.ShapeDtypeStruct((N, D), table.dtype),
        scratch_types=[pltpu.VMEM((SC_WINDOW,), jnp.int32),
                       pltpu.VMEM((SC_WINDOW, D), table.dtype),
                       pltpu.SemaphoreType.DMA],
    )
    def gather_kernel(table_hbm, idx_hbm, out_hbm, idx_v, rows_v, sem):
        wid = lax.axis_index("s") * SC_CORES + lax.axis_index("c")

        @pl.loop(0, steps)
        def _(step):
            base = wid * per_worker + step * SC_WINDOW
            pltpu.sync_copy(idx_hbm.at[pl.ds(base, SC_WINDOW)], idx_v)
            pltpu.async_copy(table_hbm.at[idx_v], rows_v, sem).wait()
            pltpu.sync_copy(rows_v, out_hbm.at[pl.ds(base, SC_WINDOW)])

    return gather_kernel(table, idx)


def _expert_kernel(be_ref, nvalid_ref, x_ref, wup_ref, bup_ref, wdn_ref, bdn_ref, y_ref,
                   wup_bf, wdn_bf):
    i = pl.program_id(0)
    blk = x_ref.shape[0]
    changed = jnp.logical_or(i == 0, be_ref[i] != be_ref[jnp.maximum(i - 1, 0)])

    @pl.when(changed)
    def _():
        wup_bf[...] = wup_ref[0].astype(jnp.bfloat16)
        wdn_bf[...] = wdn_ref[0].astype(jnp.bfloat16)

    nvalid = nvalid_ref[i]

    @pl.when(nvalid > 0)
    def _():
        row = lax.broadcasted_iota(jnp.int32, (blk, 1), 0)
        x_hi, x_lo = _unpack_bf16_pairs(jnp.where(row < nvalid, x_ref[...], jnp.uint32(0)))
        x = jnp.concatenate([x_hi.astype(jnp.bfloat16), x_lo.astype(jnp.bfloat16)], axis=1)
        glu = _dot(x, wup_bf[:, :D_EXPERT]) + bup_ref[0, :, :D_EXPERT]
        lin = _dot(x, wup_bf[:, D_EXPERT:]) + bup_ref[0, :, D_EXPERT:]
        glu = jnp.minimum(glu, SWIGLU_LIMIT)
        lin = jnp.clip(lin, -SWIGLU_LIMIT, SWIGLU_LIMIT)
        act = glu * (1.0 / (1.0 + jnp.exp(-SWIGLU_ALPHA * glu))) * (lin + 1.0)
        y = _dot(act.astype(jnp.bfloat16), wdn_bf[...]) + bdn_ref[0]
        y_ref[...] = _pack_bf16_pairs(y)

    @pl.when(nvalid <= 0)
    def _():
        y_ref[...] = jnp.zeros_like(y_ref)


def _expert_ffn(xs, block_e, block_valid, w_up, b_up, w_down, b_down):
    n_rows = xs.shape[0]
    D = D_MODEL
    blk = EXPERT_ROWS
    grid_spec = pltpu.PrefetchScalarGridSpec(
        num_scalar_prefetch=2,
        grid=(n_rows // blk,),
        in_specs=[
            pl.BlockSpec((blk, D // 2), lambda i, be, nv: (i, 0)),
            pl.BlockSpec((1, D, 2 * D_EXPERT), lambda i, be, nv: (be[i], 0, 0)),
            pl.BlockSpec((1, 1, 2 * D_EXPERT), lambda i, be, nv: (be[i], 0, 0)),
            pl.BlockSpec((1, D_EXPERT, D), lambda i, be, nv: (be[i], 0, 0)),
            pl.BlockSpec((1, 1, D), lambda i, be, nv: (be[i], 0, 0)),
        ],
        out_specs=pl.BlockSpec((blk, D // 2), lambda i, be, nv: (i, 0)),
        scratch_shapes=[pltpu.VMEM((D, 2 * D_EXPERT), jnp.bfloat16),
                        pltpu.VMEM((D_EXPERT, D), jnp.bfloat16)],
    )
    return pl.pallas_call(
        _expert_kernel,
        grid_spec=grid_spec,
        out_shape=jax.ShapeDtypeStruct((n_rows, D // 2), jnp.uint32),
        compiler_params=pltpu.CompilerParams(
            dimension_semantics=("arbitrary",), vmem_limit_bytes=VMEM_LIMIT),
        name="expert_ffn",
    )(block_e, block_valid, xs, w_up, b_up.reshape(N_EXPERTS, 1, 2 * D_EXPERT), w_down,
      b_down.reshape(N_EXPERTS, 1, D))


def _final_kernel(x1_ref, rows_ref, gate_ref, g_ref, o_ref):
    half = x1_ref.shape[1] // 2
    acc_hi = x1_ref[:, :half]
    acc_lo = x1_ref[:, half:]
    for kk in range(TOP_K):
        y_hi, y_lo = _unpack_bf16_pairs(rows_ref[kk])
        gate = gate_ref[:, kk:kk + 1]
        acc_hi = acc_hi + y_hi * gate
        acc_lo = acc_lo + y_lo * gate
    o_ref[...] = _rms_norm(jnp.concatenate([acc_hi, acc_lo], axis=1), g_ref[...])


def _final_alias_kernel(x1_ref, rows_ref, gate_ref, g_ref, prev_ref, o_ref):
    del prev_ref
    _final_kernel(x1_ref, rows_ref, gate_ref, g_ref, o_ref)


def _final_norm(x1, rows, gates_t, g_final, part, prev_out):
    T, D = x1.shape
    Tp = rows.shape[1]
    tm = SEQ_TILE
    t0 = part * (Tp // tm)
    row = pl.BlockSpec((tm, D), lambda t: (t + t0, 0))
    in_specs = [row, pl.BlockSpec((TOP_K, tm, D // 2), lambda t: (0, t, 0)),
                pl.BlockSpec((tm, TOP_K), lambda t: (t + t0, 0)),
                pl.BlockSpec((1, D), lambda t: (0, 0))]
    args = [x1, rows, gates_t, g_final.reshape(1, D)]
    body, aliases = _final_kernel, {}
    if prev_out is not None:
        in_specs.append(pl.BlockSpec(memory_space=pl.ANY))
        args.append(prev_out)
        body, aliases = _final_alias_kernel, {len(args) - 1: 0}
    return pl.pallas_call(
        body,
        grid=(Tp // tm,),
        in_specs=in_specs,
        out_specs=row,
        out_shape=jax.ShapeDtypeStruct((T, D), jnp.float32),
        input_output_aliases=aliases,
        compiler_params=pltpu.CompilerParams(dimension_semantics=("arbitrary",)),
        name="combine_final_norm",
    )(*args)


def kernel(x, g_mix, w_in, b_forget, w_pool, pool_scale, w_out, g_ffn, w_router, b_router,
           w_up, b_up, w_down, b_down, g_final):
    B, S, D = x.shape
    T = B * S
    head_order = _decay_head_order(b_forget[0])
    ypool, qt, qaug, k, kaug, vt, stats = _in_projection(x, g_mix[0], w_in[0], b_forget[0],
                                                        w_pool[0], pool_scale[0], head_order)
    yattn = _attention(qt, qaug, k, kaug, vt, _first_needed_chunk(stats))
    x1, h2, top_idx, gates, rank, counts = _out_projection(
        ypool.reshape(T, POOL_WIDTH), yattn.reshape(T, ATTN_WIDTH), x.reshape(T, D),
        w_out[0], g_ffn[0], w_router[0], b_router[0], head_order)

    blk = EXPERT_ROWS
    counts = counts[:, 0].astype(jnp.int32)
    padded = (counts + blk - 1) // blk * blk
    pad_ends = jnp.cumsum(padded)
    pad_starts = pad_ends - padded
    experts = jnp.arange(N_EXPERTS, dtype=jnp.int32)[:, None, None]
    dest = rank + jnp.sum(jnp.where(top_idx[None] == experts, pad_starts[:, None, None], 0),
                          axis=0)
    n_rows = T * TOP_K + N_EXPERTS * blk
    block_row0 = jnp.arange(n_rows // blk, dtype=jnp.int32) * blk
    block_e = jnp.sum((pad_ends[None, :] <= block_row0[:, None]).astype(jnp.int32), axis=1)
    block_e = jnp.minimum(block_e, N_EXPERTS - 1)
    of_block = block_e[:, None] == jnp.arange(N_EXPERTS, dtype=jnp.int32)[None, :]
    block_end = jnp.sum(jnp.where(of_block, (pad_starts + counts)[None, :], 0), axis=1)
    block_valid = jnp.clip(block_end - block_row0, 0, blk)
    block_valid = jnp.where(block_row0 < pad_ends[-1], block_valid, 0).astype(jnp.int32)

    xs = _dispatch_rows(h2, dest, n_rows)
    ys = _expert_ffn(xs, block_e, block_valid, w_up[0], b_up[0], w_down[0], b_down[0])
    gates_t = gates.T
    tp = T // COMBINE_PARTS
    out = None
    for part in range(COMBINE_PARTS):
        part_dest = dest[:, part * tp:(part + 1) * tp].reshape(-1)
        rows = _gather_rows(ys, part_dest).reshape(TOP_K, tp, D // 2)
        out = _final_norm(x1, rows, gates_t, g_final, part, out)
    return out.reshape(B, S, D)
```

```python
import functools

import jax
import jax.numpy as jnp
import numpy as np
from jax import lax
from jax.experimental import pallas as pl
from jax.experimental.pallas import tpu as pltpu
from jax.experimental.pallas import tpu_sc as plsc

D_MODEL = 1024
POOL_WIDTH = 512
POOL_WINDOWS = (2, 4, 8, 16)
POOL_GROUP = 128
MAX_WINDOW = 16
ATTN_WIDTH = 512
HEAD_DIM = 64
N_HEADS = 8
N_EXPERTS = 32
TOP_K = 4
D_EXPERT = 1024
SWIGLU_LIMIT = 7.0
SWIGLU_ALPHA = 1.702
RMS_EPS = 1e-5

LANES = 128
SUBLANES = 8
AUG_GROUP = 16
N_SPLIT = 3
MASK_VALUE = -1e30
LOG2_E = 1.4426950408889634
SKIP_MARGIN = 152.0
BOUND_SLACK = 0.5
NORM_SLACK = 1.02
V_ROWS = 80

SEQ_TILE = 512
Q_TILE = 512
HEADS_PER_STEP = 4
LOGITS_AHEAD = 2
KV_TILE = SEQ_TILE
ROUTE_TILE = 1024
ROUTE_PART = 512
EXPERT_ROWS = 512
COMBINE_PARTS = 8
SC_CORES = 2
SC_SUBCORES = 16
SC_WORKERS = SC_CORES * SC_SUBCORES
SC_WINDOW = 128
VMEM_LIMIT = 56 * 1024 * 1024

_NT = (((1,), (1,)), ((), ()))


def _dot(a, b):
    return jnp.dot(a, b, preferred_element_type=jnp.float32)


def _dot_nt(a, b):
    return lax.dot_general(a, b, _NT, preferred_element_type=jnp.float32)


def _rms_norm(x, g):
    return x * lax.rsqrt(jnp.mean(x * x, axis=-1, keepdims=True) + RMS_EPS) * g


def _pack_bf16_pairs(x):
    n = x.shape[1] // 2
    bits = lax.bitcast_convert_type(x.astype(jnp.bfloat16).astype(jnp.float32), jnp.uint32)
    return bits[:, :n] | (bits[:, n:] >> 16)


def _unpack_bf16_pairs(p):
    hi = lax.bitcast_convert_type(p & jnp.uint32(0xFFFF0000), jnp.float32)
    lo = lax.bitcast_convert_type(p << 16, jnp.float32)
    return hi, lo


def _split3(x):
    hi = x.astype(jnp.bfloat16)
    r1 = x - hi.astype(jnp.float32)
    mid = r1.astype(jnp.bfloat16)
    lo = (r1 - mid.astype(jnp.float32)).astype(jnp.bfloat16)
    return hi, mid, lo


def _inproj_kernel(x_ref, g_ref, wu_ref, wqt_ref, wk_ref, wvt_ref, wf_ref, bf_ref, tri_ref,
                   pqt_ref, pk_ref, oneq_ref, onek_ref, hsel_ref, wpool_ref, pscale_ref,
                   ypool_ref, qt_ref, qaug_ref, k_ref, kaug_ref, vt_ref, stats_ref, ubuf, ccarry):
    s = pl.program_id(1)
    tm = x_ref.shape[1]

    @pl.when(s == 0)
    def _():
        ubuf[...] = jnp.zeros_like(ubuf)
        ccarry[...] = jnp.zeros_like(ccarry)

    x = x_ref[0]
    h = _rms_norm(x, g_ref[...]).astype(jnp.bfloat16)
    fl = _dot(h, wf_ref[...]) + bf_ref[...]
    u = _dot(h, wu_ref[...])
    kp = _dot(h, wk_ref[...])
    qt = _dot_nt(wqt_ref[...], h)
    vt = _dot_nt(wvt_ref[...], h).astype(jnp.bfloat16)
    k_ref[0] = kp.astype(jnp.bfloat16)
    qt_ref[0] = qt.astype(jnp.bfloat16)

    logf = jnp.minimum(fl, 0.0) - jnp.log1p(jnp.exp(-jnp.abs(fl)))
    lane = lax.broadcasted_iota(jnp.int32, (tm, LANES), 1)
    logf = jnp.where(lane < N_SPLIT * N_HEADS, logf, 0.0)
    f_hi, f_mid, f_lo = _split3(logf)
    tri = tri_ref[...]
    c = _dot(tri, f_hi) + _dot(tri, f_mid) + _dot(tri, f_lo) + ccarry[...]
    ccarry[...] = c[tm - 1:tm, :]
    c2 = c * LOG2_E
    c_hi, c_mid, c_lo = _split3(c2)
    term = lane % N_SPLIT
    c3 = jnp.where(term == 0, c_hi, jnp.where(term == 1, c_mid, c_lo))
    kaug_ref[0] = (_dot(c3, pk_ref[...]) + onek_ref[...]).astype(jnp.bfloat16)
    qaug_ref[0] = (_dot_nt(pqt_ref[...], c3) + oneq_ref[...]).astype(jnp.bfloat16)
    lane_row = lax.broadcasted_iota(jnp.int32, (1, LANES), 1)
    kn2 = jnp.max(_dot((kp * kp).astype(jnp.bfloat16), hsel_ref[...]), axis=0, keepdims=True)
    qn2 = jnp.zeros((1, LANES), jnp.float32)
    for hd in range(N_HEADS):
        qs = qt[hd * HEAD_DIM:(hd + 1) * HEAD_DIM, :]
        qmax = jnp.max(jnp.sum(qs * qs, axis=0, keepdims=True), axis=1, keepdims=True)
        qn2 = jnp.where(lane_row == hd, qmax, qn2)
    stats_ref[0, 0, 0:1, :] = kn2
    stats_ref[0, 0, 1:2, :] = qn2
    stats_ref[0, 0, 2:3, :] = c2[0:1, :]
    stats_ref[0, 0, 3:4, :] = c2[tm - 1:tm, :]
    stats_ref[0, 0, 4:SUBLANES, :] = jnp.zeros((SUBLANES - 4, LANES), jnp.float32)
    ones = jnp.ones((V_ROWS - HEAD_DIM, tm), jnp.bfloat16)
    for hd in range(N_HEADS):
        vt_ref[0, 0, hd * V_ROWS:hd * V_ROWS + HEAD_DIM, :] = vt[hd * HEAD_DIM:(hd + 1) * HEAD_DIM]
        vt_ref[0, 0, hd * V_ROWS + HEAD_DIM:(hd + 1) * V_ROWS, :] = ones

    head_pos = s * tm + lax.broadcasted_iota(jnp.int32, (MAX_WINDOW, 1), 0)
    for gi, w in enumerate(POOL_WINDOWS):
        cols = slice(gi * POOL_GROUP, (gi + 1) * POOL_GROUP)
        ug = u[:, cols]
        win = jnp.concatenate([ubuf[:, cols], ug], axis=0)
        span = 1
        while span < w:
            win = win + pltpu.roll(win, span, axis=0)
            span *= 2
        acc = win[MAX_WINDOW:]
        inv_head = 1.0 / jnp.minimum(head_pos + 1, w).astype(jnp.float32)
        p = jnp.concatenate([acc[:MAX_WINDOW] * inv_head, acc[MAX_WINDOW:] * (1.0 / w)],
                            axis=0) - ug
        y = _dot(p.astype(jnp.bfloat16), wpool_ref[gi]) * pscale_ref[:, cols]
        ypool_ref[0, :, cols] = y.astype(jnp.bfloat16)
    ubuf[...] = u[tm - MAX_WINDOW:]


def _in_projection(x, g_mix, w_in, b_forget, w_pool, pool_scale, head_order):
    B, S, D = x.shape
    tm = SEQ_TILE
    bf16 = jnp.bfloat16
    o = POOL_WIDTH
    w_u = w_in[:, :o].astype(bf16)
    w_q = w_in[:, o:o + ATTN_WIDTH] * (HEAD_DIM ** -0.5 * LOG2_E)
    w_k = w_in[:, o + ATTN_WIDTH:o + 2 * ATTN_WIDTH]
    w_v = w_in[:, o + 2 * ATTN_WIDTH:o + 3 * ATTN_WIDTH]
    w_f = w_in[:, o + 3 * ATTN_WIDTH:]
    by_head = lambda w: jnp.take(w.reshape(D, N_HEADS, HEAD_DIM), head_order,
                                 axis=1).reshape(D, ATTN_WIDTH)
    w_q, w_k, w_v = by_head(w_q), by_head(w_k), by_head(w_v)
    w_f = jnp.take(w_f, head_order, axis=1)
    b_forget = jnp.take(b_forget, head_order)

    wqt = w_q.T.astype(bf16)
    wk = w_k.astype(bf16)
    wvt = w_v.T.astype(bf16)
    n_gate = N_SPLIT * N_HEADS
    wf = jnp.pad(jnp.repeat(w_f, N_SPLIT, axis=1), ((0, 0), (0, LANES - n_gate))).astype(bf16)
    bfp = jnp.pad(jnp.repeat(b_forget, N_SPLIT), (0, LANES - n_gate)).reshape(1, LANES)

    assert N_HEADS * AUG_GROUP == LANES and 2 * N_SPLIT <= AUG_GROUP
    src = np.arange(LANES)
    head, term = src // N_SPLIT, src % N_SPLIT
    dst = np.arange(LANES)
    valid = (src < n_gate)[:, None]
    q_hit = valid & (dst[None, :] == (head * AUG_GROUP + term)[:, None])
    k_hit = valid & (dst[None, :] == (head * AUG_GROUP + N_SPLIT + term)[:, None])
    pqt = jnp.asarray(q_hit.T.astype(np.float32), bf16)
    pk = jnp.asarray(-k_hit.astype(np.float32), bf16)
    slot_pos = dst % AUG_GROUP
    oneq = ((slot_pos >= N_SPLIT) & (slot_pos < 2 * N_SPLIT))
    onek = slot_pos < N_SPLIT
    oneq = jnp.asarray(oneq.astype(np.float32).reshape(-1, 1))
    onek = jnp.asarray(onek.astype(np.float32).reshape(1, -1))
    r = np.arange(tm)
    tri = jnp.asarray((r[None, :] <= r[:, None]).astype(np.float32), bf16)
    hsel = jnp.asarray((np.arange(ATTN_WIDTH)[:, None] // HEAD_DIM
                        == np.arange(LANES)[None, :]).astype(np.float32), bf16)

    full = lambda *shape: pl.BlockSpec(shape, lambda b, s: (0,) * len(shape))
    aw = ATTN_WIDTH
    return pl.pallas_call(
        _inproj_kernel,
        grid=(B, S // tm),
        in_specs=[
            pl.BlockSpec((1, tm, D), lambda b, s: (b, s, 0)),
            full(1, D), full(D, o), full(aw, D), full(D, aw), full(aw, D),
            full(D, LANES), full(1, LANES), full(tm, tm), full(LANES, LANES), full(LANES, LANES),
            full(LANES, 1), full(1, LANES), full(aw, LANES),
            full(len(POOL_WINDOWS), POOL_GROUP, POOL_GROUP),
            full(1, o),
        ],
        out_specs=[
            pl.BlockSpec((1, tm, o), lambda b, s: (b, s, 0)),
            pl.BlockSpec((1, aw, tm), lambda b, s: (b, 0, s)),
            pl.BlockSpec((1, LANES, tm), lambda b, s: (b, 0, s)),
            pl.BlockSpec((1, tm, aw), lambda b, s: (b, s, 0)),
            pl.BlockSpec((1, tm, LANES), lambda b, s: (b, s, 0)),
            pl.BlockSpec((1, 1, N_HEADS * V_ROWS, tm), lambda b, s: (b, s, 0, 0)),
            pl.BlockSpec((1, 1, SUBLANES, LANES), lambda b, s: (b, s, 0, 0)),
        ],
        out_shape=[
            jax.ShapeDtypeStruct((B, S, o), bf16),
            jax.ShapeDtypeStruct((B, aw, S), bf16),
            jax.ShapeDtypeStruct((B, LANES, S), bf16),
            jax.ShapeDtypeStruct((B, S, aw), bf16),
            jax.ShapeDtypeStruct((B, S, LANES), bf16),
            jax.ShapeDtypeStruct((B, S // tm, N_HEADS * V_ROWS, tm), bf16),
            jax.ShapeDtypeStruct((B, S // tm, SUBLANES, LANES), jnp.float32),
        ],
        scratch_shapes=[
            pltpu.VMEM((MAX_WINDOW, o), jnp.float32),
            pltpu.VMEM((1, LANES), jnp.float32),
        ],
        compiler_params=pltpu.CompilerParams(
            dimension_semantics=("arbitrary", "arbitrary"), vmem_limit_bytes=VMEM_LIMIT),
        name="in_projection",
    )(x, g_mix.reshape(1, D), w_u, wqt, wk, wvt, wf, bfp, tri, pqt, pk, oneq, onek, hsel,
      w_pool.astype(bf16), pool_scale.reshape(1, o))


def _attn_kernel(j0_ref, qt_ref, qaug_ref, k_ref, kaug_ref, vt_ref, causal_ref, o_ref,
                 qop_ref, st_buf, mx_buf, m_ref, acc_ref):
    i = pl.program_id(2)
    tile = (pl.program_id(0) * pl.num_programs(1) + pl.program_id(1)) * pl.num_programs(2) + i
    j0 = j0_ref[tile]
    tq = qt_ref.shape[2]
    tk = vt_ref.shape[3]
    heads = qt_ref.shape[1] // HEAD_DIM

    m_ref[...] = jnp.full(m_ref.shape, MASK_VALUE, jnp.float32)
    acc_ref[...] = jnp.zeros_like(acc_ref)

    first_head = pl.program_id(1) * heads
    group = lax.broadcasted_iota(jnp.int32, (LANES, tq), 0) // AUG_GROUP
    q_aug = qaug_ref[0]
    blank = jnp.zeros((HEAD_DIM, tq), q_aug.dtype)
    for hh in range(heads):
        q_own = qt_ref[0, hh * HEAD_DIM:(hh + 1) * HEAD_DIM, :]
        pair_rows = [q_own, blank] if hh % 2 == 0 else [blank, q_own]
        qop_ref[hh, 0:2 * HEAD_DIM, :] = jnp.concatenate(pair_rows, axis=0)
        qop_ref[hh, 2 * HEAD_DIM:, :] = jnp.where(group == first_head + hh, q_aug,
                                                  jnp.zeros_like(q_aug))

    def logits(j, slot, hh, masked):
        row0 = pl.multiple_of(j * tk, tk)
        pair_lanes = slice(hh // 2 * 2 * HEAD_DIM, (hh // 2 + 1) * 2 * HEAD_DIM)
        k = jnp.concatenate([k_ref[0, pl.ds(row0, tk), pair_lanes],
                             kaug_ref[0, pl.ds(row0, tk), :]], axis=1)
        st = _dot(k, qop_ref[hh])
        if masked:
            st = st + causal_ref[...]
        st_buf[slot, hh] = st
        mx_buf[slot, hh] = jnp.max(st, axis=0, keepdims=True)

    def accumulate(j, slot, hh):
        st = st_buf[slot, hh]
        m_prev = m_ref[hh]
        m_new = jnp.maximum(m_prev, mx_buf[slot, hh])
        alpha = jnp.exp2(m_prev - m_new)
        p = jnp.exp2(st - m_new)
        vt = vt_ref[0, j, hh * V_ROWS:(hh + 1) * V_ROWS, :]
        acc_ref[hh] = alpha * acc_ref[hh] + _dot(vt, p.astype(jnp.bfloat16))
        m_ref[hh] = m_new

    ahead = LOGITS_AHEAD
    assert 1 <= ahead <= heads
    for hh in range(heads):
        logits(i, 0, hh, True)
        if hh >= ahead:
            accumulate(i, 0, hh - ahead)

    def before(t):
        return jnp.where(t == j0, i, t - 1)

    def step(t, slot):
        for hh in range(heads):
            logits(t, slot, hh, False)
            if hh >= ahead:
                accumulate(t, slot, hh - ahead)
            else:
                accumulate(before(t), 1 - slot, hh - ahead + heads)

    def drain(t, slot):
        for hh in range(heads - ahead, heads):
            accumulate(t, slot, hh)

    def body(u, carry):
        step(j0 + 2 * u, 1)
        step(j0 + 2 * u + 1, 0)
        return carry

    n_chunks = i - j0
    lax.fori_loop(0, n_chunks // 2, body, 0)

    @pl.when(n_chunks % 2 == 1)
    def _():
        step(i - 1, 1)
        drain(i - 1, 1)

    @pl.when(n_chunks % 2 == 0)
    def _():
        drain(before(i), 0)

    outs = [acc_ref[hh, :HEAD_DIM] / acc_ref[hh, HEAD_DIM:HEAD_DIM + 1] for hh in range(heads)]
    ot = jnp.concatenate(outs, axis=0)
    o_ref[0] = ot.T.astype(o_ref.dtype)


def _decay_head_order(b_forget):
    return jnp.argsort(b_forget).astype(jnp.int32)


def _first_needed_chunk(stats):
    kn = jnp.sqrt(stats[:, :, 0, :N_HEADS])
    qn = jnp.sqrt(stats[:, :, 1, :N_HEADS])
    c_first = stats[:, :, 2, 0:N_SPLIT * N_HEADS:N_SPLIT]
    c_last = stats[:, :, 3, 0:N_SPLIT * N_HEADS:N_SPLIT]
    n = stats.shape[1]
    dot_bound = NORM_SLACK * qn[:, :, None, :] * kn[:, None, :, :] + BOUND_SLACK
    upper = dot_bound + c_first[:, :, None, :] - c_last[:, None, :, :]
    self_bound = NORM_SLACK * qn * kn + BOUND_SLACK
    ii = jnp.arange(n, dtype=jnp.int32)[None, :, None, None]
    jj = jnp.arange(n, dtype=jnp.int32)[None, None, :, None]
    needed = (upper + self_bound[:, :, None, :] >= -SKIP_MARGIN) & (jj < ii)
    first = jnp.min(jnp.where(needed, jj, ii), axis=2)
    g = HEADS_PER_STEP
    first = jnp.min(first.reshape(first.shape[0], n, N_HEADS // g, g), axis=3)
    return jnp.transpose(first, (0, 2, 1)).reshape(-1).astype(jnp.int32)


def _attention(qt, qaug, k, kaug, vt, first_chunk):
    B, _, S = qt.shape
    tq, tk = Q_TILE, KV_TILE
    assert tq == tk, "the diagonal is handled as a single masked key chunk"
    pair = HEADS_PER_STEP
    n_pairs = N_HEADS // pair
    pw = pair * HEAD_DIM
    causal = np.where(np.arange(tk)[:, None] <= np.arange(tq)[None, :], 0.0, MASK_VALUE)
    grid_spec = pltpu.PrefetchScalarGridSpec(
        num_scalar_prefetch=1,
        grid=(B, n_pairs, S // tq),
        in_specs=[
            pl.BlockSpec((1, pw, tq), lambda b, p, i, j0: (b, p, i)),
            pl.BlockSpec((1, LANES, tq), lambda b, p, i, j0: (b, 0, i)),
            pl.BlockSpec((1, S, pw), lambda b, p, i, j0: (b, 0, p)),
            pl.BlockSpec((1, S, LANES), lambda b, p, i, j0: (b, 0, 0)),
            pl.BlockSpec((1, S // tk, pair * V_ROWS, tk), lambda b, p, i, j0: (b, 0, p, 0)),
            pl.BlockSpec((tk, tq), lambda b, p, i, j0: (0, 0)),
        ],
        out_specs=pl.BlockSpec((1, tq, pw), lambda b, p, i, j0: (b, i, p)),
        scratch_shapes=[
            pltpu.VMEM((pair, 2 * HEAD_DIM + LANES, tq), jnp.bfloat16),
            pltpu.VMEM((2, pair, tk, tq), jnp.float32),
            pltpu.VMEM((2, pair, 1, tq), jnp.float32),
            pltpu.VMEM((pair, 1, tq), jnp.float32),
            pltpu.VMEM((pair, V_ROWS, tq), jnp.float32),
        ],
    )
    return pl.pallas_call(
        _attn_kernel,
        grid_spec=grid_spec,
        out_shape=jax.ShapeDtypeStruct((B, S, ATTN_WIDTH), jnp.bfloat16),
        compiler_params=pltpu.CompilerParams(
            dimension_semantics=("arbitrary", "arbitrary", "arbitrary"),
            vmem_limit_bytes=VMEM_LIMIT),
        name="forgetting_attention",
    )(first_chunk, qt, qaug, k, kaug, vt, jnp.asarray(causal, jnp.float32))


def _outproj_kernel(yp_ref, ya_ref, x_ref, wo1_ref, wo2_ref, g_ref, wr2_ref, wrhi_ref, br_ref,
                    upper_ref, x1_ref, h2_ref, idx_ref, gate_ref, rank_ref, cnt_ref, carry):
    t = pl.program_id(0)
    tm = x_ref.shape[0]

    @pl.when(t == 0)
    def _():
        carry[...] = jnp.zeros_like(carry)

    hm = upper_ref.shape[0]
    parts = [slice(r * hm, (r + 1) * hm) for r in range(tm // hm)]

    def project(rows):
        x1 = x_ref[rows] + _dot(yp_ref[rows], wo1_ref[...]) + _dot(ya_ref[rows], wo2_ref[...])
        x1_ref[rows] = x1
        return x1

    def route(rows, x1):
        h2 = _rms_norm(x1, g_ref[...])
        h2_hi = h2.astype(jnp.bfloat16)
        h2_lo = (h2 - h2_hi.astype(jnp.float32)).astype(jnp.bfloat16)
        h2_ref[rows] = _pack_bf16_pairs(h2)

        lg2 = _dot_nt(wr2_ref[...], h2_hi)
        logits = (lg2[:N_EXPERTS] + lg2[N_EXPERTS:] + _dot_nt(wrhi_ref[...], h2_lo)
                  + br_ref[...])

        eio = lax.broadcasted_iota(jnp.int32, (N_EXPERTS, hm), 0)
        vals, idxs = [], []
        cur = logits
        for _ in range(TOP_K):
            m = jnp.max(cur, axis=0, keepdims=True)
            ix = jnp.min(jnp.where(cur == m, eio, N_EXPERTS), axis=0, keepdims=True)
            vals.append(m)
            idxs.append(ix)
            cur = jnp.where(eio == ix, -jnp.inf, cur)
        exps = [jnp.exp(v - vals[0]) for v in vals]
        denom = exps[0] + exps[1] + exps[2] + exps[3]
        hits = [eio == ix for ix in idxs]
        cnt = jnp.zeros((N_EXPERTS, hm), jnp.float32)
        for hit in hits:
            cnt = cnt + hit.astype(jnp.float32)
        base = _dot(cnt.astype(jnp.bfloat16), upper_ref[...]) + carry[...]
        for kk in range(TOP_K):
            idx_ref[kk:kk + 1, rows] = idxs[kk]
            gate_ref[kk:kk + 1, rows] = exps[kk] / denom
            rank = jnp.sum(jnp.where(hits[kk], base, 0.0), axis=0, keepdims=True)
            rank_ref[kk:kk + 1, rows] = rank.astype(jnp.int32)
        carry[...] = carry[...] + jnp.sum(cnt, axis=1, keepdims=True)

    pending = project(parts[0])
    for r, rows in enumerate(parts):
        ahead = project(parts[r + 1]) if r + 1 < len(parts) else None
        route(rows, pending)
        pending = ahead
    cnt_ref[...] = carry[...]


def _out_projection(ypool, yattn, x, w_out, g_ffn, w_router, b_router, head_order):
    T, D = x.shape
    tm = ROUTE_TILE
    bf16 = jnp.bfloat16
    w_attn = jnp.take(w_out[POOL_WIDTH:].reshape(N_HEADS, HEAD_DIM, D), head_order, axis=0)
    wo = jnp.concatenate([w_out[:POOL_WIDTH], w_attn.reshape(ATTN_WIDTH, D)], axis=0).astype(bf16)
    wr_t = w_router.T
    wr_hi = wr_t.astype(bf16)
    wr_lo = (wr_t - wr_hi.astype(jnp.float32)).astype(bf16)
    wr2 = jnp.concatenate([wr_hi, wr_lo], axis=0)
    r = np.arange(ROUTE_PART)
    upper = jnp.asarray((r[:, None] < r[None, :]).astype(np.float32), bf16)
    full = lambda *shape: pl.BlockSpec(shape, lambda t: (0,) * len(shape))
    row = lambda w: pl.BlockSpec((tm, w), lambda t: (t, 0))
    col = pl.BlockSpec((TOP_K, tm), lambda t: (0, t))
    return pl.pallas_call(
        _outproj_kernel,
        grid=(T // tm,),
        in_specs=[row(POOL_WIDTH), row(ATTN_WIDTH), row(D), full(POOL_WIDTH, D),
                  full(ATTN_WIDTH, D), full(1, D), full(2 * N_EXPERTS, D), full(N_EXPERTS, D),
                  full(N_EXPERTS, 1), full(ROUTE_PART, ROUTE_PART)],
        out_specs=[row(D), row(D // 2), col, col, col, full(N_EXPERTS, 1)],
        out_shape=[
            jax.ShapeDtypeStruct((T, D), jnp.float32),
            jax.ShapeDtypeStruct((T, D // 2), jnp.uint32),
            jax.ShapeDtypeStruct((TOP_K, T), jnp.int32),
            jax.ShapeDtypeStruct((TOP_K, T), jnp.float32),
            jax.ShapeDtypeStruct((TOP_K, T), jnp.int32),
            jax.ShapeDtypeStruct((N_EXPERTS, 1), jnp.float32),
        ],
        scratch_shapes=[pltpu.VMEM((N_EXPERTS, 1), jnp.float32)],
        compiler_params=pltpu.CompilerParams(
            dimension_semantics=("arbitrary",), vmem_limit_bytes=VMEM_LIMIT),
        name="out_projection_router",
    )(ypool, yattn, x, wo[:POOL_WIDTH], wo[POOL_WIDTH:], g_ffn.reshape(1, D), wr2, wr_hi,
      b_router.reshape(N_EXPERTS, 1), upper)


def _sc_mesh():
    return plsc.VectorSubcoreMesh(core_axis_name="c", subcore_axis_name="s")


def _dispatch_rows(h2, dest, n_rows):
    T, D = h2.shape
    per_worker = T // SC_WORKERS
    steps = per_worker // SC_WINDOW

    @functools.partial(
        pl.kernel, mesh=_sc_mesh(),
        out_type=jax.ShapeDtypeStruct((n_rows, D), h2.dtype),
        scratch_types=[pltpu.VMEM((TOP_K, SC_WINDOW), jnp.int32),
                       pltpu.VMEM((SC_WINDOW, D), h2.dtype),
                       pltpu.SemaphoreType.DMA],
    )
    def scatter_kernel(h2_hbm, dest_hbm, xs_hbm, idx_v, rows_v, sem):
        wid = lax.axis_index("s") * SC_CORES + lax.axis_index("c")

        @pl.loop(0, steps)
        def _(step):
            base = wid * per_worker + step * SC_WINDOW
            pltpu.sync_copy(h2_hbm.at[pl.ds(base, SC_WINDOW)], rows_v)
            pltpu.sync_copy(dest_hbm.at[:, pl.ds(base, SC_WINDOW)], idx_v)
            copies = [pltpu.async_copy(rows_v, xs_hbm.at[idx_v.at[kk]], sem)
                      for kk in range(TOP_K)]
            for c in copies:
                c.wait()

    return scatter_kernel(h2, dest)


def _gather_rows(table, idx):
    N = idx.shape[0]
    D = table.shape[1]
    per_worker = N // SC_WORKERS
    steps = per_worker // SC_WINDOW

    @functools.partial(
        pl.kernel, mesh=_sc_mesh(),
        out_type=jax.ShapeDtypeStruct((N, D), table.dtype),
        scratch_types=[pltpu.VMEM((SC_WINDOW,), jnp.int32),
                       pltpu.VMEM((SC_WINDOW, D), table.dtype),
                       pltpu.SemaphoreType.DMA],
    )
    def gather_kernel(table_hbm, idx_hbm, out_hbm, idx_v, rows_v, sem):
        wid = lax.axis_index("s") * SC_CORES + lax.axis_index("c")

        @pl.loop(0, steps)
        def _(step):
            base = wid * per_worker + step * SC_WINDOW
            pltpu.sync_copy(idx_hbm.at[pl.ds(base, SC_WINDOW)], idx_v)
            pltpu.async_copy(table_hbm.at[idx_v], rows_v, sem).wait()
            pltpu.sync_copy(rows_v, out_hbm.at[pl.ds(base, SC_WINDOW)])

    return gather_kernel(table, idx)


def _expert_kernel(be_ref, nvalid_ref, x_ref, wup_ref, bup_ref, wdn_ref, bdn_ref, y_ref,
                   wup_bf, wdn_bf):
    i = pl.program_id(0)
    blk = x_ref.shape[0]
    changed = jnp.logical_or(i == 0, be_ref[i] != be_ref[jnp.maximum(i - 1, 0)])

    @pl.when(changed)
    def _():
        wup_bf[...] = wup_ref[0].astype(jnp.bfloat16)
        wdn_bf[...] = wdn_ref[0].astype(jnp.bfloat16)

    nvalid = nvalid_ref[i]

    @pl.when(nvalid > 0)
    def _():
        row = lax.broadcasted_iota(jnp.int32, (blk, 1), 0)
        x_hi, x_lo = _unpack_bf16_pairs(jnp.where(row < nvalid, x_ref[...], jnp.uint32(0)))
        x = jnp.concatenate([x_hi.astype(jnp.bfloat16), x_lo.astype(jnp.bfloat16)], axis=1)
        glu = _dot(x, wup_bf[:, :D_EXPERT]) + bup_ref[0, :, :D_EXPERT]
        lin = _dot(x, wup_bf[:, D_EXPERT:]) + bup_ref[0, :, D_EXPERT:]
        glu = jnp.minimum(glu, SWIGLU_LIMIT)
        lin = jnp.clip(lin, -SWIGLU_LIMIT, SWIGLU_LIMIT)
        act = glu * (1.0 / (1.0 + jnp.exp(-SWIGLU_ALPHA * glu))) * (lin + 1.0)
        y = _dot(act.astype(jnp.bfloat16), wdn_bf[...]) + bdn_ref[0]
        y_ref[...] = _pack_bf16_pairs(y)

    @pl.when(nvalid <= 0)
    def _():
        y_ref[...] = jnp.zeros_like(y_ref)


def _expert_ffn(xs, block_e, block_valid, w_up, b_up, w_down, b_down):
    n_rows = xs.shape[0]
    D = D_MODEL
    blk = EXPERT_ROWS
    grid_spec = pltpu.PrefetchScalarGridSpec(
        num_scalar_prefetch=2,
        grid=(n_rows // blk,),
        in_specs=[
            pl.BlockSpec((blk, D // 2), lambda i, be, nv: (i, 0)),
            pl.BlockSpec((1, D, 2 * D_EXPERT), lambda i, be, nv: (be[i], 0, 0)),
            pl.BlockSpec((1, 1, 2 * D_EXPERT), lambda i, be, nv: (be[i], 0, 0)),
            pl.BlockSpec((1, D_EXPERT, D), lambda i, be, nv: (be[i], 0, 0)),
            pl.BlockSpec((1, 1, D), lambda i, be, nv: (be[i], 0, 0)),
        ],
        out_specs=pl.BlockSpec((blk, D // 2), lambda i, be, nv: (i, 0)),
        scratch_shapes=[pltpu.VMEM((D, 2 * D_EXPERT), jnp.bfloat16),
                        pltpu.VMEM((D_EXPERT, D), jnp.bfloat16)],
    )
    return pl.pallas_call(
        _expert_kernel,
        grid_spec=grid_spec,
        out_shape=jax.ShapeDtypeStruct((n_rows, D // 2), jnp.uint32),
        compiler_params=pltpu.CompilerParams(
            dimension_semantics=("arbitrary",), vmem_limit_bytes=VMEM_LIMIT),
        name="expert_ffn",
    )(block_e, block_valid, xs, w_up, b_up.reshape(N_EXPERTS, 1, 2 * D_EXPERT), w_down,
      b_down.reshape(N_EXPERTS, 1, D))


def _final_kernel(x1_ref, rows_ref, gate_ref, g_ref, o_ref):
    half = x1_ref.shape[1] // 2
    acc_hi = x1_ref[:, :half]
    acc_lo = x1_ref[:, half:]
    for kk in range(TOP_K):
        y_hi, y_lo = _unpack_bf16_pairs(rows_ref[kk])
        gate = gate_ref[:, kk:kk + 1]
        acc_hi = acc_hi + y_hi * gate
        acc_lo = acc_lo + y_lo * gate
    o_ref[...] = _rms_norm(jnp.concatenate([acc_hi, acc_lo], axis=1), g_ref[...])


def _final_alias_kernel(x1_ref, rows_ref, gate_ref, g_ref, prev_ref, o_ref):
    del prev_ref
    _final_kernel(x1_ref, rows_ref, gate_ref, g_ref, o_ref)


def _final_norm(x1, rows, gates_t, g_final, part, prev_out):
    T, D = x1.shape
    Tp = rows.shape[1]
    tm = SEQ_TILE
    t0 = part * (Tp // tm)
    row = pl.BlockSpec((tm, D), lambda t: (t + t0, 0))
    in_specs = [row, pl.BlockSpec((TOP_K, tm, D // 2), lambda t: (0, t, 0)),
                pl.BlockSpec((tm, TOP_K), lambda t: (t + t0, 0)),
                pl.BlockSpec((1, D), lambda t: (0, 0))]
    args = [x1, rows, gates_t, g_final.reshape(1, D)]
    body, aliases = _final_kernel, {}
    if prev_out is not None:
        in_specs.append(pl.BlockSpec(memory_space=pl.ANY))
        args.append(prev_out)
        body, aliases = _final_alias_kernel, {len(args) - 1: 0}
    return pl.pallas_call(
        body,
        grid=(Tp // tm,),
        in_specs=in_specs,
        out_specs=row,
        out_shape=jax.ShapeDtypeStruct((T, D), jnp.float32),
        input_output_aliases=aliases,
        compiler_params=pltpu.CompilerParams(dimension_semantics=("arbitrary",)),
        name="combine_final_norm",
    )(*args)


def kernel(x, g_mix, w_in, b_forget, w_pool, pool_scale, w_out, g_ffn, w_router, b_router,
           w_up, b_up, w_down, b_down, g_final):
    B, S, D = x.shape
    T = B * S
    head_order = _decay_head_order(b_forget[0])
    ypool, qt, qaug, k, kaug, vt, stats = _in_projection(x, g_mix[0], w_in[0], b_forget[0],
                                                        w_pool[0], pool_scale[0], head_order)
    yattn = _attention(qt, qaug, k, kaug, vt, _first_needed_chunk(stats))
    x1, h2, top_idx, gates, rank, counts = _out_projection(
        ypool.reshape(T, POOL_WIDTH), yattn.reshape(T, ATTN_WIDTH), x.reshape(T, D),
        w_out[0], g_ffn[0], w_router[0], b_router[0], head_order)

    blk = EXPERT_ROWS
    counts = counts[:, 0].astype(jnp.int32)
    padded = (counts + blk - 1) // blk * blk
    pad_ends = jnp.cumsum(padded)
    pad_starts = pad_ends - padded
    experts = jnp.arange(N_EXPERTS, dtype=jnp.int32)[:, None, None]
    dest = rank + jnp.sum(jnp.where(top_idx[None] == experts, pad_starts[:, None, None], 0),
                          axis=0)
    n_rows = T * TOP_K + N_EXPERTS * blk
    block_row0 = jnp.arange(n_rows // blk, dtype=jnp.int32) * blk
    block_e = jnp.sum((pad_ends[None, :] <= block_row0[:, None]).astype(jnp.int32), axis=1)
    block_e = jnp.minimum(block_e, N_EXPERTS - 1)
    of_block = block_e[:, None] == jnp.arange(N_EXPERTS, dtype=jnp.int32)[None, :]
    block_end = jnp.sum(jnp.where(of_block, (pad_starts + counts)[None, :], 0), axis=1)
    block_valid = jnp.clip(block_end - block_row0, 0, blk)
    block_valid = jnp.where(block_row0 < pad_ends[-1], block_valid, 0).astype(jnp.int32)

    xs = _dispatch_rows(h2, dest, n_rows)
    ys = _expert_ffn(xs, block_e, block_valid, w_up[0], b_up[0], w_down[0], b_down[0])
    gates_t = gates.T
    tp = T // COMBINE_PARTS
    out = None
    for part in range(COMBINE_PARTS):
        part_dest = dest[:, part * tp:(part + 1) * tp].reshape(-1)
        rows = _gather_rows(ys, part_dest).reshape(TOP_K, tp, D // 2)
        out = _final_norm(x1, rows, gates_t, g_final, part, out)
    return out.reshape(B, S, D)
```

```python
import functools

import jax
import jax.numpy as jnp
import numpy as np
from jax import lax
from jax.experimental import pallas as pl
from jax.experimental.pallas import tpu as pltpu
from jax.experimental.pallas import tpu_sc as plsc

D_MODEL = 1024
POOL_WIDTH = 512
POOL_WINDOWS = (2, 4, 8, 16)
POOL_GROUP = 128
MAX_WINDOW = 16
ATTN_WIDTH = 512
HEAD_DIM = 64
N_HEADS = 8
N_EXPERTS = 32
TOP_K = 4
D_EXPERT = 1024
SWIGLU_LIMIT = 7.0
SWIGLU_ALPHA = 1.702
RMS_EPS = 1e-5

LANES = 128
SUBLANES = 8
AUG_GROUP = 16
N_SPLIT = 3
MASK_VALUE = -1e30
LOG2_E = 1.4426950408889634
SKIP_MARGIN = 152.0
BOUND_SLACK = 0.5
NORM_SLACK = 1.02
V_ROWS = 80

SEQ_TILE = 512
CUMSUM_BLOCK = 128
Q_TILE = 512
HEADS_PER_STEP = 4
LOGITS_AHEAD = 2
KV_TILE = SEQ_TILE
ROUTE_TILE = 1024
ROUTE_PART = 512
EXPERT_ROWS = 512
COMBINE_PARTS = 8
SC_CORES = 2
SC_SUBCORES = 16
SC_WORKERS = SC_CORES * SC_SUBCORES
SC_WINDOW = 128
VMEM_LIMIT = 56 * 1024 * 1024

_NT = (((1,), (1,)), ((), ()))


def _dot(a, b):
    return jnp.dot(a, b, preferred_element_type=jnp.float32)


def _dot_nt(a, b):
    return lax.dot_general(a, b, _NT, preferred_element_type=jnp.float32)


def _rms_norm(x, g):
    return x * lax.rsqrt(jnp.mean(x * x, axis=-1, keepdims=True) + RMS_EPS) * g


def _pack_bf16_pairs(x):
    n = x.shape[1] // 2
    bits = lax.bitcast_convert_type(x.astype(jnp.bfloat16).astype(jnp.float32), jnp.uint32)
    return bits[:, :n] | (bits[:, n:] >> 16)


def _unpack_bf16_pairs(p):
    hi = lax.bitcast_convert_type(p & jnp.uint32(0xFFFF0000), jnp.float32)
    lo = lax.bitcast_convert_type(p << 16, jnp.float32)
    return hi, lo


def _split3(x):
    hi = x.astype(jnp.bfloat16)
    r1 = x - hi.astype(jnp.float32)
    mid = r1.astype(jnp.bfloat16)
    lo = (r1 - mid.astype(jnp.float32)).astype(jnp.bfloat16)
    return hi, mid, lo


def _inproj_kernel(x_ref, g_ref, wu_ref, wqt_ref, wk_ref, wvt_ref, wf_ref, bf_ref, tri_ref,
                   pqt_ref, pk_ref, oneq_ref, onek_ref, hsel_ref, wpool_ref, pscale_ref,
                   ypool_ref, qt_ref, qaug_ref, k_ref, kaug_ref, vt_ref, stats_ref, ubuf, ccarry):
    s = pl.program_id(1)
    tm = x_ref.shape[1]

    @pl.when(s == 0)
    def _():
        ubuf[...] = jnp.zeros_like(ubuf)
        ccarry[...] = jnp.zeros_like(ccarry)

    x = x_ref[0]
    h = _rms_norm(x, g_ref[...]).astype(jnp.bfloat16)
    fl = _dot(h, wf_ref[...]) + bf_ref[...]
    u = _dot(h, wu_ref[...])
    kp = _dot(h, wk_ref[...])
    qt = _dot_nt(wqt_ref[...], h)
    vt = _dot_nt(wvt_ref[...], h).astype(jnp.bfloat16)
    k_ref[0] = kp.astype(jnp.bfloat16)
    qt_ref[0] = qt.astype(jnp.bfloat16)

    logf = jnp.minimum(fl, 0.0) - jnp.log1p(jnp.exp(-jnp.abs(fl)))
    lane = lax.broadcasted_iota(jnp.int32, (tm, LANES), 1)
    logf = jnp.where(lane < N_SPLIT * N_HEADS, logf, 0.0)
    f_hi, f_mid, f_lo = _split3(logf)
    tri = tri_ref[...]
    running = ccarry[...]
    blocks = []
    for r0 in range(0, tm, CUMSUM_BLOCK):
        rows = slice(r0, r0 + CUMSUM_BLOCK)
        cb = _dot(tri, f_hi[rows]) + _dot(tri, f_mid[rows]) + _dot(tri, f_lo[rows]) + running
        running = cb[CUMSUM_BLOCK - 1:CUMSUM_BLOCK, :]
        blocks.append(cb)
    c = jnp.concatenate(blocks, axis=0)
    ccarry[...] = running
    c2 = c * LOG2_E
    c_hi, c_mid, c_lo = _split3(c2)
    term = lane % N_SPLIT
    c3 = jnp.where(term == 0, c_hi, jnp.where(term == 1, c_mid, c_lo))
    kaug_ref[0] = (_dot(c3, pk_ref[...]) + onek_ref[...]).astype(jnp.bfloat16)
    qaug_ref[0] = (_dot_nt(pqt_ref[...], c3) + oneq_ref[...]).astype(jnp.bfloat16)
    lane_row = lax.broadcasted_iota(jnp.int32, (1, LANES), 1)
    kn2 = jnp.max(_dot((kp * kp).astype(jnp.bfloat16), hsel_ref[...]), axis=0, keepdims=True)
    qn2 = jnp.zeros((1, LANES), jnp.float32)
    for hd in range(N_HEADS):
        qs = qt[hd * HEAD_DIM:(hd + 1) * HEAD_DIM, :]
        qmax = jnp.max(jnp.sum(qs * qs, axis=0, keepdims=True), axis=1, keepdims=True)
        qn2 = jnp.where(lane_row == hd, qmax, qn2)
    stats_ref[0, 0, 0:1, :] = kn2
    stats_ref[0, 0, 1:2, :] = qn2
    stats_ref[0, 0, 2:3, :] = c2[0:1, :]
    stats_ref[0, 0, 3:4, :] = c2[tm - 1:tm, :]
    stats_ref[0, 0, 4:SUBLANES, :] = jnp.zeros((SUBLANES - 4, LANES), jnp.float32)
    ones = jnp.ones((V_ROWS - HEAD_DIM, tm), jnp.bfloat16)
    for hd in range(N_HEADS):
        vt_ref[0, 0, hd * V_ROWS:hd * V_ROWS + HEAD_DIM, :] = vt[hd * HEAD_DIM:(hd + 1) * HEAD_DIM]
        vt_ref[0, 0, hd * V_ROWS + HEAD_DIM:(hd + 1) * V_ROWS, :] = ones

    head_pos = s * tm + lax.broadcasted_iota(jnp.int32, (MAX_WINDOW, 1), 0)
    for gi, w in enumerate(POOL_WINDOWS):
        cols = slice(gi * POOL_GROUP, (gi + 1) * POOL_GROUP)
        ug = u[:, cols]
        win = jnp.concatenate([ubuf[:, cols], ug], axis=0)
        span = 1
        while span < w:
            win = win + pltpu.roll(win, span, axis=0)
            span *= 2
        acc = win[MAX_WINDOW:]
        inv_head = 1.0 / jnp.minimum(head_pos + 1, w).astype(jnp.float32)
        p = jnp.concatenate([acc[:MAX_WINDOW] * inv_head, acc[MAX_WINDOW:] * (1.0 / w)],
                            axis=0) - ug
        y = _dot(p.astype(jnp.bfloat16), wpool_ref[gi]) * pscale_ref[:, cols]
        ypool_ref[0, :, cols] = y.astype(jnp.bfloat16)
    ubuf[...] = u[tm - MAX_WINDOW:]


def _in_projection(x, g_mix, w_in, b_forget, w_pool, pool_scale, head_order):
    B, S, D = x.shape
    tm = SEQ_TILE
    bf16 = jnp.bfloat16
    o = POOL_WIDTH
    w_u = w_in[:, :o].astype(bf16)
    w_q = w_in[:, o:o + ATTN_WIDTH] * (HEAD_DIM ** -0.5 * LOG2_E)
    w_k = w_in[:, o + ATTN_WIDTH:o + 2 * ATTN_WIDTH]
    w_v = w_in[:, o + 2 * ATTN_WIDTH:o + 3 * ATTN_WIDTH]
    w_f = w_in[:, o + 3 * ATTN_WIDTH:]
    by_head = lambda w: jnp.take(w.reshape(D, N_HEADS, HEAD_DIM), head_order,
                                 axis=1).reshape(D, ATTN_WIDTH)
    w_q, w_k, w_v = by_head(w_q), by_head(w_k), by_head(w_v)
    w_f = jnp.take(w_f, head_order, axis=1)
    b_forget = jnp.take(b_forget, head_order)

    wqt = w_q.T.astype(bf16)
    wk = w_k.astype(bf16)
    wvt = w_v.T.astype(bf16)
    n_gate = N_SPLIT * N_HEADS
    wf = jnp.pad(jnp.repeat(w_f, N_SPLIT, axis=1), ((0, 0), (0, LANES - n_gate))).astype(bf16)
    bfp = jnp.pad(jnp.repeat(b_forget, N_SPLIT), (0, LANES - n_gate)).reshape(1, LANES)

    assert N_HEADS * AUG_GROUP == LANES and 2 * N_SPLIT <= AUG_GROUP
    src = np.arange(LANES)
    head, term = src // N_SPLIT, src % N_SPLIT
    dst = np.arange(LANES)
    valid = (src < n_gate)[:, None]
    q_hit = valid & (dst[None, :] == (head * AUG_GROUP + term)[:, None])
    k_hit = valid & (dst[None, :] == (head * AUG_GROUP + N_SPLIT + term)[:, None])
    pqt = jnp.asarray(q_hit.T.astype(np.float32), bf16)
    pk = jnp.asarray(-k_hit.astype(np.float32), bf16)
    slot_pos = dst % AUG_GROUP
    oneq = ((slot_pos >= N_SPLIT) & (slot_pos < 2 * N_SPLIT))
    onek = slot_pos < N_SPLIT
    oneq = jnp.asarray(oneq.astype(np.float32).reshape(-1, 1))
    onek = jnp.asarray(onek.astype(np.float32).reshape(1, -1))
    r = np.arange(CUMSUM_BLOCK)
    tri = jnp.asarray((r[None, :] <= r[:, None]).astype(np.float32), bf16)
    hsel = jnp.asarray((np.arange(ATTN_WIDTH)[:, None] // HEAD_DIM
                        == np.arange(LANES)[None, :]).astype(np.float32), bf16)

    full = lambda *shape: pl.BlockSpec(shape, lambda b, s: (0,) * len(shape))
    aw = ATTN_WIDTH
    return pl.pallas_call(
        _inproj_kernel,
        grid=(B, S // tm),
        in_specs=[
            pl.BlockSpec((1, tm, D), lambda b, s: (b, s, 0)),
            full(1, D), full(D, o), full(aw, D), full(D, aw), full(aw, D),
            full(D, LANES), full(1, LANES), full(CUMSUM_BLOCK, CUMSUM_BLOCK),
            full(LANES, LANES), full(LANES, LANES),
            full(LANES, 1), full(1, LANES), full(aw, LANES),
            full(len(POOL_WINDOWS), POOL_GROUP, POOL_GROUP),
            full(1, o),
        ],
        out_specs=[
            pl.BlockSpec((1, tm, o), lambda b, s: (b, s, 0)),
            pl.BlockSpec((1, aw, tm), lambda b, s: (b, 0, s)),
            pl.BlockSpec((1, LANES, tm), lambda b, s: (b, 0, s)),
            pl.BlockSpec((1, tm, aw), lambda b, s: (b, s, 0)),
            pl.BlockSpec((1, tm, LANES), lambda b, s: (b, s, 0)),
            pl.BlockSpec((1, 1, N_HEADS * V_ROWS, tm), lambda b, s: (b, s, 0, 0)),
            pl.BlockSpec((1, 1, SUBLANES, LANES), lambda b, s: (b, s, 0, 0)),
        ],
        out_shape=[
            jax.ShapeDtypeStruct((B, S, o), bf16),
            jax.ShapeDtypeStruct((B, aw, S), bf16),
            jax.ShapeDtypeStruct((B, LANES, S), bf16),
            jax.ShapeDtypeStruct((B, S, aw), bf16),
            jax.ShapeDtypeStruct((B, S, LANES), bf16),
            jax.ShapeDtypeStruct((B, S // tm, N_HEADS * V_ROWS, tm), bf16),
            jax.ShapeDtypeStruct((B, S // tm, SUBLANES, LANES), jnp.float32),
        ],
        scratch_shapes=[
            pltpu.VMEM((MAX_WINDOW, o), jnp.float32),
            pltpu.VMEM((1, LANES), jnp.float32),
        ],
        compiler_params=pltpu.CompilerParams(
            dimension_semantics=("arbitrary", "arbitrary"), vmem_limit_bytes=VMEM_LIMIT),
        name="in_projection",
    )(x, g_mix.reshape(1, D), w_u, wqt, wk, wvt, wf, bfp, tri, pqt, pk, oneq, onek, hsel,
      w_pool.astype(bf16), pool_scale.reshape(1, o))


def _attn_kernel(j0_ref, qt_ref, qaug_ref, k_ref, kaug_ref, vt_ref, causal_ref, o_ref,
                 qop_ref, st_buf, mx_buf, m_ref, acc_ref):
    i = pl.program_id(2)
    tile = (pl.program_id(0) * pl.num_programs(1) + pl.program_id(1)) * pl.num_programs(2) + i
    j0 = j0_ref[tile]
    tq = qt_ref.shape[2]
    tk = vt_ref.shape[3]
    heads = qt_ref.shape[1] // HEAD_DIM

    m_ref[...] = jnp.full(m_ref.shape, MASK_VALUE, jnp.float32)
    acc_ref[...] = jnp.zeros_like(acc_ref)

    first_head = pl.program_id(1) * heads
    group = lax.broadcasted_iota(jnp.int32, (LANES, tq), 0) // AUG_GROUP
    q_aug = qaug_ref[0]
    blank = jnp.zeros((HEAD_DIM, tq), q_aug.dtype)
    for hh in range(heads):
        q_own = qt_ref[0, hh * HEAD_DIM:(hh + 1) * HEAD_DIM, :]
        pair_rows = [q_own, blank] if hh % 2 == 0 else [blank, q_own]
        qop_ref[hh, 0:2 * HEAD_DIM, :] = jnp.concatenate(pair_rows, axis=0)
        qop_ref[hh, 2 * HEAD_DIM:, :] = jnp.where(group == first_head + hh, q_aug,
                                                  jnp.zeros_like(q_aug))

    def logits(j, slot, hh, masked):
        row0 = pl.multiple_of(j * tk, tk)
        pair_lanes = slice(hh // 2 * 2 * HEAD_DIM, (hh // 2 + 1) * 2 * HEAD_DIM)
        k = jnp.concatenate([k_ref[0, pl.ds(row0, tk), pair_lanes],
                             kaug_ref[0, pl.ds(row0, tk), :]], axis=1)
        st = _dot(k, qop_ref[hh])
        if masked:
            st = st + causal_ref[...]
        st_buf[slot, hh] = st
        mx_buf[slot, hh] = jnp.max(st, axis=0, keepdims=True)

    def accumulate(j, slot, hh):
        st = st_buf[slot, hh]
        m_prev = m_ref[hh]
        m_new = jnp.maximum(m_prev, mx_buf[slot, hh])
        alpha = jnp.exp2(m_prev - m_new)
        p = jnp.exp2(st - m_new)
        vt = vt_ref[0, j, hh * V_ROWS:(hh + 1) * V_ROWS, :]
        acc_ref[hh] = alpha * acc_ref[hh] + _dot(vt, p.astype(jnp.bfloat16))
        m_ref[hh] = m_new

    ahead = LOGITS_AHEAD
    assert 1 <= ahead <= heads
    for hh in range(heads):
        logits(i, 0, hh, True)
        if hh >= ahead:
            accumulate(i, 0, hh - ahead)

    def before(t):
        return jnp.where(t == j0, i, t - 1)

    def step(t, slot):
        for hh in range(heads):
            logits(t, slot, hh, False)
            if hh >= ahead:
                accumulate(t, slot, hh - ahead)
            else:
                accumulate(before(t), 1 - slot, hh - ahead + heads)

    def drain(t, slot):
        for hh in range(heads - ahead, heads):
            accumulate(t, slot, hh)

    def body(u, carry):
        step(j0 + 2 * u, 1)
        step(j0 + 2 * u + 1, 0)
        return carry

    n_chunks = i - j0
    lax.fori_loop(0, n_chunks // 2, body, 0)

    @pl.when(n_chunks % 2 == 1)
    def _():
        step(i - 1, 1)
        drain(i - 1, 1)

    @pl.when(n_chunks % 2 == 0)
    def _():
        drain(before(i), 0)

    outs = [acc_ref[hh, :HEAD_DIM] / acc_ref[hh, HEAD_DIM:HEAD_DIM + 1] for hh in range(heads)]
    ot = jnp.concatenate(outs, axis=0)
    o_ref[0] = ot.T.astype(o_ref.dtype)


def _decay_head_order(b_forget):
    return jnp.argsort(b_forget).astype(jnp.int32)


def _first_needed_chunk(stats):
    kn = jnp.sqrt(stats[:, :, 0, :N_HEADS])
    qn = jnp.sqrt(stats[:, :, 1, :N_HEADS])
    c_first = stats[:, :, 2, 0:N_SPLIT * N_HEADS:N_SPLIT]
    c_last = stats[:, :, 3, 0:N_SPLIT * N_HEADS:N_SPLIT]
    n = stats.shape[1]
    dot_bound = NORM_SLACK * qn[:, :, None, :] * kn[:, None, :, :] + BOUND_SLACK
    upper = dot_bound + c_first[:, :, None, :] - c_last[:, None, :, :]
    self_bound = NORM_SLACK * qn * kn + BOUND_SLACK
    ii = jnp.arange(n, dtype=jnp.int32)[None, :, None, None]
    jj = jnp.arange(n, dtype=jnp.int32)[None, None, :, None]
    needed = (upper + self_bound[:, :, None, :] >= -SKIP_MARGIN) & (jj < ii)
    first = jnp.min(jnp.where(needed, jj, ii), axis=2)
    g = HEADS_PER_STEP
    first = jnp.min(first.reshape(first.shape[0], n, N_HEADS // g, g), axis=3)
    return jnp.transpose(first, (0, 2, 1)).reshape(-1).astype(jnp.int32)


def _attention(qt, qaug, k, kaug, vt, first_chunk):
    B, _, S = qt.shape
    tq, tk = Q_TILE, KV_TILE
    assert tq == tk, "the diagonal is handled as a single masked key chunk"
    pair = HEADS_PER_STEP
    n_pairs = N_HEADS // pair
    pw = pair * HEAD_DIM
    causal = np.where(np.arange(tk)[:, None] <= np.arange(tq)[None, :], 0.0, MASK_VALUE)
    grid_spec = pltpu.PrefetchScalarGridSpec(
        num_scalar_prefetch=1,
        grid=(B, n_pairs, S // tq),
        in_specs=[
            pl.BlockSpec((1, pw, tq), lambda b, p, i, j0: (b, p, i)),
            pl.BlockSpec((1, LANES, tq), lambda b, p, i, j0: (b, 0, i)),
            pl.BlockSpec((1, S, pw), lambda b, p, i, j0: (b, 0, p)),
            pl.BlockSpec((1, S, LANES), lambda b, p, i, j0: (b, 0, 0)),
            pl.BlockSpec((1, S // tk, pair * V_ROWS, tk), lambda b, p, i, j0: (b, 0, p, 0)),
            pl.BlockSpec((tk, tq), lambda b, p, i, j0: (0, 0)),
        ],
        out_specs=pl.BlockSpec((1, tq, pw), lambda b, p, i, j0: (b, i, p)),
        scratch_shapes=[
            pltpu.VMEM((pair, 2 * HEAD_DIM + LANES, tq), jnp.bfloat16),
            pltpu.VMEM((2, pair, tk, tq), jnp.float32),
            pltpu.VMEM((2, pair, 1, tq), jnp.float32),
            pltpu.VMEM((pair, 1, tq), jnp.float32),
            pltpu.VMEM((pair, V_ROWS, tq), jnp.float32),
        ],
    )
    return pl.pallas_call(
        _attn_kernel,
        grid_spec=grid_spec,
        out_shape=jax.ShapeDtypeStruct((B, S, ATTN_WIDTH), jnp.bfloat16),
        compiler_params=pltpu.CompilerParams(
            dimension_semantics=("arbitrary", "arbitrary", "arbitrary"),
            vmem_limit_bytes=VMEM_LIMIT),
        name="forgetting_attention",
    )(first_chunk, qt, qaug, k, kaug, vt, jnp.asarray(causal, jnp.float32))


def _outproj_kernel(yp_ref, ya_ref, x_ref, wo1_ref, wo2_ref, g_ref, wr2_ref, wrhi_ref, br_ref,
                    upper_ref, x1_ref, h2_ref, idx_ref, gate_ref, rank_ref, cnt_ref, carry):
    t = pl.program_id(0)
    tm = x_ref.shape[0]

    @pl.when(t == 0)
    def _():
        carry[...] = jnp.zeros_like(carry)

    hm = upper_ref.shape[0]
    parts = [slice(r * hm, (r + 1) * hm) for r in range(tm // hm)]

    def project(rows):
        x1 = x_ref[rows] + _dot(yp_ref[rows], wo1_ref[...]) + _dot(ya_ref[rows], wo2_ref[...])
        x1_ref[rows] = x1
        return x1

    def route(rows, x1):
        h2 = _rms_norm(x1, g_ref[...])
        h2_hi = h2.astype(jnp.bfloat16)
        h2_lo = (h2 - h2_hi.astype(jnp.float32)).astype(jnp.bfloat16)
        h2_ref[rows] = _pack_bf16_pairs(h2)

        lg2 = _dot_nt(wr2_ref[...], h2_hi)
        logits = (lg2[:N_EXPERTS] + lg2[N_EXPERTS:] + _dot_nt(wrhi_ref[...], h2_lo)
                  + br_ref[...])

        eio = lax.broadcasted_iota(jnp.int32, (N_EXPERTS, hm), 0)
        vals, idxs = [], []
        cur = logits
        for _ in range(TOP_K):
            m = jnp.max(cur, axis=0, keepdims=True)
            ix = jnp.min(jnp.where(cur == m, eio, N_EXPERTS), axis=0, keepdims=True)
            vals.append(m)
            idxs.append(ix)
            cur = jnp.where(eio == ix, -jnp.inf, cur)
        exps = [jnp.exp(v - vals[0]) for v in vals]
        denom = exps[0] + exps[1] + exps[2] + exps[3]
        hits = [eio == ix for ix in idxs]
        cnt = jnp.zeros((N_EXPERTS, hm), jnp.float32)
        for hit in hits:
            cnt = cnt + hit.astype(jnp.float32)
        base = _dot(cnt.astype(jnp.bfloat16), upper_ref[...]) + carry[...]
        for kk in range(TOP_K):
            idx_ref[kk:kk + 1, rows] = idxs[kk]
            gate_ref[kk:kk + 1, rows] = exps[kk] / denom
            rank = jnp.sum(jnp.where(hits[kk], base, 0.0), axis=0, keepdims=True)
            rank_ref[kk:kk + 1, rows] = rank.astype(jnp.int32)
        carry[...] = carry[...] + jnp.sum(cnt, axis=1, keepdims=True)

    pending = project(parts[0])
    for r, rows in enumerate(parts):
        ahead = project(parts[r + 1]) if r + 1 < len(parts) else None
        route(rows, pending)
        pending = ahead
    cnt_ref[...] = carry[...]


def _out_projection(ypool, yattn, x, w_out, g_ffn, w_router, b_router, head_order):
    T, D = x.shape
    tm = ROUTE_TILE
    bf16 = jnp.bfloat16
    w_attn = jnp.take(w_out[POOL_WIDTH:].reshape(N_HEADS, HEAD_DIM, D), head_order, axis=0)
    wo = jnp.concatenate([w_out[:POOL_WIDTH], w_attn.reshape(ATTN_WIDTH, D)], axis=0).astype(bf16)
    wr_t = w_router.T
    wr_hi = wr_t.astype(bf16)
    wr_lo = (wr_t - wr_hi.astype(jnp.float32)).astype(bf16)
    wr2 = jnp.concatenate([wr_hi, wr_lo], axis=0)
    r = np.arange(ROUTE_PART)
    upper = jnp.asarray((r[:, None] < r[None, :]).astype(np.float32), bf16)
    full = lambda *shape: pl.BlockSpec(shape, lambda t: (0,) * len(shape))
    row = lambda w: pl.BlockSpec((tm, w), lambda t: (t, 0))
    col = pl.BlockSpec((TOP_K, tm), lambda t: (0, t))
    return pl.pallas_call(
        _outproj_kernel,
        grid=(T // tm,),
        in_specs=[row(POOL_WIDTH), row(ATTN_WIDTH), row(D), full(POOL_WIDTH, D),
                  full(ATTN_WIDTH, D), full(1, D), full(2 * N_EXPERTS, D), full(N_EXPERTS, D),
                  full(N_EXPERTS, 1), full(ROUTE_PART, ROUTE_PART)],
        out_specs=[row(D), row(D // 2), col, col, col, full(N_EXPERTS, 1)],
        out_shape=[
            jax.ShapeDtypeStruct((T, D), jnp.float32),
            jax.ShapeDtypeStruct((T, D // 2), jnp.uint32),
            jax.ShapeDtypeStruct((TOP_K, T), jnp.int32),
            jax.ShapeDtypeStruct((TOP_K, T), jnp.float32),
            jax.ShapeDtypeStruct((TOP_K, T), jnp.int32),
            jax.ShapeDtypeStruct((N_EXPERTS, 1), jnp.float32),
        ],
        scratch_shapes=[pltpu.VMEM((N_EXPERTS, 1), jnp.float32)],
        compiler_params=pltpu.CompilerParams(
            dimension_semantics=("arbitrary",), vmem_limit_bytes=VMEM_LIMIT),
        name="out_projection_router",
    )(ypool, yattn, x, wo[:POOL_WIDTH], wo[POOL_WIDTH:], g_ffn.reshape(1, D), wr2, wr_hi,
      b_router.reshape(N_EXPERTS, 1), upper)


def _sc_mesh():
    return plsc.VectorSubcoreMesh(core_axis_name="c", subcore_axis_name="s")


def _dispatch_rows(h2, dest, n_rows):
    T, D = h2.shape
    per_worker = T // SC_WORKERS
    steps = per_worker // SC_WINDOW

    @functools.partial(
        pl.kernel, mesh=_sc_mesh(),
        out_type=jax.ShapeDtypeStruct((n_rows, D), h2.dtype),
        scratch_types=[pltpu.VMEM((TOP_K, SC_WINDOW), jnp.int32),
                       pltpu.VMEM((SC_WINDOW, D), h2.dtype),
                       pltpu.SemaphoreType.DMA],
    )
    def scatter_kernel(h2_hbm, dest_hbm, xs_hbm, idx_v, rows_v, sem):
        wid = lax.axis_index("s") * SC_CORES + lax.axis_index("c")

        @pl.loop(0, steps)
        def _(step):
            base = wid * per_worker + step * SC_WINDOW
            pltpu.sync_copy(h2_hbm.at[pl.ds(base, SC_WINDOW)], rows_v)
            pltpu.sync_copy(dest_hbm.at[:, pl.ds(base, SC_WINDOW)], idx_v)
            copies = [pltpu.async_copy(rows_v, xs_hbm.at[idx_v.at[kk]], sem)
                      for kk in range(TOP_K)]
            for c in copies:
                c.wait()

    return scatter_kernel(h2, dest)


def _gather_rows(table, idx):
    N = idx.shape[0]
    D = table.shape[1]
    per_worker = N // SC_WORKERS
    steps = per_worker // SC_WINDOW

    @functools.partial(
        pl.kernel, mesh=_sc_mesh(),
        out_type=jax.ShapeDtypeStruct((N, D), table.dtype),
        scratch_types=[pltpu.VMEM((SC_WINDOW,), jnp.int32),
                       pltpu.VMEM((SC_WINDOW, D), table.dtype),
                       pltpu.SemaphoreType.DMA],
    )
    def gather_kernel(table_hbm, idx_hbm, out_hbm, idx_v, rows_v, sem):
        wid = lax.axis_index("s") * SC_CORES + lax.axis_index("c")

        @pl.loop(0, steps)
        def _(step):
            base = wid * per_worker + step * SC_WINDOW
            pltpu.sync_copy(idx_hbm.at[pl.ds(base, SC_WINDOW)], idx_v)
            pltpu.async_copy(table_hbm.at[idx_v], rows_v, sem).wait()
            pltpu.sync_copy(rows_v, out_hbm.at[pl.ds(base, SC_WINDOW)])

    return gather_kernel(table, idx)


def _expert_kernel(be_ref, nvalid_ref, x_ref, wup_ref, bup_ref, wdn_ref, bdn_ref, y_ref,
                   wup_bf, wdn_bf):
    i = pl.program_id(0)
    blk = x_ref.shape[0]
    changed = jnp.logical_or(i == 0, be_ref[i] != be_ref[jnp.maximum(i - 1, 0)])

    @pl.when(changed)
    def _():
        wup_bf[...] = wup_ref[0].astype(jnp.bfloat16)
        wdn_bf[...] = wdn_ref[0].astype(jnp.bfloat16)

    nvalid = nvalid_ref[i]

    @pl.when(nvalid > 0)
    def _():
        row = lax.broadcasted_iota(jnp.int32, (blk, 1), 0)
        x_hi, x_lo = _unpack_bf16_pairs(jnp.where(row < nvalid, x_ref[...], jnp.uint32(0)))
        x = jnp.concatenate([x_hi.astype(jnp.bfloat16), x_lo.astype(jnp.bfloat16)], axis=1)
        glu = _dot(x, wup_bf[:, :D_EXPERT]) + bup_ref[0, :, :D_EXPERT]
        lin = _dot(x, wup_bf[:, D_EXPERT:]) + bup_ref[0, :, D_EXPERT:]
        glu = jnp.minimum(glu, SWIGLU_LIMIT)
        lin = jnp.clip(lin, -SWIGLU_LIMIT, SWIGLU_LIMIT)
        act = glu * (1.0 / (1.0 + jnp.exp(-SWIGLU_ALPHA * glu))) * (lin + 1.0)
        y = _dot(act.astype(jnp.bfloat16), wdn_bf[...]) + bdn_ref[0]
        y_ref[...] = _pack_bf16_pairs(y)

    @pl.when(nvalid <= 0)
    def _():
        y_ref[...] = jnp.zeros_like(y_ref)


def _expert_ffn(xs, block_e, block_valid, w_up, b_up, w_down, b_down):
    n_rows = xs.shape[0]
    D = D_MODEL
    blk = EXPERT_ROWS
    grid_spec = pltpu.PrefetchScalarGridSpec(
        num_scalar_prefetch=2,
        grid=(n_rows // blk,),
        in_specs=[
            pl.BlockSpec((blk, D // 2), lambda i, be, nv: (i, 0)),
            pl.BlockSpec((1, D, 2 * D_EXPERT), lambda i, be, nv: (be[i], 0, 0)),
            pl.BlockSpec((1, 1, 2 * D_EXPERT), lambda i, be, nv: (be[i], 0, 0)),
            pl.BlockSpec((1, D_EXPERT, D), lambda i, be, nv: (be[i], 0, 0)),
            pl.BlockSpec((1, 1, D), lambda i, be, nv: (be[i], 0, 0)),
        ],
        out_specs=pl.BlockSpec((blk, D // 2), lambda i, be, nv: (i, 0)),
        scratch_shapes=[pltpu.VMEM((D, 2 * D_EXPERT), jnp.bfloat16),
                        pltpu.VMEM((D_EXPERT, D), jnp.bfloat16)],
    )
    return pl.pallas_call(
        _expert_kernel,
        grid_spec=grid_spec,
        out_shape=jax.ShapeDtypeStruct((n_rows, D // 2), jnp.uint32),
        compiler_params=pltpu.CompilerParams(
            dimension_semantics=("arbitrary",), vmem_limit_bytes=VMEM_LIMIT),
        name="expert_ffn",
    )(block_e, block_valid, xs, w_up, b_up.reshape(N_EXPERTS, 1, 2 * D_EXPERT), w_down,
      b_down.reshape(N_EXPERTS, 1, D))


def _final_kernel(x1_ref, rows_ref, gate_ref, g_ref, o_ref):
    half = x1_ref.shape[1] // 2
    acc_hi = x1_ref[:, :half]
    acc_lo = x1_ref[:, half:]
    for kk in range(TOP_K):
        y_hi, y_lo = _unpack_bf16_pairs(rows_ref[kk])
        gate = gate_ref[:, kk:kk + 1]
        acc_hi = acc_hi + y_hi * gate
        acc_lo = acc_lo + y_lo * gate
    o_ref[...] = _rms_norm(jnp.concatenate([acc_hi, acc_lo], axis=1), g_ref[...])


def _final_alias_kernel(x1_ref, rows_ref, gate_ref, g_ref, prev_ref, o_ref):
    del prev_ref
    _final_kernel(x1_ref, rows_ref, gate_ref, g_ref, o_ref)


def _final_norm(x1, rows, gates_t, g_final, part, prev_out):
    T, D = x1.shape
    Tp = rows.shape[1]
    tm = SEQ_TILE
    t0 = part * (Tp // tm)
    row = pl.BlockSpec((tm, D), lambda t: (t + t0, 0))
    in_specs = [row, pl.BlockSpec((TOP_K, tm, D // 2), lambda t: (0, t, 0)),
                pl.BlockSpec((tm, TOP_K), lambda t: (t + t0, 0)),
                pl.BlockSpec((1, D), lambda t: (0, 0))]
    args = [x1, rows, gates_t, g_final.reshape(1, D)]
    body, aliases = _final_kernel, {}
    if prev_out is not None:
        in_specs.append(pl.BlockSpec(memory_space=pl.ANY))
        args.append(prev_out)
        body, aliases = _final_alias_kernel, {len(args) - 1: 0}
    return pl.pallas_call(
        body,
        grid=(Tp // tm,),
        in_specs=in_specs,
        out_specs=row,
        out_shape=jax.ShapeDtypeStruct((T, D), jnp.float32),
        input_output_aliases=aliases,
        compiler_params=pltpu.CompilerParams(dimension_semantics=("arbitrary",)),
        name="combine_final_norm",
    )(*args)


def kernel(x, g_mix, w_in, b_forget, w_pool, pool_scale, w_out, g_ffn, w_router, b_router,
           w_up, b_up, w_down, b_down, g_final):
    B, S, D = x.shape
    T = B * S
    head_order = _decay_head_order(b_forget[0])
    ypool, qt, qaug, k, kaug, vt, stats = _in_projection(x, g_mix[0], w_in[0], b_forget[0],
                                                        w_pool[0], pool_scale[0], head_order)
    yattn = _attention(qt, qaug, k, kaug, vt, _first_needed_chunk(stats))
    x1, h2, top_idx, gates, rank, counts = _out_projection(
        ypool.reshape(T, POOL_WIDTH), yattn.reshape(T, ATTN_WIDTH), x.reshape(T, D),
        w_out[0], g_ffn[0], w_router[0], b_router[0], head_order)

    blk = EXPERT_ROWS
    counts = counts[:, 0].astype(jnp.int32)
    padded = (counts + blk - 1) // blk * blk
    pad_ends = jnp.cumsum(padded)
    pad_starts = pad_ends - padded
    experts = jnp.arange(N_EXPERTS, dtype=jnp.int32)[:, None, None]
    dest = rank + jnp.sum(jnp.where(top_idx[None] == experts, pad_starts[:, None, None], 0),
                          axis=0)
    n_rows = T * TOP_K + N_EXPERTS * blk
    block_row0 = jnp.arange(n_rows // blk, dtype=jnp.int32) * blk
    block_e = jnp.sum((pad_ends[None, :] <= block_row0[:, None]).astype(jnp.int32), axis=1)
    block_e = jnp.minimum(block_e, N_EXPERTS - 1)
    of_block = block_e[:, None] == jnp.arange(N_EXPERTS, dtype=jnp.int32)[None, :]
    block_end = jnp.sum(jnp.where(of_block, (pad_starts + counts)[None, :], 0), axis=1)
    block_valid = jnp.clip(block_end - block_row0, 0, blk)
    block_valid = jnp.where(block_row0 < pad_ends[-1], block_valid, 0).astype(jnp.int32)

    xs = _dispatch_rows(h2, dest, n_rows)
    ys = _expert_ffn(xs, block_e, block_valid, w_up[0], b_up[0], w_down[0], b_down[0])
    gates_t = gates.T
    tp = T // COMBINE_PARTS
    out = None
    for part in range(COMBINE_PARTS):
        part_dest = dest[:, part * tp:(part + 1) * tp].reshape(-1)
        rows = _gather_rows(ys, part_dest).reshape(TOP_K, tp, D // 2)
        out = _final_norm(x1, rows, gates_t, g_final, part, out)
    return out.reshape(B, S, D)
```
